```python
import math
import numpy as np
import jax
import jax.numpy as jnp
from jax import lax

D_MODEL = 1024
BATCH = 4
SEQ = 4096
DEPTH = 2
DEC_BATCH = 32
DEC_SEQ = 64
PAST_LEN = 4096

CHUNK = 64
QBLOCK = 128
N_GROUPS = 4
GROUP_W = D_MODEL // N_GROUPS
N_HEADS = 4
HEAD_DIM = GROUP_W // N_HEADS
DIFF_HALF = HEAD_DIM // 2
CONV_W = 4
A_CONV_CH = 3 * GROUP_W
MEM_TOKENS = 256
MEM_HEADS = 4
MEM_HEAD_DIM = D_MODEL // MEM_HEADS
D_FF = ((8 * D_MODEL + 3 * 256 - 1) // (3 * 256)) * 256
EPS = 1e-6
NEG = -1e30
IN_SIZES = (3 * GROUP_W, GROUP_W, N_HEADS, N_HEADS,
            GROUP_W, GROUP_W, GROUP_W,
            GROUP_W, GROUP_W, GROUP_W,
            GROUP_W, GROUP_W, GROUP_W, GROUP_W)
D_IN = 14 * GROUP_W + 2 * N_HEADS

kernel_name = "hybrid_streaming_parallel_groups_step"


def rmsnorm(x, w):
    xf = x.astype(jnp.float32)
    y = xf * lax.rsqrt(jnp.mean(xf * xf, axis=-1, keepdims=True) + EPS)
    return (y * w.astype(jnp.float32)).astype(x.dtype)


def l2norm(x):
    xf = x.astype(jnp.float32)
    return (xf * lax.rsqrt(jnp.sum(xf * xf, axis=-1, keepdims=True) + EPS)).astype(x.dtype)


def gated_rmsnorm(o, w, z):
    of = o.astype(jnp.float32)
    y = of * lax.rsqrt(jnp.mean(of * of, axis=-1, keepdims=True) + EPS) * w.astype(jnp.float32)
    return (y * jax.nn.silu(z.astype(jnp.float32))).astype(z.dtype)


def _chunks(t, L):
    b, n = t.shape[0], t.shape[1] // L
    t = t.reshape((b, n, L) + t.shape[2:])
    return jnp.moveaxis(jnp.moveaxis(t, 1, 0), 2, 3)


def _unchunk(o, b, t):
    o = jnp.moveaxis(jnp.moveaxis(o, 3, 2), 0, 1)
    return o.reshape((b, t) + o.shape[3:])


def _masked_decay(diff, mask):
    return jnp.where(mask, jnp.exp(jnp.where(mask, diff, 0.0)), 0.0)


def gated_delta_chunked(q, k, v, g, beta, s0):
    b, t = q.shape[0], q.shape[1]
    L = min(CHUNK, t)
    f32 = jnp.float32
    incl = jnp.tril(jnp.ones((L, L), bool))
    strict = jnp.tril(jnp.ones((L, L), bool), -1)
    eye = jnp.eye(L, dtype=f32)

    def step(S, inp):
        qc, kc, vc, gc, bc = inp
        G = jnp.cumsum(gc, axis=-1)
        decay = _masked_decay(G[..., :, None] - G[..., None, :], incl)
        kb = kc * bc[..., None]
        a = jnp.where(strict, jnp.einsum("bhid,bhjd->bhij", kb, kc) * decay, 0.0)
        rhs = vc * bc[..., None] - jnp.einsum("bhid,bhde->bhie", kb * jnp.exp(G)[..., None], S)
        w = lax.linalg.triangular_solve(eye + a, rhs, left_side=True, lower=True, unit_diagonal=True)
        o = (jnp.einsum("bhid,bhde->bhie", qc * jnp.exp(G)[..., None], S)
             + jnp.einsum("bhij,bhje->bhie", jnp.einsum("bhid,bhjd->bhij", qc, kc) * decay, w))
        gl = G[..., -1]
        S_new = (S * jnp.exp(gl)[..., None, None]
                 + jnp.einsum("bhjd,bhje->bhde", kc * jnp.exp(gl[..., None] - G)[..., None], w))
        return S_new, o

    xs = tuple(_chunks(a.astype(f32), L) for a in (q, k, v, g, beta))
    S, o = lax.scan(step, s0.astype(f32), xs)
    return _unchunk(o, b, t), S


def hgrn2_chunked(q, k, v, log_f, s0):
    b, t = q.shape[0], q.shape[1]
    L = min(CHUNK, t)
    f32 = jnp.float32
    incl = jnp.tril(jnp.ones((L, L), bool))[..., None]

    def step(S, inp):
        qc, kc, vc, lf = inp
        Bc = jnp.cumsum(lf, axis=2)
        diff = Bc[:, :, :, None, :] - Bc[:, :, None, :, :]
        decay = _masked_decay(diff, incl)
        a = jnp.einsum("bhid,bhjd,bhijd->bhij", qc, kc, decay)
        o = jnp.einsum("bhid,bhde->bhie", qc * jnp.exp(Bc), S) + jnp.einsum("bhij,bhje->bhie", a, vc)
        bl = Bc[:, :, -1]
        S_new = S * jnp.exp(bl)[..., None] + jnp.einsum("bhjd,bhje->bhde", kc * jnp.exp(bl[:, :, None] - Bc), vc)
        return S_new, o

    xs = tuple(_chunks(a.astype(f32), L) for a in (q, k, v, log_f))
    S, o = lax.scan(step, s0.astype(f32), xs)
    return _unchunk(o, b, t), S


def sweep_queries(fn, q, pos_q):
    b, t = q.shape[0], q.shape[1]
    if t <= QBLOCK:
        return fn(q, pos_q)
    nb = t // QBLOCK
    qb = jnp.moveaxis(q.reshape((b, nb, QBLOCK) + q.shape[2:]), 1, 0)
    pb = pos_q.reshape(nb, QBLOCK)
    out = lax.map(lambda qp: fn(qp[0], qp[1]), (qb, pb))
    out = jnp.moveaxis(out, 0, 1)
    return out.reshape((b, t) + out.shape[3:])


def diff_attn_block(qb, pq, k, v, pk, lam, slopes):
    b, tq, h, _ = qb.shape
    q2 = qb.reshape(b, tq, h, 2, DIFF_HALF)
    k2 = k.reshape(k.shape[0], k.shape[1], h, 2, DIFF_HALF)
    s = jnp.einsum("bqhcd,bkhcd->bchqk", q2, k2).astype(jnp.float32) * DIFF_HALF ** -0.5
    rel = jnp.abs(pq[:, None] - pk[None, :]).astype(jnp.float32)
    allowed = (pk[None, :] // CHUNK) <= (pq[:, None] // CHUNK)
    s = jnp.where(allowed, s - slopes[:, None, None] * rel, NEG)
    p = jax.nn.softmax(s, axis=-1)
    w = p[:, 0] - lam * p[:, 1]
    return jnp.einsum("bhqk,bkhe->bqhe", w.astype(v.dtype), v)


def stick_breaking_block(qb, pq, k, v, pk):
    z = jnp.einsum("bqhd,bkhd->bhqk", qb, k).astype(jnp.float32) * HEAD_DIM ** -0.5
    mask = pk[None, :] < pq[:, None]
    log_keep = jnp.where(mask, jax.nn.log_sigmoid(-z), 0.0)
    after = lax.cumsum(log_keep, axis=3, reverse=True) - log_keep
    a = jnp.where(mask, jnp.exp(jnp.where(mask, jax.nn.log_sigmoid(z) + after, 0.0)), 0.0)
    return jnp.einsum("bhqk,bkhe->bqhe", a.astype(v.dtype), v)


def hgrn_lower_bounds(lb_raw):
    p = jax.nn.softmax(lb_raw.astype(jnp.float32), axis=0)
    return jnp.cumsum(p, axis=0) - p[0]


def _layer(x, l, st, mk, mv, W, lb, slopes):
    b, t, _ = x.shape
    dt = x.dtype
    f32 = jnp.float32
    heads = lambda a: a.reshape(b, t, N_HEADS, HEAD_DIM)
    past = 0 if st["b_k"] is None else st["b_k"].shape[1]
    pos_q = past + jnp.arange(t, dtype=jnp.int32)
    pos_k = jnp.arange(past + t, dtype=jnp.int32)
    with_past = lambda old, new: new if old is None else jnp.concatenate([old.astype(dt), new], axis=1)

    h = rmsnorm(x, W["norm_mix"][l])
    split_idx = np.cumsum(IN_SIZES)[:-1].tolist()
    (a_qkv, a_z, a_b, a_g, b_q, b_k, b_v, c_q, c_k, c_v,
     d_q, d_f, d_i, d_g) = jnp.split(h @ W["w_in"][l], split_idx, axis=-1)

    conv_in = jnp.concatenate([st["a_conv"].astype(dt), a_qkv], axis=1)
    u = lax.conv_general_dilated(conv_in, W["a_conv_w"][l][:, None, :].astype(dt), (1,), "VALID",
                                 dimension_numbers=("NWC", "WIO", "NWC"), feature_group_count=A_CONV_CH)
    aq, ak, av = (heads(a) for a in jnp.split(jax.nn.silu(u), 3, axis=-1))
    beta = jax.nn.sigmoid(a_b.astype(f32))
    g = -jnp.exp(W["a_A_log"][l].astype(f32)) * jax.nn.softplus(a_g.astype(f32) + W["a_dt_bias"][l].astype(f32))
    o_a, a_S = gated_delta_chunked(l2norm(aq) * HEAD_DIM ** -0.5, l2norm(ak), av, g, beta, st["a_S"])
    o_a = gated_rmsnorm(o_a, W["a_norm"][l], heads(a_z))

    bk, bv = heads(b_k), heads(b_v)
    lam_init = 0.8 - 0.6 * math.exp(-0.3 * l)
    lam = (jnp.exp(jnp.sum(W["b_lam_q1"][l].astype(f32) * W["b_lam_k1"][l].astype(f32)))
           - jnp.exp(jnp.sum(W["b_lam_q2"][l].astype(f32) * W["b_lam_k2"][l].astype(f32))) + lam_init)
    kb_all, vb_all = with_past(st["b_k"], bk), with_past(st["b_v"], bv)
    o_b = sweep_queries(lambda qb, pq: diff_attn_block(qb, pq, kb_all, vb_all, pos_k, lam, slopes), heads(b_q), pos_q)
    o_b = (rmsnorm(o_b, W["b_norm"][l]).astype(f32) * (1.0 - lam_init)).astype(dt)

    ck, cv = heads(c_k), heads(c_v)
    kc_all, vc_all = with_past(st["c_k"], ck), with_past(st["c_v"], cv)
    o_c = sweep_queries(lambda qb, pq: stick_breaking_block(qb, pq, kc_all, vc_all, pos_k), heads(c_q), pos_q)

    lbh = lb[l].reshape(N_HEADS, HEAD_DIM)
    fl = heads(d_f).astype(f32)
    log_f = jax.nn.log_sigmoid(fl) + jnp.log1p(lbh * jnp.exp(-fl))
    d_key = (1.0 - lbh) * jax.nn.sigmoid(-fl)
    o_d, d_S = hgrn2_chunked(heads(d_q) * HEAD_DIM ** -0.5, d_key, heads(d_i), log_f, st["d_S"])
    o_d = gated_rmsnorm(o_d, W["d_norm"][l], heads(d_g))

    o_mix = jnp.concatenate([o_a, o_b.astype(dt), o_c.astype(dt), o_d], axis=2).reshape(b, t, N_GROUPS * GROUP_W)
    x = x + o_mix @ W["w_out"][l]

    cq = (rmsnorm(x, W["norm_cross"][l]) @ W["w_cq"][l]).reshape(b, t, MEM_HEADS, MEM_HEAD_DIM)
    s = jnp.einsum("bthd,bmhd->bhtm", cq, mk.astype(dt)).astype(f32) * MEM_HEAD_DIM ** -0.5
    p = jax.nn.softmax(s, axis=-1).astype(dt)
    co = jnp.einsum("bhtm,bmhd->bthd", p, mv.astype(dt)).reshape(b, t, D_MODEL)
    x = x + co @ W["w_co"][l]

    hf = rmsnorm(x, W["norm_ffn"][l])
    x = x + (jax.nn.silu(hf @ W["w_gate"][l]) * (hf @ W["w_up"][l])) @ W["w_down"][l]

    new = (conv_in[:, -(CONV_W - 1):], a_S.astype(dt), bk, bv, ck, cv, d_S.astype(dt))
    return x, new


def setup_inputs(seed: int = 0) -> dict:
    key = jax.random.key(seed)
    keys = iter(jax.random.split(key, 48))
    f32 = jnp.float32

    def nrm(shape, scale):
        return jax.random.normal(next(keys), shape, f32) * scale

    def gain(shape):
        return 1.0 + nrm(shape, 0.02)

    a_A = jax.random.uniform(next(keys), (DEPTH, N_HEADS), f32, 1.0, 16.0)
    a_dt = jnp.exp(jax.random.uniform(next(keys), (DEPTH, N_HEADS), f32, math.log(1e-3), math.log(1e-1)))
    kv_shape = (DEPTH, DEC_BATCH, PAST_LEN, N_HEADS, HEAD_DIM)
    mem_shape = (DEPTH, DEC_BATCH, MEM_TOKENS, MEM_HEADS, MEM_HEAD_DIM)
    return {
        "x_prompt": nrm((BATCH, SEQ, D_MODEL), 1.0),
        "x_sample": nrm((DEC_BATCH, DEC_SEQ, D_MODEL), 1.0),
        "mem_prompt": nrm((BATCH, MEM_TOKENS, D_MODEL), 1.0),
        "state_a_conv": nrm((DEPTH, DEC_BATCH, CONV_W - 1, A_CONV_CH), 1.0),
        "state_a_S": nrm((DEPTH, DEC_BATCH, N_HEADS, HEAD_DIM, HEAD_DIM), 0.1),
        "cache_b_k": nrm(kv_shape, 1.0),
        "cache_b_v": nrm(kv_shape, 1.0),
        "cache_c_k": nrm(kv_shape, 1.0),
        "cache_c_v": nrm(kv_shape, 1.0),
        "state_d_S": nrm((DEPTH, DEC_BATCH, N_HEADS, HEAD_DIM, HEAD_DIM), 0.3),
        "cache_mem_k": nrm(mem_shape, 1.0),
        "cache_mem_v": nrm(mem_shape, 1.0),
        "norm_mix": gain((DEPTH, D_MODEL)),
        "w_in": nrm((DEPTH, D_MODEL, D_IN), D_MODEL ** -0.5),
        "a_conv_w": nrm((DEPTH, CONV_W, A_CONV_CH), CONV_W ** -0.5),
        "a_A_log": jnp.log(a_A),
        "a_dt_bias": jnp.log(jnp.expm1(a_dt)),
        "a_norm": gain((DEPTH, HEAD_DIM)),
        "b_lam_q1": nrm((DEPTH, DIFF_HALF), 0.1),
        "b_lam_k1": nrm((DEPTH, DIFF_HALF), 0.1),
        "b_lam_q2": nrm((DEPTH, DIFF_HALF), 0.1),
        "b_lam_k2": nrm((DEPTH, DIFF_HALF), 0.1),
        "b_norm": gain((DEPTH, HEAD_DIM)),
        "d_lb": nrm((DEPTH, GROUP_W), 0.1),
        "d_norm": gain((DEPTH, HEAD_DIM)),
        "w_out": nrm((DEPTH, N_GROUPS * GROUP_W, D_MODEL), (N_GROUPS * GROUP_W) ** -0.5),
        "norm_cross": gain((DEPTH, D_MODEL)),
        "norm_memtok": gain((DEPTH, D_MODEL)),
        "w_cq": nrm((DEPTH, D_MODEL, D_MODEL), D_MODEL ** -0.5),
        "w_ck": nrm((DEPTH, D_MODEL, D_MODEL), D_MODEL ** -0.5),
        "w_cv": nrm((DEPTH, D_MODEL, D_MODEL), D_MODEL ** -0.5),
        "w_co": nrm((DEPTH, D_MODEL, D_MODEL), D_MODEL ** -0.5),
        "norm_ffn": gain((DEPTH, D_MODEL)),
        "w_gate": nrm((DEPTH, D_MODEL, D_FF), D_MODEL ** -0.5),
        "w_up": nrm((DEPTH, D_MODEL, D_FF), D_MODEL ** -0.5),
        "w_down": nrm((DEPTH, D_FF, D_MODEL), D_FF ** -0.5),
        "norm_final": gain((D_MODEL,)),
    }


def reference(x_prompt, x_sample, mem_prompt, state_a_conv, state_a_S, cache_b_k, cache_b_v,
              cache_c_k, cache_c_v, state_d_S, cache_mem_k, cache_mem_v,
              norm_mix, w_in, a_conv_w, a_A_log, a_dt_bias, a_norm, b_lam_q1, b_lam_k1, b_lam_q2,
              b_lam_k2, b_norm, d_lb, d_norm, w_out, norm_cross, norm_memtok, w_cq, w_ck, w_cv, w_co,
              norm_ffn, w_gate, w_up, w_down, norm_final):
    W = {"norm_mix": norm_mix, "w_in": w_in, "a_conv_w": a_conv_w, "a_A_log": a_A_log,
         "a_dt_bias": a_dt_bias, "a_norm": a_norm, "b_lam_q1": b_lam_q1, "b_lam_k1": b_lam_k1,
         "b_lam_q2": b_lam_q2, "b_lam_k2": b_lam_k2, "b_norm": b_norm, "d_norm": d_norm,
         "w_out": w_out, "norm_cross": norm_cross, "w_cq": w_cq, "w_co": w_co,
         "norm_ffn": norm_ffn, "w_gate": w_gate, "w_up": w_up, "w_down": w_down}
    lb = hgrn_lower_bounds(d_lb)
    slopes = jnp.exp2(-8.0 * jnp.arange(1, N_HEADS + 1, dtype=jnp.float32) / N_HEADS)

    bp, dt = x_prompt.shape[0], x_prompt.dtype
    h = x_prompt
    p_new = []
    for l in range(DEPTH):
        mem_h = rmsnorm(mem_prompt, norm_memtok[l])
        mk = (mem_h @ w_ck[l]).reshape(bp, -1, MEM_HEADS, MEM_HEAD_DIM)
        mv = (mem_h @ w_cv[l]).reshape(bp, -1, MEM_HEADS, MEM_HEAD_DIM)
        st = {"a_conv": jnp.zeros((bp, CONV_W - 1, A_CONV_CH), dt),
              "a_S": jnp.zeros((bp, N_HEADS, HEAD_DIM, HEAD_DIM), jnp.float32),
              "b_k": None, "b_v": None, "c_k": None, "c_v": None,
              "d_S": jnp.zeros((bp, N_HEADS, HEAD_DIM, HEAD_DIM), jnp.float32)}
        h, new = _layer(h, l, st, mk, mv, W, lb, slopes)
        p_new.append(new + (mk, mv))
    y_prompt = rmsnorm(h, norm_final)
    p_a_conv, p_a_S, p_b_k, p_b_v, p_c_k, p_c_v, p_d_S, p_mem_k, p_mem_v = [jnp.stack(c) for c in zip(*p_new)]

    h = x_sample
    s_new = []
    for l in range(DEPTH):
        st = {"a_conv": state_a_conv[l], "a_S": state_a_S[l], "b_k": cache_b_k[l], "b_v": cache_b_v[l],
              "c_k": cache_c_k[l], "c_v": cache_c_v[l], "d_S": state_d_S[l]}
        h, new = _layer(h, l, st, cache_mem_k[l], cache_mem_v[l], W, lb, slopes)
        s_new.append(new)
    y_sample = rmsnorm(h, norm_final)
    s_a_conv, s_a_S, s_b_k, s_b_v, s_c_k, s_c_v, s_d_S = [jnp.stack(c) for c in zip(*s_new)]

    return (y_prompt, y_sample, p_a_conv, p_a_S, p_b_k, p_b_v, p_c_k, p_c_v, p_d_S, p_mem_k, p_mem_v,
            s_a_conv, s_a_S, s_b_k, s_b_v, s_c_k, s_c_v, s_d_S)
```

```python
import functools
import math

import jax
import jax.numpy as jnp
from jax import lax
from jax.experimental import pallas as pl
from jax.experimental.pallas import tpu as pltpu

F32 = jnp.float32
BF16 = jnp.bfloat16
HIGHEST = lax.Precision.HIGHEST

D_MODEL = 1024
GROUP_W = 256
N_HEADS = 4
HEAD_DIM = 64
DIFF_HALF = 32
CHUNK = 64
CONV_W = 4
MEM_HEADS = 4
MEM_HEAD_DIM = 256
D_FF = 2816
EPS = 1e-6
NEG = -1e30
ABG_W = 128
FF_CHUNK = 256
VMEM_LIMIT = 56 * 1024 * 1024


def _cparams(*sem):
    return pltpu.CompilerParams(dimension_semantics=sem, vmem_limit_bytes=VMEM_LIMIT)


def _dot(a, b):
    return jnp.dot(a, b, preferred_element_type=F32)


def _dot_nt(a, b):
    return lax.dot_general(a, b, (((1,), (1,)), ((), ())), preferred_element_type=F32)


def _hdot(a, b):
    return jnp.dot(a, b, precision=HIGHEST, preferred_element_type=F32)


def _hdot_nt(a, b):
    return lax.dot_general(a, b, (((1,), (1,)), ((), ())), precision=HIGHEST, preferred_element_type=F32)


def _hdot_tn(a, b):
    return lax.dot_general(a, b, (((0,), (0,)), ((), ())), precision=HIGHEST, preferred_element_type=F32)


def _rms(x, g):
    return x * lax.rsqrt(jnp.mean(x * x, axis=-1, keepdims=True) + EPS) * g


def _sigmoid(x):
    return 1.0 / (1.0 + jnp.exp(-x))


def _log_sigmoid(x):
    return jnp.minimum(x, 0.0) - jnp.log(1.0 + jnp.exp(-jnp.abs(x)))


def _softplus(x):
    return jnp.maximum(x, 0.0) + jnp.log(1.0 + jnp.exp(-jnp.abs(x)))


def _iota(shape, dim):
    return lax.broadcasted_iota(jnp.int32, shape, dim)


def _head_block_ones(n):
    return ((_iota((n, n), 0) >> 6) == (_iota((n, n), 1) >> 6)).astype(F32)


def _resident(shape):
    return pl.BlockSpec(shape, lambda *_: (0,) * len(shape), pipeline_mode=pl.Buffered(1))


def _linear_body(*refs, ks, segs, has_norm, has_res):
    n_in = len(ks)
    x_refs = refs[:n_in]
    pos = n_in
    g_ref = None
    if has_norm:
        g_ref = refs[pos]
        pos += 1
    w_ref = refs[pos]
    pos += 1
    r_ref = None
    if has_res:
        r_ref = refs[pos]
        pos += 1
    out_refs = refs[pos:]
    xb = []
    for x_ref in x_refs:
        x = x_ref[...].astype(F32)
        if has_norm:
            x = _rms(x, g_ref[...])
        xb.append(x.astype(BF16))
    for o_ref, (s, e) in zip(out_refs, segs):
        acc = None
        k0 = 0
        for xk, k in zip(xb, ks):
            t = _dot(xk, w_ref[k0:k0 + k, s:e])
            acc = t if acc is None else acc + t
            k0 += k
        if has_res:
            acc = acc + r_ref[:, s:e]
        o_ref[...] = acc.astype(o_ref.dtype)


def _fused_linear(xs, w, widths, gain=None, res=None, tm=256):
    m = xs[0].shape[0]
    ks = tuple(x.shape[1] for x in xs)
    n = w.shape[1]
    segs = []
    s = 0
    for wd in widths:
        segs.append((s, s + wd))
        s += wd
    assert s == n and m % tm == 0
    in_specs = [pl.BlockSpec((tm, k), lambda i: (i, 0)) for k in ks]
    args = list(xs)
    if gain is not None:
        in_specs.append(_resident((1, ks[0])))
        args.append(gain.reshape(1, -1))
    in_specs.append(_resident((sum(ks), n)))
    args.append(w)
    if res is not None:
        in_specs.append(pl.BlockSpec((tm, n), lambda i: (i, 0)))
        args.append(res)
    return pl.pallas_call(
        functools.partial(_linear_body, ks=ks, segs=tuple(segs), has_norm=gain is not None, has_res=res is not None),
        grid=(m // tm,),
        in_specs=in_specs,
        out_specs=[pl.BlockSpec((tm, wd), lambda i: (i, 0)) for wd in widths],
        out_shape=[jax.ShapeDtypeStruct((m, wd), F32) for wd in widths],
        compiler_params=_cparams("parallel"),
    )(*args)


def _ffn_body(*refs, final):
    if final:
        x_ref, g_ref, wg_ref, wu_ref, wd_ref, gf_ref, o_ref = refs
    else:
        x_ref, g_ref, wg_ref, wu_ref, wd_ref, o_ref = refs
    x = x_ref[...]
    hb = _rms(x, g_ref[...]).astype(BF16)
    acc = x
    for c in range(0, D_FF, FF_CHUNK):
        gate = _dot(hb, wg_ref[:, c:c + FF_CHUNK])
        up = _dot(hb, wu_ref[:, c:c + FF_CHUNK])
        act = (gate * _sigmoid(gate) * up).astype(BF16)
        acc = acc + _dot(act, wd_ref[c:c + FF_CHUNK, :])
    if final:
        acc = _rms(acc, gf_ref[...])
    o_ref[...] = acc


def _ffn(x, gain, wg, wu, wd, final_gain=None, tm=256):
    m = x.shape[0]
    final = final_gain is not None
    in_specs = [pl.BlockSpec((tm, D_MODEL), lambda i: (i, 0)), _resident((1, D_MODEL)),
                _resident((D_MODEL, D_FF)), _resident((D_MODEL, D_FF)), _resident((D_FF, D_MODEL))]
    args = [x, gain.reshape(1, -1), wg, wu, wd]
    if final:
        in_specs.append(_resident((1, D_MODEL)))
        args.append(final_gain.reshape(1, -1))
    return pl.pallas_call(
        functools.partial(_ffn_body, final=final),
        grid=(m // tm,),
        in_specs=in_specs,
        out_specs=pl.BlockSpec((tm, D_MODEL), lambda i: (i, 0)),
        out_shape=jax.ShapeDtypeStruct((m, D_MODEL), F32),
        compiler_params=_cparams("parallel"),
    )(*args)


def _cross_body(x_ref, g_ref, wq_ref, mk_ref, mv_ref, wo_ref, o_ref):
    x = x_ref[0]
    hb = _rms(x, g_ref[...]).astype(BF16)
    q = _dot(hb, wq_ref[...]) * MEM_HEAD_DIM ** -0.5
    acc = x
    for h in range(MEM_HEADS):
        sl = slice(h * MEM_HEAD_DIM, (h + 1) * MEM_HEAD_DIM)
        s = _dot_nt(q[:, sl].astype(BF16), mk_ref[0, :, sl].astype(BF16))
        p = jnp.exp(s - jnp.max(s, axis=-1, keepdims=True))
        p = p / jnp.sum(p, axis=-1, keepdims=True)
        oh = _dot(p.astype(BF16), mv_ref[0, :, sl].astype(BF16))
        acc = acc + _dot(oh.astype(BF16), wo_ref[sl, :])
    o_ref[0] = acc


def _cross(x, gain, wq, mk, mv, layer, wo, tq):
    b, t, _ = x.shape
    nm = mk.shape[2]
    return pl.pallas_call(
        _cross_body,
        grid=(b, t // tq),
        in_specs=[pl.BlockSpec((1, tq, D_MODEL), lambda i, j: (i, j, 0)), _resident((1, D_MODEL)),
                  _resident((D_MODEL, D_MODEL)),
                  pl.BlockSpec((None, 1, nm, D_MODEL), lambda i, j: (layer, i, 0, 0)),
                  pl.BlockSpec((None, 1, nm, D_MODEL), lambda i, j: (layer, i, 0, 0)),
                  _resident((D_MODEL, D_MODEL))],
        out_specs=pl.BlockSpec((1, tq, D_MODEL), lambda i, j: (i, j, 0)),
        out_shape=jax.ShapeDtypeStruct((b, t, D_MODEL), F32),
        compiler_params=_cparams("parallel", "parallel"),
    )(x, gain.reshape(1, -1), wq, mk, mv, wo)


def _diff_body(*refs, tq, tk, past, lam_init):
    if past:
        lam_ref, q_ref, kn_ref, vn_ref, kp_ref, vp_ref, bn_ref, o_ref, q8_sc, m_sc, l_sc, acc_sc = refs
    else:
        lam_ref, q_ref, kn_ref, vn_ref, bn_ref, o_ref, q8_sc, m_sc, l_sc, acc_sc = refs
    i = pl.program_id(1)
    ng = 2 * N_HEADS
    q = q_ref[0] * DIFF_HALF ** -0.5
    lane = _iota((tq, GROUP_W), 1)
    for g in range(ng):
        lo = (g // 2) * HEAD_DIM + (g % 2) * DIFF_HALF
        q8_sc[g * tq:(g + 1) * tq, :] = jnp.where((lane >= lo) & (lane < lo + DIFF_HALF), q, 0.0).astype(BF16)
    m_sc[...] = jnp.full(m_sc.shape, -jnp.inf, F32)
    l_sc[...] = jnp.zeros(l_sc.shape, F32)
    acc_sc[...] = jnp.zeros(acc_sc.shape, F32)
    pq = past + i * tq + _iota((tq, 1), 0)
    head = _iota((ng, 1, 1), 0) >> 1
    slope = jnp.where(head == 0, 2.0 ** -2, jnp.where(head == 1, 2.0 ** -4, jnp.where(head == 2, 2.0 ** -6, 2.0 ** -8)))

    def block(kf, vf, k0, masked):
        n = kf.shape[0]
        pk = k0 + _iota((1, n), 1)
        rel = jnp.abs(pq - pk).astype(F32)
        s = _dot_nt(q8_sc[...], kf.astype(BF16)).reshape(ng, tq, n) - slope * rel[None]
        if masked:
            s = jnp.where(((pk >> 6) <= (pq >> 6))[None], s, NEG)
        m_prev = m_sc[...]
        m_new = jnp.maximum(m_prev, jnp.max(s, axis=-1, keepdims=True))
        p = jnp.exp(s - m_new)
        alpha = jnp.exp(m_prev - m_new)
        l_sc[...] = alpha * l_sc[...] + jnp.sum(p, axis=-1, keepdims=True)
        m_sc[...] = m_new
        pv = _dot(p.reshape(ng * tq, n).astype(BF16), vf.astype(BF16)).reshape(ng, tq, GROUP_W)
        acc_sc[...] = alpha * acc_sc[...] + pv

    if past:
        def past_step(j, carry):
            k0 = pl.multiple_of(j * tk, tk)
            block(kp_ref[0, pl.ds(k0, tk), :], vp_ref[0, pl.ds(k0, tk), :], k0, False)
            return carry
        lax.fori_loop(0, past // tk, past_step, 0)
        block(kn_ref[0], vn_ref[0], past, False)
    else:
        def prev_step(j, carry):
            k0 = pl.multiple_of(j * tk, tk)
            block(kn_ref[0, pl.ds(k0, tk), :], vn_ref[0, pl.ds(k0, tk), :], k0, False)
            return carry
        lax.fori_loop(0, i, prev_step, 0)
        k0 = pl.multiple_of(i * tq, tq)
        block(kn_ref[0, pl.ds(k0, tq), :], vn_ref[0, pl.ds(k0, tq), :], k0, True)

    o8 = acc_sc[...] / l_sc[...]
    lam = lam_ref[0]
    lane3 = _iota((1, 1, GROUP_W), 2) >> 6
    o = jnp.zeros((tq, GROUP_W), F32)
    for h in range(N_HEADS):
        oh = o8[2 * h] - lam * o8[2 * h + 1]
        o = jnp.where(lane3[0] == h, oh, o)
    ms = _hdot(o * o, _head_block_ones(GROUP_W)) * (1.0 / HEAD_DIM)
    o_ref[0] = o * lax.rsqrt(ms + EPS) * bn_ref[...] * (1.0 - lam_init)


def _diff_attn(lam, q, kn, vn, kp, vp, layer, bnorm, lam_init):
    b, t, _ = q.shape
    past = 0 if kp is None else kp.shape[2]
    if past:
        tq, tk = t, 512
        assert t == CHUNK and past % tk == 0
    else:
        tq = tk = min(256, t)
        assert t % tq == 0 and tq % CHUNK == 0
    seq_spec = lambda n: pl.BlockSpec((1, n, GROUP_W), lambda i, j: (i, 0, 0))
    past_spec = pl.BlockSpec((None, 1, past, GROUP_W), lambda i, j: (layer, i, 0, 0))
    in_specs = [pl.BlockSpec(memory_space=pltpu.SMEM),
                pl.BlockSpec((1, tq, GROUP_W), lambda i, j: (i, j, 0)), seq_spec(t), seq_spec(t)]
    args = [lam, q, kn, vn]
    if past:
        in_specs += [past_spec, past_spec]
        args += [kp, vp]
    in_specs.append(_resident((1, GROUP_W)))
    args.append(bnorm)
    ng = 2 * N_HEADS
    return pl.pallas_call(
        functools.partial(_diff_body, tq=tq, tk=tk, past=past, lam_init=lam_init),
        grid=(b, t // tq),
        in_specs=in_specs,
        out_specs=pl.BlockSpec((1, tq, GROUP_W), lambda i, j: (i, j, 0)),
        out_shape=jax.ShapeDtypeStruct((b, t, GROUP_W), F32),
        scratch_shapes=[pltpu.VMEM((ng * tq, GROUP_W), BF16), pltpu.VMEM((ng, tq, 1), F32),
                        pltpu.VMEM((ng, tq, 1), F32), pltpu.VMEM((ng, tq, GROUP_W), F32)],
        compiler_params=_cparams("parallel", "parallel"),
    )(*args)


def _stick_body(*refs, tq, tk, past):
    if past:
        q_ref, kn_ref, vn_ref, kp_ref, vp_ref, o_ref, q4_sc, c_sc, acc_sc = refs
    else:
        q_ref, kn_ref, vn_ref, o_ref, q4_sc, c_sc, acc_sc = refs
    i = pl.program_id(1)
    q = q_ref[0] * HEAD_DIM ** -0.5
    lane = _iota((tq, GROUP_W), 1) >> 6
    for h in range(N_HEADS):
        q4_sc[h * tq:(h + 1) * tq, :] = jnp.where(lane == h, q, 0.0).astype(BF16)
    c_sc[...] = jnp.zeros(c_sc.shape, F32)
    acc_sc[...] = jnp.zeros(acc_sc.shape, F32)
    pq = past + i * tq + _iota((tq, 1), 0)

    def block(kf, vf, k0, masked):
        n = kf.shape[0]
        z = _dot_nt(q4_sc[...], kf.astype(BF16)).reshape(N_HEADS, tq, n)
        ls = _log_sigmoid(z)
        lk = ls - z
        if masked:
            mask = ((k0 + _iota((1, n), 1)) < pq)[None]
            lk = jnp.where(mask, lk, 0.0)
        later = (_iota((n, n), 0) > _iota((n, n), 1)).astype(BF16)
        lk2 = lk.reshape(N_HEADS * tq, n)
        hi = lk2.astype(BF16)
        lo = (lk2 - hi.astype(F32)).astype(BF16)
        after = (_dot(hi, later) + _dot(lo, later)).reshape(N_HEADS, tq, n)
        carry = c_sc[...]
        e = ls + after + carry
        if masked:
            a = jnp.where(mask, jnp.exp(jnp.where(mask, e, 0.0)), 0.0)
        else:
            a = jnp.exp(e)
        c_sc[...] = carry + after[:, :, 0:1] + lk[:, :, 0:1]
        pv = _dot(a.reshape(N_HEADS * tq, n).astype(BF16), vf.astype(BF16))
        acc_sc[...] = acc_sc[...] + pv.reshape(N_HEADS, tq, GROUP_W)

    if past:
        block(kn_ref[0], vn_ref[0], past, True)
        nb = past // tk

        def past_step(j, carry):
            k0 = pl.multiple_of((nb - 1 - j) * tk, tk)
            block(kp_ref[0, pl.ds(k0, tk), :], vp_ref[0, pl.ds(k0, tk), :], k0, False)
            return carry
        lax.fori_loop(0, nb, past_step, 0)
    else:
        k0 = pl.multiple_of(i * tq, tq)
        block(kn_ref[0, pl.ds(k0, tq), :], vn_ref[0, pl.ds(k0, tq), :], k0, True)

        def prev_step(j, carry):
            k0 = pl.multiple_of((i - 1 - j) * tk, tk)
            block(kn_ref[0, pl.ds(k0, tk), :], vn_ref[0, pl.ds(k0, tk), :], k0, False)
            return carry
        lax.fori_loop(0, i, prev_step, 0)

    acc = acc_sc[...]
    o = jnp.zeros((tq, GROUP_W), F32)
    for h in range(N_HEADS):
        o = jnp.where(lane == h, acc[h], o)
    o_ref[0] = o


def _stick_attn(q, kn, vn, kp, vp, layer):
    b, t, _ = q.shape
    past = 0 if kp is None else kp.shape[2]
    if past:
        tq, tk = t, 256
        assert past % tk == 0
    else:
        tq = tk = min(256, t)
        assert t % tq == 0
    seq_spec = lambda n: pl.BlockSpec((1, n, GROUP_W), lambda i, j: (i, 0, 0))
    past_spec = pl.BlockSpec((None, 1, past, GROUP_W), lambda i, j: (layer, i, 0, 0))
    in_specs = [pl.BlockSpec((1, tq, GROUP_W), lambda i, j: (i, j, 0)), seq_spec(t), seq_spec(t)]
    args = [q, kn, vn]
    if past:
        in_specs += [past_spec, past_spec]
        args += [kp, vp]
    return pl.pallas_call(
        functools.partial(_stick_body, tq=tq, tk=tk, past=past),
        grid=(b, t // tq),
        in_specs=in_specs,
        out_specs=pl.BlockSpec((1, tq, GROUP_W), lambda i, j: (i, j, 0)),
        out_shape=jax.ShapeDtypeStruct((b, t, GROUP_W), F32),
        scratch_shapes=[pltpu.VMEM((N_HEADS * tq, GROUP_W), BF16), pltpu.VMEM((N_HEADS, tq, 1), F32),
                        pltpu.VMEM((N_HEADS, tq, GROUP_W), F32)],
        compiler_params=_cparams("parallel", "parallel"),
    )(*args)


def _pair_mask(n, s):
    r, c = _iota((n, n), 0), _iota((n, n), 1)
    return ((r // (2 * s)) == (c // (2 * s))) & (((r // s) & 1) == 1) & (((c // s) & 1) == 0)


def _gdn_body(x_ref, z_ref, abg_ref, cw_ref, cs_ref, s0_ref, alog_ref, dtb_ref, an_ref,
              o_ref, sout_ref, s_sc, xb_sc, *, nc):
    c = pl.program_id(1)
    L = CHUNK
    pad = 8

    @pl.when(c == 0)
    def _():
        s_sc[...] = s0_ref[0]
        xb_sc[pad - 3:pad, :] = cs_ref[0]

    x = x_ref[0]
    xb_sc[pad:pad + L, :] = x
    cw = cw_ref[...]
    u = (x * cw[3:4] + xb_sc[pad - 1:pad - 1 + L, :] * cw[2:3]
         + xb_sc[pad - 2:pad - 2 + L, :] * cw[1:2] + xb_sc[pad - 3:pad - 3 + L, :] * cw[0:1])
    xb_sc[pad - 3:pad, :] = x[L - 3:L, :]
    u = u * _sigmoid(u)
    ones_bd = _head_block_ones(GROUP_W)
    q = u[:, :GROUP_W]
    k = u[:, GROUP_W:2 * GROUP_W]
    v = u[:, 2 * GROUP_W:]
    q = q * lax.rsqrt(_hdot(q * q, ones_bd) + EPS) * HEAD_DIM ** -0.5
    k = k * lax.rsqrt(_hdot(k * k, ones_bd) + EPS)

    abg = abg_ref[0]
    beta_n = _sigmoid(abg)
    g_n = -jnp.exp(alog_ref[...]) * _softplus(abg + dtb_ref[...])
    er, ec = _iota((ABG_W, GROUP_W), 0), _iota((ABG_W, GROUP_W), 1) >> 6
    beta = _hdot(beta_n, (er == ec).astype(F32))
    tri = (_iota((L, L), 0) >= _iota((L, L), 1)).astype(F32)
    gc_n = _hdot(tri, g_n)
    gc_t = gc_n.T
    gc = _hdot(gc_n, (er == ec + N_HEADS).astype(F32))
    exp_g = jnp.exp(gc)
    g_last = gc[L - 1:L, :]

    kb = k * beta
    s = s_sc[...]
    rhs = v * beta - _hdot(kb * exp_g, s)

    ri, ci = _iota((L, L), 0), _iota((L, L), 1)
    incl = ri >= ci
    strict = ri > ci
    eye = (ri == ci).astype(F32)
    lane_head = _iota((1, GROUP_W), 1) >> 6
    w = jnp.zeros((L, GROUP_W), F32)
    o_intra = jnp.zeros((L, GROUP_W), F32)
    for h in range(N_HEADS):
        hm = lane_head == h
        diff = gc_n[:, N_HEADS + h:N_HEADS + h + 1] - gc_t[N_HEADS + h:N_HEADS + h + 1, :]
        dec = jnp.where(incl, jnp.exp(jnp.where(incl, diff, 0.0)), 0.0)
        a = jnp.where(strict, _hdot_nt(jnp.where(hm, kb, 0.0), k) * dec, 0.0)
        t_inv = eye - jnp.where(_pair_mask(L, 1), a, 0.0)
        for sz in (2, 4, 8, 16, 32):
            t_inv = t_inv - _hdot(t_inv, _hdot(jnp.where(_pair_mask(L, sz), a, 0.0), t_inv))
        w_h = _hdot(t_inv, rhs)
        qk = _hdot_nt(jnp.where(hm, q, 0.0), k) * dec
        w = jnp.where(hm, w_h, w)
        o_intra = jnp.where(hm, _hdot(qk, w_h), o_intra)

    o = _hdot(q * exp_g, s) + o_intra
    k_dec = k * jnp.exp(g_last - gc)
    s_new = s * jnp.exp(g_last) + _hdot_tn(k_dec, w) * ones_bd
    s_sc[...] = s_new

    ms = _hdot(o * o, ones_bd) * (1.0 / HEAD_DIM)
    zg = z_ref[0]
    o_ref[0] = o * lax.rsqrt(ms + EPS) * an_ref[...] * (zg * _sigmoid(zg))

    @pl.when(c == nc - 1)
    def _():
        sout_ref[0] = s_new


def _gdn(aqkv, az, abg, conv_w, conv_state, s0, alog_row, dtb_row, anorm_row):
    b, t, _ = aqkv.shape
    nc = t // CHUNK
    cmap = lambda i, j: (i, j, 0)
    bmap = lambda i, j: (i, 0, 0)
    return pl.pallas_call(
        functools.partial(_gdn_body, nc=nc),
        grid=(b, nc),
        in_specs=[pl.BlockSpec((1, CHUNK, 3 * GROUP_W), cmap), pl.BlockSpec((1, CHUNK, GROUP_W), cmap),
                  pl.BlockSpec((1, CHUNK, ABG_W), cmap), _resident((CONV_W, 3 * GROUP_W)),
                  pl.BlockSpec((1, CONV_W - 1, 3 * GROUP_W), bmap), pl.BlockSpec((1, GROUP_W, GROUP_W), bmap),
                  _resident((1, ABG_W)), _resident((1, ABG_W)), _resident((1, GROUP_W))],
        out_specs=[pl.BlockSpec((1, CHUNK, GROUP_W), cmap), pl.BlockSpec((1, GROUP_W, GROUP_W), bmap)],
        out_shape=[jax.ShapeDtypeStruct((b, t, GROUP_W), F32), jax.ShapeDtypeStruct((b, GROUP_W, GROUP_W), F32)],
        scratch_shapes=[pltpu.VMEM((GROUP_W, GROUP_W), F32), pltpu.VMEM((8 + CHUNK, 3 * GROUP_W), F32)],
        compiler_params=_cparams("parallel", "arbitrary"),
    )(aqkv, az, abg, conv_w, conv_state, s0, alog_row, dtb_row, anorm_row)


def _hgrn_body(q_ref, f_ref, i_ref, g_ref, lb_ref, dn_ref, s0_ref, o_ref, sout_ref, s_sc, *, nc):
    c = pl.program_id(1)
    L = CHUNK

    @pl.when(c == 0)
    def _():
        s_sc[...] = s0_ref[0]

    lb = lb_ref[...]
    fl = f_ref[0]
    log_f = _log_sigmoid(fl) + jnp.log(1.0 + lb * jnp.exp(-fl))
    k = (1.0 - lb) * _sigmoid(-fl)
    q = q_ref[0] * HEAD_DIM ** -0.5
    v = i_ref[0]
    ri, ci = _iota((L, L), 0), _iota((L, L), 1)
    bc = _hdot((ri >= ci).astype(F32), log_f)
    lane_head = _iota((1, GROUP_W), 1) >> 6
    rows = _iota((L, 1), 0)

    a = [jnp.where(ri == ci, _hdot_nt(jnp.where(lane_head == h, q, 0.0), k), 0.0) for h in range(N_HEADS)]
    for sz in (1, 2, 4, 8, 16, 32):
        ref_row = (ri // (2 * sz)) * (2 * sz) + sz
        b_ref = _hdot((ci == ref_row).astype(F32), bc)
        later = ((rows // sz) & 1) == 1
        x = jnp.where(later, q * jnp.exp(jnp.where(later, bc - b_ref, 0.0)), 0.0)
        y = jnp.where(later, 0.0, k * jnp.exp(jnp.where(later, 0.0, b_ref - bc)))
        pm = _pair_mask(L, sz)
        for h in range(N_HEADS):
            a[h] = a[h] + jnp.where(pm, _hdot_nt(jnp.where(lane_head == h, x, 0.0), y), 0.0)

    st = s_sc[...]
    o = _hdot_nt(q * jnp.exp(bc), st)
    for h in range(N_HEADS):
        o = o + jnp.where(lane_head == h, _hdot(a[h], v), 0.0)
    b_last = bc[L - 1:L, :]
    st_new = st * jnp.exp(b_last) + _hdot_tn(v, k * jnp.exp(b_last - bc)) * _head_block_ones(GROUP_W)
    s_sc[...] = st_new

    ms = _hdot(o * o, _head_block_ones(GROUP_W)) * (1.0 / HEAD_DIM)
    zg = g_ref[0]
    o_ref[0] = o * lax.rsqrt(ms + EPS) * dn_ref[...] * (zg * _sigmoid(zg))

    @pl.when(c == nc - 1)
    def _():
        sout_ref[0] = st_new


def _hgrn(dq, df, di, dg, lb_row, dnorm_row, s0t):
    b, t, _ = dq.shape
    nc = t // CHUNK
    cmap = lambda i, j: (i, j, 0)
    bmap = lambda i, j: (i, 0, 0)
    cspec = pl.BlockSpec((1, CHUNK, GROUP_W), cmap)
    return pl.pallas_call(
        functools.partial(_hgrn_body, nc=nc),
        grid=(b, nc),
        in_specs=[cspec, cspec, cspec, cspec, _resident((1, GROUP_W)), _resident((1, GROUP_W)),
                  pl.BlockSpec((1, GROUP_W, GROUP_W), bmap)],
        out_specs=[cspec, pl.BlockSpec((1, GROUP_W, GROUP_W), bmap)],
        out_shape=[jax.ShapeDtypeStruct((b, t, GROUP_W), F32), jax.ShapeDtypeStruct((b, GROUP_W, GROUP_W), F32)],
        scratch_shapes=[pltpu.VMEM((GROUP_W, GROUP_W), F32)],
        compiler_params=_cparams("parallel", "arbitrary"),
    )(dq, df, di, dg, lb_row, dnorm_row, s0t)


def _to_block_diag(s, transpose):
    if transpose:
        s = jnp.swapaxes(s, -1, -2)
    b = s.shape[0]
    eye = jnp.eye(N_HEADS, dtype=s.dtype)
    return (s[:, :, :, None, :] * eye[None, :, None, :, None]).reshape(b, GROUP_W, GROUP_W)


def _from_block_diag(s, transpose):
    b = s.shape[0]
    s5 = s.reshape(b, N_HEADS, HEAD_DIM, N_HEADS, HEAD_DIM)
    out = jnp.stack([s5[:, h, :, h, :] for h in range(N_HEADS)], axis=1)
    return jnp.swapaxes(out, -1, -2) if transpose else out


IN_WIDTHS = (3 * GROUP_W,) + (GROUP_W,) * 11 + (ABG_W,)


def _layer(x, l, st, mk, mv, mem_layer, prm, final_gain):
    b, t, _ = x.shape
    m = b * t
    tm = 256
    outs = _fused_linear([x.reshape(m, D_MODEL)], prm["w_in"][l], IN_WIDTHS, gain=prm["norm_mix"][l], tm=tm)
    (a_qkv, a_z, b_q, b_k, b_v, c_q, c_k, c_v, d_q, d_f, d_i, d_g, a_bg) = [o.reshape(b, t, -1) for o in outs]

    o_a, a_s = _gdn(a_qkv, a_z, a_bg, prm["a_conv_w"][l], st["a_conv"], _to_block_diag(st["a_S"], False),
                    prm["alog_row"][l], prm["dtb_row"][l], prm["a_norm"][l])
    o_b = _diff_attn(prm["lam"][l], b_q, b_k, b_v, st["b_k"], st["b_v"], l, prm["b_norm"][l],
                     0.8 - 0.6 * math.exp(-0.3 * l))
    o_c = _stick_attn(c_q, c_k, c_v, st["c_k"], st["c_v"], l)
    o_d, d_s = _hgrn(d_q, d_f, d_i, d_g, prm["lb"][l], prm["d_norm"][l], _to_block_diag(st["d_S"], True))

    x2 = _fused_linear([o.reshape(m, GROUP_W) for o in (o_a, o_b, o_c, o_d)], prm["w_out"][l], (D_MODEL,),
                       res=x.reshape(m, D_MODEL), tm=tm)[0]
    x2 = _cross(x2.reshape(b, t, D_MODEL), prm["norm_cross"][l], prm["w_cq"][l], mk, mv, mem_layer, prm["w_co"][l],
                tq=min(256, t))
    x2 = _ffn(x2.reshape(m, D_MODEL), prm["norm_ffn"][l], prm["w_gate"][l], prm["w_up"][l], prm["w_down"][l],
              final_gain=final_gain, tm=tm)

    if t >= CONV_W - 1:
        conv_new = a_qkv[:, t - (CONV_W - 1):, :]
    else:
        conv_new = jnp.concatenate([st["a_conv"], a_qkv], axis=1)[:, -(CONV_W - 1):, :]
    heads = lambda a: a.reshape(b, t, N_HEADS, HEAD_DIM)
    new = (conv_new, _from_block_diag(a_s, False), heads(b_k), heads(b_v), heads(c_k), heads(c_v),
           _from_block_diag(d_s, True))
    return x2.reshape(b, t, D_MODEL), new


def kernel(x_prompt, x_sample, mem_prompt, state_a_conv, state_a_S, cache_b_k, cache_b_v, cache_c_k, cache_c_v,
           state_d_S, cache_mem_k, cache_mem_v, norm_mix, w_in, a_conv_w, a_A_log, a_dt_bias, a_norm, b_lam_q1,
           b_lam_k1, b_lam_q2, b_lam_k2, b_norm, d_lb, d_norm, w_out, norm_cross, norm_memtok, w_cq, w_ck, w_cv,
           w_co, norm_ffn, w_gate, w_up, w_down, norm_final):
    depth = w_in.shape[0]
    n_main = 4 * GROUP_W
    w_in_p = jnp.concatenate(
        [w_in[:, :, :n_main], w_in[:, :, n_main + 2 * N_HEADS:], w_in[:, :, n_main:n_main + 2 * N_HEADS],
         jnp.zeros((depth, D_MODEL, ABG_W - 2 * N_HEADS), w_in.dtype)], axis=2).astype(BF16)
    pad_row = lambda a: jnp.pad(a.astype(F32), ((0, 0), (N_HEADS, ABG_W - 2 * N_HEADS)))[:, None, :]
    tile_row = lambda a: jnp.tile(a.astype(F32), (1, N_HEADS))[:, None, :]
    p_lb = jax.nn.softmax(d_lb.astype(F32), axis=0)
    lam = (jnp.exp(jnp.sum(b_lam_q1.astype(F32) * b_lam_k1.astype(F32), axis=-1))
           - jnp.exp(jnp.sum(b_lam_q2.astype(F32) * b_lam_k2.astype(F32), axis=-1))
           + jnp.array([0.8 - 0.6 * math.exp(-0.3 * l) for l in range(depth)], F32))
    prm = {
        "w_in": w_in_p, "norm_mix": norm_mix, "a_conv_w": a_conv_w.astype(F32),
        "alog_row": pad_row(a_A_log), "dtb_row": pad_row(a_dt_bias), "a_norm": tile_row(a_norm),
        "lam": lam[:, None], "b_norm": tile_row(b_norm), "lb": (jnp.cumsum(p_lb, axis=0) - p_lb[0])[:, None, :],
        "d_norm": tile_row(d_norm), "w_out": w_out.astype(BF16), "norm_cross": norm_cross,
        "w_cq": w_cq.astype(BF16), "w_co": w_co.astype(BF16), "norm_ffn": norm_ffn,
        "w_gate": w_gate.astype(BF16), "w_up": w_up.astype(BF16), "w_down": w_down.astype(BF16),
    }
    w_ckv = jnp.concatenate([w_ck, w_cv], axis=2).astype(BF16)

    bp, tp, _ = x_prompt.shape
    n_mem = mem_prompt.shape[1]
    h = x_prompt
    p_new = []
    for l in range(depth):
        mk, mv = _fused_linear([mem_prompt.reshape(bp * n_mem, D_MODEL)], w_ckv[l], (D_MODEL, D_MODEL),
                               gain=norm_memtok[l], tm=256)
        mk = mk.reshape(bp, n_mem, D_MODEL)
        mv = mv.reshape(bp, n_mem, D_MODEL)
        st = {"a_conv": jnp.zeros((bp, CONV_W - 1, 3 * GROUP_W), F32),
              "a_S": jnp.zeros((bp, N_HEADS, HEAD_DIM, HEAD_DIM), F32), "b_k": None, "b_v": None,
              "c_k": None, "c_v": None, "d_S": jnp.zeros((bp, N_HEADS, HEAD_DIM, HEAD_DIM), F32)}
        h, new = _layer(h, l, st, mk[None], mv[None], 0, prm, norm_final if l == depth - 1 else None)
        mem4 = lambda a: a.reshape(bp, n_mem, MEM_HEADS, MEM_HEAD_DIM)
        p_new.append(new + (mem4(mk), mem4(mv)))
    y_prompt = h
    p_stacked = [jnp.stack(c) for c in zip(*p_new)]

    bs, ts, _ = x_sample.shape
    past = cache_b_k.shape[2]
    h = x_sample
    s_new = []
    flat = lambda a: a.reshape(depth, bs, past, GROUP_W)
    caches = {"b_k": flat(cache_b_k), "b_v": flat(cache_b_v), "c_k": flat(cache_c_k), "c_v": flat(cache_c_v)}
    mem_k = cache_mem_k.reshape(depth, bs, -1, D_MODEL)
    mem_v = cache_mem_v.reshape(depth, bs, -1, D_MODEL)
    for l in range(depth):
        st = {"a_conv": state_a_conv[l], "a_S": state_a_S[l], "d_S": state_d_S[l], **caches}
        h, new = _layer(h, l, st, mem_k, mem_v, l, prm, norm_final if l == depth - 1 else None)
        s_new.append(new)
    y_sample = h
    s_stacked = [jnp.stack(c) for c in zip(*s_new)]

    return (y_prompt, y_sample, *p_stacked, *s_stacked)
```

```python
import functools
import math

import jax
import jax.numpy as jnp
from jax import lax
from jax.experimental import pallas as pl
from jax.experimental.pallas import tpu as pltpu

F32 = jnp.float32
BF16 = jnp.bfloat16

D_MODEL = 1024
GROUP_W = 256
N_HEADS = 4
HEAD_DIM = 64
DIFF_HALF = 32
CHUNK = 64
CONV_W = 4
MEM_HEADS = 4
MEM_HEAD_DIM = 256
D_FF = 2816
EPS = 1e-6
NEG = -1e30
LOG2E = 1.4426950408889634
LANES = 128
ABG_W = 128
FF_CHUNK = 256
SEQ_PER_STEP = 4
VMEM_LIMIT = 56 * 1024 * 1024


def _cparams(*sem):
    return pltpu.CompilerParams(dimension_semantics=sem, vmem_limit_bytes=VMEM_LIMIT)


def _dot(a, b):
    return jnp.dot(a, b, preferred_element_type=F32)


def _dot_nt(a, b):
    return lax.dot_general(a, b, (((1,), (1,)), ((), ())), preferred_element_type=F32)


_NN = (((1,), (0,)), ((), ()))
_NT = (((1,), (1,)), ((), ()))
_BNN = (((2,), (1,)), ((0,), (0,)))


def _split(x):
    hi = x.astype(BF16)
    return hi, (x - hi.astype(F32)).astype(BF16)


def _bdot(a, b, dims):
    return lax.dot_general(a, b, dims, preferred_element_type=F32)


def _mm(a, b, dims=_NN):
    ah, al = a if isinstance(a, tuple) else _split(a)
    bh, bl = b if isinstance(b, tuple) else _split(b)
    return _bdot(ah, bh, dims) + (_bdot(ah, bl, dims) + _bdot(al, bh, dims))


def _mm1(a, b, dims=_NN):
    return _bdot(a.astype(BF16), b.astype(BF16), dims)


def _mm_r(a, b16):
    ah, al = _split(a)
    return _bdot(ah, b16, _NN) + _bdot(al, b16, _NN)


def _mm_l(a16, b):
    bh, bl = _split(b)
    return _bdot(a16, bh, _NN) + _bdot(a16, bl, _NN)


def _rms(x, g):
    return x * lax.rsqrt(jnp.mean(x * x, axis=-1, keepdims=True) + EPS) * g


def _sigmoid(x):
    return 1.0 / (1.0 + jnp.exp(-x))


def _log_sigmoid(x):
    return jnp.minimum(x, 0.0) - jnp.log(1.0 + jnp.exp(-jnp.abs(x)))


def _softplus(x):
    return jnp.maximum(x, 0.0) + jnp.log(1.0 + jnp.exp(-jnp.abs(x)))


def _iota(shape, dim):
    return lax.broadcasted_iota(jnp.int32, shape, dim)


def _head_block_ones(n):
    return ((_iota((n, n), 0) >> 6) == (_iota((n, n), 1) >> 6)).astype(F32)


def _resident(shape):
    return pl.BlockSpec(shape, lambda *_: (0,) * len(shape), pipeline_mode=pl.Buffered(1))


def _linear_body(*refs, ks, segs, has_norm, has_res):
    n_in = len(ks)
    x_refs = refs[:n_in]
    pos = n_in
    g_ref = None
    if has_norm:
        g_ref = refs[pos]
        pos += 1
    w_ref = refs[pos]
    pos += 1
    r_ref = None
    if has_res:
        r_ref = refs[pos]
        pos += 1
    out_refs = refs[pos:]
    xb = []
    for x_ref in x_refs:
        x = x_ref[...].astype(F32)
        if has_norm:
            x = _rms(x, g_ref[...])
        xb.append(x.astype(BF16))
    for o_ref, (s, e) in zip(out_refs, segs):
        acc = None
        k0 = 0
        for xk, k in zip(xb, ks):
            t = _dot(xk, w_ref[k0:k0 + k, s:e])
            acc = t if acc is None else acc + t
            k0 += k
        if has_res:
            acc = acc + r_ref[:, s:e]
        o_ref[...] = acc.astype(o_ref.dtype)


def _fused_linear(xs, w, widths, gain=None, res=None, tm=256):
    m = xs[0].shape[0]
    ks = tuple(x.shape[1] for x in xs)
    n = w.shape[1]
    segs = []
    s = 0
    for wd in widths:
        segs.append((s, s + wd))
        s += wd
    assert s == n and m % tm == 0
    in_specs = [pl.BlockSpec((tm, k), lambda i: (i, 0)) for k in ks]
    args = list(xs)
    if gain is not None:
        in_specs.append(_resident((1, ks[0])))
        args.append(gain.reshape(1, -1))
    in_specs.append(_resident((sum(ks), n)))
    args.append(w)
    if res is not None:
        in_specs.append(pl.BlockSpec((tm, n), lambda i: (i, 0)))
        args.append(res)
    return pl.pallas_call(
        functools.partial(_linear_body, ks=ks, segs=tuple(segs), has_norm=gain is not None, has_res=res is not None),
        grid=(m // tm,),
        in_specs=in_specs,
        out_specs=[pl.BlockSpec((tm, wd), lambda i: (i, 0)) for wd in widths],
        out_shape=[jax.ShapeDtypeStruct((m, wd), F32) for wd in widths],
        compiler_params=_cparams("parallel"),
    )(*args)


def _ffn_body(*refs, final):
    if final:
        x_ref, g_ref, wg_ref, wu_ref, wd_ref, gf_ref, o_ref = refs
    else:
        x_ref, g_ref, wg_ref, wu_ref, wd_ref, o_ref = refs
    x = x_ref[...]
    hb = _rms(x, g_ref[...]).astype(BF16)
    acc = x
    for c in range(0, D_FF, FF_CHUNK):
        gate = _dot(hb, wg_ref[:, c:c + FF_CHUNK])
        up = _dot(hb, wu_ref[:, c:c + FF_CHUNK])
        act = (gate * _sigmoid(gate) * up).astype(BF16)
        acc = acc + _dot(act, wd_ref[c:c + FF_CHUNK, :])
    if final:
        acc = _rms(acc, gf_ref[...])
    o_ref[...] = acc


def _ffn(x, gain, wg, wu, wd, final_gain=None, tm=256):
    m = x.shape[0]
    final = final_gain is not None
    in_specs = [pl.BlockSpec((tm, D_MODEL), lambda i: (i, 0)), _resident((1, D_MODEL)),
                _resident((D_MODEL, D_FF)), _resident((D_MODEL, D_FF)), _resident((D_FF, D_MODEL))]
    args = [x, gain.reshape(1, -1), wg, wu, wd]
    if final:
        in_specs.append(_resident((1, D_MODEL)))
        args.append(final_gain.reshape(1, -1))
    return pl.pallas_call(
        functools.partial(_ffn_body, final=final),
        grid=(m // tm,),
        in_specs=in_specs,
        out_specs=pl.BlockSpec((tm, D_MODEL), lambda i: (i, 0)),
        out_shape=jax.ShapeDtypeStruct((m, D_MODEL), F32),
        compiler_params=_cparams("parallel"),
    )(*args)


def _cross_body(x_ref, g_ref, wq_ref, mk_ref, mv_ref, wo_ref, o_ref):
    x = x_ref[0]
    hb = _rms(x, g_ref[...]).astype(BF16)
    q = _dot(hb, wq_ref[...]) * MEM_HEAD_DIM ** -0.5
    acc = x
    for h in range(MEM_HEADS):
        sl = slice(h * MEM_HEAD_DIM, (h + 1) * MEM_HEAD_DIM)
        s = _dot_nt(q[:, sl].astype(BF16), mk_ref[0, :, sl].astype(BF16))
        p = jnp.exp(s - jnp.max(s, axis=-1, keepdims=True))
        p = p / jnp.sum(p, axis=-1, keepdims=True)
        oh = _dot(p.astype(BF16), mv_ref[0, :, sl].astype(BF16))
        acc = acc + _dot(oh.astype(BF16), wo_ref[sl, :])
    o_ref[0] = acc


def _cross(x, gain, wq, mk, mv, layer, wo, tq):
    b, t, _ = x.shape
    nm = mk.shape[2]
    return pl.pallas_call(
        _cross_body,
        grid=(b, t // tq),
        in_specs=[pl.BlockSpec((1, tq, D_MODEL), lambda i, j: (i, j, 0)), _resident((1, D_MODEL)),
                  _resident((D_MODEL, D_MODEL)),
                  pl.BlockSpec((None, 1, nm, D_MODEL), lambda i, j: (layer, i, 0, 0)),
                  pl.BlockSpec((None, 1, nm, D_MODEL), lambda i, j: (layer, i, 0, 0)),
                  _resident((D_MODEL, D_MODEL))],
        out_specs=pl.BlockSpec((1, tq, D_MODEL), lambda i, j: (i, j, 0)),
        out_shape=jax.ShapeDtypeStruct((b, t, D_MODEL), F32),
        compiler_params=_cparams("parallel", "parallel"),
    )(x, gain.reshape(1, -1), wq, mk, mv, wo)


def _diff_body(*refs, tq, tk, past, lam_init):
    if past:
        lam_ref, q_ref, kn_ref, vn_ref, kp_ref, vp_ref, bn_ref, o_ref, q8_sc, m_sc, l_sc, acc_sc = refs
    else:
        lam_ref, q_ref, kn_ref, vn_ref, bn_ref, o_ref, q8_sc, m_sc, l_sc, acc_sc = refs
    i = pl.program_id(1)
    ng = 2 * N_HEADS
    q = q_ref[0] * (DIFF_HALF ** -0.5 * LOG2E)
    lane = _iota((tq, GROUP_W), 1)
    for g in range(ng):
        lo = (g // 2) * HEAD_DIM + (g % 2) * DIFF_HALF
        q8_sc[g * tq:(g + 1) * tq, :] = jnp.where((lane >= lo) & (lane < lo + DIFF_HALF), q, 0.0).astype(BF16)
    m_sc[...] = jnp.full(m_sc.shape, -jnp.inf, F32)
    l_sc[...] = jnp.zeros(l_sc.shape, F32)
    acc_sc[...] = jnp.zeros(acc_sc.shape, F32)
    pq = past + i * tq + _iota((tq, 1), 0)
    pq_f = pq.astype(F32)

    def lanes(x, n):
        return x[:, :n] if n <= LANES else pltpu.repeat(x, n // LANES, axis=1)

    def block(k_ref, v_ref, k0, n, near, pos0=0):
        kb = k_ref[0, pl.ds(k0, n), :].astype(BF16)
        vb = v_ref[0, pl.ds(k0, n), :].astype(BF16)
        pk = pos0 + k0 + _iota((1, n), 1)
        pk_f = pk.astype(F32)
        if near:
            allowed = (pk >> 6) <= (pq >> 6)
            shift = pq_f - jnp.abs(pq_f - pk_f)
        for g in range(ng):
            slope = 2.0 ** (-2 * (g // 2 + 1)) * LOG2E
            s = _dot_nt(q8_sc[g * tq:(g + 1) * tq, :], kb)
            if near:
                s = jnp.where(allowed, s + slope * shift, NEG)
            else:
                s = s + slope * pk_f
            m_prev = m_sc[g]
            m_new = jnp.maximum(m_prev, jnp.max(s, axis=-1, keepdims=True))
            p = jnp.exp2(s - lanes(m_new, n))
            alpha = jnp.exp2(m_prev - m_new)
            l_sc[g] = alpha * l_sc[g] + jnp.sum(p, axis=-1, keepdims=True)
            m_sc[g] = m_new
            acc_sc[g] = acc_sc[g] * lanes(alpha, GROUP_W) + _dot(p.astype(BF16), vb)

    if past:
        def past_step(j, carry):
            block(kp_ref, vp_ref, pl.multiple_of(j * tk, tk), tk, False)
            return carry
        lax.fori_loop(0, past // tk, past_step, 0)
        block(kn_ref, vn_ref, 0, tq, True, pos0=past)
    else:
        def prev_step(j, carry):
            block(kn_ref, vn_ref, pl.multiple_of(j * tk, tk), tk, False)
            return carry
        lax.fori_loop(0, (i * tq) // tk, prev_step, 0)

        @pl.when((i * tq) % tk != 0)
        def _():
            block(kn_ref, vn_ref, pl.multiple_of((i - 1) * tq, tq), tq, False)
        block(kn_ref, vn_ref, pl.multiple_of(i * tq, tq), tq, True)

    lam = lam_ref[0]
    lane_head = _iota((1, GROUP_W), 1) >> 6
    o = jnp.zeros((tq, GROUP_W), F32)
    for h in range(N_HEADS):
        o0 = acc_sc[2 * h] / lanes(l_sc[2 * h], GROUP_W)
        o1 = acc_sc[2 * h + 1] / lanes(l_sc[2 * h + 1], GROUP_W)
        o = jnp.where(lane_head == h, o0 - lam * o1, o)
    ms = _mm_r(o * o, _head_block_ones(GROUP_W).astype(BF16)) * (1.0 / HEAD_DIM)
    o_ref[0] = o * lax.rsqrt(ms + EPS) * bn_ref[...] * (1.0 - lam_init)


def _diff_attn(lam, q, kn, vn, kp, vp, layer, bnorm, lam_init):
    b, t, _ = q.shape
    past = 0 if kp is None else kp.shape[2]
    if past:
        tq, tk = t, 512
        assert t == CHUNK and past % tk == 0
    else:
        tq = min(256, t)
        tk = 2 * tq if t % (2 * tq) == 0 else tq
        assert t % tq == 0 and tq % CHUNK == 0
    seq_spec = lambda n: pl.BlockSpec((1, n, GROUP_W), lambda i, j: (i, 0, 0))
    past_spec = pl.BlockSpec((None, 1, past, GROUP_W), lambda i, j: (layer, i, 0, 0))
    in_specs = [pl.BlockSpec(memory_space=pltpu.SMEM),
                pl.BlockSpec((1, tq, GROUP_W), lambda i, j: (i, j, 0)), seq_spec(t), seq_spec(t)]
    args = [lam, q, kn, vn]
    if past:
        in_specs += [past_spec, past_spec]
        args += [kp, vp]
    in_specs.append(_resident((1, GROUP_W)))
    args.append(bnorm)
    ng = 2 * N_HEADS
    return pl.pallas_call(
        functools.partial(_diff_body, tq=tq, tk=tk, past=past, lam_init=lam_init),
        grid=(b, t // tq),
        in_specs=in_specs,
        out_specs=pl.BlockSpec((1, tq, GROUP_W), lambda i, j: (i, j, 0)),
        out_shape=jax.ShapeDtypeStruct((b, t, GROUP_W), F32),
        scratch_shapes=[pltpu.VMEM((ng * tq, GROUP_W), BF16), pltpu.VMEM((ng, tq, LANES), F32),
                        pltpu.VMEM((ng, tq, LANES), F32), pltpu.VMEM((ng, tq, GROUP_W), F32)],
        compiler_params=_cparams("parallel", "parallel"),
    )(*args)


def _stick_body(*refs, tq, tk, past):
    if past:
        q_ref, kn_ref, vn_ref, kp_ref, vp_ref, o_ref, q4_sc, c_sc, acc_sc = refs
    else:
        q_ref, kn_ref, vn_ref, o_ref, q4_sc, c_sc, acc_sc = refs
    i = pl.program_id(1)
    q = q_ref[0] * HEAD_DIM ** -0.5
    lane = _iota((tq, GROUP_W), 1) >> 6
    for h in range(N_HEADS):
        q4_sc[h * tq:(h + 1) * tq, :] = jnp.where(lane == h, q, 0.0).astype(BF16)
    c_sc[...] = jnp.zeros(c_sc.shape, F32)
    acc_sc[...] = jnp.zeros(acc_sc.shape, F32)
    pq = past + i * tq + _iota((tq, 1), 0)

    def block(kf, vf, k0, masked):
        n = kf.shape[0]
        z = _dot_nt(q4_sc[...], kf.astype(BF16)).reshape(N_HEADS, tq, n)
        ls = _log_sigmoid(z)
        lk = ls - z
        if masked:
            mask = ((k0 + _iota((1, n), 1)) < pq)[None]
            lk = jnp.where(mask, lk, 0.0)
        later = (_iota((n, n), 0) > _iota((n, n), 1)).astype(BF16)
        lk2 = lk.reshape(N_HEADS * tq, n)
        hi = lk2.astype(BF16)
        lo = (lk2 - hi.astype(F32)).astype(BF16)
        after = (_dot(hi, later) + _dot(lo, later)).reshape(N_HEADS, tq, n)
        carry = c_sc[...]
        e = ls + after + carry
        if masked:
            a = jnp.where(mask, jnp.exp(jnp.where(mask, e, 0.0)), 0.0)
        else:
            a = jnp.exp(e)
        c_sc[...] = carry + after[:, :, 0:1] + lk[:, :, 0:1]
        pv = _dot(a.reshape(N_HEADS * tq, n).astype(BF16), vf.astype(BF16))
        acc_sc[...] = acc_sc[...] + pv.reshape(N_HEADS, tq, GROUP_W)

    if past:
        block(kn_ref[0], vn_ref[0], past, True)
        nb = past // tk

        def past_step(j, carry):
            k0 = pl.multiple_of((nb - 1 - j) * tk, tk)
            block(kp_ref[0, pl.ds(k0, tk), :], vp_ref[0, pl.ds(k0, tk), :], k0, False)
            return carry
        lax.fori_loop(0, nb, past_step, 0)
    else:
        k0 = pl.multiple_of(i * tq, tq)
        block(kn_ref[0, pl.ds(k0, tq), :], vn_ref[0, pl.ds(k0, tq), :], k0, True)

        def prev_step(j, carry):
            k0 = pl.multiple_of((i - 1 - j) * tk, tk)
            block(kn_ref[0, pl.ds(k0, tk), :], vn_ref[0, pl.ds(k0, tk), :], k0, False)
            return carry
        lax.fori_loop(0, i, prev_step, 0)

    acc = acc_sc[...]
    o = jnp.zeros((tq, GROUP_W), F32)
    for h in range(N_HEADS):
        o = jnp.where(lane == h, acc[h], o)
    o_ref[0] = o


def _stick_attn(q, kn, vn, kp, vp, layer):
    b, t, _ = q.shape
    past = 0 if kp is None else kp.shape[2]
    if past:
        tq, tk = t, 256
        assert past % tk == 0
    else:
        tq = tk = min(256, t)
        assert t % tq == 0
    seq_spec = lambda n: pl.BlockSpec((1, n, GROUP_W), lambda i, j: (i, 0, 0))
    past_spec = pl.BlockSpec((None, 1, past, GROUP_W), lambda i, j: (layer, i, 0, 0))
    in_specs = [pl.BlockSpec((1, tq, GROUP_W), lambda i, j: (i, j, 0)), seq_spec(t), seq_spec(t)]
    args = [q, kn, vn]
    if past:
        in_specs += [past_spec, past_spec]
        args += [kp, vp]
    return pl.pallas_call(
        functools.partial(_stick_body, tq=tq, tk=tk, past=past),
        grid=(b, t // tq),
        in_specs=in_specs,
        out_specs=pl.BlockSpec((1, tq, GROUP_W), lambda i, j: (i, j, 0)),
        out_shape=jax.ShapeDtypeStruct((b, t, GROUP_W), F32),
        scratch_shapes=[pltpu.VMEM((N_HEADS * tq, GROUP_W), BF16), pltpu.VMEM((N_HEADS, tq, 1), F32),
                        pltpu.VMEM((N_HEADS, tq, GROUP_W), F32)],
        compiler_params=_cparams("parallel", "parallel"),
    )(*args)


def _pair_mask(n, s):
    r, c = _iota((n, n), 0), _iota((n, n), 1)
    return ((r // (2 * s)) == (c // (2 * s))) & (((r // s) & 1) == 1) & (((c // s) & 1) == 0)


def _head_rows(x, lane_head):
    return jnp.concatenate([jnp.where(lane_head == h, x, 0.0) for h in range(N_HEADS)], axis=0)


def _head_diag(x, lane_head):
    n = x.shape[0] // N_HEADS
    out = x[:n]
    for h in range(1, N_HEADS):
        out = jnp.where(lane_head == h, x[h * n:(h + 1) * n], out)
    return out


def _gdn_body(x_ref, z_ref, abg_ref, cw_ref, cs_ref, s0_ref, alog_ref, dtb_ref, an_ref,
              o_ref, sout_ref, s_sc, xb_sc, *, nc, nb):
    c = pl.program_id(1)
    L = CHUNK
    pad = 8

    @pl.when(c == 0)
    def _():
        s_sc[...] = s0_ref[...]
        xb_sc[:, pad - 3:pad, :] = cs_ref[...]

    ones_bd = _head_block_ones(GROUP_W)
    ones_bd16 = ones_bd.astype(BF16)
    er, ec = _iota((ABG_W, GROUP_W), 0), _iota((ABG_W, GROUP_W), 1) >> 6
    e_beta = (er == ec).astype(BF16)
    e_g = (er == ec + N_HEADS).astype(BF16)
    ri, ci = _iota((L, L), 0), _iota((L, L), 1)
    tri16 = (ri >= ci).astype(BF16)
    incl = ri >= ci
    strict = ri > ci
    eye = (ri == ci).astype(F32)
    pair_masks = [_pair_mask(L, sz) for sz in (1, 2, 4, 8, 16, 32)]
    lane_head = _iota((1, GROUP_W), 1) >> 6
    cw = cw_ref[...]

    a_all, per_seq = [], []
    for b in range(nb):
        x = x_ref[b]
        xb_sc[b, pad:pad + L, :] = x
        u = (x * cw[3:4] + xb_sc[b, pad - 1:pad - 1 + L, :] * cw[2:3]
             + xb_sc[b, pad - 2:pad - 2 + L, :] * cw[1:2] + xb_sc[b, pad - 3:pad - 3 + L, :] * cw[0:1])
        xb_sc[b, pad - 3:pad, :] = x[L - 3:L, :]
        u = u * _sigmoid(u)
        q = u[:, :GROUP_W]
        k = u[:, GROUP_W:2 * GROUP_W]
        v = u[:, 2 * GROUP_W:]
        q = q * lax.rsqrt(_mm_r(q * q, ones_bd16) + EPS) * HEAD_DIM ** -0.5
        k = k * lax.rsqrt(_mm_r(k * k, ones_bd16) + EPS)

        abg = abg_ref[b]
        beta_n = _sigmoid(abg)
        g_n = -jnp.exp(alog_ref[...]) * _softplus(abg + dtb_ref[...])
        beta = _mm_r(beta_n, e_beta)
        gc_n = _mm_l(tri16, g_n)
        gc_t = gc_n.T
        gc = _mm_r(gc_n, e_g)
        exp_g = jnp.exp(gc)
        g_last = gc[L - 1:L, :]

        kb = k * beta
        k_split = _split(k)
        a4 = _mm(_head_rows(kb, lane_head), k_split, _NT)
        qk4 = _bdot(_head_rows(q, lane_head).astype(BF16), k_split[0], _NT)
        dec = []
        for h in range(N_HEADS):
            diff = gc_n[:, N_HEADS + h:N_HEADS + h + 1] - gc_t[N_HEADS + h:N_HEADS + h + 1, :]
            dec.append(jnp.where(incl, jnp.exp(jnp.where(incl, diff, 0.0)), 0.0))
        dec = jnp.stack(dec)
        a_all.append(jnp.where(strict, a4.reshape(N_HEADS, L, L) * dec, 0.0))
        per_seq.append((q, k, v * beta, kb * exp_g, q * exp_g, k * jnp.exp(g_last - gc), jnp.exp(g_last),
                        qk4.reshape(N_HEADS, L, L) * dec))

    a_all = jnp.concatenate(a_all, axis=0)
    t_inv = eye - jnp.where(pair_masks[0], a_all, 0.0)
    for pm in pair_masks[1:]:
        t_split = _split(t_inv)
        t_inv = t_inv - _mm(t_split, _mm(jnp.where(pm, a_all, 0.0), t_split, _BNN), _BNN)

    for b in range(nb):
        q, k, vb, kbg, qg, k_dec, decay_last, qk = per_seq[b]
        s = s_sc[b]
        s_split = _split(s)
        rhs = vb - _mm(kbg, s_split)
        t4 = t_inv[b * N_HEADS:(b + 1) * N_HEADS].reshape(N_HEADS * L, L)
        w = _head_diag(_mm(t4, rhs), lane_head)
        o = _mm(qg, s_split) + _head_diag(_mm1(qk.reshape(N_HEADS * L, L), w), lane_head)
        s_sc[b] = s * decay_last + _mm(k_dec.T, w) * ones_bd

        ms = _mm_r(o * o, ones_bd16) * (1.0 / HEAD_DIM)
        zg = z_ref[b]
        o_ref[b] = o * lax.rsqrt(ms + EPS) * an_ref[...] * (zg * _sigmoid(zg))

    @pl.when(c == nc - 1)
    def _():
        sout_ref[...] = s_sc[...]


def _gdn(aqkv, az, abg, conv_w, conv_state, s0, alog_row, dtb_row, anorm_row):
    b, t, _ = aqkv.shape
    nc = t // CHUNK
    nb = SEQ_PER_STEP
    assert b % nb == 0
    cmap = lambda i, j: (i, j, 0)
    bmap = lambda i, j: (i, 0, 0)
    return pl.pallas_call(
        functools.partial(_gdn_body, nc=nc, nb=nb),
        grid=(b // nb, nc),
        in_specs=[pl.BlockSpec((nb, CHUNK, 3 * GROUP_W), cmap), pl.BlockSpec((nb, CHUNK, GROUP_W), cmap),
                  pl.BlockSpec((nb, CHUNK, ABG_W), cmap), _resident((CONV_W, 3 * GROUP_W)),
                  pl.BlockSpec((nb, CONV_W - 1, 3 * GROUP_W), bmap), pl.BlockSpec((nb, GROUP_W, GROUP_W), bmap),
                  _resident((1, ABG_W)), _resident((1, ABG_W)), _resident((1, GROUP_W))],
        out_specs=[pl.BlockSpec((nb, CHUNK, GROUP_W), cmap), pl.BlockSpec((nb, GROUP_W, GROUP_W), bmap)],
        out_shape=[jax.ShapeDtypeStruct((b, t, GROUP_W), F32), jax.ShapeDtypeStruct((b, GROUP_W, GROUP_W), F32)],
        scratch_shapes=[pltpu.VMEM((nb, GROUP_W, GROUP_W), F32), pltpu.VMEM((nb, 8 + CHUNK, 3 * GROUP_W), F32)],
        compiler_params=_cparams("parallel", "arbitrary"),
    )(aqkv, az, abg, conv_w, conv_state, s0, alog_row, dtb_row, anorm_row)


def _hgrn_body(q_ref, f_ref, i_ref, g_ref, lb_ref, dn_ref, s0_ref, o_ref, sout_ref, s_sc, *, nc, nb):
    c = pl.program_id(1)
    L = CHUNK

    @pl.when(c == 0)
    def _():
        s_sc[...] = s0_ref[...]

    lb = lb_ref[...]
    ones_bd = _head_block_ones(GROUP_W)
    ones_bd16 = ones_bd.astype(BF16)
    ri, ci = _iota((L, L), 0), _iota((L, L), 1)
    tri16 = (ri >= ci).astype(BF16)
    eye = ri == ci
    sizes = (1, 2, 4, 8, 16, 32)
    sel16 = jnp.concatenate([(ci == (ri // (2 * sz)) * (2 * sz) + sz).astype(BF16) for sz in sizes], axis=0)
    pair_masks = [_pair_mask(L, sz) for sz in sizes]
    lane_head = _iota((1, GROUP_W), 1) >> 6
    rows = _iota((L, 1), 0)

    for b in range(nb):
        fl = f_ref[b]
        log_f = _log_sigmoid(fl) + jnp.log(1.0 + lb * jnp.exp(-fl))
        k = (1.0 - lb) * _sigmoid(-fl)
        q = q_ref[b] * HEAD_DIM ** -0.5
        v = i_ref[b]
        bc = _mm_l(tri16, log_f)
        b_ref_all = _mm_l(sel16, bc)

        a4 = _mm1(_head_rows(q, lane_head), k, _NT)
        a = [jnp.where(eye, a4[h * L:(h + 1) * L], 0.0) for h in range(N_HEADS)]
        for lv, sz in enumerate(sizes):
            b_ref = b_ref_all[lv * L:(lv + 1) * L]
            later = ((rows // sz) & 1) == 1
            x = jnp.where(later, q * jnp.exp(jnp.where(later, bc - b_ref, 0.0)), 0.0)
            y = jnp.where(later, 0.0, k * jnp.exp(jnp.where(later, 0.0, b_ref - bc)))
            a4 = _mm1(_head_rows(x, lane_head), y, _NT)
            for h in range(N_HEADS):
                a[h] = a[h] + jnp.where(pair_masks[lv], a4[h * L:(h + 1) * L], 0.0)

        st = s_sc[b]
        o = _mm(q * jnp.exp(bc), st, _NT) + _head_diag(_mm1(jnp.concatenate(a, axis=0), v), lane_head)
        b_last = bc[L - 1:L, :]
        st_new = st * jnp.exp(b_last) + _mm(v.T, k * jnp.exp(b_last - bc)) * ones_bd
        s_sc[b] = st_new

        ms = _mm_r(o * o, ones_bd16) * (1.0 / HEAD_DIM)
        zg = g_ref[b]
        o_ref[b] = o * lax.rsqrt(ms + EPS) * dn_ref[...] * (zg * _sigmoid(zg))

    @pl.when(c == nc - 1)
    def _():
        sout_ref[...] = s_sc[...]


def _hgrn(dq, df, di, dg, lb_row, dnorm_row, s0t):
    b, t, _ = dq.shape
    nc = t // CHUNK
    nb = SEQ_PER_STEP
    assert b % nb == 0
    cmap = lambda i, j: (i, j, 0)
    bmap = lambda i, j: (i, 0, 0)
    cspec = pl.BlockSpec((nb, CHUNK, GROUP_W), cmap)
    return pl.pallas_call(
        functools.partial(_hgrn_body, nc=nc, nb=nb),
        grid=(b // nb, nc),
        in_specs=[cspec, cspec, cspec, cspec, _resident((1, GROUP_W)), _resident((1, GROUP_W)),
                  pl.BlockSpec((nb, GROUP_W, GROUP_W), bmap)],
        out_specs=[cspec, pl.BlockSpec((nb, GROUP_W, GROUP_W), bmap)],
        out_shape=[jax.ShapeDtypeStruct((b, t, GROUP_W), F32), jax.ShapeDtypeStruct((b, GROUP_W, GROUP_W), F32)],
        scratch_shapes=[pltpu.VMEM((nb, GROUP_W, GROUP_W), F32)],
        compiler_params=_cparams("parallel", "arbitrary"),
    )(dq, df, di, dg, lb_row, dnorm_row, s0t)


def _to_block_diag(s, transpose):
    if transpose:
        s = jnp.swapaxes(s, -1, -2)
    b = s.shape[0]
    eye = jnp.eye(N_HEADS, dtype=s.dtype)
    return (s[:, :, :, None, :] * eye[None, :, None, :, None]).reshape(b, GROUP_W, GROUP_W)


def _from_block_diag(s, transpose):
    b = s.shape[0]
    s5 = s.reshape(b, N_HEADS, HEAD_DIM, N_HEADS, HEAD_DIM)
    out = jnp.stack([s5[:, h, :, h, :] for h in range(N_HEADS)], axis=1)
    return jnp.swapaxes(out, -1, -2) if transpose else out


IN_WIDTHS = (3 * GROUP_W,) + (GROUP_W,) * 11 + (ABG_W,)


def _layer(x, l, st, mk, mv, mem_layer, prm, final_gain):
    b, t, _ = x.shape
    m = b * t
    tm = 256
    outs = _fused_linear([x.reshape(m, D_MODEL)], prm["w_in"][l], IN_WIDTHS, gain=prm["norm_mix"][l], tm=tm)
    (a_qkv, a_z, b_q, b_k, b_v, c_q, c_k, c_v, d_q, d_f, d_i, d_g, a_bg) = [o.reshape(b, t, -1) for o in outs]

    o_a, a_s = _gdn(a_qkv, a_z, a_bg, prm["a_conv_w"][l], st["a_conv"], _to_block_diag(st["a_S"], False),
                    prm["alog_row"][l], prm["dtb_row"][l], prm["a_norm"][l])
    o_b = _diff_attn(prm["lam"][l], b_q, b_k, b_v, st["b_k"], st["b_v"], l, prm["b_norm"][l],
                     0.8 - 0.6 * math.exp(-0.3 * l))
    o_c = _stick_attn(c_q, c_k, c_v, st["c_k"], st["c_v"], l)
    o_d, d_s = _hgrn(d_q, d_f, d_i, d_g, prm["lb"][l], prm["d_norm"][l], _to_block_diag(st["d_S"], True))

    x2 = _fused_linear([o.reshape(m, GROUP_W) for o in (o_a, o_b, o_c, o_d)], prm["w_out"][l], (D_MODEL,),
                       res=x.reshape(m, D_MODEL), tm=tm)[0]
    x2 = _cross(x2.reshape(b, t, D_MODEL), prm["norm_cross"][l], prm["w_cq"][l], mk, mv, mem_layer, prm["w_co"][l],
                tq=min(256, t))
    x2 = _ffn(x2.reshape(m, D_MODEL), prm["norm_ffn"][l], prm["w_gate"][l], prm["w_up"][l], prm["w_down"][l],
              final_gain=final_gain, tm=tm)

    if t >= CONV_W - 1:
        conv_new = a_qkv[:, t - (CONV_W - 1):, :]
    else:
        conv_new = jnp.concatenate([st["a_conv"], a_qkv], axis=1)[:, -(CONV_W - 1):, :]
    heads = lambda a: a.reshape(b, t, N_HEADS, HEAD_DIM)
    new = (conv_new, _from_block_diag(a_s, False), heads(b_k), heads(b_v), heads(c_k), heads(c_v),
           _from_block_diag(d_s, True))
    return x2.reshape(b, t, D_MODEL), new


def kernel(x_prompt, x_sample, mem_prompt, state_a_conv, state_a_S, cache_b_k, cache_b_v, cache_c_k, cache_c_v,
           state_d_S, cache_mem_k, cache_mem_v, norm_mix, w_in, a_conv_w, a_A_log, a_dt_bias, a_norm, b_lam_q1,
           b_lam_k1, b_lam_q2, b_lam_k2, b_norm, d_lb, d_norm, w_out, norm_cross, norm_memtok, w_cq, w_ck, w_cv,
           w_co, norm_ffn, w_gate, w_up, w_down, norm_final):
    depth = w_in.shape[0]
    n_main = 4 * GROUP_W
    w_in_p = jnp.concatenate(
        [w_in[:, :, :n_main], w_in[:, :, n_main + 2 * N_HEADS:], w_in[:, :, n_main:n_main + 2 * N_HEADS],
         jnp.zeros((depth, D_MODEL, ABG_W - 2 * N_HEADS), w_in.dtype)], axis=2).astype(BF16)
    pad_row = lambda a: jnp.pad(a.astype(F32), ((0, 0), (N_HEADS, ABG_W - 2 * N_HEADS)))[:, None, :]
    tile_row = lambda a: jnp.tile(a.astype(F32), (1, N_HEADS))[:, None, :]
    p_lb = jax.nn.softmax(d_lb.astype(F32), axis=0)
    lam = (jnp.exp(jnp.sum(b_lam_q1.astype(F32) * b_lam_k1.astype(F32), axis=-1))
           - jnp.exp(jnp.sum(b_lam_q2.astype(F32) * b_lam_k2.astype(F32), axis=-1))
           + jnp.array([0.8 - 0.6 * math.exp(-0.3 * l) for l in range(depth)], F32))
    prm = {
        "w_in": w_in_p, "norm_mix": norm_mix, "a_conv_w": a_conv_w.astype(F32),
        "alog_row": pad_row(a_A_log), "dtb_row": pad_row(a_dt_bias), "a_norm": tile_row(a_norm),
        "lam": lam[:, None], "b_norm": tile_row(b_norm), "lb": (jnp.cumsum(p_lb, axis=0) - p_lb[0])[:, None, :],
        "d_norm": tile_row(d_norm), "w_out": w_out.astype(BF16), "norm_cross": norm_cross,
        "w_cq": w_cq.astype(BF16), "w_co": w_co.astype(BF16), "norm_ffn": norm_ffn,
        "w_gate": w_gate.astype(BF16), "w_up": w_up.astype(BF16), "w_down": w_down.astype(BF16),
    }
    w_ckv = jnp.concatenate([w_ck, w_cv], axis=2).astype(BF16)

    bp, tp, _ = x_prompt.shape
    n_mem = mem_prompt.shape[1]
    h = x_prompt
    p_new = []
    for l in range(depth):
        mk, mv = _fused_linear([mem_prompt.reshape(bp * n_mem, D_MODEL)], w_ckv[l], (D_MODEL, D_MODEL),
                               gain=norm_memtok[l], tm=256)
        mk = mk.reshape(bp, n_mem, D_MODEL)
        mv = mv.reshape(bp, n_mem, D_MODEL)
        st = {"a_conv": jnp.zeros((bp, CONV_W - 1, 3 * GROUP_W), F32),
              "a_S": jnp.zeros((bp, N_HEADS, HEAD_DIM, HEAD_DIM), F32), "b_k": None, "b_v": None,
              "c_k": None, "c_v": None, "d_S": jnp.zeros((bp, N_HEADS, HEAD_DIM, HEAD_DIM), F32)}
        h, new = _layer(h, l, st, mk[None], mv[None], 0, prm, norm_final if l == depth - 1 else None)
        mem4 = lambda a: a.reshape(bp, n_mem, MEM_HEADS, MEM_HEAD_DIM)
        p_new.append(new + (mem4(mk), mem4(mv)))
    y_prompt = h
    p_stacked = [jnp.stack(c) for c in zip(*p_new)]

    bs, ts, _ = x_sample.shape
    past = cache_b_k.shape[2]
    h = x_sample
    s_new = []
    flat = lambda a: a.reshape(depth, bs, past, GROUP_W)
    caches = {"b_k": flat(cache_b_k), "b_v": flat(cache_b_v), "c_k": flat(cache_c_k), "c_v": flat(cache_c_v)}
    mem_k = cache_mem_k.reshape(depth, bs, -1, D_MODEL)
    mem_v = cache_mem_v.reshape(depth, bs, -1, D_MODEL)
    for l in range(depth):
        st = {"a_conv": state_a_conv[l], "a_S": state_a_S[l], "d_S": state_d_S[l], **caches}
        h, new = _layer(h, l, st, mem_k, mem_v, l, prm, norm_final if l == depth - 1 else None)
        s_new.append(new)
    y_sample = h
    s_stacked = [jnp.stack(c) for c in zip(*s_new)]

    return (y_prompt, y_sample, *p_stacked, *s_stacked)
```

```python
import functools
import math

import jax
import jax.numpy as jnp
from jax import lax
from jax.experimental import pallas as pl
from jax.experimental.pallas import tpu as pltpu

F32 = jnp.float32
BF16 = jnp.bfloat16

D_MODEL = 1024
GROUP_W = 256
N_HEADS = 4
HEAD_DIM = 64
DIFF_HALF = 32
CHUNK = 64
CONV_W = 4
MEM_HEADS = 4
MEM_HEAD_DIM = 256
D_FF = 2816
EPS = 1e-6
NEG = -1e30
LOG2E = 1.4426950408889634
LANES = 128
ABG_W = 128
FF_CHUNK = 256
STICK_SUB = 256
SEQ_PER_STEP = 4
VMEM_LIMIT = 56 * 1024 * 1024


def _cparams(*sem):
    return pltpu.CompilerParams(dimension_semantics=sem, vmem_limit_bytes=VMEM_LIMIT)


def _dot(a, b):
    return jnp.dot(a, b, preferred_element_type=F32)


def _dot_nt(a, b):
    return lax.dot_general(a, b, (((1,), (1,)), ((), ())), preferred_element_type=F32)


_NN = (((1,), (0,)), ((), ()))
_NT = (((1,), (1,)), ((), ()))
_BNN = (((2,), (1,)), ((0,), (0,)))


def _split(x):
    hi = x.astype(BF16)
    return hi, (x - hi.astype(F32)).astype(BF16)


def _bdot(a, b, dims):
    return lax.dot_general(a, b, dims, preferred_element_type=F32)


def _mm(a, b, dims=_NN):
    ah, al = a if isinstance(a, tuple) else _split(a)
    bh, bl = b if isinstance(b, tuple) else _split(b)
    return _bdot(ah, bh, dims) + (_bdot(ah, bl, dims) + _bdot(al, bh, dims))


def _mm1(a, b, dims=_NN):
    return _bdot(a.astype(BF16), b.astype(BF16), dims)


def _mm_r(a, b16):
    ah, al = _split(a)
    return _bdot(ah, b16, _NN) + _bdot(al, b16, _NN)


def _mm_l(a16, b):
    bh, bl = _split(b)
    return _bdot(a16, bh, _NN) + _bdot(a16, bl, _NN)


def _rms(x, g):
    return x * lax.rsqrt(jnp.mean(x * x, axis=-1, keepdims=True) + EPS) * g


def _sigmoid(x):
    return 1.0 / (1.0 + jnp.exp(-x))


def _log_sigmoid(x):
    return jnp.minimum(x, 0.0) - jnp.log(1.0 + jnp.exp(-jnp.abs(x)))


def _softplus(x):
    return jnp.maximum(x, 0.0) + jnp.log(1.0 + jnp.exp(-jnp.abs(x)))


def _iota(shape, dim):
    return lax.broadcasted_iota(jnp.int32, shape, dim)


def _lanes(x, n):
    return x[:, :n] if n <= LANES else jnp.concatenate([x] * (n // LANES), axis=1)


def _head_block_ones(n):
    return ((_iota((n, n), 0) >> 6) == (_iota((n, n), 1) >> 6)).astype(F32)


def _resident(shape):
    return pl.BlockSpec(shape, lambda *_: (0,) * len(shape), pipeline_mode=pl.Buffered(1))


def _linear_body(*refs, ks, segs, has_norm, has_res, feat_major):
    n_in = len(ks)
    x_refs = refs[:n_in]
    pos = n_in
    g_ref = None
    if has_norm:
        g_ref = refs[pos]
        pos += 1
    w_ref = refs[pos]
    pos += 1
    r_ref = None
    if has_res:
        r_ref = refs[pos]
        pos += 1
    out_refs = refs[pos:]
    xb = []
    for x_ref in x_refs:
        x = x_ref[...].astype(F32)
        if has_norm:
            x = _rms(x, g_ref[...])
        xb.append(x.astype(BF16))
    for idx, (o_ref, (s, e)) in enumerate(zip(out_refs, segs)):
        acc = None
        k0 = 0
        for xk, k in zip(xb, ks):
            t = _dot(xk, w_ref[k0:k0 + k, s:e])
            acc = t if acc is None else acc + t
            k0 += k
        if has_res:
            acc = acc + r_ref[:, s:e]
        if idx in feat_major:
            rows = acc.shape[0] // o_ref.shape[0]
            for sq in range(o_ref.shape[0]):
                o_ref[sq] = acc[sq * rows:(sq + 1) * rows].T
        else:
            o_ref[...] = acc


def _fused_linear(xs, w, widths, gain=None, res=None, tm=256, feat_major=(), rows_per_seq=None):
    m = xs[0].shape[0]
    ks = tuple(x.shape[1] for x in xs)
    n = w.shape[1]
    segs = []
    s = 0
    for wd in widths:
        segs.append((s, s + wd))
        s += wd
    assert s == n and m % tm == 0
    out_specs, out_shape = [], []
    for idx, wd in enumerate(widths):
        if idx in feat_major:
            if rows_per_seq >= tm:
                tiles = rows_per_seq // tm
                assert rows_per_seq % tm == 0
                out_specs.append(pl.BlockSpec((1, wd, tm), lambda i, tiles=tiles: (i // tiles, 0, i % tiles)))
            else:
                assert tm % rows_per_seq == 0
                out_specs.append(pl.BlockSpec((tm // rows_per_seq, wd, rows_per_seq), lambda i: (i, 0, 0)))
            out_shape.append(jax.ShapeDtypeStruct((m // rows_per_seq, wd, rows_per_seq), F32))
        else:
            out_specs.append(pl.BlockSpec((tm, wd), lambda i: (i, 0)))
            out_shape.append(jax.ShapeDtypeStruct((m, wd), F32))
    in_specs = [pl.BlockSpec((tm, k), lambda i: (i, 0)) for k in ks]
    args = list(xs)
    if gain is not None:
        in_specs.append(_resident((1, ks[0])))
        args.append(gain.reshape(1, -1))
    in_specs.append(_resident((sum(ks), n)))
    args.append(w)
    if res is not None:
        in_specs.append(pl.BlockSpec((tm, n), lambda i: (i, 0)))
        args.append(res)
    return pl.pallas_call(
        functools.partial(_linear_body, ks=ks, segs=tuple(segs), has_norm=gain is not None, has_res=res is not None,
                          feat_major=tuple(feat_major)),
        grid=(m // tm,),
        in_specs=in_specs,
        out_specs=out_specs,
        out_shape=out_shape,
        compiler_params=_cparams("parallel"),
    )(*args)


def _ffn_body(*refs, final):
    if final:
        x_ref, g_ref, wg_ref, wu_ref, wd_ref, gf_ref, o_ref = refs
    else:
        x_ref, g_ref, wg_ref, wu_ref, wd_ref, o_ref = refs
    x = x_ref[...]
    hb = _rms(x, g_ref[...]).astype(BF16)
    acc = x
    for c in range(0, D_FF, FF_CHUNK):
        gate = _dot(hb, wg_ref[:, c:c + FF_CHUNK])
        up = _dot(hb, wu_ref[:, c:c + FF_CHUNK])
        act = (gate * _sigmoid(gate) * up).astype(BF16)
        acc = acc + _dot(act, wd_ref[c:c + FF_CHUNK, :])
    if final:
        acc = _rms(acc, gf_ref[...])
    o_ref[...] = acc


def _ffn(x, gain, wg, wu, wd, final_gain=None, tm=256):
    m = x.shape[0]
    final = final_gain is not None
    in_specs = [pl.BlockSpec((tm, D_MODEL), lambda i: (i, 0)), _resident((1, D_MODEL)),
                _resident((D_MODEL, D_FF)), _resident((D_MODEL, D_FF)), _resident((D_FF, D_MODEL))]
    args = [x, gain.reshape(1, -1), wg, wu, wd]
    if final:
        in_specs.append(_resident((1, D_MODEL)))
        args.append(final_gain.reshape(1, -1))
    return pl.pallas_call(
        functools.partial(_ffn_body, final=final),
        grid=(m // tm,),
        in_specs=in_specs,
        out_specs=pl.BlockSpec((tm, D_MODEL), lambda i: (i, 0)),
        out_shape=jax.ShapeDtypeStruct((m, D_MODEL), F32),
        compiler_params=_cparams("parallel"),
    )(*args)


def _cross_body(x_ref, g_ref, wq_ref, mk_ref, mv_ref, wo_ref, o_ref):
    x = x_ref[0]
    hb = _rms(x, g_ref[...]).astype(BF16)
    q = _dot(hb, wq_ref[...]) * MEM_HEAD_DIM ** -0.5
    acc = x
    for h in range(MEM_HEADS):
        sl = slice(h * MEM_HEAD_DIM, (h + 1) * MEM_HEAD_DIM)
        s = _dot_nt(q[:, sl].astype(BF16), mk_ref[0, :, sl].astype(BF16))
        p = jnp.exp(s - jnp.max(s, axis=-1, keepdims=True))
        p = p / jnp.sum(p, axis=-1, keepdims=True)
        oh = _dot(p.astype(BF16), mv_ref[0, :, sl].astype(BF16))
        acc = acc + _dot(oh.astype(BF16), wo_ref[sl, :])
    o_ref[0] = acc


def _cross(x, gain, wq, mk, mv, layer, wo, tq):
    b, t, _ = x.shape
    nm = mk.shape[2]
    return pl.pallas_call(
        _cross_body,
        grid=(b, t // tq),
        in_specs=[pl.BlockSpec((1, tq, D_MODEL), lambda i, j: (i, j, 0)), _resident((1, D_MODEL)),
                  _resident((D_MODEL, D_MODEL)),
                  pl.BlockSpec((None, 1, nm, D_MODEL), lambda i, j: (layer, i, 0, 0)),
                  pl.BlockSpec((None, 1, nm, D_MODEL), lambda i, j: (layer, i, 0, 0)),
                  _resident((D_MODEL, D_MODEL))],
        out_specs=pl.BlockSpec((1, tq, D_MODEL), lambda i, j: (i, j, 0)),
        out_shape=jax.ShapeDtypeStruct((b, t, D_MODEL), F32),
        compiler_params=_cparams("parallel", "parallel"),
    )(x, gain.reshape(1, -1), wq, mk, mv, wo)


def _diff_body(*refs, tq, tk, past, lam_init, gm):
    if past:
        lam_ref, q_ref, kn_ref, vn_ref, kp_ref, vp_ref, bn_ref, o_ref, q8_sc, m_sc, l_sc, acc_sc = refs
    else:
        lam_ref, q_ref, kn_ref, vn_ref, bn_ref, o_ref, q8_sc, m_sc, l_sc, acc_sc = refs
    i = pl.program_id(1)
    ng = 2 * N_HEADS
    q = q_ref[0] * (DIFF_HALF ** -0.5 * LOG2E)
    lane = _iota((tq, GROUP_W), 1)
    for g in range(ng):
        lo = (g // 2) * HEAD_DIM + (g % 2) * DIFF_HALF
        q8_sc[g * tq:(g + 1) * tq, :] = jnp.where((lane >= lo) & (lane < lo + DIFF_HALF), q, 0.0).astype(BF16)
    m_sc[...] = jnp.full(m_sc.shape, -jnp.inf, F32)
    l_sc[...] = jnp.zeros(l_sc.shape, F32)
    acc_sc[...] = jnp.zeros(acc_sc.shape, F32)
    pq = past + i * tq + _iota((tq, 1), 0)
    pq_f = pq.astype(F32)

    lanes = _lanes

    def block(k_ref, v_ref, k0, n, near, pos0=0):
        kb = k_ref[0, :, pl.ds(k0, n)].astype(BF16)
        vb = v_ref[0, :, pl.ds(k0, n)].astype(BF16)
        pk = pos0 + k0 + _iota((1, n), 1)
        pk_f = pk.astype(F32)
        if near:
            allowed = (pk >> 6) <= (pq >> 6)
            shift = pq_f - jnp.abs(pq_f - pk_f)
        for st in range(ng // gm):
            q_st = q8_sc[st * gm * tq:(st + 1) * gm * tq, :]
            s_st = _dot(q_st, kb)
            ps, alphas = [], []
            for gi in range(gm):
                g = st * gm + gi
                slope = 2.0 ** (-2 * (g // 2 + 1)) * LOG2E
                s = s_st[gi * tq:(gi + 1) * tq]
                if near:
                    s = jnp.where(allowed, s + slope * shift, NEG)
                else:
                    s = s + slope * pk_f
                m_prev = m_sc[g]
                m_new = jnp.maximum(m_prev, jnp.max(s, axis=-1, keepdims=True))
                p = jnp.exp2(s - lanes(m_new, n))
                alpha = jnp.exp2(m_prev - m_new)
                l_sc[g] = alpha * l_sc[g] + jnp.sum(p, axis=-1, keepdims=True)
                m_sc[g] = m_new
                ps.append(p.astype(BF16))
                alphas.append(alpha)
            p_st = ps[0] if gm == 1 else jnp.concatenate(ps, axis=0)
            pv = _dot_nt(p_st, vb)
            for gi in range(gm):
                g = st * gm + gi
                acc_sc[g] = acc_sc[g] * lanes(alphas[gi], GROUP_W) + pv[gi * tq:(gi + 1) * tq]

    if past:
        def past_step(j, carry):
            block(kp_ref, vp_ref, pl.multiple_of(j * tk, tk), tk, False)
            return carry
        lax.fori_loop(0, past // tk, past_step, 0, unroll=2)
        block(kn_ref, vn_ref, 0, tq, True, pos0=past)
    else:
        def prev_step(j, carry):
            block(kn_ref, vn_ref, pl.multiple_of(j * tk, tk), tk, False)
            return carry
        lax.fori_loop(0, (i * tq) // tk, prev_step, 0)

        @pl.when((i * tq) % tk != 0)
        def _():
            block(kn_ref, vn_ref, pl.multiple_of((i - 1) * tq, tq), tq, False)
        block(kn_ref, vn_ref, pl.multiple_of(i * tq, tq), tq, True)

    lam = lam_ref[0]
    lane_head = _iota((1, GROUP_W), 1) >> 6
    o = jnp.zeros((tq, GROUP_W), F32)
    for h in range(N_HEADS):
        o0 = acc_sc[2 * h] / lanes(l_sc[2 * h], GROUP_W)
        o1 = acc_sc[2 * h + 1] / lanes(l_sc[2 * h + 1], GROUP_W)
        o = jnp.where(lane_head == h, o0 - lam * o1, o)
    ms = _mm_r(o * o, _head_block_ones(GROUP_W).astype(BF16)) * (1.0 / HEAD_DIM)
    o_ref[0] = o * lax.rsqrt(ms + EPS) * bn_ref[...] * (1.0 - lam_init)


def _diff_attn(lam, q, kn, vn, kp, vp, layer, bnorm, lam_init):
    b, t, _ = q.shape
    past = 0 if kp is None else kp.shape[3]
    ng = 2 * N_HEADS
    if past:
        tq, tk, gm = t, 512, ng
        assert t == CHUNK and past % tk == 0
    else:
        tq = min(256, t)
        tk = 2 * tq if t % (2 * tq) == 0 else tq
        gm = 1
        assert t % tq == 0 and tq % CHUNK == 0
    seq_spec = lambda n: pl.BlockSpec((1, GROUP_W, n), lambda i, j: (i, 0, 0))
    past_spec = pl.BlockSpec((None, 1, GROUP_W, past), lambda i, j: (layer, i, 0, 0))
    in_specs = [pl.BlockSpec(memory_space=pltpu.SMEM),
                pl.BlockSpec((1, tq, GROUP_W), lambda i, j: (i, j, 0)), seq_spec(t), seq_spec(t)]
    args = [lam, q, kn, vn]
    if past:
        in_specs += [past_spec, past_spec]
        args += [kp, vp]
    in_specs.append(_resident((1, GROUP_W)))
    args.append(bnorm)
    return pl.pallas_call(
        functools.partial(_diff_body, tq=tq, tk=tk, past=past, lam_init=lam_init, gm=gm),
        grid=(b, t // tq),
        in_specs=in_specs,
        out_specs=pl.BlockSpec((1, tq, GROUP_W), lambda i, j: (i, j, 0)),
        out_shape=jax.ShapeDtypeStruct((b, t, GROUP_W), F32),
        scratch_shapes=[pltpu.VMEM((ng * tq, GROUP_W), BF16), pltpu.VMEM((ng, tq, LANES), F32),
                        pltpu.VMEM((ng, tq, LANES), F32), pltpu.VMEM((ng, tq, GROUP_W), F32)],
        compiler_params=_cparams("parallel", "parallel"),
    )(*args)


def _stick_body(*refs, tq, tk, past):
    if past:
        q_ref, kn_ref, vn_ref, kp_ref, vp_ref, o_ref, q4_sc, c_sc, acc_sc = refs
    else:
        q_ref, kn_ref, vn_ref, o_ref, q4_sc, c_sc, acc_sc = refs
    i = pl.program_id(1)
    nr = N_HEADS * tq
    q = q_ref[0] * (HEAD_DIM ** -0.5 * LOG2E)
    lane = _iota((tq, GROUP_W), 1) >> 6
    for h in range(N_HEADS):
        q4_sc[h * tq:(h + 1) * tq, :] = jnp.where(lane == h, q, 0.0).astype(BF16)
    c_sc[...] = jnp.zeros(c_sc.shape, F32)
    acc_sc[...] = jnp.zeros(acc_sc.shape, F32)
    pq = past + i * tq + (_iota((nr, 1), 0) & (tq - 1))
    laters = {w: (_iota((w, w), 0) > _iota((w, w), 1)).astype(BF16)
              for w in {min(tq, STICK_SUB), min(tk, STICK_SUB)}}

    def block(k_ref, v_ref, k0, n, masked, pos0=0):
        kb = k_ref[0, :, pl.ds(k0, n)].astype(BF16)
        vb = v_ref[0, :, pl.ds(k0, n)].astype(BF16)
        sub = min(n, STICK_SUB)
        later = laters[sub]
        z = _dot(q4_sc[...], kb)
        ls = jnp.minimum(z, 0.0) - jnp.log2(1.0 + jnp.exp2(-jnp.abs(z)))
        lk = ls - z
        if masked:
            mask = (pos0 + k0 + _iota((1, n), 1)) < pq
            lk = jnp.where(mask, lk, 0.0)
        hi = lk.astype(BF16)
        lo = (lk - hi.astype(F32)).astype(BF16)
        carry = c_sc[...]
        es = []
        for sb in reversed(range(n // sub)):
            sl = slice(sb * sub, (sb + 1) * sub)
            after = _dot(hi[:, sl], later) + _dot(lo[:, sl], later)
            es.append(ls[:, sl] + after + _lanes(carry, sub))
            carry = carry + (after[:, 0:1] + lk[:, sb * sub:sb * sub + 1])
        c_sc[...] = carry
        e = es[0] if len(es) == 1 else jnp.concatenate(es[::-1], axis=1)
        if masked:
            a = jnp.where(mask, jnp.exp2(jnp.where(mask, e, 0.0)), 0.0)
        else:
            a = jnp.exp2(e)
        acc_sc[...] = acc_sc[...] + _dot_nt(a.astype(BF16), vb)

    if past:
        block(kn_ref, vn_ref, 0, tq, True, pos0=past)
        nb = past // tk

        def past_step(j, carry):
            block(kp_ref, vp_ref, pl.multiple_of((nb - 1 - j) * tk, tk), tk, False)
            return carry
        lax.fori_loop(0, nb, past_step, 0)
    else:
        block(kn_ref, vn_ref, pl.multiple_of(i * tq, tq), tq, True)

        @pl.when((i * tq) % tk != 0)
        def _():
            block(kn_ref, vn_ref, pl.multiple_of((i - 1) * tq, tq), tq, False)
        nfull = (i * tq) // tk

        def prev_step(j, carry):
            block(kn_ref, vn_ref, pl.multiple_of((nfull - 1 - j) * tk, tk), tk, False)
            return carry
        lax.fori_loop(0, nfull, prev_step, 0)

    o = acc_sc[0:tq, :]
    for h in range(1, N_HEADS):
        o = jnp.where(lane == h, acc_sc[h * tq:(h + 1) * tq, :], o)
    o_ref[0] = o


def _stick_attn(q, kn, vn, kp, vp, layer):
    b, t, _ = q.shape
    past = 0 if kp is None else kp.shape[3]
    if past:
        tq, tk = t, math.gcd(past, 1024)
    else:
        tq = min(256, t)
        tk = 2 * tq if t % (2 * tq) == 0 else tq
        assert t % tq == 0
    assert tq & (tq - 1) == 0 and tk % min(tk, STICK_SUB) == 0
    seq_spec = lambda n: pl.BlockSpec((1, GROUP_W, n), lambda i, j: (i, 0, 0))
    past_spec = pl.BlockSpec((None, 1, GROUP_W, past), lambda i, j: (layer, i, 0, 0))
    in_specs = [pl.BlockSpec((1, tq, GROUP_W), lambda i, j: (i, j, 0)), seq_spec(t), seq_spec(t)]
    args = [q, kn, vn]
    if past:
        in_specs += [past_spec, past_spec]
        args += [kp, vp]
    return pl.pallas_call(
        functools.partial(_stick_body, tq=tq, tk=tk, past=past),
        grid=(b, t // tq),
        in_specs=in_specs,
        out_specs=pl.BlockSpec((1, tq, GROUP_W), lambda i, j: (i, j, 0)),
        out_shape=jax.ShapeDtypeStruct((b, t, GROUP_W), F32),
        scratch_shapes=[pltpu.VMEM((N_HEADS * tq, GROUP_W), BF16), pltpu.VMEM((N_HEADS * tq, LANES), F32),
                        pltpu.VMEM((N_HEADS * tq, GROUP_W), F32)],
        compiler_params=_cparams("parallel", "parallel"),
    )(*args)


def _pair_mask(n, s):
    r, c = _iota((n, n), 0), _iota((n, n), 1)
    return ((r // (2 * s)) == (c // (2 * s))) & (((r // s) & 1) == 1) & (((c // s) & 1) == 0)


def _head_rows(x, lane_head):
    return jnp.concatenate([jnp.where(lane_head == h, x, 0.0) for h in range(N_HEADS)], axis=0)


def _head_diag(x, lane_head):
    n = x.shape[0] // N_HEADS
    out = x[:n]
    for h in range(1, N_HEADS):
        out = jnp.where(lane_head == h, x[h * n:(h + 1) * n], out)
    return out


def _gdn_body(x_ref, z_ref, abg_ref, cw_ref, cs_ref, s0_ref, alog_ref, dtb_ref, an_ref,
              o_ref, sout_ref, s_sc, xb_sc, *, nc, nb):
    c = pl.program_id(1)
    L = CHUNK
    pad = 8

    @pl.when(c == 0)
    def _():
        s_sc[...] = s0_ref[...]
        xb_sc[:, pad - 3:pad, :] = cs_ref[...]

    ones_bd = _head_block_ones(GROUP_W)
    ones_bd16 = ones_bd.astype(BF16)
    er, ec = _iota((ABG_W, GROUP_W), 0), _iota((ABG_W, GROUP_W), 1) >> 6
    e_beta = (er == ec).astype(BF16)
    e_g = (er == ec + N_HEADS).astype(BF16)
    ri, ci = _iota((L, L), 0), _iota((L, L), 1)
    tri16 = (ri >= ci).astype(BF16)
    incl = ri >= ci
    strict = ri > ci
    eye = (ri == ci).astype(F32)
    pair_masks = [_pair_mask(L, sz) for sz in (1, 2, 4, 8, 16, 32)]
    lane_head = _iota((1, GROUP_W), 1) >> 6
    cw = cw_ref[...]

    a_all, per_seq = [], []
    for b in range(nb):
        x = x_ref[b]
        xb_sc[b, pad:pad + L, :] = x
        u = (x * cw[3:4] + xb_sc[b, pad - 1:pad - 1 + L, :] * cw[2:3]
             + xb_sc[b, pad - 2:pad - 2 + L, :] * cw[1:2] + xb_sc[b, pad - 3:pad - 3 + L, :] * cw[0:1])
        xb_sc[b, pad - 3:pad, :] = x[L - 3:L, :]
        u = u * _sigmoid(u)
        q = u[:, :GROUP_W]
        k = u[:, GROUP_W:2 * GROUP_W]
        v = u[:, 2 * GROUP_W:]
        q = q * lax.rsqrt(_mm_r(q * q, ones_bd16) + EPS) * HEAD_DIM ** -0.5
        k = k * lax.rsqrt(_mm_r(k * k, ones_bd16) + EPS)

        abg = abg_ref[b]
        beta_n = _sigmoid(abg)
        g_n = -jnp.exp(alog_ref[...]) * _softplus(abg + dtb_ref[...])
        beta = _mm_r(beta_n, e_beta)
        gc_n = _mm_l(tri16, g_n)
        gc_t = gc_n.T
        gc = _mm_r(gc_n, e_g)
        exp_g = jnp.exp(gc)
        g_last = gc[L - 1:L, :]

        kb = k * beta
        k_split = _split(k)
        a4 = _mm(_head_rows(kb, lane_head), k_split, _NT)
        qk4 = _bdot(_head_rows(q, lane_head).astype(BF16), k_split[0], _NT)
        dec = []
        for h in range(N_HEADS):
            diff = gc_n[:, N_HEADS + h:N_HEADS + h + 1] - gc_t[N_HEADS + h:N_HEADS + h + 1, :]
            dec.append(jnp.where(incl, jnp.exp(jnp.where(incl, diff, 0.0)), 0.0))
        dec = jnp.stack(dec)
        a_all.append(jnp.where(strict, a4.reshape(N_HEADS, L, L) * dec, 0.0))
        per_seq.append((q, k, v * beta, kb * exp_g, q * exp_g, k * jnp.exp(g_last - gc), jnp.exp(g_last),
                        qk4.reshape(N_HEADS, L, L) * dec))

    a_all = jnp.concatenate(a_all, axis=0)
    t_inv = eye - jnp.where(pair_masks[0], a_all, 0.0)
    for pm in pair_masks[1:]:
        t_split = _split(t_inv)
        t_inv = t_inv - _mm(t_split, _mm(jnp.where(pm, a_all, 0.0), t_split, _BNN), _BNN)

    for b in range(nb):
        q, k, vb, kbg, qg, k_dec, decay_last, qk = per_seq[b]
        s = s_sc[b]
        s_split = _split(s)
        rhs = vb - _mm(kbg, s_split)
        t4 = t_inv[b * N_HEADS:(b + 1) * N_HEADS].reshape(N_HEADS * L, L)
        w = _head_diag(_mm(t4, rhs), lane_head)
        o = _mm(qg, s_split) + _head_diag(_mm1(qk.reshape(N_HEADS * L, L), w), lane_head)
        s_sc[b] = s * decay_last + _mm(k_dec.T, w) * ones_bd

        ms = _mm_r(o * o, ones_bd16) * (1.0 / HEAD_DIM)
        zg = z_ref[b]
        o_ref[b] = o * lax.rsqrt(ms + EPS) * an_ref[...] * (zg * _sigmoid(zg))

    @pl.when(c == nc - 1)
    def _():
        sout_ref[...] = s_sc[...]


def _gdn(aqkv, az, abg, conv_w, conv_state, s0, alog_row, dtb_row, anorm_row):
    b, t, _ = aqkv.shape
    nc = t // CHUNK
    nb = SEQ_PER_STEP
    assert b % nb == 0
    cmap = lambda i, j: (i, j, 0)
    bmap = lambda i, j: (i, 0, 0)
    return pl.pallas_call(
        functools.partial(_gdn_body, nc=nc, nb=nb),
        grid=(b // nb, nc),
        in_specs=[pl.BlockSpec((nb, CHUNK, 3 * GROUP_W), cmap), pl.BlockSpec((nb, CHUNK, GROUP_W), cmap),
                  pl.BlockSpec((nb, CHUNK, ABG_W), cmap), _resident((CONV_W, 3 * GROUP_W)),
                  pl.BlockSpec((nb, CONV_W - 1, 3 * GROUP_W), bmap), pl.BlockSpec((nb, GROUP_W, GROUP_W), bmap),
                  _resident((1, ABG_W)), _resident((1, ABG_W)), _resident((1, GROUP_W))],
        out_specs=[pl.BlockSpec((nb, CHUNK, GROUP_W), cmap), pl.BlockSpec((nb, GROUP_W, GROUP_W), bmap)],
        out_shape=[jax.ShapeDtypeStruct((b, t, GROUP_W), F32), jax.ShapeDtypeStruct((b, GROUP_W, GROUP_W), F32)],
        scratch_shapes=[pltpu.VMEM((nb, GROUP_W, GROUP_W), F32), pltpu.VMEM((nb, 8 + CHUNK, 3 * GROUP_W), F32)],
        compiler_params=_cparams("parallel", "arbitrary"),
    )(aqkv, az, abg, conv_w, conv_state, s0, alog_row, dtb_row, anorm_row)


def _hgrn_body(q_ref, f_ref, i_ref, g_ref, lb_ref, dn_ref, s0_ref, o_ref, sout_ref, s_sc, *, nc, nb):
    c = pl.program_id(1)
    L = CHUNK

    @pl.when(c == 0)
    def _():
        s_sc[...] = s0_ref[...]

    lb = lb_ref[...]
    ones_bd = _head_block_ones(GROUP_W)
    ones_bd16 = ones_bd.astype(BF16)
    ri, ci = _iota((L, L), 0), _iota((L, L), 1)
    tri16 = (ri >= ci).astype(BF16)
    eye = ri == ci
    sizes = (1, 2, 4, 8, 16, 32)
    sel16 = jnp.concatenate([(ci == (ri // (2 * sz)) * (2 * sz) + sz).astype(BF16) for sz in sizes], axis=0)
    pair_masks = [_pair_mask(L, sz) for sz in sizes]
    lane_head = _iota((1, GROUP_W), 1) >> 6
    rows = _iota((L, 1), 0)

    for b in range(nb):
        fl = f_ref[b]
        log_f = _log_sigmoid(fl) + jnp.log(1.0 + lb * jnp.exp(-fl))
        k = (1.0 - lb) * _sigmoid(-fl)
        q = q_ref[b] * HEAD_DIM ** -0.5
        v = i_ref[b]
        bc = _mm_l(tri16, log_f)
        b_ref_all = _mm_l(sel16, bc)

        a4 = _mm1(_head_rows(q, lane_head), k, _NT)
        a = [jnp.where(eye, a4[h * L:(h + 1) * L], 0.0) for h in range(N_HEADS)]
        for lv, sz in enumerate(sizes):
            b_ref = b_ref_all[lv * L:(lv + 1) * L]
            later = ((rows // sz) & 1) == 1
            x = jnp.where(later, q * jnp.exp(jnp.where(later, bc - b_ref, 0.0)), 0.0)
            y = jnp.where(later, 0.0, k * jnp.exp(jnp.where(later, 0.0, b_ref - bc)))
            a4 = _mm1(_head_rows(x, lane_head), y, _NT)
            for h in range(N_HEADS):
                a[h] = a[h] + jnp.where(pair_masks[lv], a4[h * L:(h + 1) * L], 0.0)

        st = s_sc[b]
        o = _mm(q * jnp.exp(bc), st, _NT) + _head_diag(_mm1(jnp.concatenate(a, axis=0), v), lane_head)
        b_last = bc[L - 1:L, :]
        st_new = st * jnp.exp(b_last) + _mm(v.T, k * jnp.exp(b_last - bc)) * ones_bd
        s_sc[b] = st_new

        ms = _mm_r(o * o, ones_bd16) * (1.0 / HEAD_DIM)
        zg = g_ref[b]
        o_ref[b] = o * lax.rsqrt(ms + EPS) * dn_ref[...] * (zg * _sigmoid(zg))

    @pl.when(c == nc - 1)
    def _():
        sout_ref[...] = s_sc[...]


def _hgrn(dq, df, di, dg, lb_row, dnorm_row, s0t):
    b, t, _ = dq.shape
    nc = t // CHUNK
    nb = SEQ_PER_STEP
    assert b % nb == 0
    cmap = lambda i, j: (i, j, 0)
    bmap = lambda i, j: (i, 0, 0)
    cspec = pl.BlockSpec((nb, CHUNK, GROUP_W), cmap)
    return pl.pallas_call(
        functools.partial(_hgrn_body, nc=nc, nb=nb),
        grid=(b // nb, nc),
        in_specs=[cspec, cspec, cspec, cspec, _resident((1, GROUP_W)), _resident((1, GROUP_W)),
                  pl.BlockSpec((nb, GROUP_W, GROUP_W), bmap)],
        out_specs=[cspec, pl.BlockSpec((nb, GROUP_W, GROUP_W), bmap)],
        out_shape=[jax.ShapeDtypeStruct((b, t, GROUP_W), F32), jax.ShapeDtypeStruct((b, GROUP_W, GROUP_W), F32)],
        scratch_shapes=[pltpu.VMEM((nb, GROUP_W, GROUP_W), F32)],
        compiler_params=_cparams("parallel", "arbitrary"),
    )(dq, df, di, dg, lb_row, dnorm_row, s0t)


def _to_block_diag(s, transpose):
    if transpose:
        s = jnp.swapaxes(s, -1, -2)
    b = s.shape[0]
    eye = jnp.eye(N_HEADS, dtype=s.dtype)
    return (s[:, :, :, None, :] * eye[None, :, None, :, None]).reshape(b, GROUP_W, GROUP_W)


def _from_block_diag(s, transpose):
    b = s.shape[0]
    s5 = s.reshape(b, N_HEADS, HEAD_DIM, N_HEADS, HEAD_DIM)
    out = jnp.stack([s5[:, h, :, h, :] for h in range(N_HEADS)], axis=1)
    return jnp.swapaxes(out, -1, -2) if transpose else out


IN_WIDTHS = (3 * GROUP_W,) + (GROUP_W,) * 11 + (ABG_W,)
KV_SEGMENTS = (3, 4, 6, 7)


def _layer(x, l, st, mk, mv, mem_layer, prm, final_gain):
    b, t, _ = x.shape
    m = b * t
    tm = 256
    outs = _fused_linear([x.reshape(m, D_MODEL)], prm["w_in"][l], IN_WIDTHS, gain=prm["norm_mix"][l], tm=tm,
                         feat_major=KV_SEGMENTS, rows_per_seq=t)
    outs = [o if idx in KV_SEGMENTS else o.reshape(b, t, -1) for idx, o in enumerate(outs)]
    (a_qkv, a_z, b_q, b_k, b_v, c_q, c_k, c_v, d_q, d_f, d_i, d_g, a_bg) = outs

    o_a, a_s = _gdn(a_qkv, a_z, a_bg, prm["a_conv_w"][l], st["a_conv"], _to_block_diag(st["a_S"], False),
                    prm["alog_row"][l], prm["dtb_row"][l], prm["a_norm"][l])
    o_b = _diff_attn(prm["lam"][l], b_q, b_k, b_v, st["b_k"], st["b_v"], l, prm["b_norm"][l],
                     0.8 - 0.6 * math.exp(-0.3 * l))
    o_c = _stick_attn(c_q, c_k, c_v, st["c_k"], st["c_v"], l)
    o_d, d_s = _hgrn(d_q, d_f, d_i, d_g, prm["lb"][l], prm["d_norm"][l], _to_block_diag(st["d_S"], True))

    x2 = _fused_linear([o.reshape(m, GROUP_W) for o in (o_a, o_b, o_c, o_d)], prm["w_out"][l], (D_MODEL,),
                       res=x.reshape(m, D_MODEL), tm=tm)[0]
    x2 = _cross(x2.reshape(b, t, D_MODEL), prm["norm_cross"][l], prm["w_cq"][l], mk, mv, mem_layer, prm["w_co"][l],
                tq=min(256, t))
    x2 = _ffn(x2.reshape(m, D_MODEL), prm["norm_ffn"][l], prm["w_gate"][l], prm["w_up"][l], prm["w_down"][l],
              final_gain=final_gain, tm=tm)

    if t >= CONV_W - 1:
        conv_new = a_qkv[:, t - (CONV_W - 1):, :]
    else:
        conv_new = jnp.concatenate([st["a_conv"], a_qkv], axis=1)[:, -(CONV_W - 1):, :]
    heads = lambda a: jnp.transpose(a.reshape(b, N_HEADS, HEAD_DIM, t), (0, 3, 1, 2))
    new = (conv_new, _from_block_diag(a_s, False), heads(b_k), heads(b_v), heads(c_k), heads(c_v),
           _from_block_diag(d_s, True))
    return x2.reshape(b, t, D_MODEL), new


def kernel(x_prompt, x_sample, mem_prompt, state_a_conv, state_a_S, cache_b_k, cache_b_v, cache_c_k, cache_c_v,
           state_d_S, cache_mem_k, cache_mem_v, norm_mix, w_in, a_conv_w, a_A_log, a_dt_bias, a_norm, b_lam_q1,
           b_lam_k1, b_lam_q2, b_lam_k2, b_norm, d_lb, d_norm, w_out, norm_cross, norm_memtok, w_cq, w_ck, w_cv,
           w_co, norm_ffn, w_gate, w_up, w_down, norm_final):
    depth = w_in.shape[0]
    n_main = 4 * GROUP_W
    w_in_p = jnp.concatenate(
        [w_in[:, :, :n_main], w_in[:, :, n_main + 2 * N_HEADS:], w_in[:, :, n_main:n_main + 2 * N_HEADS],
         jnp.zeros((depth, D_MODEL, ABG_W - 2 * N_HEADS), w_in.dtype)], axis=2).astype(BF16)
    pad_row = lambda a: jnp.pad(a.astype(F32), ((0, 0), (N_HEADS, ABG_W - 2 * N_HEADS)))[:, None, :]
    tile_row = lambda a: jnp.tile(a.astype(F32), (1, N_HEADS))[:, None, :]
    p_lb = jax.nn.softmax(d_lb.astype(F32), axis=0)
    lam = (jnp.exp(jnp.sum(b_lam_q1.astype(F32) * b_lam_k1.astype(F32), axis=-1))
           - jnp.exp(jnp.sum(b_lam_q2.astype(F32) * b_lam_k2.astype(F32), axis=-1))
           + jnp.array([0.8 - 0.6 * math.exp(-0.3 * l) for l in range(depth)], F32))
    prm = {
        "w_in": w_in_p, "norm_mix": norm_mix, "a_conv_w": a_conv_w.astype(F32),
        "alog_row": pad_row(a_A_log), "dtb_row": pad_row(a_dt_bias), "a_norm": tile_row(a_norm),
        "lam": lam[:, None], "b_norm": tile_row(b_norm), "lb": (jnp.cumsum(p_lb, axis=0) - p_lb[0])[:, None, :],
        "d_norm": tile_row(d_norm), "w_out": w_out.astype(BF16), "norm_cross": norm_cross,
        "w_cq": w_cq.astype(BF16), "w_co": w_co.astype(BF16), "norm_ffn": norm_ffn,
        "w_gate": w_gate.astype(BF16), "w_up": w_up.astype(BF16), "w_down": w_down.astype(BF16),
    }
    w_ckv = jnp.concatenate([w_ck, w_cv], axis=2).astype(BF16)

    bp, tp, _ = x_prompt.shape
    n_mem = mem_prompt.shape[1]
    h = x_prompt
    p_new = []
    for l in range(depth):
        mk, mv = _fused_linear([mem_prompt.reshape(bp * n_mem, D_MODEL)], w_ckv[l], (D_MODEL, D_MODEL),
                               gain=norm_memtok[l], tm=256)
        mk = mk.reshape(bp, n_mem, D_MODEL)
        mv = mv.reshape(bp, n_mem, D_MODEL)
        st = {"a_conv": jnp.zeros((bp, CONV_W - 1, 3 * GROUP_W), F32),
              "a_S": jnp.zeros((bp, N_HEADS, HEAD_DIM, HEAD_DIM), F32), "b_k": None, "b_v": None,
              "c_k": None, "c_v": None, "d_S": jnp.zeros((bp, N_HEADS, HEAD_DIM, HEAD_DIM), F32)}
        h, new = _layer(h, l, st, mk[None], mv[None], 0, prm, norm_final if l == depth - 1 else None)
        mem4 = lambda a: a.reshape(bp, n_mem, MEM_HEADS, MEM_HEAD_DIM)
        p_new.append(new + (mem4(mk), mem4(mv)))
    y_prompt = h
    p_stacked = [jnp.stack(c) for c in zip(*p_new)]

    bs, ts, _ = x_sample.shape
    past = cache_b_k.shape[2]
    h = x_sample
    s_new = []
    flat = lambda a: jnp.transpose(a, (0, 1, 3, 4, 2)).reshape(depth, bs, GROUP_W, past)
    caches = {"b_k": flat(cache_b_k), "b_v": flat(cache_b_v), "c_k": flat(cache_c_k), "c_v": flat(cache_c_v)}
    mem_k = cache_mem_k.reshape(depth, bs, -1, D_MODEL)
    mem_v = cache_mem_v.reshape(depth, bs, -1, D_MODEL)
    for l in range(depth):
        st = {"a_conv": state_a_conv[l], "a_S": state_a_S[l], "d_S": state_d_S[l], **caches}
        h, new = _layer(h, l, st, mem_k, mem_v, l, prm, norm_final if l == depth - 1 else None)
        s_new.append(new)
    y_sample = h
    s_stacked = [jnp.stack(c) for c in zip(*s_new)]

    return (y_prompt, y_sample, *p_stacked, *s_stacked)
```

```python
import functools
import math

import jax
import jax.numpy as jnp
from jax import lax
from jax.experimental import pallas as pl
from jax.experimental.pallas import tpu as pltpu

F32 = jnp.float32
BF16 = jnp.bfloat16

D_MODEL = 1024
GROUP_W = 256
N_HEADS = 4
HEAD_DIM = 64
DIFF_HALF = 32
CHUNK = 64
CONV_W = 4
MEM_HEADS = 4
MEM_HEAD_DIM = 256
D_FF = 2816
EPS = 1e-6
NEG = -1e30
LOG2E = 1.4426950408889634
LANES = 128
ABG_W = 128
FF_CHUNK = 256
CROSS_ROWS = 256
STICK_SUB = 256
SEQ_PER_STEP = 4
VMEM_LIMIT = 56 * 1024 * 1024


def _cparams(*sem):
    return pltpu.CompilerParams(dimension_semantics=sem, vmem_limit_bytes=VMEM_LIMIT)


def _dot(a, b):
    return jnp.dot(a, b, preferred_element_type=F32)


def _dot_nt(a, b):
    return lax.dot_general(a, b, (((1,), (1,)), ((), ())), preferred_element_type=F32)


_NN = (((1,), (0,)), ((), ()))
_NT = (((1,), (1,)), ((), ()))
_BNN = (((2,), (1,)), ((0,), (0,)))


def _split(x):
    hi = x.astype(BF16)
    return hi, (x - hi.astype(F32)).astype(BF16)


def _bdot(a, b, dims):
    return lax.dot_general(a, b, dims, preferred_element_type=F32)


def _mm(a, b, dims=_NN):
    ah, al = a if isinstance(a, tuple) else _split(a)
    bh, bl = b if isinstance(b, tuple) else _split(b)
    return _bdot(ah, bh, dims) + (_bdot(ah, bl, dims) + _bdot(al, bh, dims))


def _mm1(a, b, dims=_NN):
    return _bdot(a.astype(BF16), b.astype(BF16), dims)


def _mm_r(a, b16):
    ah, al = _split(a)
    return _bdot(ah, b16, _NN) + _bdot(al, b16, _NN)


def _mm_l(a16, b):
    bh, bl = _split(b)
    return _bdot(a16, bh, _NN) + _bdot(a16, bl, _NN)


def _rms(x, g):
    return x * lax.rsqrt(jnp.mean(x * x, axis=-1, keepdims=True) + EPS) * g


def _sigmoid(x):
    return 1.0 / (1.0 + jnp.exp(-x))


def _log_sigmoid(x):
    return jnp.minimum(x, 0.0) - jnp.log(1.0 + jnp.exp(-jnp.abs(x)))


def _softplus(x):
    return jnp.maximum(x, 0.0) + jnp.log(1.0 + jnp.exp(-jnp.abs(x)))


def _iota(shape, dim):
    return lax.broadcasted_iota(jnp.int32, shape, dim)


def _lanes(x, n):
    return x[:, :n] if n <= LANES else jnp.concatenate([x] * (n // LANES), axis=1)


def _head_block_ones(n):
    return ((_iota((n, n), 0) >> 6) == (_iota((n, n), 1) >> 6)).astype(F32)


def _resident(shape):
    return pl.BlockSpec(shape, lambda *_: (0,) * len(shape), pipeline_mode=pl.Buffered(1))


def _linear_body(*refs, ks, segs, has_norm, has_res, feat_major):
    n_in = len(ks)
    x_refs = refs[:n_in]
    pos = n_in
    g_ref = None
    if has_norm:
        g_ref = refs[pos]
        pos += 1
    w_ref = refs[pos]
    pos += 1
    r_ref = None
    if has_res:
        r_ref = refs[pos]
        pos += 1
    out_refs = refs[pos:]
    xb = []
    for x_ref in x_refs:
        x = x_ref[...].astype(F32)
        if has_norm:
            x = _rms(x, g_ref[...])
        xb.append(x.astype(BF16))
    for idx, (o_ref, (s, e)) in enumerate(zip(out_refs, segs)):
        acc = None
        k0 = 0
        for xk, k in zip(xb, ks):
            t = _dot(xk, w_ref[k0:k0 + k, s:e])
            acc = t if acc is None else acc + t
            k0 += k
        if has_res:
            acc = acc + r_ref[:, s:e]
        if idx in feat_major:
            rows = acc.shape[0] // o_ref.shape[0]
            for sq in range(o_ref.shape[0]):
                o_ref[sq] = acc[sq * rows:(sq + 1) * rows].T
        else:
            o_ref[...] = acc


def _fused_linear(xs, w, widths, gain=None, res=None, tm=256, feat_major=(), rows_per_seq=None):
    m = xs[0].shape[0]
    ks = tuple(x.shape[1] for x in xs)
    n = w.shape[1]
    segs = []
    s = 0
    for wd in widths:
        segs.append((s, s + wd))
        s += wd
    assert s == n and m % tm == 0
    out_specs, out_shape = [], []
    for idx, wd in enumerate(widths):
        if idx in feat_major:
            if rows_per_seq >= tm:
                tiles = rows_per_seq // tm
                assert rows_per_seq % tm == 0
                out_specs.append(pl.BlockSpec((1, wd, tm), lambda i, tiles=tiles: (i // tiles, 0, i % tiles)))
            else:
                assert tm % rows_per_seq == 0
                out_specs.append(pl.BlockSpec((tm // rows_per_seq, wd, rows_per_seq), lambda i: (i, 0, 0)))
            out_shape.append(jax.ShapeDtypeStruct((m // rows_per_seq, wd, rows_per_seq), F32))
        else:
            out_specs.append(pl.BlockSpec((tm, wd), lambda i: (i, 0)))
            out_shape.append(jax.ShapeDtypeStruct((m, wd), F32))
    in_specs = [pl.BlockSpec((tm, k), lambda i: (i, 0)) for k in ks]
    args = list(xs)
    if gain is not None:
        in_specs.append(_resident((1, ks[0])))
        args.append(gain.reshape(1, -1))
    in_specs.append(_resident((sum(ks), n)))
    args.append(w)
    if res is not None:
        in_specs.append(pl.BlockSpec((tm, n), lambda i: (i, 0)))
        args.append(res)
    return pl.pallas_call(
        functools.partial(_linear_body, ks=ks, segs=tuple(segs), has_norm=gain is not None, has_res=res is not None,
                          feat_major=tuple(feat_major)),
        grid=(m // tm,),
        in_specs=in_specs,
        out_specs=out_specs,
        out_shape=out_shape,
        compiler_params=_cparams("parallel"),
    )(*args)


def _ffn_body(*refs, final):
    if final:
        x_ref, g_ref, wg_ref, wu_ref, wd_ref, gf_ref, o_ref = refs
    else:
        x_ref, g_ref, wg_ref, wu_ref, wd_ref, o_ref = refs
    x = x_ref[...]
    hb = _rms(x, g_ref[...]).astype(BF16)
    acc = x
    for c in range(0, D_FF, FF_CHUNK):
        gate = _dot(hb, wg_ref[:, c:c + FF_CHUNK])
        up = _dot(hb, wu_ref[:, c:c + FF_CHUNK])
        act = (gate * _sigmoid(gate) * up).astype(BF16)
        acc = acc + _dot(act, wd_ref[c:c + FF_CHUNK, :])
    if final:
        acc = _rms(acc, gf_ref[...])
    o_ref[...] = acc


def _ffn(x, gain, wg, wu, wd, final_gain=None, tm=256):
    m = x.shape[0]
    final = final_gain is not None
    in_specs = [pl.BlockSpec((tm, D_MODEL), lambda i: (i, 0)), _resident((1, D_MODEL)),
                _resident((D_MODEL, D_FF)), _resident((D_MODEL, D_FF)), _resident((D_FF, D_MODEL))]
    args = [x, gain.reshape(1, -1), wg, wu, wd]
    if final:
        in_specs.append(_resident((1, D_MODEL)))
        args.append(final_gain.reshape(1, -1))
    return pl.pallas_call(
        functools.partial(_ffn_body, final=final),
        grid=(m // tm,),
        in_specs=in_specs,
        out_specs=pl.BlockSpec((tm, D_MODEL), lambda i: (i, 0)),
        out_shape=jax.ShapeDtypeStruct((m, D_MODEL), F32),
        compiler_params=_cparams("parallel"),
    )(*args)


def _cross_body(x_ref, oa_ref, ob_ref, oc_ref, od_ref, wout_ref, g_ref, wq_ref, mk_ref, mv_ref, wo_ref, o_ref):
    ns, tq, _ = x_ref.shape
    rows = ns * tq
    x = x_ref[...].reshape(rows, D_MODEL)
    for gi, m_ref in enumerate((oa_ref, ob_ref, oc_ref, od_ref)):
        x = x + _dot(m_ref[...].reshape(rows, GROUP_W).astype(BF16), wout_ref[gi * GROUP_W:(gi + 1) * GROUP_W, :])
    hb = _rms(x, g_ref[...]).astype(BF16)
    q = (_dot(hb, wq_ref[...]) * MEM_HEAD_DIM ** -0.5).astype(BF16)
    head = lambda h: slice(h * MEM_HEAD_DIM, (h + 1) * MEM_HEAD_DIM)
    pairs = [(sq, h) for sq in range(ns) for h in range(MEM_HEADS)]
    if len(mk_ref.shape) == 4:
        mem = lambda ref, sq, h: ref[sq, :, h, :].astype(BF16)
    else:
        mem = lambda ref, sq, h: ref[sq, :, head(h)].astype(BF16)
    scores = [_dot_nt(q[sq * tq:(sq + 1) * tq, head(h)], mem(mk_ref, sq, h)) for sq, h in pairs]
    probs = []
    for s in scores:
        p = jnp.exp(s - jnp.max(s, axis=-1, keepdims=True))
        probs.append((p / jnp.sum(p, axis=-1, keepdims=True)).astype(BF16))
    outs = [_dot(p, mem(mv_ref, sq, h)).astype(BF16) for p, (sq, h) in zip(probs, pairs)]
    o = jnp.concatenate([jnp.concatenate(outs[sq * MEM_HEADS:(sq + 1) * MEM_HEADS], axis=1) for sq in range(ns)], axis=0)
    o_ref[...] = (x + _dot(o, wo_ref[...])).reshape(ns, tq, D_MODEL)


def _cross(x, mix, w_out, gain, wq, mk, mv, layer, wo):
    b, t, _ = x.shape
    nm = mk.shape[2]
    tq = min(CROSS_ROWS, t)
    ns = max(CROSS_ROWS // t, 1)
    assert t % tq == 0 and b % ns == 0
    row_spec = lambda w: pl.BlockSpec((ns, tq, w), lambda i, j: (i, j, 0))
    mem_spec = pl.BlockSpec((None, ns) + mk.shape[2:], lambda i, j: (layer, i) + (0,) * (mk.ndim - 2))
    return pl.pallas_call(
        _cross_body,
        grid=(b // ns, t // tq),
        in_specs=[row_spec(D_MODEL)] + [row_spec(GROUP_W)] * 4 + [_resident((D_MODEL, D_MODEL)), _resident((1, D_MODEL)),
                  _resident((D_MODEL, D_MODEL)), mem_spec, mem_spec, _resident((D_MODEL, D_MODEL))],
        out_specs=row_spec(D_MODEL),
        out_shape=jax.ShapeDtypeStruct((b, t, D_MODEL), F32),
        compiler_params=_cparams("parallel", "parallel"),
    )(x, *mix, w_out, gain.reshape(1, -1), wq, mk, mv, wo)


def _diff_body(*refs, tq, tk, past, lam_init, gm):
    if past:
        lam_ref, q_ref, kn_ref, vn_ref, kp_ref, vp_ref, bn_ref, o_ref, q8_sc, m_sc, l_sc, acc_sc = refs
    else:
        lam_ref, q_ref, kn_ref, vn_ref, bn_ref, o_ref, q8_sc, m_sc, l_sc, acc_sc = refs
    i = pl.program_id(1)
    ng = 2 * N_HEADS
    q = q_ref[0] * (DIFF_HALF ** -0.5 * LOG2E)
    lane = _iota((tq, GROUP_W), 1)
    for g in range(ng):
        lo = (g // 2) * HEAD_DIM + (g % 2) * DIFF_HALF
        q8_sc[g * tq:(g + 1) * tq, :] = jnp.where((lane >= lo) & (lane < lo + DIFF_HALF), q, 0.0).astype(BF16)
    m_sc[...] = jnp.full(m_sc.shape, -jnp.inf, F32)
    l_sc[...] = jnp.zeros(l_sc.shape, F32)
    acc_sc[...] = jnp.zeros(acc_sc.shape, F32)
    pq = past + i * tq + _iota((tq, 1), 0)
    pq_f = pq.astype(F32)

    lanes = _lanes

    def block(k_ref, v_ref, k0, n, near, pos0=0):
        kb = k_ref[0, :, pl.ds(k0, n)].astype(BF16)
        vb = v_ref[0, :, pl.ds(k0, n)].astype(BF16)
        pk = pos0 + k0 + _iota((1, n), 1)
        pk_f = pk.astype(F32)
        if near:
            allowed = (pk >> 6) <= (pq >> 6)
            shift = pq_f - jnp.abs(pq_f - pk_f)
        for st in range(ng // gm):
            q_st = q8_sc[st * gm * tq:(st + 1) * gm * tq, :]
            s_st = _dot(q_st, kb)
            ps, alphas = [], []
            for gi in range(gm):
                g = st * gm + gi
                slope = 2.0 ** (-2 * (g // 2 + 1)) * LOG2E
                s = s_st[gi * tq:(gi + 1) * tq]
                if near:
                    s = jnp.where(allowed, s + slope * shift, NEG)
                else:
                    s = s + slope * pk_f
                m_prev = m_sc[g]
                m_new = jnp.maximum(m_prev, jnp.max(s, axis=-1, keepdims=True))
                p = jnp.exp2(s - lanes(m_new, n))
                alpha = jnp.exp2(m_prev - m_new)
                l_sc[g] = alpha * l_sc[g] + jnp.sum(p, axis=-1, keepdims=True)
                m_sc[g] = m_new
                ps.append(p.astype(BF16))
                alphas.append(alpha)
            p_st = ps[0] if gm == 1 else jnp.concatenate(ps, axis=0)
            pv = _dot_nt(p_st, vb)
            for gi in range(gm):
                g = st * gm + gi
                acc_sc[g] = acc_sc[g] * lanes(alphas[gi], GROUP_W) + pv[gi * tq:(gi + 1) * tq]

    if past:
        def past_step(j, carry):
            block(kp_ref, vp_ref, pl.multiple_of(j * tk, tk), tk, False)
            return carry
        lax.fori_loop(0, past // tk, past_step, 0, unroll=2)
        block(kn_ref, vn_ref, 0, tq, True, pos0=past)
    else:
        def prev_step(j, carry):
            block(kn_ref, vn_ref, pl.multiple_of(j * tk, tk), tk, False)
            return carry
        lax.fori_loop(0, (i * tq) // tk, prev_step, 0)

        @pl.when((i * tq) % tk != 0)
        def _():
            block(kn_ref, vn_ref, pl.multiple_of((i - 1) * tq, tq), tq, False)
        block(kn_ref, vn_ref, pl.multiple_of(i * tq, tq), tq, True)

    lam = lam_ref[0]
    lane_head = _iota((1, GROUP_W), 1) >> 6
    o = jnp.zeros((tq, GROUP_W), F32)
    for h in range(N_HEADS):
        o0 = acc_sc[2 * h] / lanes(l_sc[2 * h], GROUP_W)
        o1 = acc_sc[2 * h + 1] / lanes(l_sc[2 * h + 1], GROUP_W)
        o = jnp.where(lane_head == h, o0 - lam * o1, o)
    ms = _mm_r(o * o, _head_block_ones(GROUP_W).astype(BF16)) * (1.0 / HEAD_DIM)
    o_ref[0] = o * lax.rsqrt(ms + EPS) * bn_ref[...] * (1.0 - lam_init)


def _diff_attn(lam, q, kn, vn, kp, vp, layer, bnorm, lam_init):
    b, t, _ = q.shape
    past = 0 if kp is None else kp.shape[3]
    ng = 2 * N_HEADS
    if past:
        tq, tk, gm = t, 512, ng
        assert t == CHUNK and past % tk == 0
    else:
        tq = min(256, t)
        tk = 2 * tq if t % (2 * tq) == 0 else tq
        gm = 1
        assert t % tq == 0 and tq % CHUNK == 0
    seq_spec = lambda n: pl.BlockSpec((1, GROUP_W, n), lambda i, j: (i, 0, 0))
    past_spec = pl.BlockSpec((None, 1, GROUP_W, past), lambda i, j: (layer, i, 0, 0))
    in_specs = [pl.BlockSpec(memory_space=pltpu.SMEM),
                pl.BlockSpec((1, tq, GROUP_W), lambda i, j: (i, j, 0)), seq_spec(t), seq_spec(t)]
    args = [lam, q, kn, vn]
    if past:
        in_specs += [past_spec, past_spec]
        args += [kp, vp]
    in_specs.append(_resident((1, GROUP_W)))
    args.append(bnorm)
    return pl.pallas_call(
        functools.partial(_diff_body, tq=tq, tk=tk, past=past, lam_init=lam_init, gm=gm),
        grid=(b, t // tq),
        in_specs=in_specs,
        out_specs=pl.BlockSpec((1, tq, GROUP_W), lambda i, j: (i, j, 0)),
        out_shape=jax.ShapeDtypeStruct((b, t, GROUP_W), F32),
        scratch_shapes=[pltpu.VMEM((ng * tq, GROUP_W), BF16), pltpu.VMEM((ng, tq, LANES), F32),
                        pltpu.VMEM((ng, tq, LANES), F32), pltpu.VMEM((ng, tq, GROUP_W), F32)],
        compiler_params=_cparams("parallel", "parallel"),
    )(*args)


def _stick_body(*refs, tq, tk, past):
    if past:
        q_ref, kn_ref, vn_ref, kp_ref, vp_ref, o_ref, q4_sc, c_sc, acc_sc = refs
    else:
        q_ref, kn_ref, vn_ref, o_ref, q4_sc, c_sc, acc_sc = refs
    i = pl.program_id(1)
    nr = N_HEADS * tq
    q = q_ref[0] * (HEAD_DIM ** -0.5 * LOG2E)
    lane = _iota((tq, GROUP_W), 1) >> 6
    for h in range(N_HEADS):
        q4_sc[h * tq:(h + 1) * tq, :] = jnp.where(lane == h, q, 0.0).astype(BF16)
    c_sc[...] = jnp.zeros(c_sc.shape, F32)
    acc_sc[...] = jnp.zeros(acc_sc.shape, F32)
    pq = past + i * tq + (_iota((nr, 1), 0) & (tq - 1))
    laters = {w: (_iota((w, w), 0) > _iota((w, w), 1)).astype(BF16)
              for w in {min(tq, STICK_SUB), min(tk, STICK_SUB)}}

    def block(k_ref, v_ref, k0, n, masked, pos0=0):
        kb = k_ref[0, :, pl.ds(k0, n)].astype(BF16)
        vb = v_ref[0, :, pl.ds(k0, n)].astype(BF16)
        sub = min(n, STICK_SUB)
        later = laters[sub]
        z = _dot(q4_sc[...], kb)
        ls = jnp.minimum(z, 0.0) - jnp.log2(1.0 + jnp.exp2(-jnp.abs(z)))
        lk = ls - z
        if masked:
            mask = (pos0 + k0 + _iota((1, n), 1)) < pq
            lk = jnp.where(mask, lk, 0.0)
        hi = lk.astype(BF16)
        lo = (lk - hi.astype(F32)).astype(BF16)
        carry = c_sc[...]
        es = []
        for sb in reversed(range(n // sub)):
            sl = slice(sb * sub, (sb + 1) * sub)
            after = _dot(hi[:, sl], later) + _dot(lo[:, sl], later)
            es.append(ls[:, sl] + after + _lanes(carry, sub))
            carry = carry + (after[:, 0:1] + lk[:, sb * sub:sb * sub + 1])
        c_sc[...] = carry
        e = es[0] if len(es) == 1 else jnp.concatenate(es[::-1], axis=1)
        if masked:
            a = jnp.where(mask, jnp.exp2(jnp.where(mask, e, 0.0)), 0.0)
        else:
            a = jnp.exp2(e)
        acc_sc[...] = acc_sc[...] + _dot_nt(a.astype(BF16), vb)

    if past:
        block(kn_ref, vn_ref, 0, tq, True, pos0=past)
        nb = past // tk

        def past_step(j, carry):
            block(kp_ref, vp_ref, pl.multiple_of((nb - 1 - j) * tk, tk), tk, False)
            return carry
        lax.fori_loop(0, nb, past_step, 0)
    else:
        block(kn_ref, vn_ref, pl.multiple_of(i * tq, tq), tq, True)

        @pl.when((i * tq) % tk != 0)
        def _():
            block(kn_ref, vn_ref, pl.multiple_of((i - 1) * tq, tq), tq, False)
        nfull = (i * tq) // tk

        def prev_step(j, carry):
            block(kn_ref, vn_ref, pl.multiple_of((nfull - 1 - j) * tk, tk), tk, False)
            return carry
        lax.fori_loop(0, nfull, prev_step, 0)

    o = acc_sc[0:tq, :]
    for h in range(1, N_HEADS):
        o = jnp.where(lane == h, acc_sc[h * tq:(h + 1) * tq, :], o)
    o_ref[0] = o


def _stick_attn(q, kn, vn, kp, vp, layer):
    b, t, _ = q.shape
    past = 0 if kp is None else kp.shape[3]
    if past:
        tq, tk = t, math.gcd(past, 1024)
    else:
        tq = min(256, t)
        tk = 2 * tq if t % (2 * tq) == 0 else tq
        assert t % tq == 0
    assert tq & (tq - 1) == 0 and tk % min(tk, STICK_SUB) == 0
    seq_spec = lambda n: pl.BlockSpec((1, GROUP_W, n), lambda i, j: (i, 0, 0))
    past_spec = pl.BlockSpec((None, 1, GROUP_W, past), lambda i, j: (layer, i, 0, 0))
    in_specs = [pl.BlockSpec((1, tq, GROUP_W), lambda i, j: (i, j, 0)), seq_spec(t), seq_spec(t)]
    args = [q, kn, vn]
    if past:
        in_specs += [past_spec, past_spec]
        args += [kp, vp]
    return pl.pallas_call(
        functools.partial(_stick_body, tq=tq, tk=tk, past=past),
        grid=(b, t // tq),
        in_specs=in_specs,
        out_specs=pl.BlockSpec((1, tq, GROUP_W), lambda i, j: (i, j, 0)),
        out_shape=jax.ShapeDtypeStruct((b, t, GROUP_W), F32),
        scratch_shapes=[pltpu.VMEM((N_HEADS * tq, GROUP_W), BF16), pltpu.VMEM((N_HEADS * tq, LANES), F32),
                        pltpu.VMEM((N_HEADS * tq, GROUP_W), F32)],
        compiler_params=_cparams("parallel", "parallel"),
    )(*args)


def _pair_mask(n, s):
    r, c = _iota((n, n), 0), _iota((n, n), 1)
    return ((r // (2 * s)) == (c // (2 * s))) & (((r // s) & 1) == 1) & (((c // s) & 1) == 0)


def _head_rows(x, lane_head):
    return jnp.concatenate([jnp.where(lane_head == h, x, 0.0) for h in range(N_HEADS)], axis=0)


def _head_diag(x, lane_head):
    n = x.shape[0] // N_HEADS
    out = x[:n]
    for h in range(1, N_HEADS):
        out = jnp.where(lane_head == h, x[h * n:(h + 1) * n], out)
    return out


def _gdn_body(x_ref, z_ref, abg_ref, cw_ref, cs_ref, s0_ref, alog_ref, dtb_ref, an_ref,
              o_ref, sout_ref, s_sc, xb_sc, *, nc, nb):
    c = pl.program_id(1)
    L = CHUNK
    pad = 8

    @pl.when(c == 0)
    def _():
        s_sc[...] = s0_ref[...]
        xb_sc[:, pad - 3:pad, :] = cs_ref[...]

    ones_bd = _head_block_ones(GROUP_W)
    ones_bd16 = ones_bd.astype(BF16)
    er, ec = _iota((ABG_W, GROUP_W), 0), _iota((ABG_W, GROUP_W), 1) >> 6
    e_beta = (er == ec).astype(BF16)
    e_g = (er == ec + N_HEADS).astype(BF16)
    ri, ci = _iota((L, L), 0), _iota((L, L), 1)
    tri16 = (ri >= ci).astype(BF16)
    incl = ri >= ci
    strict = ri > ci
    eye = (ri == ci).astype(F32)
    pair_masks = [_pair_mask(L, sz) for sz in (1, 2, 4, 8, 16, 32)]
    lane_head = _iota((1, GROUP_W), 1) >> 6
    cw = cw_ref[...]

    a_all, per_seq = [], []
    for b in range(nb):
        x = x_ref[b]
        xb_sc[b, pad:pad + L, :] = x
        u = (x * cw[3:4] + xb_sc[b, pad - 1:pad - 1 + L, :] * cw[2:3]
             + xb_sc[b, pad - 2:pad - 2 + L, :] * cw[1:2] + xb_sc[b, pad - 3:pad - 3 + L, :] * cw[0:1])
        xb_sc[b, pad - 3:pad, :] = x[L - 3:L, :]
        u = u * _sigmoid(u)
        q = u[:, :GROUP_W]
        k = u[:, GROUP_W:2 * GROUP_W]
        v = u[:, 2 * GROUP_W:]
        q = q * lax.rsqrt(_mm_r(q * q, ones_bd16) + EPS) * HEAD_DIM ** -0.5
        k = k * lax.rsqrt(_mm_r(k * k, ones_bd16) + EPS)

        abg = abg_ref[b]
        beta_n = _sigmoid(abg)
        g_n = -jnp.exp(alog_ref[...]) * _softplus(abg + dtb_ref[...])
        beta = _mm_r(beta_n, e_beta)
        gc_n = _mm_l(tri16, g_n)
        gc_t = gc_n.T
        gc = _mm_r(gc_n, e_g)
        exp_g = jnp.exp(gc)
        g_last = gc[L - 1:L, :]

        kb = k * beta
        k16 = k.astype(BF16)
        a4 = _bdot(_head_rows(kb, lane_head).astype(BF16), k16, _NT)
        qk4 = _bdot(_head_rows(q, lane_head).astype(BF16), k16, _NT)
        dec = []
        for h in range(N_HEADS):
            diff = gc_n[:, N_HEADS + h:N_HEADS + h + 1] - gc_t[N_HEADS + h:N_HEADS + h + 1, :]
            dec.append(jnp.where(incl, jnp.exp(jnp.where(incl, diff, 0.0)), 0.0))
        dec = jnp.stack(dec)
        a_all.append(jnp.where(strict, a4.reshape(N_HEADS, L, L) * dec, 0.0))
        per_seq.append((q, k, v * beta, kb * exp_g, q * exp_g, k * jnp.exp(g_last - gc), jnp.exp(g_last),
                        qk4.reshape(N_HEADS, L, L) * dec))

    a_all = jnp.concatenate(a_all, axis=0)
    t_inv = eye - jnp.where(pair_masks[0], a_all, 0.0)
    a16 = a_all.astype(BF16)
    for pm in pair_masks[1:]:
        t16 = t_inv.astype(BF16)
        t_inv = t_inv - _bdot(t16, _bdot(jnp.where(pm, a16, 0.0), t16, _BNN).astype(BF16), _BNN)

    for b in range(nb):
        q, k, vb, kbg, qg, k_dec, decay_last, qk = per_seq[b]
        s = s_sc[b]
        s16 = s.astype(BF16)
        rhs = vb - _bdot(kbg.astype(BF16), s16, _NN)
        t4 = t_inv[b * N_HEADS:(b + 1) * N_HEADS].reshape(N_HEADS * L, L)
        w = _head_diag(_mm1(t4, rhs), lane_head)
        o = _bdot(qg.astype(BF16), s16, _NN) + _head_diag(_mm1(qk.reshape(N_HEADS * L, L), w), lane_head)
        s_sc[b] = s * decay_last + _mm1(k_dec.T, w) * ones_bd

        ms = _mm_r(o * o, ones_bd16) * (1.0 / HEAD_DIM)
        zg = z_ref[b]
        o_ref[b] = o * lax.rsqrt(ms + EPS) * an_ref[...] * (zg * _sigmoid(zg))

    @pl.when(c == nc - 1)
    def _():
        sout_ref[...] = s_sc[...]


def _gdn(aqkv, az, abg, conv_w, conv_state, s0, alog_row, dtb_row, anorm_row):
    b, t, _ = aqkv.shape
    nc = t // CHUNK
    nb = SEQ_PER_STEP
    assert b % nb == 0
    cmap = lambda i, j: (i, j, 0)
    bmap = lambda i, j: (i, 0, 0)
    return pl.pallas_call(
        functools.partial(_gdn_body, nc=nc, nb=nb),
        grid=(b // nb, nc),
        in_specs=[pl.BlockSpec((nb, CHUNK, 3 * GROUP_W), cmap), pl.BlockSpec((nb, CHUNK, GROUP_W), cmap),
                  pl.BlockSpec((nb, CHUNK, ABG_W), cmap), _resident((CONV_W, 3 * GROUP_W)),
                  pl.BlockSpec((nb, CONV_W - 1, 3 * GROUP_W), bmap), pl.BlockSpec((nb, GROUP_W, GROUP_W), bmap),
                  _resident((1, ABG_W)), _resident((1, ABG_W)), _resident((1, GROUP_W))],
        out_specs=[pl.BlockSpec((nb, CHUNK, GROUP_W), cmap), pl.BlockSpec((nb, GROUP_W, GROUP_W), bmap)],
        out_shape=[jax.ShapeDtypeStruct((b, t, GROUP_W), F32), jax.ShapeDtypeStruct((b, GROUP_W, GROUP_W), F32)],
        scratch_shapes=[pltpu.VMEM((nb, GROUP_W, GROUP_W), F32), pltpu.VMEM((nb, 8 + CHUNK, 3 * GROUP_W), F32)],
        compiler_params=_cparams("parallel", "arbitrary"),
    )(aqkv, az, abg, conv_w, conv_state, s0, alog_row, dtb_row, anorm_row)


def _hgrn_body(q_ref, f_ref, i_ref, g_ref, lb_ref, dn_ref, s0_ref, o_ref, sout_ref, s_sc, *, nc, nb):
    c = pl.program_id(1)
    L = CHUNK

    @pl.when(c == 0)
    def _():
        s_sc[...] = s0_ref[...]

    lb = lb_ref[...]
    ones_bd = _head_block_ones(GROUP_W)
    ones_bd16 = ones_bd.astype(BF16)
    ri, ci = _iota((L, L), 0), _iota((L, L), 1)
    tri16 = (ri >= ci).astype(BF16)
    eye = ri == ci
    sizes = (1, 2, 4, 8, 16, 32)
    sel16 = jnp.concatenate([(ci == (ri // (2 * sz)) * (2 * sz) + sz).astype(BF16) for sz in sizes], axis=0)
    pair_masks = [_pair_mask(L, sz) for sz in sizes]
    lane_head = _iota((1, GROUP_W), 1) >> 6
    rows = _iota((L, 1), 0)

    for b in range(nb):
        fl = f_ref[b]
        log_f = _log_sigmoid(fl) + jnp.log(1.0 + lb * jnp.exp(-fl))
        k = (1.0 - lb) * _sigmoid(-fl)
        q = q_ref[b] * HEAD_DIM ** -0.5
        v = i_ref[b]
        bc = _mm_l(tri16, log_f)
        b_ref_all = _mm_l(sel16, bc)

        a4 = _mm1(_head_rows(q, lane_head), k, _NT)
        a = [jnp.where(eye, a4[h * L:(h + 1) * L], 0.0) for h in range(N_HEADS)]
        for lv, sz in enumerate(sizes):
            b_ref = b_ref_all[lv * L:(lv + 1) * L]
            later = ((rows // sz) & 1) == 1
            x = jnp.where(later, q * jnp.exp(jnp.where(later, bc - b_ref, 0.0)), 0.0)
            y = jnp.where(later, 0.0, k * jnp.exp(jnp.where(later, 0.0, b_ref - bc)))
            a4 = _mm1(_head_rows(x, lane_head), y, _NT)
            for h in range(N_HEADS):
                a[h] = a[h] + jnp.where(pair_masks[lv], a4[h * L:(h + 1) * L], 0.0)

        st = s_sc[b]
        o = _mm1(q * jnp.exp(bc), st, _NT) + _head_diag(_mm1(jnp.concatenate(a, axis=0), v), lane_head)
        b_last = bc[L - 1:L, :]
        st_new = st * jnp.exp(b_last) + _mm1(v.T, k * jnp.exp(b_last - bc)) * ones_bd
        s_sc[b] = st_new

        ms = _mm_r(o * o, ones_bd16) * (1.0 / HEAD_DIM)
        zg = g_ref[b]
        o_ref[b] = o * lax.rsqrt(ms + EPS) * dn_ref[...] * (zg * _sigmoid(zg))

    @pl.when(c == nc - 1)
    def _():
        sout_ref[...] = s_sc[...]


def _hgrn(dq, df, di, dg, lb_row, dnorm_row, s0t):
    b, t, _ = dq.shape
    nc = t // CHUNK
    nb = SEQ_PER_STEP
    assert b % nb == 0
    cmap = lambda i, j: (i, j, 0)
    bmap = lambda i, j: (i, 0, 0)
    cspec = pl.BlockSpec((nb, CHUNK, GROUP_W), cmap)
    return pl.pallas_call(
        functools.partial(_hgrn_body, nc=nc, nb=nb),
        grid=(b // nb, nc),
        in_specs=[cspec, cspec, cspec, cspec, _resident((1, GROUP_W)), _resident((1, GROUP_W)),
                  pl.BlockSpec((nb, GROUP_W, GROUP_W), bmap)],
        out_specs=[cspec, pl.BlockSpec((nb, GROUP_W, GROUP_W), bmap)],
        out_shape=[jax.ShapeDtypeStruct((b, t, GROUP_W), F32), jax.ShapeDtypeStruct((b, GROUP_W, GROUP_W), F32)],
        scratch_shapes=[pltpu.VMEM((nb, GROUP_W, GROUP_W), F32)],
        compiler_params=_cparams("parallel", "arbitrary"),
    )(dq, df, di, dg, lb_row, dnorm_row, s0t)


def _to_block_diag(s, transpose):
    if transpose:
        s = jnp.swapaxes(s, -1, -2)
    b = s.shape[0]
    eye = jnp.eye(N_HEADS, dtype=s.dtype)
    return (s[:, :, :, None, :] * eye[None, :, None, :, None]).reshape(b, GROUP_W, GROUP_W)


def _from_block_diag(s, transpose):
    b = s.shape[0]
    s5 = s.reshape(b, N_HEADS, HEAD_DIM, N_HEADS, HEAD_DIM)
    out = jnp.stack([s5[:, h, :, h, :] for h in range(N_HEADS)], axis=1)
    return jnp.swapaxes(out, -1, -2) if transpose else out


IN_WIDTHS = (3 * GROUP_W,) + (GROUP_W,) * 11 + (ABG_W,)
KV_SEGMENTS = (3, 4, 6, 7)


def _layer(x, l, st, mk, mv, mem_layer, prm, final_gain):
    b, t, _ = x.shape
    m = b * t
    tm = 256
    outs = _fused_linear([x.reshape(m, D_MODEL)], prm["w_in"][l], IN_WIDTHS, gain=prm["norm_mix"][l], tm=tm,
                         feat_major=KV_SEGMENTS, rows_per_seq=t)
    outs = [o if idx in KV_SEGMENTS else o.reshape(b, t, -1) for idx, o in enumerate(outs)]
    (a_qkv, a_z, b_q, b_k, b_v, c_q, c_k, c_v, d_q, d_f, d_i, d_g, a_bg) = outs

    o_a, a_s = _gdn(a_qkv, a_z, a_bg, prm["a_conv_w"][l], st["a_conv"], _to_block_diag(st["a_S"], False),
                    prm["alog_row"][l], prm["dtb_row"][l], prm["a_norm"][l])
    o_b = _diff_attn(prm["lam"][l], b_q, b_k, b_v, st["b_k"], st["b_v"], l, prm["b_norm"][l],
                     0.8 - 0.6 * math.exp(-0.3 * l))
    o_c = _stick_attn(c_q, c_k, c_v, st["c_k"], st["c_v"], l)
    o_d, d_s = _hgrn(d_q, d_f, d_i, d_g, prm["lb"][l], prm["d_norm"][l], _to_block_diag(st["d_S"], True))

    x2 = _cross(x, (o_a, o_b, o_c, o_d), prm["w_out"][l], prm["norm_cross"][l], prm["w_cq"][l], mk, mv, mem_layer,
                prm["w_co"][l])
    x2 = _ffn(x2.reshape(m, D_MODEL), prm["norm_ffn"][l], prm["w_gate"][l], prm["w_up"][l], prm["w_down"][l],
              final_gain=final_gain, tm=tm)

    if t >= CONV_W - 1:
        conv_new = a_qkv[:, t - (CONV_W - 1):, :]
    else:
        conv_new = jnp.concatenate([st["a_conv"], a_qkv], axis=1)[:, -(CONV_W - 1):, :]
    heads = lambda a: jnp.transpose(a.reshape(b, N_HEADS, HEAD_DIM, t), (0, 3, 1, 2))
    new = (conv_new, _from_block_diag(a_s, False), heads(b_k), heads(b_v), heads(c_k), heads(c_v),
           _from_block_diag(d_s, True))
    return x2.reshape(b, t, D_MODEL), new


def kernel(x_prompt, x_sample, mem_prompt, state_a_conv, state_a_S, cache_b_k, cache_b_v, cache_c_k, cache_c_v,
           state_d_S, cache_mem_k, cache_mem_v, norm_mix, w_in, a_conv_w, a_A_log, a_dt_bias, a_norm, b_lam_q1,
           b_lam_k1, b_lam_q2, b_lam_k2, b_norm, d_lb, d_norm, w_out, norm_cross, norm_memtok, w_cq, w_ck, w_cv,
           w_co, norm_ffn, w_gate, w_up, w_down, norm_final):
    depth = w_in.shape[0]
    n_main = 4 * GROUP_W
    w_in_p = jnp.concatenate(
        [w_in[:, :, :n_main], w_in[:, :, n_main + 2 * N_HEADS:], w_in[:, :, n_main:n_main + 2 * N_HEADS],
         jnp.zeros((depth, D_MODEL, ABG_W - 2 * N_HEADS), w_in.dtype)], axis=2).astype(BF16)
    pad_row = lambda a: jnp.pad(a.astype(F32), ((0, 0), (N_HEADS, ABG_W - 2 * N_HEADS)))[:, None, :]
    tile_row = lambda a: jnp.tile(a.astype(F32), (1, N_HEADS))[:, None, :]
    p_lb = jax.nn.softmax(d_lb.astype(F32), axis=0)
    lam = (jnp.exp(jnp.sum(b_lam_q1.astype(F32) * b_lam_k1.astype(F32), axis=-1))
           - jnp.exp(jnp.sum(b_lam_q2.astype(F32) * b_lam_k2.astype(F32), axis=-1))
           + jnp.array([0.8 - 0.6 * math.exp(-0.3 * l) for l in range(depth)], F32))
    prm = {
        "w_in": w_in_p, "norm_mix": norm_mix, "a_conv_w": a_conv_w.astype(F32),
        "alog_row": pad_row(a_A_log), "dtb_row": pad_row(a_dt_bias), "a_norm": tile_row(a_norm),
        "lam": lam[:, None], "b_norm": tile_row(b_norm), "lb": (jnp.cumsum(p_lb, axis=0) - p_lb[0])[:, None, :],
        "d_norm": tile_row(d_norm), "w_out": w_out.astype(BF16), "norm_cross": norm_cross,
        "w_cq": w_cq.astype(BF16), "w_co": w_co.astype(BF16), "norm_ffn": norm_ffn,
        "w_gate": w_gate.astype(BF16), "w_up": w_up.astype(BF16), "w_down": w_down.astype(BF16),
    }
    w_ckv = jnp.concatenate([w_ck, w_cv], axis=2).astype(BF16)

    bp, tp, _ = x_prompt.shape
    n_mem = mem_prompt.shape[1]
    h = x_prompt
    p_new = []
    for l in range(depth):
        mk, mv = _fused_linear([mem_prompt.reshape(bp * n_mem, D_MODEL)], w_ckv[l], (D_MODEL, D_MODEL),
                               gain=norm_memtok[l], tm=256)
        mk = mk.reshape(1, bp, n_mem, D_MODEL)
        mv = mv.reshape(1, bp, n_mem, D_MODEL)
        st = {"a_conv": jnp.zeros((bp, CONV_W - 1, 3 * GROUP_W), F32),
              "a_S": jnp.zeros((bp, N_HEADS, HEAD_DIM, HEAD_DIM), F32), "b_k": None, "b_v": None,
              "c_k": None, "c_v": None, "d_S": jnp.zeros((bp, N_HEADS, HEAD_DIM, HEAD_DIM), F32)}
        h, new = _layer(h, l, st, mk, mv, 0, prm, norm_final if l == depth - 1 else None)
        mem4 = lambda a: a.reshape(bp, n_mem, MEM_HEADS, MEM_HEAD_DIM)
        p_new.append(new + (mem4(mk), mem4(mv)))
    y_prompt = h
    p_stacked = [jnp.stack(c) for c in zip(*p_new)]

    bs, ts, _ = x_sample.shape
    past = cache_b_k.shape[2]
    h = x_sample
    s_new = []
    flat = lambda a: jnp.transpose(a, (0, 1, 3, 4, 2)).reshape(depth, bs, GROUP_W, past)
    caches = {"b_k": flat(cache_b_k), "b_v": flat(cache_b_v), "c_k": flat(cache_c_k), "c_v": flat(cache_c_v)}
    for l in range(depth):
        st = {"a_conv": state_a_conv[l], "a_S": state_a_S[l], "d_S": state_d_S[l], **caches}
        h, new = _layer(h, l, st, cache_mem_k, cache_mem_v, l, prm, norm_final if l == depth - 1 else None)
        s_new.append(new)
    y_sample = h
    s_stacked = [jnp.stack(c) for c in zip(*s_new)]

    return (y_prompt, y_sample, *p_stacked, *s_stacked)
```

```python
import functools
import math

import jax
import jax.numpy as jnp
from jax import lax
from jax.experimental import pallas as pl
from jax.experimental.pallas import tpu as pltpu

F32 = jnp.float32
BF16 = jnp.bfloat16

D_MODEL = 1024
GROUP_W = 256
N_HEADS = 4
HEAD_DIM = 64
DIFF_HALF = 32
CHUNK = 64
CONV_W = 4
MEM_HEADS = 4
MEM_HEAD_DIM = 256
D_FF = 2816
EPS = 1e-6
NEG = -1e30
LOG2E = 1.4426950408889634
LANES = 128
ABG_W = 128
FF_CHUNK = 256
ROW_TILE = 512
CROSS_SEQS = 4
STICK_SUB = 256
SEQ_PER_STEP = 4
VMEM_LIMIT = 56 * 1024 * 1024


def _cparams(*sem):
    return pltpu.CompilerParams(dimension_semantics=sem, vmem_limit_bytes=VMEM_LIMIT)


def _dot(a, b):
    return jnp.dot(a, b, preferred_element_type=F32)


def _dot_nt(a, b):
    return lax.dot_general(a, b, (((1,), (1,)), ((), ())), preferred_element_type=F32)


_NN = (((1,), (0,)), ((), ()))
_NT = (((1,), (1,)), ((), ()))
_BNN = (((2,), (1,)), ((0,), (0,)))


def _split(x):
    hi = x.astype(BF16)
    return hi, (x - hi.astype(F32)).astype(BF16)


def _bdot(a, b, dims):
    return lax.dot_general(a, b, dims, preferred_element_type=F32)


def _mm(a, b, dims=_NN):
    ah, al = a if isinstance(a, tuple) else _split(a)
    bh, bl = b if isinstance(b, tuple) else _split(b)
    return _bdot(ah, bh, dims) + (_bdot(ah, bl, dims) + _bdot(al, bh, dims))


def _mm1(a, b, dims=_NN):
    return _bdot(a.astype(BF16), b.astype(BF16), dims)


def _mm_r(a, b16):
    ah, al = _split(a)
    return _bdot(ah, b16, _NN) + _bdot(al, b16, _NN)


def _mm_l(a16, b):
    bh, bl = _split(b)
    return _bdot(a16, bh, _NN) + _bdot(a16, bl, _NN)


def _rms(x, g):
    return x * lax.rsqrt(jnp.mean(x * x, axis=-1, keepdims=True) + EPS) * g


def _sigmoid(x):
    return 1.0 / (1.0 + jnp.exp(-x))


def _log_sigmoid(x):
    return jnp.minimum(x, 0.0) - jnp.log(1.0 + jnp.exp(-jnp.abs(x)))


def _softplus(x):
    return jnp.maximum(x, 0.0) + jnp.log(1.0 + jnp.exp(-jnp.abs(x)))


def _iota(shape, dim):
    return lax.broadcasted_iota(jnp.int32, shape, dim)


def _lanes(x, n):
    return x[:, :n] if n <= LANES else jnp.concatenate([x] * (n // LANES), axis=1)


def _head_block_ones(n):
    return ((_iota((n, n), 0) >> 6) == (_iota((n, n), 1) >> 6)).astype(F32)


def _resident(shape):
    return pl.BlockSpec(shape, lambda *_: (0,) * len(shape), pipeline_mode=pl.Buffered(1))


def _linear_body(*refs, ks, segs, has_norm, has_res, feat_major):
    n_in = len(ks)
    x_refs = refs[:n_in]
    pos = n_in
    g_ref = None
    if has_norm:
        g_ref = refs[pos]
        pos += 1
    w_ref = refs[pos]
    pos += 1
    r_ref = None
    if has_res:
        r_ref = refs[pos]
        pos += 1
    out_refs = refs[pos:]
    xb = []
    for x_ref in x_refs:
        x = x_ref[...].astype(F32)
        if has_norm:
            x = _rms(x, g_ref[...])
        xb.append(x.astype(BF16))
    for idx, (o_ref, (s, e)) in enumerate(zip(out_refs, segs)):
        acc = None
        k0 = 0
        for xk, k in zip(xb, ks):
            t = _dot(xk, w_ref[k0:k0 + k, s:e])
            acc = t if acc is None else acc + t
            k0 += k
        if has_res:
            acc = acc + r_ref[:, s:e]
        if idx in feat_major:
            rows = acc.shape[0] // o_ref.shape[0]
            for sq in range(o_ref.shape[0]):
                o_ref[sq] = acc[sq * rows:(sq + 1) * rows].T
        else:
            o_ref[...] = acc


def _fused_linear(xs, w, widths, gain=None, res=None, tm=256, feat_major=(), rows_per_seq=None):
    m = xs[0].shape[0]
    ks = tuple(x.shape[1] for x in xs)
    n = w.shape[1]
    segs = []
    s = 0
    for wd in widths:
        segs.append((s, s + wd))
        s += wd
    assert s == n and m % tm == 0
    out_specs, out_shape = [], []
    for idx, wd in enumerate(widths):
        if idx in feat_major:
            if rows_per_seq >= tm:
                tiles = rows_per_seq // tm
                assert rows_per_seq % tm == 0
                out_specs.append(pl.BlockSpec((1, wd, tm), lambda i, tiles=tiles: (i // tiles, 0, i % tiles)))
            else:
                assert tm % rows_per_seq == 0
                out_specs.append(pl.BlockSpec((tm // rows_per_seq, wd, rows_per_seq), lambda i: (i, 0, 0)))
            out_shape.append(jax.ShapeDtypeStruct((m // rows_per_seq, wd, rows_per_seq), F32))
        else:
            out_specs.append(pl.BlockSpec((tm, wd), lambda i: (i, 0)))
            out_shape.append(jax.ShapeDtypeStruct((m, wd), F32))
    in_specs = [pl.BlockSpec((tm, k), lambda i: (i, 0)) for k in ks]
    args = list(xs)
    if gain is not None:
        in_specs.append(_resident((1, ks[0])))
        args.append(gain.reshape(1, -1))
    in_specs.append(_resident((sum(ks), n)))
    args.append(w)
    if res is not None:
        in_specs.append(pl.BlockSpec((tm, n), lambda i: (i, 0)))
        args.append(res)
    return pl.pallas_call(
        functools.partial(_linear_body, ks=ks, segs=tuple(segs), has_norm=gain is not None, has_res=res is not None,
                          feat_major=tuple(feat_major)),
        grid=(m // tm,),
        in_specs=in_specs,
        out_specs=out_specs,
        out_shape=out_shape,
        compiler_params=_cparams("parallel"),
    )(*args)


def _ffn_body(*refs, final):
    if final:
        x_ref, g_ref, wg_ref, wu_ref, wd_ref, gf_ref, o_ref = refs
    else:
        x_ref, g_ref, wg_ref, wu_ref, wd_ref, o_ref = refs
    x = x_ref[...]
    hb = _rms(x, g_ref[...]).astype(BF16)
    acc = x
    for c in range(0, D_FF, FF_CHUNK):
        gate = _dot(hb, wg_ref[:, c:c + FF_CHUNK])
        up = _dot(hb, wu_ref[:, c:c + FF_CHUNK])
        act = (gate * _sigmoid(gate) * up).astype(BF16)
        acc = acc + _dot(act, wd_ref[c:c + FF_CHUNK, :])
    if final:
        acc = _rms(acc, gf_ref[...])
    o_ref[...] = acc


def _ffn(x, gain, wg, wu, wd, final_gain=None, tm=256):
    m = x.shape[0]
    final = final_gain is not None
    in_specs = [pl.BlockSpec((tm, D_MODEL), lambda i: (i, 0)), _resident((1, D_MODEL)),
                _resident((D_MODEL, D_FF)), _resident((D_MODEL, D_FF)), _resident((D_FF, D_MODEL))]
    args = [x, gain.reshape(1, -1), wg, wu, wd]
    if final:
        in_specs.append(_resident((1, D_MODEL)))
        args.append(final_gain.reshape(1, -1))
    return pl.pallas_call(
        functools.partial(_ffn_body, final=final),
        grid=(m // tm,),
        in_specs=in_specs,
        out_specs=pl.BlockSpec((tm, D_MODEL), lambda i: (i, 0)),
        out_shape=jax.ShapeDtypeStruct((m, D_MODEL), F32),
        compiler_params=_cparams("parallel"),
    )(*args)


def _cross_body(x_ref, oa_ref, ob_ref, oc_ref, od_ref, wout_ref, g_ref, wq_ref, mk_ref, mv_ref, wo_ref, o_ref):
    ns, tq, _ = x_ref.shape
    rows = ns * tq
    x = x_ref[...].reshape(rows, D_MODEL)
    for gi, m_ref in enumerate((oa_ref, ob_ref, oc_ref, od_ref)):
        x = x + _dot(m_ref[...].reshape(rows, GROUP_W).astype(BF16), wout_ref[gi * GROUP_W:(gi + 1) * GROUP_W, :])
    hb = _rms(x, g_ref[...]).astype(BF16)
    q = (_dot(hb, wq_ref[...]) * MEM_HEAD_DIM ** -0.5).astype(BF16)
    head = lambda h: slice(h * MEM_HEAD_DIM, (h + 1) * MEM_HEAD_DIM)
    pairs = [(sq, h) for sq in range(ns) for h in range(MEM_HEADS)]
    if len(mk_ref.shape) == 4:
        mem = lambda ref, sq, h: ref[sq, :, h, :].astype(BF16)
    else:
        mem = lambda ref, sq, h: ref[sq, :, head(h)].astype(BF16)
    scores = [_dot_nt(q[sq * tq:(sq + 1) * tq, head(h)], mem(mk_ref, sq, h)) for sq, h in pairs]
    probs = []
    for s in scores:
        p = jnp.exp(s - jnp.max(s, axis=-1, keepdims=True))
        probs.append((p / jnp.sum(p, axis=-1, keepdims=True)).astype(BF16))
    outs = [_dot(p, mem(mv_ref, sq, h)).astype(BF16) for p, (sq, h) in zip(probs, pairs)]
    o = jnp.concatenate([jnp.concatenate(outs[sq * MEM_HEADS:(sq + 1) * MEM_HEADS], axis=1) for sq in range(ns)], axis=0)
    o_ref[...] = (x + _dot(o, wo_ref[...])).reshape(ns, tq, D_MODEL)


def _cross(x, mix, w_out, gain, wq, mk, mv, layer, wo):
    b, t, _ = x.shape
    nm = mk.shape[2]
    tq = min(ROW_TILE, t)
    ns = CROSS_SEQS if CROSS_SEQS * t <= ROW_TILE else 1
    assert t % tq == 0 and b % ns == 0
    row_spec = lambda w: pl.BlockSpec((ns, tq, w), lambda i, j: (i, j, 0))
    mem_spec = pl.BlockSpec((None, ns) + mk.shape[2:], lambda i, j: (layer, i) + (0,) * (mk.ndim - 2))
    return pl.pallas_call(
        _cross_body,
        grid=(b // ns, t // tq),
        in_specs=[row_spec(D_MODEL)] + [row_spec(GROUP_W)] * 4 + [_resident((D_MODEL, D_MODEL)), _resident((1, D_MODEL)),
                  _resident((D_MODEL, D_MODEL)), mem_spec, mem_spec, _resident((D_MODEL, D_MODEL))],
        out_specs=row_spec(D_MODEL),
        out_shape=jax.ShapeDtypeStruct((b, t, D_MODEL), F32),
        compiler_params=_cparams("parallel", "parallel"),
    )(x, *mix, w_out, gain.reshape(1, -1), wq, mk, mv, wo)


def _diff_body(*refs, tq, tk, past, lam_init, gm):
    if past:
        lam_ref, q_ref, kn_ref, vn_ref, kp_ref, vp_ref, bn_ref, o_ref, q8_sc, m_sc, l_sc, acc_sc = refs
    else:
        lam_ref, q_ref, kn_ref, vn_ref, bn_ref, o_ref, q8_sc, m_sc, l_sc, acc_sc = refs
    i = pl.program_id(1)
    ng = 2 * N_HEADS
    q = q_ref[0] * (DIFF_HALF ** -0.5 * LOG2E)
    lane = _iota((tq, GROUP_W), 1)
    for g in range(ng):
        lo = (g // 2) * HEAD_DIM + (g % 2) * DIFF_HALF
        q8_sc[g * tq:(g + 1) * tq, :] = jnp.where((lane >= lo) & (lane < lo + DIFF_HALF), q, 0.0).astype(BF16)
    m_sc[...] = jnp.full(m_sc.shape, -jnp.inf, F32)
    l_sc[...] = jnp.zeros(l_sc.shape, F32)
    acc_sc[...] = jnp.zeros(acc_sc.shape, F32)
    pq = past + i * tq + _iota((tq, 1), 0)
    pq_f = pq.astype(F32)

    lanes = _lanes

    def block(k_ref, v_ref, k0, n, near, pos0=0):
        kb = k_ref[0, :, pl.ds(k0, n)].astype(BF16)
        vb = v_ref[0, :, pl.ds(k0, n)].astype(BF16)
        pk = pos0 + k0 + _iota((1, n), 1)
        pk_f = pk.astype(F32)
        if near:
            allowed = (pk >> 6) <= (pq >> 6)
            shift = pq_f - jnp.abs(pq_f - pk_f)
        for st in range(ng // gm):
            q_st = q8_sc[st * gm * tq:(st + 1) * gm * tq, :]
            s_st = _dot(q_st, kb)
            ps, alphas = [], []
            for gi in range(gm):
                g = st * gm + gi
                slope = 2.0 ** (-2 * (g // 2 + 1)) * LOG2E
                s = s_st[gi * tq:(gi + 1) * tq]
                if near:
                    s = jnp.where(allowed, s + slope * shift, NEG)
                else:
                    s = s + slope * pk_f
                m_prev = m_sc[g]
                m_new = jnp.maximum(m_prev, jnp.max(s, axis=-1, keepdims=True))
                p = jnp.exp2(s - lanes(m_new, n))
                alpha = jnp.exp2(m_prev - m_new)
                l_sc[g] = alpha * l_sc[g] + jnp.sum(p, axis=-1, keepdims=True)
                m_sc[g] = m_new
                ps.append(p.astype(BF16))
                alphas.append(alpha)
            p_st = ps[0] if gm == 1 else jnp.concatenate(ps, axis=0)
            pv = _dot_nt(p_st, vb)
            for gi in range(gm):
                g = st * gm + gi
                acc_sc[g] = acc_sc[g] * lanes(alphas[gi], GROUP_W) + pv[gi * tq:(gi + 1) * tq]

    if past:
        def past_step(j, carry):
            block(kp_ref, vp_ref, pl.multiple_of(j * tk, tk), tk, False)
            return carry
        lax.fori_loop(0, past // tk, past_step, 0, unroll=2)
        block(kn_ref, vn_ref, 0, tq, True, pos0=past)
    else:
        def prev_step(j, carry):
            block(kn_ref, vn_ref, pl.multiple_of(j * tk, tk), tk, False)
            return carry
        lax.fori_loop(0, (i * tq) // tk, prev_step, 0)

        @pl.when((i * tq) % tk != 0)
        def _():
            block(kn_ref, vn_ref, pl.multiple_of((i - 1) * tq, tq), tq, False)
        block(kn_ref, vn_ref, pl.multiple_of(i * tq, tq), tq, True)

    lam = lam_ref[0]
    lane_head = _iota((1, GROUP_W), 1) >> 6
    o = jnp.zeros((tq, GROUP_W), F32)
    for h in range(N_HEADS):
        o0 = acc_sc[2 * h] / lanes(l_sc[2 * h], GROUP_W)
        o1 = acc_sc[2 * h + 1] / lanes(l_sc[2 * h + 1], GROUP_W)
        o = jnp.where(lane_head == h, o0 - lam * o1, o)
    ms = _mm_r(o * o, _head_block_ones(GROUP_W).astype(BF16)) * (1.0 / HEAD_DIM)
    o_ref[0] = o * lax.rsqrt(ms + EPS) * bn_ref[...] * (1.0 - lam_init)


def _diff_attn(lam, q, kn, vn, kp, vp, layer, bnorm, lam_init):
    b, t, _ = q.shape
    past = 0 if kp is None else kp.shape[3]
    ng = 2 * N_HEADS
    if past:
        tq, tk, gm = t, math.gcd(past // 2, 512), ng
        assert t == CHUNK and tk % LANES == 0
    else:
        tq = min(256, t)
        tk = 2 * tq if t % (2 * tq) == 0 else tq
        gm = 1
        assert t % tq == 0 and tq % CHUNK == 0
    seq_spec = lambda n: pl.BlockSpec((1, GROUP_W, n), lambda i, j: (i, 0, 0))
    past_spec = pl.BlockSpec((None, 1, GROUP_W, past), lambda i, j: (layer, i, 0, 0))
    in_specs = [pl.BlockSpec(memory_space=pltpu.SMEM),
                pl.BlockSpec((1, tq, GROUP_W), lambda i, j: (i, j, 0)), seq_spec(t), seq_spec(t)]
    args = [lam, q, kn, vn]
    if past:
        in_specs += [past_spec, past_spec]
        args += [kp, vp]
    in_specs.append(_resident((1, GROUP_W)))
    args.append(bnorm)
    return pl.pallas_call(
        functools.partial(_diff_body, tq=tq, tk=tk, past=past, lam_init=lam_init, gm=gm),
        grid=(b, t // tq),
        in_specs=in_specs,
        out_specs=pl.BlockSpec((1, tq, GROUP_W), lambda i, j: (i, j, 0)),
        out_shape=jax.ShapeDtypeStruct((b, t, GROUP_W), F32),
        scratch_shapes=[pltpu.VMEM((ng * tq, GROUP_W), BF16), pltpu.VMEM((ng, tq, LANES), F32),
                        pltpu.VMEM((ng, tq, LANES), F32), pltpu.VMEM((ng, tq, GROUP_W), F32)],
        compiler_params=_cparams("parallel", "parallel"),
    )(*args)


def _stick_body(*refs, tq, tk, past):
    if past:
        q_ref, kn_ref, vn_ref, kp_ref, vp_ref, o_ref, q4_sc, c_sc, acc_sc = refs
    else:
        q_ref, kn_ref, vn_ref, o_ref, q4_sc, c_sc, acc_sc = refs
    i = pl.program_id(1)
    nr = N_HEADS * tq
    q = q_ref[0] * (HEAD_DIM ** -0.5 * LOG2E)
    lane = _iota((tq, GROUP_W), 1) >> 6
    for h in range(N_HEADS):
        q4_sc[h * tq:(h + 1) * tq, :] = jnp.where(lane == h, q, 0.0).astype(BF16)
    c_sc[...] = jnp.zeros(c_sc.shape, F32)
    acc_sc[...] = jnp.zeros(acc_sc.shape, F32)
    pq = past + i * tq + (_iota((nr, 1), 0) & (tq - 1))
    laters = {w: (_iota((w, w), 0) > _iota((w, w), 1)).astype(BF16)
              for w in {min(tq, STICK_SUB), min(tk, STICK_SUB)}}

    def block(k_ref, v_ref, k0, n, masked, pos0=0):
        kb = k_ref[0, :, pl.ds(k0, n)].astype(BF16)
        vb = v_ref[0, :, pl.ds(k0, n)].astype(BF16)
        sub = min(n, STICK_SUB)
        later = laters[sub]
        z = _dot(q4_sc[...], kb)
        ls = jnp.minimum(z, 0.0) - jnp.log2(1.0 + jnp.exp2(-jnp.abs(z)))
        lk = ls - z
        if masked:
            mask = (pos0 + k0 + _iota((1, n), 1)) < pq
            lk = jnp.where(mask, lk, 0.0)
        hi = lk.astype(BF16)
        lo = (lk - hi.astype(F32)).astype(BF16)
        carry = c_sc[...]
        es = []
        for sb in reversed(range(n // sub)):
            sl = slice(sb * sub, (sb + 1) * sub)
            after = _dot(hi[:, sl], later) + _dot(lo[:, sl], later)
            es.append(ls[:, sl] + after + _lanes(carry, sub))
            carry = carry + (after[:, 0:1] + lk[:, sb * sub:sb * sub + 1])
        c_sc[...] = carry
        e = es[0] if len(es) == 1 else jnp.concatenate(es[::-1], axis=1)
        if masked:
            a = jnp.where(mask, jnp.exp2(jnp.where(mask, e, 0.0)), 0.0)
        else:
            a = jnp.exp2(e)
        acc_sc[...] = acc_sc[...] + _dot_nt(a.astype(BF16), vb)

    if past:
        block(kn_ref, vn_ref, 0, tq, True, pos0=past)
        nb = past // tk

        def past_step(j, carry):
            block(kp_ref, vp_ref, pl.multiple_of((nb - 1 - j) * tk, tk), tk, False)
            return carry
        lax.fori_loop(0, nb, past_step, 0)
    else:
        block(kn_ref, vn_ref, pl.multiple_of(i * tq, tq), tq, True)

        @pl.when((i * tq) % tk != 0)
        def _():
            block(kn_ref, vn_ref, pl.multiple_of((i - 1) * tq, tq), tq, False)
        nfull = (i * tq) // tk

        def prev_step(j, carry):
            block(kn_ref, vn_ref, pl.multiple_of((nfull - 1 - j) * tk, tk), tk, False)
            return carry
        lax.fori_loop(0, nfull, prev_step, 0)

    o = acc_sc[0:tq, :]
    for h in range(1, N_HEADS):
        o = jnp.where(lane == h, acc_sc[h * tq:(h + 1) * tq, :], o)
    o_ref[0] = o


def _stick_attn(q, kn, vn, kp, vp, layer):
    b, t, _ = q.shape
    past = 0 if kp is None else kp.shape[3]
    if past:
        tq, tk = t, math.gcd(past, 1024)
    else:
        tq = min(256, t)
        tk = 2 * tq if t % (2 * tq) == 0 else tq
        assert t % tq == 0
    assert tq & (tq - 1) == 0 and tk % min(tk, STICK_SUB) == 0
    seq_spec = lambda n: pl.BlockSpec((1, GROUP_W, n), lambda i, j: (i, 0, 0))
    past_spec = pl.BlockSpec((None, 1, GROUP_W, past), lambda i, j: (layer, i, 0, 0))
    in_specs = [pl.BlockSpec((1, tq, GROUP_W), lambda i, j: (i, j, 0)), seq_spec(t), seq_spec(t)]
    args = [q, kn, vn]
    if past:
        in_specs += [past_spec, past_spec]
        args += [kp, vp]
    return pl.pallas_call(
        functools.partial(_stick_body, tq=tq, tk=tk, past=past),
        grid=(b, t // tq),
        in_specs=in_specs,
        out_specs=pl.BlockSpec((1, tq, GROUP_W), lambda i, j: (i, j, 0)),
        out_shape=jax.ShapeDtypeStruct((b, t, GROUP_W), F32),
        scratch_shapes=[pltpu.VMEM((N_HEADS * tq, GROUP_W), BF16), pltpu.VMEM((N_HEADS * tq, LANES), F32),
                        pltpu.VMEM((N_HEADS * tq, GROUP_W), F32)],
        compiler_params=_cparams("parallel", "parallel"),
    )(*args)


def _pair_mask(n, s, reps=1):
    r, c = _iota((reps * n, n), 0) & (n - 1), _iota((reps * n, n), 1)
    return ((r // (2 * s)) == (c // (2 * s))) & (((r // s) & 1) == 1) & (((c // s) & 1) == 0)


def _state_spec(nb):
    return pl.BlockSpec((nb, N_HEADS, HEAD_DIM, HEAD_DIM), lambda i, j: (i, 0, 0, 0))


def _load_state(s_sc, s0_ref, transpose):
    s_sc[...] = jnp.zeros(s_sc.shape, F32)
    for b in range(s_sc.shape[0]):
        for h in range(N_HEADS):
            blk = s0_ref[b, h]
            s_sc[b, h * HEAD_DIM:(h + 1) * HEAD_DIM, h * HEAD_DIM:(h + 1) * HEAD_DIM] = blk.T if transpose else blk


def _store_state(sout_ref, s_sc, transpose):
    for b in range(s_sc.shape[0]):
        for h in range(N_HEADS):
            blk = s_sc[b, h * HEAD_DIM:(h + 1) * HEAD_DIM, h * HEAD_DIM:(h + 1) * HEAD_DIM]
            sout_ref[b, h] = blk.T if transpose else blk


def _head_rows(x, lane_head):
    return jnp.concatenate([jnp.where(lane_head == h, x, 0.0) for h in range(N_HEADS)], axis=0)


def _head_diag(x, lane_head):
    n = x.shape[0] // N_HEADS
    out = x[:n]
    for h in range(1, N_HEADS):
        out = jnp.where(lane_head == h, x[h * n:(h + 1) * n], out)
    return out


def _gdn_body(x_ref, z_ref, abg_ref, cw_ref, cs_ref, s0_ref, alog_ref, dtb_ref, an_ref,
              o_ref, sout_ref, s_sc, xb_sc, *, nc, nb):
    c = pl.program_id(1)
    L = CHUNK
    pad = 8

    @pl.when(c == 0)
    def _():
        _load_state(s_sc, s0_ref, transpose=False)
        xb_sc[:, pad - 3:pad, :] = cs_ref[...]

    ones_bd = _head_block_ones(GROUP_W)
    ones_bd16 = ones_bd.astype(BF16)
    er, ec = _iota((ABG_W, GROUP_W), 0), _iota((ABG_W, GROUP_W), 1) >> 6
    e_beta = (er == ec).astype(BF16)
    e_g = (er == ec + N_HEADS).astype(BF16)
    ri, ci = _iota((L, L), 0), _iota((L, L), 1)
    tri16 = (ri >= ci).astype(BF16)
    incl = ri >= ci
    strict = ri > ci
    eye = (ri == ci).astype(F32)
    pair_masks = [_pair_mask(L, sz) for sz in (1, 2, 4, 8, 16, 32)]
    lane_head = _iota((1, GROUP_W), 1) >> 6
    cw = cw_ref[...]

    a_all, per_seq = [], []
    for b in range(nb):
        x = x_ref[b]
        xb_sc[b, pad:pad + L, :] = x
        u = (x * cw[3:4] + xb_sc[b, pad - 1:pad - 1 + L, :] * cw[2:3]
             + xb_sc[b, pad - 2:pad - 2 + L, :] * cw[1:2] + xb_sc[b, pad - 3:pad - 3 + L, :] * cw[0:1])
        xb_sc[b, pad - 3:pad, :] = x[L - 3:L, :]
        u = u * _sigmoid(u)
        q = u[:, :GROUP_W]
        k = u[:, GROUP_W:2 * GROUP_W]
        v = u[:, 2 * GROUP_W:]
        q = q * lax.rsqrt(_mm_r(q * q, ones_bd16) + EPS) * HEAD_DIM ** -0.5
        k = k * lax.rsqrt(_mm_r(k * k, ones_bd16) + EPS)

        abg = abg_ref[b]
        beta_n = _sigmoid(abg)
        g_n = -jnp.exp(alog_ref[...]) * _softplus(abg + dtb_ref[...])
        beta = _mm_r(beta_n, e_beta)
        gc_n = _mm_l(tri16, g_n)
        gc_t = gc_n.T
        gc = _mm_r(gc_n, e_g)
        exp_g = jnp.exp(gc)
        g_last = gc[L - 1:L, :]

        kb = k * beta
        k16 = k.astype(BF16)
        a4 = _bdot(_head_rows(kb, lane_head).astype(BF16), k16, _NT)
        qk4 = _bdot(_head_rows(q, lane_head).astype(BF16), k16, _NT)
        dec = []
        for h in range(N_HEADS):
            diff = gc_n[:, N_HEADS + h:N_HEADS + h + 1] - gc_t[N_HEADS + h:N_HEADS + h + 1, :]
            dec.append(jnp.where(incl, jnp.exp(jnp.where(incl, diff, 0.0)), 0.0))
        dec = jnp.stack(dec)
        a_all.append(jnp.where(strict, a4.reshape(N_HEADS, L, L) * dec, 0.0))
        per_seq.append((q, k, v * beta, kb * exp_g, q * exp_g, k * jnp.exp(g_last - gc), jnp.exp(g_last),
                        qk4.reshape(N_HEADS, L, L) * dec))

    a_all = jnp.concatenate(a_all, axis=0)
    t_inv = eye - jnp.where(pair_masks[0], a_all, 0.0)
    a16 = a_all.astype(BF16)
    for pm in pair_masks[1:]:
        t16 = t_inv.astype(BF16)
        t_inv = t_inv - _bdot(t16, _bdot(jnp.where(pm, a16, 0.0), t16, _BNN).astype(BF16), _BNN)

    for b in range(nb):
        q, k, vb, kbg, qg, k_dec, decay_last, qk = per_seq[b]
        s = s_sc[b]
        s16 = s.astype(BF16)
        rhs = vb - _bdot(kbg.astype(BF16), s16, _NN)
        t4 = t_inv[b * N_HEADS:(b + 1) * N_HEADS].reshape(N_HEADS * L, L)
        w = _head_diag(_mm1(t4, rhs), lane_head)
        o = _bdot(qg.astype(BF16), s16, _NN) + _head_diag(_mm1(qk.reshape(N_HEADS * L, L), w), lane_head)
        s_sc[b] = s * decay_last + _mm1(k_dec.T, w) * ones_bd

        ms = _mm_r(o * o, ones_bd16) * (1.0 / HEAD_DIM)
        zg = z_ref[b]
        o_ref[b] = o * lax.rsqrt(ms + EPS) * an_ref[...] * (zg * _sigmoid(zg))

    @pl.when(c == nc - 1)
    def _():
        _store_state(sout_ref, s_sc, transpose=False)


def _gdn(aqkv, az, abg, conv_w, conv_state, s0, alog_row, dtb_row, anorm_row):
    b, t, _ = aqkv.shape
    nc = t // CHUNK
    nb = SEQ_PER_STEP
    assert b % nb == 0
    cmap = lambda i, j: (i, j, 0)
    bmap = lambda i, j: (i, 0, 0)
    return pl.pallas_call(
        functools.partial(_gdn_body, nc=nc, nb=nb),
        grid=(b // nb, nc),
        in_specs=[pl.BlockSpec((nb, CHUNK, 3 * GROUP_W), cmap), pl.BlockSpec((nb, CHUNK, GROUP_W), cmap),
                  pl.BlockSpec((nb, CHUNK, ABG_W), cmap), _resident((CONV_W, 3 * GROUP_W)),
                  pl.BlockSpec((nb, CONV_W - 1, 3 * GROUP_W), bmap), _state_spec(nb),
                  _resident((1, ABG_W)), _resident((1, ABG_W)), _resident((1, GROUP_W))],
        out_specs=[pl.BlockSpec((nb, CHUNK, GROUP_W), cmap), _state_spec(nb)],
        out_shape=[jax.ShapeDtypeStruct((b, t, GROUP_W), F32), jax.ShapeDtypeStruct(s0.shape, F32)],
        scratch_shapes=[pltpu.VMEM((nb, GROUP_W, GROUP_W), F32), pltpu.VMEM((nb, 8 + CHUNK, 3 * GROUP_W), F32)],
        compiler_params=_cparams("parallel", "arbitrary"),
    )(aqkv, az, abg, conv_w, conv_state, s0, alog_row, dtb_row, anorm_row)


def _hgrn_body(q_ref, f_ref, i_ref, g_ref, lb_ref, dn_ref, s0_ref, o_ref, sout_ref, s_sc, *, nc, nb):
    c = pl.program_id(1)
    L = CHUNK

    @pl.when(c == 0)
    def _():
        _load_state(s_sc, s0_ref, transpose=True)

    lb = lb_ref[...]
    ones_bd = _head_block_ones(GROUP_W)
    ones_bd16 = ones_bd.astype(BF16)
    ri, ci = _iota((L, L), 0), _iota((L, L), 1)
    tri16 = (ri >= ci).astype(BF16)
    sizes = (1, 2, 4, 8, 16, 32)
    sel16 = jnp.concatenate([(ci == (ri // (2 * sz)) * (2 * sz) + sz).astype(BF16) for sz in sizes], axis=0)
    eye4 = (_iota((N_HEADS * L, L), 0) & (L - 1)) == _iota((N_HEADS * L, L), 1)
    pair_masks4 = [_pair_mask(L, sz, reps=N_HEADS) for sz in sizes]
    lane_head = _iota((1, GROUP_W), 1) >> 6
    rows = _iota((L, 1), 0)

    pre = []
    for b in range(nb):
        fl = f_ref[b]
        log_f = _log_sigmoid(fl) + jnp.log(1.0 + lb * jnp.exp(-fl))
        k = (1.0 - lb) * _sigmoid(-fl)
        q = q_ref[b] * HEAD_DIM ** -0.5
        bc = _mm_l(tri16, log_f)
        pre.append((q, k, bc))
    b_refs = [_mm_l(sel16, bc) for _, _, bc in pre]
    a_all = [jnp.where(eye4, _mm1(_head_rows(q, lane_head), k, _NT), 0.0) for q, k, _ in pre]
    for lv, sz in enumerate(sizes):
        later = ((rows // sz) & 1) == 1
        for b in range(nb):
            q, k, bc = pre[b]
            b_ref = b_refs[b][lv * L:(lv + 1) * L]
            x = jnp.where(later, q * jnp.exp(jnp.where(later, bc - b_ref, 0.0)), 0.0)
            y = jnp.where(later, 0.0, k * jnp.exp(jnp.where(later, 0.0, b_ref - bc)))
            a_all[b] = a_all[b] + jnp.where(pair_masks4[lv], _mm1(_head_rows(x, lane_head), y, _NT), 0.0)

    for b in range(nb):
        q, k, bc = pre[b]
        v = i_ref[b]
        st = s_sc[b]
        o = _mm1(q * jnp.exp(bc), st, _NT) + _head_diag(_mm1(a_all[b], v), lane_head)
        b_last = bc[L - 1:L, :]
        st_new = st * jnp.exp(b_last) + _mm1(v.T, k * jnp.exp(b_last - bc)) * ones_bd
        s_sc[b] = st_new

        ms = _mm_r(o * o, ones_bd16) * (1.0 / HEAD_DIM)
        zg = g_ref[b]
        o_ref[b] = o * lax.rsqrt(ms + EPS) * dn_ref[...] * (zg * _sigmoid(zg))

    @pl.when(c == nc - 1)
    def _():
        _store_state(sout_ref, s_sc, transpose=True)


def _hgrn(dq, df, di, dg, lb_row, dnorm_row, s0):
    b, t, _ = dq.shape
    nc = t // CHUNK
    nb = SEQ_PER_STEP
    assert b % nb == 0
    cmap = lambda i, j: (i, j, 0)
    bmap = lambda i, j: (i, 0, 0)
    cspec = pl.BlockSpec((nb, CHUNK, GROUP_W), cmap)
    return pl.pallas_call(
        functools.partial(_hgrn_body, nc=nc, nb=nb),
        grid=(b // nb, nc),
        in_specs=[cspec, cspec, cspec, cspec, _resident((1, GROUP_W)), _resident((1, GROUP_W)),
                  _state_spec(nb)],
        out_specs=[cspec, _state_spec(nb)],
        out_shape=[jax.ShapeDtypeStruct((b, t, GROUP_W), F32), jax.ShapeDtypeStruct(s0.shape, F32)],
        scratch_shapes=[pltpu.VMEM((nb, GROUP_W, GROUP_W), F32)],
        compiler_params=_cparams("parallel", "arbitrary"),
    )(dq, df, di, dg, lb_row, dnorm_row, s0)


IN_WIDTHS = (3 * GROUP_W,) + (GROUP_W,) * 11 + (ABG_W,)
KV_SEGMENTS = (3, 4, 6, 7)


def _layer(x, l, st, mk, mv, mem_layer, prm, final_gain):
    b, t, _ = x.shape
    m = b * t
    tm = math.gcd(m, ROW_TILE)
    outs = _fused_linear([x.reshape(m, D_MODEL)], prm["w_in"][l], IN_WIDTHS, gain=prm["norm_mix"][l], tm=tm,
                         feat_major=KV_SEGMENTS, rows_per_seq=t)
    outs = [o if idx in KV_SEGMENTS else o.reshape(b, t, -1) for idx, o in enumerate(outs)]
    (a_qkv, a_z, b_q, b_k, b_v, c_q, c_k, c_v, d_q, d_f, d_i, d_g, a_bg) = outs

    o_a, a_s = _gdn(a_qkv, a_z, a_bg, prm["a_conv_w"][l], st["a_conv"], st["a_S"].astype(F32),
                    prm["alog_row"][l], prm["dtb_row"][l], prm["a_norm"][l])
    o_b = _diff_attn(prm["lam"][l], b_q, b_k, b_v, st["b_k"], st["b_v"], l, prm["b_norm"][l],
                     0.8 - 0.6 * math.exp(-0.3 * l))
    o_c = _stick_attn(c_q, c_k, c_v, st["c_k"], st["c_v"], l)
    o_d, d_s = _hgrn(d_q, d_f, d_i, d_g, prm["lb"][l], prm["d_norm"][l], st["d_S"].astype(F32))

    x2 = _cross(x, (o_a, o_b, o_c, o_d), prm["w_out"][l], prm["norm_cross"][l], prm["w_cq"][l], mk, mv, mem_layer,
                prm["w_co"][l])
    x2 = _ffn(x2.reshape(m, D_MODEL), prm["norm_ffn"][l], prm["w_gate"][l], prm["w_up"][l], prm["w_down"][l],
              final_gain=final_gain, tm=tm)

    if t >= CONV_W - 1:
        conv_new = a_qkv[:, t - (CONV_W - 1):, :]
    else:
        conv_new = jnp.concatenate([st["a_conv"], a_qkv], axis=1)[:, -(CONV_W - 1):, :]
    heads = lambda a: jnp.transpose(a.reshape(b, N_HEADS, HEAD_DIM, t), (0, 3, 1, 2))
    new = (conv_new, a_s, heads(b_k), heads(b_v), heads(c_k), heads(c_v), d_s)
    return x2.reshape(b, t, D_MODEL), new


def kernel(x_prompt, x_sample, mem_prompt, state_a_conv, state_a_S, cache_b_k, cache_b_v, cache_c_k, cache_c_v,
           state_d_S, cache_mem_k, cache_mem_v, norm_mix, w_in, a_conv_w, a_A_log, a_dt_bias, a_norm, b_lam_q1,
           b_lam_k1, b_lam_q2, b_lam_k2, b_norm, d_lb, d_norm, w_out, norm_cross, norm_memtok, w_cq, w_ck, w_cv,
           w_co, norm_ffn, w_gate, w_up, w_down, norm_final):
    depth = w_in.shape[0]
    n_main = 4 * GROUP_W
    w_in_p = jnp.concatenate(
        [w_in[:, :, :n_main], w_in[:, :, n_main + 2 * N_HEADS:], w_in[:, :, n_main:n_main + 2 * N_HEADS],
         jnp.zeros((depth, D_MODEL, ABG_W - 2 * N_HEADS), w_in.dtype)], axis=2).astype(BF16)
    pad_row = lambda a: jnp.pad(a.astype(F32), ((0, 0), (N_HEADS, ABG_W - 2 * N_HEADS)))[:, None, :]
    tile_row = lambda a: jnp.tile(a.astype(F32), (1, N_HEADS))[:, None, :]
    p_lb = jax.nn.softmax(d_lb.astype(F32), axis=0)
    lam = (jnp.exp(jnp.sum(b_lam_q1.astype(F32) * b_lam_k1.astype(F32), axis=-1))
           - jnp.exp(jnp.sum(b_lam_q2.astype(F32) * b_lam_k2.astype(F32), axis=-1))
           + jnp.array([0.8 - 0.6 * math.exp(-0.3 * l) for l in range(depth)], F32))
    prm = {
        "w_in": w_in_p, "norm_mix": norm_mix, "a_conv_w": a_conv_w.astype(F32),
        "alog_row": pad_row(a_A_log), "dtb_row": pad_row(a_dt_bias), "a_norm": tile_row(a_norm),
        "lam": lam[:, None], "b_norm": tile_row(b_norm), "lb": (jnp.cumsum(p_lb, axis=0) - p_lb[0])[:, None, :],
        "d_norm": tile_row(d_norm), "w_out": w_out.astype(BF16), "norm_cross": norm_cross,
        "w_cq": w_cq.astype(BF16), "w_co": w_co.astype(BF16), "norm_ffn": norm_ffn,
        "w_gate": w_gate.astype(BF16), "w_up": w_up.astype(BF16), "w_down": w_down.astype(BF16),
    }
    w_ckv = jnp.concatenate([w_ck, w_cv], axis=2).astype(BF16)

    bp, tp, _ = x_prompt.shape
    n_mem = mem_prompt.shape[1]
    h = x_prompt
    p_new = []
    for l in range(depth):
        mk, mv = _fused_linear([mem_prompt.reshape(bp * n_mem, D_MODEL)], w_ckv[l], (D_MODEL, D_MODEL),
                               gain=norm_memtok[l], tm=256)
        mk = mk.reshape(1, bp, n_mem, D_MODEL)
        mv = mv.reshape(1, bp, n_mem, D_MODEL)
        st = {"a_conv": jnp.zeros((bp, CONV_W - 1, 3 * GROUP_W), F32),
              "a_S": jnp.zeros((bp, N_HEADS, HEAD_DIM, HEAD_DIM), F32), "b_k": None, "b_v": None,
              "c_k": None, "c_v": None, "d_S": jnp.zeros((bp, N_HEADS, HEAD_DIM, HEAD_DIM), F32)}
        h, new = _layer(h, l, st, mk, mv, 0, prm, norm_final if l == depth - 1 else None)
        mem4 = lambda a: a.reshape(bp, n_mem, MEM_HEADS, MEM_HEAD_DIM)
        p_new.append(new + (mem4(mk), mem4(mv)))
    y_prompt = h
    p_stacked = [jnp.stack(c) for c in zip(*p_new)]

    bs, ts, _ = x_sample.shape
    past = cache_b_k.shape[2]
    h = x_sample
    s_new = []
    flat = lambda a: jnp.transpose(a, (0, 1, 3, 4, 2)).reshape(depth, bs, GROUP_W, past)
    caches = {"b_k": flat(cache_b_k), "b_v": flat(cache_b_v), "c_k": flat(cache_c_k), "c_v": flat(cache_c_v)}
    for l in range(depth):
        st = {"a_conv": state_a_conv[l], "a_S": state_a_S[l], "d_S": state_d_S[l], **caches}
        h, new = _layer(h, l, st, cache_mem_k, cache_mem_v, l, prm, norm_final if l == depth - 1 else None)
        s_new.append(new)
    y_sample = h
    s_stacked = [jnp.stack(c) for c in zip(*s_new)]

    return (y_prompt, y_sample, *p_stacked, *s_stacked)
```

```python
import functools
import math

import jax
import jax.numpy as jnp
from jax import lax
from jax.experimental import pallas as pl
from jax.experimental.pallas import tpu as pltpu

F32 = jnp.float32
BF16 = jnp.bfloat16

D_MODEL = 1024
GROUP_W = 256
N_HEADS = 4
HEAD_DIM = 64
DIFF_HALF = 32
CHUNK = 64
CONV_W = 4
MEM_HEADS = 4
MEM_HEAD_DIM = 256
D_FF = 2816
EPS = 1e-6
NEG = -1e30
LOG2E = 1.4426950408889634
LANES = 128
ABG_W = 128
FF_CHUNK = 256
ROW_TILE = 512
CROSS_SEQS = 4
STICK_SUB = 256
SEQ_PER_STEP = 4
VMEM_LIMIT = 56 * 1024 * 1024


def _cparams(*sem):
    return pltpu.CompilerParams(dimension_semantics=sem, vmem_limit_bytes=VMEM_LIMIT)


def _dot(a, b):
    return jnp.dot(a, b, preferred_element_type=F32)


def _dot_nt(a, b):
    return lax.dot_general(a, b, (((1,), (1,)), ((), ())), preferred_element_type=F32)


_NN = (((1,), (0,)), ((), ()))
_NT = (((1,), (1,)), ((), ()))
_BNN = (((2,), (1,)), ((0,), (0,)))


def _split(x):
    hi = x.astype(BF16)
    return hi, (x - hi.astype(F32)).astype(BF16)


def _bdot(a, b, dims):
    return lax.dot_general(a, b, dims, preferred_element_type=F32)


def _mm(a, b, dims=_NN):
    ah, al = a if isinstance(a, tuple) else _split(a)
    bh, bl = b if isinstance(b, tuple) else _split(b)
    return _bdot(ah, bh, dims) + (_bdot(ah, bl, dims) + _bdot(al, bh, dims))


def _mm1(a, b, dims=_NN):
    return _bdot(a.astype(BF16), b.astype(BF16), dims)


def _mm_r(a, b16):
    ah, al = _split(a)
    return _bdot(ah, b16, _NN) + _bdot(al, b16, _NN)


def _mm_l(a16, b):
    bh, bl = _split(b)
    return _bdot(a16, bh, _NN) + _bdot(a16, bl, _NN)


def _rms(x, g):
    return x * lax.rsqrt(jnp.mean(x * x, axis=-1, keepdims=True) + EPS) * g


def _sigmoid(x):
    return 1.0 / (1.0 + jnp.exp(-x))


def _log_sigmoid(x):
    return jnp.minimum(x, 0.0) - jnp.log(1.0 + jnp.exp(-jnp.abs(x)))


def _softplus(x):
    return jnp.maximum(x, 0.0) + jnp.log(1.0 + jnp.exp(-jnp.abs(x)))


def _iota(shape, dim):
    return lax.broadcasted_iota(jnp.int32, shape, dim)


def _lanes(x, n):
    return x[:, :n] if n <= LANES else jnp.concatenate([x] * (n // LANES), axis=1)


def _head_block_ones(n):
    return ((_iota((n, n), 0) >> 6) == (_iota((n, n), 1) >> 6)).astype(F32)


def _resident(shape):
    return pl.BlockSpec(shape, lambda *_: (0,) * len(shape), pipeline_mode=pl.Buffered(1))


def _linear_body(*refs, ks, segs, has_norm, has_res, feat_major, rows_too):
    n_in = len(ks)
    x_refs = refs[:n_in]
    pos = n_in
    g_ref = None
    if has_norm:
        g_ref = refs[pos]
        pos += 1
    w_ref = refs[pos]
    pos += 1
    r_ref = None
    if has_res:
        r_ref = refs[pos]
        pos += 1
    out_refs = refs[pos:]
    xb = []
    for x_ref in x_refs:
        x = x_ref[...].astype(F32)
        if has_norm:
            x = _rms(x, g_ref[...])
        xb.append(x.astype(BF16))
    extra_refs = dict(zip(rows_too, out_refs[len(segs):]))
    for idx, (o_ref, (s, e)) in enumerate(zip(out_refs, segs)):
        acc = None
        k0 = 0
        for xk, k in zip(xb, ks):
            t = _dot(xk, w_ref[k0:k0 + k, s:e])
            acc = t if acc is None else acc + t
            k0 += k
        if has_res:
            acc = acc + r_ref[:, s:e]
        if idx in feat_major:
            rows = acc.shape[0] // o_ref.shape[0]
            for sq in range(o_ref.shape[0]):
                o_ref[sq] = acc[sq * rows:(sq + 1) * rows].T
        else:
            o_ref[...] = acc
        if idx in extra_refs:
            extra_refs[idx][...] = acc


def _fused_linear(xs, w, widths, gain=None, res=None, tm=256, feat_major=(), rows_per_seq=None, rows_too=()):
    m = xs[0].shape[0]
    ks = tuple(x.shape[1] for x in xs)
    n = w.shape[1]
    segs = []
    s = 0
    for wd in widths:
        segs.append((s, s + wd))
        s += wd
    assert s == n and m % tm == 0
    out_specs, out_shape = [], []
    for idx, wd in enumerate(widths):
        if idx in feat_major:
            if rows_per_seq >= tm:
                tiles = rows_per_seq // tm
                assert rows_per_seq % tm == 0
                out_specs.append(pl.BlockSpec((1, wd, tm), lambda i, tiles=tiles: (i // tiles, 0, i % tiles)))
            else:
                assert tm % rows_per_seq == 0
                out_specs.append(pl.BlockSpec((tm // rows_per_seq, wd, rows_per_seq), lambda i: (i, 0, 0)))
            out_shape.append(jax.ShapeDtypeStruct((m // rows_per_seq, wd, rows_per_seq), F32))
        else:
            out_specs.append(pl.BlockSpec((tm, wd), lambda i: (i, 0)))
            out_shape.append(jax.ShapeDtypeStruct((m, wd), F32))
    for idx in rows_too:
        out_specs.append(pl.BlockSpec((tm, widths[idx]), lambda i: (i, 0)))
        out_shape.append(jax.ShapeDtypeStruct((m, widths[idx]), F32))
    in_specs = [pl.BlockSpec((tm, k), lambda i: (i, 0)) for k in ks]
    args = list(xs)
    if gain is not None:
        in_specs.append(_resident((1, ks[0])))
        args.append(gain.reshape(1, -1))
    in_specs.append(_resident((sum(ks), n)))
    args.append(w)
    if res is not None:
        in_specs.append(pl.BlockSpec((tm, n), lambda i: (i, 0)))
        args.append(res)
    return pl.pallas_call(
        functools.partial(_linear_body, ks=ks, segs=tuple(segs), has_norm=gain is not None, has_res=res is not None,
                          feat_major=tuple(feat_major), rows_too=tuple(rows_too)),
        grid=(m // tm,),
        in_specs=in_specs,
        out_specs=out_specs,
        out_shape=out_shape,
        compiler_params=_cparams("parallel"),
    )(*args)


def _ffn_body(*refs, final):
    if final:
        x_ref, g_ref, wg_ref, wu_ref, wd_ref, gf_ref, o_ref = refs
    else:
        x_ref, g_ref, wg_ref, wu_ref, wd_ref, o_ref = refs
    x = x_ref[...]
    hb = _rms(x, g_ref[...]).astype(BF16)
    acc = x
    for c in range(0, D_FF, FF_CHUNK):
        gate = _dot(hb, wg_ref[:, c:c + FF_CHUNK])
        up = _dot(hb, wu_ref[:, c:c + FF_CHUNK])
        act = (gate * _sigmoid(gate) * up).astype(BF16)
        acc = acc + _dot(act, wd_ref[c:c + FF_CHUNK, :])
    if final:
        acc = _rms(acc, gf_ref[...])
    o_ref[...] = acc


def _ffn(x, gain, wg, wu, wd, final_gain=None, tm=256):
    m = x.shape[0]
    final = final_gain is not None
    in_specs = [pl.BlockSpec((tm, D_MODEL), lambda i: (i, 0)), _resident((1, D_MODEL)),
                _resident((D_MODEL, D_FF)), _resident((D_MODEL, D_FF)), _resident((D_FF, D_MODEL))]
    args = [x, gain.reshape(1, -1), wg, wu, wd]
    if final:
        in_specs.append(_resident((1, D_MODEL)))
        args.append(final_gain.reshape(1, -1))
    return pl.pallas_call(
        functools.partial(_ffn_body, final=final),
        grid=(m // tm,),
        in_specs=in_specs,
        out_specs=pl.BlockSpec((tm, D_MODEL), lambda i: (i, 0)),
        out_shape=jax.ShapeDtypeStruct((m, D_MODEL), F32),
        compiler_params=_cparams("parallel"),
    )(*args)


def _cross_body(x_ref, oa_ref, ob_ref, oc_ref, od_ref, wout_ref, g_ref, wq_ref, mk_ref, mv_ref, wo_ref, o_ref):
    ns, tq, _ = x_ref.shape
    rows = ns * tq
    x = x_ref[...].reshape(rows, D_MODEL)
    for gi, m_ref in enumerate((oa_ref, ob_ref, oc_ref, od_ref)):
        x = x + _dot(m_ref[...].reshape(rows, GROUP_W).astype(BF16), wout_ref[gi * GROUP_W:(gi + 1) * GROUP_W, :])
    hb = _rms(x, g_ref[...]).astype(BF16)
    q = (_dot(hb, wq_ref[...]) * MEM_HEAD_DIM ** -0.5).astype(BF16)
    head = lambda h: slice(h * MEM_HEAD_DIM, (h + 1) * MEM_HEAD_DIM)
    pairs = [(sq, h) for sq in range(ns) for h in range(MEM_HEADS)]
    if len(mk_ref.shape) == 4:
        mem = lambda ref, sq, h: ref[sq, :, h, :].astype(BF16)
    else:
        mem = lambda ref, sq, h: ref[sq, :, head(h)].astype(BF16)
    scores = [_dot_nt(q[sq * tq:(sq + 1) * tq, head(h)], mem(mk_ref, sq, h)) for sq, h in pairs]
    probs = []
    for s in scores:
        p = jnp.exp(s - jnp.max(s, axis=-1, keepdims=True))
        probs.append((p / jnp.sum(p, axis=-1, keepdims=True)).astype(BF16))
    outs = [_dot(p, mem(mv_ref, sq, h)).astype(BF16) for p, (sq, h) in zip(probs, pairs)]
    o = jnp.concatenate([jnp.concatenate(outs[sq * MEM_HEADS:(sq + 1) * MEM_HEADS], axis=1) for sq in range(ns)], axis=0)
    o_ref[...] = (x + _dot(o, wo_ref[...])).reshape(ns, tq, D_MODEL)


def _cross(x, mix, w_out, gain, wq, mk, mv, layer, wo):
    b, t, _ = x.shape
    nm = mk.shape[2]
    tq = min(ROW_TILE, t)
    ns = CROSS_SEQS if CROSS_SEQS * t <= ROW_TILE else 1
    assert t % tq == 0 and b % ns == 0
    row_spec = lambda w: pl.BlockSpec((ns, tq, w), lambda i, j: (i, j, 0))
    mem_spec = pl.BlockSpec((None, ns) + mk.shape[2:], lambda i, j: (layer, i) + (0,) * (mk.ndim - 2))
    return pl.pallas_call(
        _cross_body,
        grid=(b // ns, t // tq),
        in_specs=[row_spec(D_MODEL)] + [row_spec(GROUP_W)] * 4 + [_resident((D_MODEL, D_MODEL)), _resident((1, D_MODEL)),
                  _resident((D_MODEL, D_MODEL)), mem_spec, mem_spec, _resident((D_MODEL, D_MODEL))],
        out_specs=row_spec(D_MODEL),
        out_shape=jax.ShapeDtypeStruct((b, t, D_MODEL), F32),
        compiler_params=_cparams("parallel", "parallel"),
    )(x, *mix, w_out, gain.reshape(1, -1), wq, mk, mv, wo)


def _diff_body(lam_ref, q_ref, kn_ref, vn_ref, kp_ref, vp_ref, bn_ref, o_ref, q8_sc, m_sc, l_sc, acc_sc, *,
               tq, tk, past, lam_init, gm):
    i = pl.program_id(1)
    ng = 2 * N_HEADS
    q = q_ref[0] * (DIFF_HALF ** -0.5 * LOG2E)
    lane = _iota((tq, GROUP_W), 1)
    for g in range(ng):
        lo = (g // 2) * HEAD_DIM + (g % 2) * DIFF_HALF
        q8_sc[g * tq:(g + 1) * tq, :] = jnp.where((lane >= lo) & (lane < lo + DIFF_HALF), q, 0.0).astype(BF16)
    m_sc[...] = jnp.full(m_sc.shape, -jnp.inf, F32)
    l_sc[...] = jnp.zeros(l_sc.shape, F32)
    acc_sc[...] = jnp.zeros(acc_sc.shape, F32)
    pq = past + i * tq + _iota((tq, 1), 0)
    pq_f = pq.astype(F32)

    lanes = _lanes

    def block(k_ref, v_ref, k0, n, near, pos0=0):
        kb = k_ref[0, :, pl.ds(k0, n)].astype(BF16)
        vb = v_ref[0, :, pl.ds(k0, n)].astype(BF16)
        pk = pos0 + k0 + _iota((1, n), 1)
        pk_f = pk.astype(F32)
        if near:
            allowed = (pk >> 6) <= (pq >> 6)
            shift = pq_f - jnp.abs(pq_f - pk_f)
        for st in range(ng // gm):
            q_st = q8_sc[st * gm * tq:(st + 1) * gm * tq, :]
            s_st = _dot(q_st, kb)
            ps, alphas = [], []
            for gi in range(gm):
                g = st * gm + gi
                slope = 2.0 ** (-2 * (g // 2 + 1)) * LOG2E
                s = s_st[gi * tq:(gi + 1) * tq]
                if near:
                    s = jnp.where(allowed, s + slope * shift, NEG)
                else:
                    s = s + slope * pk_f
                m_prev = m_sc[g]
                m_new = jnp.maximum(m_prev, jnp.max(s, axis=-1, keepdims=True))
                p = jnp.exp2(s - lanes(m_new, n))
                alpha = jnp.exp2(m_prev - m_new)
                l_sc[g] = alpha * l_sc[g] + jnp.sum(p, axis=-1, keepdims=True)
                m_sc[g] = m_new
                ps.append(p.astype(BF16))
                alphas.append(alpha)
            p_st = ps[0] if gm == 1 else jnp.concatenate(ps, axis=0)
            pv = _dot_nt(p_st, vb)
            for gi in range(gm):
                g = st * gm + gi
                acc_sc[g] = acc_sc[g] * lanes(alphas[gi], GROUP_W) + pv[gi * tq:(gi + 1) * tq]

    def past_step(j, carry):
        block(kp_ref, vp_ref, pl.multiple_of(j * tk, tk), tk, False)
        return carry
    lax.fori_loop(0, past // tk, past_step, 0, unroll=2)
    block(kn_ref, vn_ref, 0, tq, True, pos0=past)

    lam = lam_ref[0]
    lane_head = _iota((1, GROUP_W), 1) >> 6
    o = jnp.zeros((tq, GROUP_W), F32)
    for h in range(N_HEADS):
        o0 = acc_sc[2 * h] / lanes(l_sc[2 * h], GROUP_W)
        o1 = acc_sc[2 * h + 1] / lanes(l_sc[2 * h + 1], GROUP_W)
        o = jnp.where(lane_head == h, o0 - lam * o1, o)
    ms = _mm_r(o * o, _head_block_ones(GROUP_W).astype(BF16)) * (1.0 / HEAD_DIM)
    o_ref[0] = o * lax.rsqrt(ms + EPS) * bn_ref[...] * (1.0 - lam_init)


def _diff_attn(lam, q, kn, vn, kp, vp, layer, bnorm, lam_init):
    b, t, _ = q.shape
    past = kp.shape[3]
    ng = 2 * N_HEADS
    tq, tk, gm = t, math.gcd(past // 2, 512), ng
    assert t == CHUNK and past % CHUNK == 0 and tk % LANES == 0
    seq_spec = pl.BlockSpec((1, GROUP_W, t), lambda i, j: (i, 0, 0))
    past_spec = pl.BlockSpec((None, 1, GROUP_W, past), lambda i, j: (layer, i, 0, 0))
    return pl.pallas_call(
        functools.partial(_diff_body, tq=tq, tk=tk, past=past, lam_init=lam_init, gm=gm),
        grid=(b, t // tq),
        in_specs=[pl.BlockSpec(memory_space=pltpu.SMEM), pl.BlockSpec((1, tq, GROUP_W), lambda i, j: (i, j, 0)),
                  seq_spec, seq_spec, past_spec, past_spec, _resident((1, GROUP_W))],
        out_specs=pl.BlockSpec((1, tq, GROUP_W), lambda i, j: (i, j, 0)),
        out_shape=jax.ShapeDtypeStruct((b, t, GROUP_W), F32),
        scratch_shapes=[pltpu.VMEM((ng * tq, GROUP_W), BF16), pltpu.VMEM((ng, tq, LANES), F32),
                        pltpu.VMEM((ng, tq, LANES), F32), pltpu.VMEM((ng, tq, GROUP_W), F32)],
        compiler_params=_cparams("parallel", "parallel"),
    )(lam, q, kn, vn, kp, vp, bnorm)


def _diff_prompt_body(lam_ref, q_ref, k_ref, v_ref, bn_ref, o_ref, qt_sc, m_sc, l_sc, acc_sc, *, tq, tk, lam_init):
    i = pl.program_id(1)
    ng = 2 * N_HEADS
    qt = (q_ref[0] * (DIFF_HALF ** -0.5 * LOG2E)).T
    feat = _iota((GROUP_W, tq), 0)
    for g in range(ng):
        lo = (g // 2) * HEAD_DIM + (g % 2) * DIFF_HALF
        qt_sc[g] = jnp.where((feat >= lo) & (feat < lo + DIFF_HALF), qt, 0.0).astype(BF16)
    m_sc[...] = jnp.full(m_sc.shape, -jnp.inf, F32)
    l_sc[...] = jnp.zeros(l_sc.shape, F32)
    acc_sc[...] = jnp.zeros(acc_sc.shape, F32)
    pq = i * tq + _iota((1, tq), 1)
    pq_f = pq.astype(F32)

    def block(k0, n, near):
        kb = k_ref[0, pl.ds(k0, n), :].astype(BF16)
        vb = v_ref[0, :, pl.ds(k0, n)].astype(BF16)
        pk = k0 + _iota((n, LANES), 0)
        pk_f = pk.astype(F32)
        if near:
            allowed = (_lanes(pk, tq) >> 6) <= (pq >> 6)
            shift = pq_f - jnp.abs(pq_f - _lanes(pk_f, tq))
        groups = range(ng)
        scores = []
        for g in groups:
            slope = 2.0 ** (-2 * (g // 2 + 1)) * LOG2E
            s = _dot(kb, qt_sc[g])
            scores.append(jnp.where(allowed, s + slope * shift, NEG) if near else s + _lanes(slope * pk_f, tq))
        m_prevs = [m_sc[g] for g in groups]
        m_news = [jnp.maximum(m_prevs[g], jnp.max(scores[g], axis=0, keepdims=True)) for g in groups]
        probs = [jnp.exp2(scores[g] - m_news[g]) for g in groups]
        for g in groups:
            alpha = jnp.exp2(m_prevs[g] - m_news[g])
            l_sc[g] = alpha * l_sc[g] + jnp.sum(probs[g], axis=0, keepdims=True)
            m_sc[g] = m_news[g]
            h = g // 2
            acc_sc[g] = acc_sc[g] * alpha + _dot(vb[h * HEAD_DIM:(h + 1) * HEAD_DIM, :], probs[g].astype(BF16))

    def prev_step(j, carry):
        block(pl.multiple_of(j * tk, tk), tk, False)
        return carry
    lax.fori_loop(0, (i * tq) // tk, prev_step, 0)

    @pl.when((i * tq) % tk != 0)
    def _():
        block(pl.multiple_of((i - 1) * tq, tq), tq, False)
    block(pl.multiple_of(i * tq, tq), tq, True)

    lam = lam_ref[0]
    heads = [acc_sc[2 * h] / l_sc[2 * h] - lam * (acc_sc[2 * h + 1] / l_sc[2 * h + 1]) for h in range(N_HEADS)]
    o = jnp.concatenate(heads, axis=0).T
    ms = _mm_r(o * o, _head_block_ones(GROUP_W).astype(BF16)) * (1.0 / HEAD_DIM)
    o_ref[0] = o * lax.rsqrt(ms + EPS) * bn_ref[...] * (1.0 - lam_init)


def _diff_attn_prompt(lam, q, k_rows, v_feat, bnorm, lam_init):
    b, t, _ = q.shape
    ng = 2 * N_HEADS
    tq = min(256, t)
    tk = 2 * tq if t % (2 * tq) == 0 else tq
    assert t % tq == 0 and tq % LANES == 0
    return pl.pallas_call(
        functools.partial(_diff_prompt_body, tq=tq, tk=tk, lam_init=lam_init),
        grid=(b, t // tq),
        in_specs=[pl.BlockSpec(memory_space=pltpu.SMEM), pl.BlockSpec((1, tq, GROUP_W), lambda i, j: (i, j, 0)),
                  pl.BlockSpec((1, t, GROUP_W), lambda i, j: (i, 0, 0)),
                  pl.BlockSpec((1, GROUP_W, t), lambda i, j: (i, 0, 0)), _resident((1, GROUP_W))],
        out_specs=pl.BlockSpec((1, tq, GROUP_W), lambda i, j: (i, j, 0)),
        out_shape=jax.ShapeDtypeStruct((b, t, GROUP_W), F32),
        scratch_shapes=[pltpu.VMEM((ng, GROUP_W, tq), BF16), pltpu.VMEM((ng, 1, tq), F32),
                        pltpu.VMEM((ng, 1, tq), F32), pltpu.VMEM((ng, HEAD_DIM, tq), F32)],
        compiler_params=_cparams("parallel", "parallel"),
    )(lam, q, k_rows, v_feat, bnorm)


def _stick_body(q_ref, kn_ref, vn_ref, kp_ref, vp_ref, o_ref, q4_sc, c_sc, acc_sc, *, tq, tk, past):
    i = pl.program_id(1)
    nr = N_HEADS * tq
    q = q_ref[0] * (HEAD_DIM ** -0.5 * LOG2E)
    lane = _iota((tq, GROUP_W), 1) >> 6
    for h in range(N_HEADS):
        q4_sc[h * tq:(h + 1) * tq, :] = jnp.where(lane == h, q, 0.0).astype(BF16)
    c_sc[...] = jnp.zeros(c_sc.shape, F32)
    acc_sc[...] = jnp.zeros(acc_sc.shape, F32)
    pq = past + i * tq + (_iota((nr, 1), 0) & (tq - 1))
    laters = {w: (_iota((w, w), 0) > _iota((w, w), 1)).astype(BF16)
              for w in {min(tq, STICK_SUB), min(tk, STICK_SUB)}}

    def block(k_ref, v_ref, k0, n, masked, pos0=0):
        kb = k_ref[0, :, pl.ds(k0, n)].astype(BF16)
        vb = v_ref[0, :, pl.ds(k0, n)].astype(BF16)
        sub = min(n, STICK_SUB)
        later = laters[sub]
        z = _dot(q4_sc[...], kb)
        ls = jnp.minimum(z, 0.0) - jnp.log2(1.0 + jnp.exp2(-jnp.abs(z)))
        lk = ls - z
        if masked:
            mask = (pos0 + k0 + _iota((1, n), 1)) < pq
            lk = jnp.where(mask, lk, 0.0)
        hi = lk.astype(BF16)
        lo = (lk - hi.astype(F32)).astype(BF16)
        carry = c_sc[...]
        es = []
        for sb in reversed(range(n // sub)):
            sl = slice(sb * sub, (sb + 1) * sub)
            after = _dot(hi[:, sl], later) + _dot(lo[:, sl], later)
            es.append(ls[:, sl] + after + _lanes(carry, sub))
            carry = carry + (after[:, 0:1] + lk[:, sb * sub:sb * sub + 1])
        c_sc[...] = carry
        e = es[0] if len(es) == 1 else jnp.concatenate(es[::-1], axis=1)
        if masked:
            a = jnp.where(mask, jnp.exp2(jnp.where(mask, e, 0.0)), 0.0)
        else:
            a = jnp.exp2(e)
        acc_sc[...] = acc_sc[...] + _dot_nt(a.astype(BF16), vb)

    block(kn_ref, vn_ref, 0, tq, True, pos0=past)
    nb = past // tk

    def past_step(j, carry):
        block(kp_ref, vp_ref, pl.multiple_of((nb - 1 - j) * tk, tk), tk, False)
        return carry
    lax.fori_loop(0, nb, past_step, 0)

    o = acc_sc[0:tq, :]
    for h in range(1, N_HEADS):
        o = jnp.where(lane == h, acc_sc[h * tq:(h + 1) * tq, :], o)
    o_ref[0] = o


def _stick_attn(q, kn, vn, kp, vp, layer):
    b, t, _ = q.shape
    past = kp.shape[3]
    tq, tk = t, math.gcd(past, 1024)
    assert tq & (tq - 1) == 0 and tk % min(tk, STICK_SUB) == 0
    seq_spec = pl.BlockSpec((1, GROUP_W, t), lambda i, j: (i, 0, 0))
    past_spec = pl.BlockSpec((None, 1, GROUP_W, past), lambda i, j: (layer, i, 0, 0))
    return pl.pallas_call(
        functools.partial(_stick_body, tq=tq, tk=tk, past=past),
        grid=(b, t // tq),
        in_specs=[pl.BlockSpec((1, tq, GROUP_W), lambda i, j: (i, j, 0)), seq_spec, seq_spec, past_spec, past_spec],
        out_specs=pl.BlockSpec((1, tq, GROUP_W), lambda i, j: (i, j, 0)),
        out_shape=jax.ShapeDtypeStruct((b, t, GROUP_W), F32),
        scratch_shapes=[pltpu.VMEM((N_HEADS * tq, GROUP_W), BF16), pltpu.VMEM((N_HEADS * tq, LANES), F32),
                        pltpu.VMEM((N_HEADS * tq, GROUP_W), F32)],
        compiler_params=_cparams("parallel", "parallel"),
    )(q, kn, vn, kp, vp)


def _stick_prompt_body(q_ref, k_ref, v_ref, o_ref, qt_sc, c_sc, acc_sc, *, tq, tk):
    i = pl.program_id(1)
    qt = (q_ref[0] * (HEAD_DIM ** -0.5 * LOG2E)).T
    feat_head = _iota((GROUP_W, tq), 0) >> 6
    for h in range(N_HEADS):
        qt_sc[h] = jnp.where(feat_head == h, qt, 0.0).astype(BF16)
    c_sc[...] = jnp.zeros(c_sc.shape, F32)
    acc_sc[...] = jnp.zeros(acc_sc.shape, F32)
    pq = i * tq + _iota((1, tq), 1)
    sub = min(tq, STICK_SUB)
    later = (_iota((sub, sub), 1) > _iota((sub, sub), 0)).astype(BF16)

    def block(k0, n, masked):
        kb = k_ref[0, pl.ds(k0, n), :].astype(BF16)
        vb = v_ref[0, :, pl.ds(k0, n)].astype(BF16)
        if masked:
            mask = _lanes(k0 + _iota((n, LANES), 0), tq) < pq
        heads = range(N_HEADS)
        zs = [_dot(kb, qt_sc[h]) for h in heads]
        lss = [jnp.minimum(z, 0.0) - jnp.log2(1.0 + jnp.exp2(-jnp.abs(z))) for z in zs]
        lks = [ls - z for ls, z in zip(lss, zs)]
        if masked:
            lks = [jnp.where(mask, lk, 0.0) for lk in lks]
        es = []
        for h in heads:
            lk = lks[h]
            hi = lk.astype(BF16)
            lo = (lk - hi.astype(F32)).astype(BF16)
            carry = c_sc[h]
            parts = []
            for sb in reversed(range(n // sub)):
                sl = slice(sb * sub, (sb + 1) * sub)
                after = _dot(later, hi[sl]) + _dot(later, lo[sl])
                parts.append(lss[h][sl] + after + carry)
                carry = carry + (after[0:1] + lk[sb * sub:sb * sub + 1])
            c_sc[h] = carry
            es.append(parts[0] if len(parts) == 1 else jnp.concatenate(parts[::-1], axis=0))
        for h in heads:
            if masked:
                a = jnp.where(mask, jnp.exp2(jnp.where(mask, es[h], 0.0)), 0.0)
            else:
                a = jnp.exp2(es[h])
            acc_sc[h] = acc_sc[h] + _dot(vb[h * HEAD_DIM:(h + 1) * HEAD_DIM, :], a.astype(BF16))

    block(pl.multiple_of(i * tq, tq), tq, True)

    @pl.when((i * tq) % tk != 0)
    def _():
        block(pl.multiple_of((i - 1) * tq, tq), tq, False)
    nfull = (i * tq) // tk

    def prev_step(j, carry):
        block(pl.multiple_of((nfull - 1 - j) * tk, tk), tk, False)
        return carry
    lax.fori_loop(0, nfull, prev_step, 0)

    o_ref[0] = jnp.concatenate([acc_sc[h] for h in range(N_HEADS)], axis=0).T


def _stick_attn_prompt(q, k_rows, v_feat):
    b, t, _ = q.shape
    tq = min(256, t)
    tk = 2 * tq if t % (2 * tq) == 0 else tq
    assert t % tq == 0 and tq % LANES == 0 and tk % min(tk, STICK_SUB) == 0
    return pl.pallas_call(
        functools.partial(_stick_prompt_body, tq=tq, tk=tk),
        grid=(b, t // tq),
        in_specs=[pl.BlockSpec((1, tq, GROUP_W), lambda i, j: (i, j, 0)),
                  pl.BlockSpec((1, t, GROUP_W), lambda i, j: (i, 0, 0)),
                  pl.BlockSpec((1, GROUP_W, t), lambda i, j: (i, 0, 0))],
        out_specs=pl.BlockSpec((1, tq, GROUP_W), lambda i, j: (i, j, 0)),
        out_shape=jax.ShapeDtypeStruct((b, t, GROUP_W), F32),
        scratch_shapes=[pltpu.VMEM((N_HEADS, GROUP_W, tq), BF16), pltpu.VMEM((N_HEADS, 1, tq), F32),
                        pltpu.VMEM((N_HEADS, HEAD_DIM, tq), F32)],
        compiler_params=_cparams("parallel", "parallel"),
    )(q, k_rows, v_feat)


def _pair_mask(n, s, reps=1):
    r, c = _iota((reps * n, n), 0) & (n - 1), _iota((reps * n, n), 1)
    return ((r // (2 * s)) == (c // (2 * s))) & (((r // s) & 1) == 1) & (((c // s) & 1) == 0)


def _state_spec(nb):
    return pl.BlockSpec((nb, N_HEADS, HEAD_DIM, HEAD_DIM), lambda i, j: (i, 0, 0, 0))


def _load_state(s_sc, s0_ref, transpose):
    s_sc[...] = jnp.zeros(s_sc.shape, F32)
    for b in range(s_sc.shape[0]):
        for h in range(N_HEADS):
            blk = s0_ref[b, h]
            s_sc[b, h * HEAD_DIM:(h + 1) * HEAD_DIM, h * HEAD_DIM:(h + 1) * HEAD_DIM] = blk.T if transpose else blk


def _store_state(sout_ref, s_sc, transpose):
    for b in range(s_sc.shape[0]):
        for h in range(N_HEADS):
            blk = s_sc[b, h * HEAD_DIM:(h + 1) * HEAD_DIM, h * HEAD_DIM:(h + 1) * HEAD_DIM]
            sout_ref[b, h] = blk.T if transpose else blk


def _head_rows(x, lane_head):
    return jnp.concatenate([jnp.where(lane_head == h, x, 0.0) for h in range(N_HEADS)], axis=0)


def _head_diag(x, lane_head):
    n = x.shape[0] // N_HEADS
    out = x[:n]
    for h in range(1, N_HEADS):
        out = jnp.where(lane_head == h, x[h * n:(h + 1) * n], out)
    return out


def _gdn_body(x_ref, z_ref, abg_ref, cw_ref, cs_ref, s0_ref, alog_ref, dtb_ref, an_ref,
              o_ref, sout_ref, s_sc, xb_sc, *, nc, nb):
    c = pl.program_id(1)
    L = CHUNK
    pad = 8

    @pl.when(c == 0)
    def _():
        _load_state(s_sc, s0_ref, transpose=False)
        xb_sc[:, pad - 3:pad, :] = cs_ref[...]

    ones_bd = _head_block_ones(GROUP_W)
    ones_bd16 = ones_bd.astype(BF16)
    er, ec = _iota((ABG_W, GROUP_W), 0), _iota((ABG_W, GROUP_W), 1) >> 6
    e_beta = (er == ec).astype(BF16)
    e_g = (er == ec + N_HEADS).astype(BF16)
    ri, ci = _iota((L, L), 0), _iota((L, L), 1)
    tri16 = (ri >= ci).astype(BF16)
    incl = ri >= ci
    strict = ri > ci
    eye = (ri == ci).astype(F32)
    pair_masks = [_pair_mask(L, sz) for sz in (1, 2, 4, 8, 16, 32)]
    lane_head = _iota((1, GROUP_W), 1) >> 6
    cw = cw_ref[...]

    a_all, per_seq = [], []
    for b in range(nb):
        x = x_ref[b]
        xb_sc[b, pad:pad + L, :] = x
        u = (x * cw[3:4] + xb_sc[b, pad - 1:pad - 1 + L, :] * cw[2:3]
             + xb_sc[b, pad - 2:pad - 2 + L, :] * cw[1:2] + xb_sc[b, pad - 3:pad - 3 + L, :] * cw[0:1])
        xb_sc[b, pad - 3:pad, :] = x[L - 3:L, :]
        u = u * _sigmoid(u)
        q = u[:, :GROUP_W]
        k = u[:, GROUP_W:2 * GROUP_W]
        v = u[:, 2 * GROUP_W:]
        q = q * lax.rsqrt(_mm_r(q * q, ones_bd16) + EPS) * HEAD_DIM ** -0.5
        k = k * lax.rsqrt(_mm_r(k * k, ones_bd16) + EPS)

        abg = abg_ref[b]
        beta_n = _sigmoid(abg)
        g_n = -jnp.exp(alog_ref[...]) * _softplus(abg + dtb_ref[...])
        beta = _mm_r(beta_n, e_beta)
        gc_n = _mm_l(tri16, g_n)
        gc_t = gc_n.T
        gc = _mm_r(gc_n, e_g)
        exp_g = jnp.exp(gc)
        g_last = gc[L - 1:L, :]

        kb = k * beta
        k16 = k.astype(BF16)
        a4 = _bdot(_head_rows(kb, lane_head).astype(BF16), k16, _NT)
        qk4 = _bdot(_head_rows(q, lane_head).astype(BF16), k16, _NT)
        dec = []
        for h in range(N_HEADS):
            diff = gc_n[:, N_HEADS + h:N_HEADS + h + 1] - gc_t[N_HEADS + h:N_HEADS + h + 1, :]
            dec.append(jnp.where(incl, jnp.exp(jnp.where(incl, diff, 0.0)), 0.0))
        dec = jnp.stack(dec)
        a_all.append(jnp.where(strict, a4.reshape(N_HEADS, L, L) * dec, 0.0))
        per_seq.append((q, k, v * beta, kb * exp_g, q * exp_g, k * jnp.exp(g_last - gc), jnp.exp(g_last),
                        qk4.reshape(N_HEADS, L, L) * dec))

    a_all = jnp.concatenate(a_all, axis=0)
    t_inv = eye - jnp.where(pair_masks[0], a_all, 0.0)
    a16 = a_all.astype(BF16)
    for pm in pair_masks[1:]:
        t16 = t_inv.astype(BF16)
        t_inv = t_inv - _bdot(t16, _bdot(jnp.where(pm, a16, 0.0), t16, _BNN).astype(BF16), _BNN)

    for b in range(nb):
        q, k, vb, kbg, qg, k_dec, decay_last, qk = per_seq[b]
        s = s_sc[b]
        s16 = s.astype(BF16)
        rhs = vb - _bdot(kbg.astype(BF16), s16, _NN)
        t4 = t_inv[b * N_HEADS:(b + 1) * N_HEADS].reshape(N_HEADS * L, L)
        w = _head_diag(_mm1(t4, rhs), lane_head)
        o = _bdot(qg.astype(BF16), s16, _NN) + _head_diag(_mm1(qk.reshape(N_HEADS * L, L), w), lane_head)
        s_sc[b] = s * decay_last + _mm1(k_dec.T, w) * ones_bd

        ms = _mm_r(o * o, ones_bd16) * (1.0 / HEAD_DIM)
        zg = z_ref[b]
        o_ref[b] = o * lax.rsqrt(ms + EPS) * an_ref[...] * (zg * _sigmoid(zg))

    @pl.when(c == nc - 1)
    def _():
        _store_state(sout_ref, s_sc, transpose=False)


def _gdn(aqkv, az, abg, conv_w, conv_state, s0, alog_row, dtb_row, anorm_row):
    b, t, _ = aqkv.shape
    nc = t // CHUNK
    nb = SEQ_PER_STEP
    assert b % nb == 0
    cmap = lambda i, j: (i, j, 0)
    bmap = lambda i, j: (i, 0, 0)
    return pl.pallas_call(
        functools.partial(_gdn_body, nc=nc, nb=nb),
        grid=(b // nb, nc),
        in_specs=[pl.BlockSpec((nb, CHUNK, 3 * GROUP_W), cmap), pl.BlockSpec((nb, CHUNK, GROUP_W), cmap),
                  pl.BlockSpec((nb, CHUNK, ABG_W), cmap), _resident((CONV_W, 3 * GROUP_W)),
                  pl.BlockSpec((nb, CONV_W - 1, 3 * GROUP_W), bmap), _state_spec(nb),
                  _resident((1, ABG_W)), _resident((1, ABG_W)), _resident((1, GROUP_W))],
        out_specs=[pl.BlockSpec((nb, CHUNK, GROUP_W), cmap), _state_spec(nb)],
        out_shape=[jax.ShapeDtypeStruct((b, t, GROUP_W), F32), jax.ShapeDtypeStruct(s0.shape, F32)],
        scratch_shapes=[pltpu.VMEM((nb, GROUP_W, GROUP_W), F32), pltpu.VMEM((nb, 8 + CHUNK, 3 * GROUP_W), F32)],
        compiler_params=_cparams("parallel", "arbitrary"),
    )(aqkv, az, abg, conv_w, conv_state, s0, alog_row, dtb_row, anorm_row)


def _hgrn_body(q_ref, f_ref, i_ref, g_ref, lb_ref, dn_ref, s0_ref, o_ref, sout_ref, s_sc, *, nc, nb):
    c = pl.program_id(1)
    L = CHUNK

    @pl.when(c == 0)
    def _():
        _load_state(s_sc, s0_ref, transpose=True)

    lb = lb_ref[...]
    ones_bd = _head_block_ones(GROUP_W)
    ones_bd16 = ones_bd.astype(BF16)
    ri, ci = _iota((L, L), 0), _iota((L, L), 1)
    tri16 = (ri >= ci).astype(BF16)
    sizes = (1, 2, 4, 8, 16, 32)
    sel16 = jnp.concatenate([(ci == (ri // (2 * sz)) * (2 * sz) + sz).astype(BF16) for sz in sizes], axis=0)
    eye4 = (_iota((N_HEADS * L, L), 0) & (L - 1)) == _iota((N_HEADS * L, L), 1)
    pair_masks4 = [_pair_mask(L, sz, reps=N_HEADS) for sz in sizes]
    lane_head = _iota((1, GROUP_W), 1) >> 6
    rows = _iota((L, 1), 0)

    pre = []
    for b in range(nb):
        fl = f_ref[b]
        log_f = _log_sigmoid(fl) + jnp.log(1.0 + lb * jnp.exp(-fl))
        k = (1.0 - lb) * _sigmoid(-fl)
        q = q_ref[b] * HEAD_DIM ** -0.5
        bc = _mm_l(tri16, log_f)
        pre.append((q, k, bc))
    b_refs = [_mm_l(sel16, bc) for _, _, bc in pre]
    a_all = [jnp.where(eye4, _mm1(_head_rows(q, lane_head), k, _NT), 0.0) for q, k, _ in pre]
    for lv, sz in enumerate(sizes):
        later = ((rows // sz) & 1) == 1
        for b in range(nb):
            q, k, bc = pre[b]
            b_ref = b_refs[b][lv * L:(lv + 1) * L]
            x = jnp.where(later, q * jnp.exp(jnp.where(later, bc - b_ref, 0.0)), 0.0)
            y = jnp.where(later, 0.0, k * jnp.exp(jnp.where(later, 0.0, b_ref - bc)))
            a_all[b] = a_all[b] + jnp.where(pair_masks4[lv], _mm1(_head_rows(x, lane_head), y, _NT), 0.0)

    for b in range(nb):
        q, k, bc = pre[b]
        v = i_ref[b]
        st = s_sc[b]
        o = _mm1(q * jnp.exp(bc), st, _NT) + _head_diag(_mm1(a_all[b], v), lane_head)
        b_last = bc[L - 1:L, :]
        st_new = st * jnp.exp(b_last) + _mm1(v.T, k * jnp.exp(b_last - bc)) * ones_bd
        s_sc[b] = st_new

        ms = _mm_r(o * o, ones_bd16) * (1.0 / HEAD_DIM)
        zg = g_ref[b]
        o_ref[b] = o * lax.rsqrt(ms + EPS) * dn_ref[...] * (zg * _sigmoid(zg))

    @pl.when(c == nc - 1)
    def _():
        _store_state(sout_ref, s_sc, transpose=True)


def _hgrn(dq, df, di, dg, lb_row, dnorm_row, s0):
    b, t, _ = dq.shape
    nc = t // CHUNK
    nb = SEQ_PER_STEP
    assert b % nb == 0
    cmap = lambda i, j: (i, j, 0)
    bmap = lambda i, j: (i, 0, 0)
    cspec = pl.BlockSpec((nb, CHUNK, GROUP_W), cmap)
    return pl.pallas_call(
        functools.partial(_hgrn_body, nc=nc, nb=nb),
        grid=(b // nb, nc),
        in_specs=[cspec, cspec, cspec, cspec, _resident((1, GROUP_W)), _resident((1, GROUP_W)),
                  _state_spec(nb)],
        out_specs=[cspec, _state_spec(nb)],
        out_shape=[jax.ShapeDtypeStruct((b, t, GROUP_W), F32), jax.ShapeDtypeStruct(s0.shape, F32)],
        scratch_shapes=[pltpu.VMEM((nb, GROUP_W, GROUP_W), F32)],
        compiler_params=_cparams("parallel", "arbitrary"),
    )(dq, df, di, dg, lb_row, dnorm_row, s0)


IN_WIDTHS = (3 * GROUP_W,) + (GROUP_W,) * 11 + (ABG_W,)
KV_SEGMENTS = (3, 4, 6, 7)
K_SEGMENTS = (3, 6)


def _layer(x, l, st, mk, mv, mem_layer, prm, final_gain):
    b, t, _ = x.shape
    m = b * t
    tm = math.gcd(m, ROW_TILE)
    has_past = st["b_k"] is not None
    outs = _fused_linear([x.reshape(m, D_MODEL)], prm["w_in"][l], IN_WIDTHS, gain=prm["norm_mix"][l], tm=tm,
                         feat_major=KV_SEGMENTS, rows_per_seq=t, rows_too=() if has_past else K_SEGMENTS)
    outs = [o if idx in KV_SEGMENTS else o.reshape(b, t, -1) for idx, o in enumerate(outs)]
    (a_qkv, a_z, b_q, b_k, b_v, c_q, c_k, c_v, d_q, d_f, d_i, d_g, a_bg) = outs[:len(IN_WIDTHS)]

    o_a, a_s = _gdn(a_qkv, a_z, a_bg, prm["a_conv_w"][l], st["a_conv"], st["a_S"].astype(F32),
                    prm["alog_row"][l], prm["dtb_row"][l], prm["a_norm"][l])
    lam_init = 0.8 - 0.6 * math.exp(-0.3 * l)
    if has_past:
        o_b = _diff_attn(prm["lam"][l], b_q, b_k, b_v, st["b_k"], st["b_v"], l, prm["b_norm"][l], lam_init)
        o_c = _stick_attn(c_q, c_k, c_v, st["c_k"], st["c_v"], l)
    else:
        b_k_rows, c_k_rows = outs[len(IN_WIDTHS):]
        o_b = _diff_attn_prompt(prm["lam"][l], b_q, b_k_rows, b_v, prm["b_norm"][l], lam_init)
        o_c = _stick_attn_prompt(c_q, c_k_rows, c_v)
    o_d, d_s = _hgrn(d_q, d_f, d_i, d_g, prm["lb"][l], prm["d_norm"][l], st["d_S"].astype(F32))

    x2 = _cross(x, (o_a, o_b, o_c, o_d), prm["w_out"][l], prm["norm_cross"][l], prm["w_cq"][l], mk, mv, mem_layer,
                prm["w_co"][l])
    x2 = _ffn(x2.reshape(m, D_MODEL), prm["norm_ffn"][l], prm["w_gate"][l], prm["w_up"][l], prm["w_down"][l],
              final_gain=final_gain, tm=tm)

    if t >= CONV_W - 1:
        conv_new = a_qkv[:, t - (CONV_W - 1):, :]
    else:
        conv_new = jnp.concatenate([st["a_conv"], a_qkv], axis=1)[:, -(CONV_W - 1):, :]
    heads = lambda a: jnp.transpose(a.reshape(b, N_HEADS, HEAD_DIM, t), (0, 3, 1, 2))
    new = (conv_new, a_s, heads(b_k), heads(b_v), heads(c_k), heads(c_v), d_s)
    return x2.reshape(b, t, D_MODEL), new


def kernel(x_prompt, x_sample, mem_prompt, state_a_conv, state_a_S, cache_b_k, cache_b_v, cache_c_k, cache_c_v,
           state_d_S, cache_mem_k, cache_mem_v, norm_mix, w_in, a_conv_w, a_A_log, a_dt_bias, a_norm, b_lam_q1,
           b_lam_k1, b_lam_q2, b_lam_k2, b_norm, d_lb, d_norm, w_out, norm_cross, norm_memtok, w_cq, w_ck, w_cv,
           w_co, norm_ffn, w_gate, w_up, w_down, norm_final):
    depth = w_in.shape[0]
    n_main = 4 * GROUP_W
    w_in_p = jnp.concatenate(
        [w_in[:, :, :n_main], w_in[:, :, n_main + 2 * N_HEADS:], w_in[:, :, n_main:n_main + 2 * N_HEADS],
         jnp.zeros((depth, D_MODEL, ABG_W - 2 * N_HEADS), w_in.dtype)], axis=2).astype(BF16)
    pad_row = lambda a: jnp.pad(a.astype(F32), ((0, 0), (N_HEADS, ABG_W - 2 * N_HEADS)))[:, None, :]
    tile_row = lambda a: jnp.tile(a.astype(F32), (1, N_HEADS))[:, None, :]
    p_lb = jax.nn.softmax(d_lb.astype(F32), axis=0)
    lam = (jnp.exp(jnp.sum(b_lam_q1.astype(F32) * b_lam_k1.astype(F32), axis=-1))
           - jnp.exp(jnp.sum(b_lam_q2.astype(F32) * b_lam_k2.astype(F32), axis=-1))
           + jnp.array([0.8 - 0.6 * math.exp(-0.3 * l) for l in range(depth)], F32))
    prm = {
        "w_in": w_in_p, "norm_mix": norm_mix, "a_conv_w": a_conv_w.astype(F32),
        "alog_row": pad_row(a_A_log), "dtb_row": pad_row(a_dt_bias), "a_norm": tile_row(a_norm),
        "lam": lam[:, None], "b_norm": tile_row(b_norm), "lb": (jnp.cumsum(p_lb, axis=0) - p_lb[0])[:, None, :],
        "d_norm": tile_row(d_norm), "w_out": w_out.astype(BF16), "norm_cross": norm_cross,
        "w_cq": w_cq.astype(BF16), "w_co": w_co.astype(BF16), "norm_ffn": norm_ffn,
        "w_gate": w_gate.astype(BF16), "w_up": w_up.astype(BF16), "w_down": w_down.astype(BF16),
    }
    w_ckv = jnp.concatenate([w_ck, w_cv], axis=2).astype(BF16)

    bp, tp, _ = x_prompt.shape
    n_mem = mem_prompt.shape[1]
    h = x_prompt
    p_new = []
    for l in range(depth):
        mk, mv = _fused_linear([mem_prompt.reshape(bp * n_mem, D_MODEL)], w_ckv[l], (D_MODEL, D_MODEL),
                               gain=norm_memtok[l], tm=256)
        mk = mk.reshape(1, bp, n_mem, D_MODEL)
        mv = mv.reshape(1, bp, n_mem, D_MODEL)
        st = {"a_conv": jnp.zeros((bp, CONV_W - 1, 3 * GROUP_W), F32),
              "a_S": jnp.zeros((bp, N_HEADS, HEAD_DIM, HEAD_DIM), F32), "b_k": None, "b_v": None,
              "c_k": None, "c_v": None, "d_S": jnp.zeros((bp, N_HEADS, HEAD_DIM, HEAD_DIM), F32)}
        h, new = _layer(h, l, st, mk, mv, 0, prm, norm_final if l == depth - 1 else None)
        mem4 = lambda a: a.reshape(bp, n_mem, MEM_HEADS, MEM_HEAD_DIM)
        p_new.append(new + (mem4(mk), mem4(mv)))
    y_prompt = h
    p_stacked = [jnp.stack(c) for c in zip(*p_new)]

    bs, ts, _ = x_sample.shape
    past = cache_b_k.shape[2]
    h = x_sample
    s_new = []
    flat = lambda a: jnp.transpose(a, (0, 1, 3, 4, 2)).reshape(depth, bs, GROUP_W, past)
    caches = {"b_k": flat(cache_b_k), "b_v": flat(cache_b_v), "c_k": flat(cache_c_k), "c_v": flat(cache_c_v)}
    for l in range(depth):
        st = {"a_conv": state_a_conv[l], "a_S": state_a_S[l], "d_S": state_d_S[l], **caches}
        h, new = _layer(h, l, st, cache_mem_k, cache_mem_v, l, prm, norm_final if l == depth - 1 else None)
        s_new.append(new)
    y_sample = h
    s_stacked = [jnp.stack(c) for c in zip(*s_new)]

    return (y_prompt, y_sample, *p_stacked, *s_stacked)
```

```python
import functools
import math

import jax
import jax.numpy as jnp
from jax import lax
from jax.experimental import pallas as pl
from jax.experimental.pallas import tpu as pltpu

F32 = jnp.float32
BF16 = jnp.bfloat16

D_MODEL = 1024
GROUP_W = 256
N_HEADS = 4
HEAD_DIM = 64
DIFF_HALF = 32
CHUNK = 64
CONV_W = 4
MEM_HEADS = 4
MEM_HEAD_DIM = 256
D_FF = 2816
EPS = 1e-6
NEG = -1e30
LOG2E = 1.4426950408889634
LANES = 128
ABG_W = 128
FF_CHUNK = 256
ROW_TILE = 512
DECODE_SEQS = 2
CROSS_SEQS = 4
STICK_SUB = 256
SEQ_PER_STEP = 4
VMEM_LIMIT = 56 * 1024 * 1024


def _cparams(*sem):
    return pltpu.CompilerParams(dimension_semantics=sem, vmem_limit_bytes=VMEM_LIMIT)


def _dot(a, b):
    return jnp.dot(a, b, preferred_element_type=F32)


def _dot_nt(a, b):
    return lax.dot_general(a, b, (((1,), (1,)), ((), ())), preferred_element_type=F32)


_NN = (((1,), (0,)), ((), ()))
_NT = (((1,), (1,)), ((), ()))
_BNN = (((2,), (1,)), ((0,), (0,)))


def _split(x):
    hi = x.astype(BF16)
    return hi, (x - hi.astype(F32)).astype(BF16)


def _bdot(a, b, dims):
    return lax.dot_general(a, b, dims, preferred_element_type=F32)


def _mm(a, b, dims=_NN):
    ah, al = a if isinstance(a, tuple) else _split(a)
    bh, bl = b if isinstance(b, tuple) else _split(b)
    return _bdot(ah, bh, dims) + (_bdot(ah, bl, dims) + _bdot(al, bh, dims))


def _mm1(a, b, dims=_NN):
    return _bdot(a.astype(BF16), b.astype(BF16), dims)


def _mm_r(a, b16):
    ah, al = _split(a)
    return _bdot(ah, b16, _NN) + _bdot(al, b16, _NN)


def _mm_l(a16, b):
    bh, bl = _split(b)
    return _bdot(a16, bh, _NN) + _bdot(a16, bl, _NN)


def _rms(x, g):
    return x * lax.rsqrt(jnp.mean(x * x, axis=-1, keepdims=True) + EPS) * g


def _sigmoid(x):
    return 1.0 / (1.0 + jnp.exp(-x))


def _log_sigmoid(x):
    return jnp.minimum(x, 0.0) - jnp.log(1.0 + jnp.exp(-jnp.abs(x)))


def _softplus(x):
    return jnp.maximum(x, 0.0) + jnp.log(1.0 + jnp.exp(-jnp.abs(x)))


def _iota(shape, dim):
    return lax.broadcasted_iota(jnp.int32, shape, dim)


def _lanes(x, n):
    return x[:, :n] if n <= LANES else jnp.concatenate([x] * (n // LANES), axis=1)


def _head_block_ones(n):
    return ((_iota((n, n), 0) >> 6) == (_iota((n, n), 1) >> 6)).astype(F32)


def _resident(shape):
    return pl.BlockSpec(shape, lambda *_: (0,) * len(shape), pipeline_mode=pl.Buffered(1))


def _weight(w, layer):
    return pl.BlockSpec((None,) + w.shape[1:], lambda *_: (layer, 0, 0), pipeline_mode=pl.Buffered(1))


def _linear_body(*refs, ks, segs, has_norm, has_res, feat_major, rows_too):
    n_in = len(ks)
    x_refs = refs[:n_in]
    pos = n_in
    g_ref = None
    if has_norm:
        g_ref = refs[pos]
        pos += 1
    w_ref = refs[pos]
    pos += 1
    r_ref = None
    if has_res:
        r_ref = refs[pos]
        pos += 1
    out_refs = refs[pos:]
    xb = []
    for x_ref in x_refs:
        x = x_ref[...].astype(F32)
        if has_norm:
            x = _rms(x, g_ref[...])
        xb.append(x.astype(BF16))
    extra_refs = dict(zip(rows_too, out_refs[len(segs):]))
    for idx, (o_ref, (s, e)) in enumerate(zip(out_refs, segs)):
        acc = None
        k0 = 0
        for xk, k in zip(xb, ks):
            t = _dot(xk, w_ref[k0:k0 + k, s:e])
            acc = t if acc is None else acc + t
            k0 += k
        if has_res:
            acc = acc + r_ref[:, s:e]
        if idx in feat_major:
            rows = acc.shape[0] // o_ref.shape[0]
            for sq in range(o_ref.shape[0]):
                o_ref[sq] = acc[sq * rows:(sq + 1) * rows].T
        else:
            o_ref[...] = acc
        if idx in extra_refs:
            extra_refs[idx][...] = acc


def _fused_linear(xs, w, layer, widths, gain=None, res=None, tm=256, feat_major=(), rows_per_seq=None, rows_too=()):
    m = xs[0].shape[0]
    ks = tuple(x.shape[1] for x in xs)
    n = w.shape[2]
    segs = []
    s = 0
    for wd in widths:
        segs.append((s, s + wd))
        s += wd
    assert s == n and m % tm == 0
    out_specs, out_shape = [], []
    for idx, wd in enumerate(widths):
        if idx in feat_major:
            if rows_per_seq >= tm:
                tiles = rows_per_seq // tm
                assert rows_per_seq % tm == 0
                out_specs.append(pl.BlockSpec((1, wd, tm), lambda i, tiles=tiles: (i // tiles, 0, i % tiles)))
            else:
                assert tm % rows_per_seq == 0
                out_specs.append(pl.BlockSpec((tm // rows_per_seq, wd, rows_per_seq), lambda i: (i, 0, 0)))
            out_shape.append(jax.ShapeDtypeStruct((m // rows_per_seq, wd, rows_per_seq), F32))
        else:
            out_specs.append(pl.BlockSpec((tm, wd), lambda i: (i, 0)))
            out_shape.append(jax.ShapeDtypeStruct((m, wd), F32))
    for idx in rows_too:
        out_specs.append(pl.BlockSpec((tm, widths[idx]), lambda i: (i, 0)))
        out_shape.append(jax.ShapeDtypeStruct((m, widths[idx]), F32))
    in_specs = [pl.BlockSpec((tm, k), lambda i: (i, 0)) for k in ks]
    args = list(xs)
    if gain is not None:
        in_specs.append(_resident((1, ks[0])))
        args.append(gain.reshape(1, -1))
    assert w.shape[1] == sum(ks)
    in_specs.append(_weight(w, layer))
    args.append(w)
    if res is not None:
        in_specs.append(pl.BlockSpec((tm, n), lambda i: (i, 0)))
        args.append(res)
    return pl.pallas_call(
        functools.partial(_linear_body, ks=ks, segs=tuple(segs), has_norm=gain is not None, has_res=res is not None,
                          feat_major=tuple(feat_major), rows_too=tuple(rows_too)),
        grid=(m // tm,),
        in_specs=in_specs,
        out_specs=out_specs,
        out_shape=out_shape,
        compiler_params=_cparams("parallel"),
    )(*args)


def _ffn_body(*refs, final):
    if final:
        x_ref, g_ref, wg_ref, wu_ref, wd_ref, gf_ref, o_ref = refs
    else:
        x_ref, g_ref, wg_ref, wu_ref, wd_ref, o_ref = refs
    x = x_ref[...]
    hb = _rms(x, g_ref[...]).astype(BF16)
    acc = x
    for c in range(0, D_FF, FF_CHUNK):
        gate = _dot(hb, wg_ref[:, c:c + FF_CHUNK])
        up = _dot(hb, wu_ref[:, c:c + FF_CHUNK])
        act = (gate * _sigmoid(gate) * up).astype(BF16)
        acc = acc + _dot(act, wd_ref[c:c + FF_CHUNK, :])
    if final:
        acc = _rms(acc, gf_ref[...])
    o_ref[...] = acc


def _ffn(x, gain, wg, wu, wd, layer, final_gain=None, tm=256):
    m = x.shape[0]
    final = final_gain is not None
    in_specs = [pl.BlockSpec((tm, D_MODEL), lambda i: (i, 0)), _resident((1, D_MODEL)),
                _weight(wg, layer), _weight(wu, layer), _weight(wd, layer)]
    args = [x, gain.reshape(1, -1), wg, wu, wd]
    if final:
        in_specs.append(_resident((1, D_MODEL)))
        args.append(final_gain.reshape(1, -1))
    return pl.pallas_call(
        functools.partial(_ffn_body, final=final),
        grid=(m // tm,),
        in_specs=in_specs,
        out_specs=pl.BlockSpec((tm, D_MODEL), lambda i: (i, 0)),
        out_shape=jax.ShapeDtypeStruct((m, D_MODEL), F32),
        compiler_params=_cparams("parallel"),
    )(*args)


def _cross_body(x_ref, oa_ref, ob_ref, oc_ref, od_ref, wout_ref, g_ref, wq_ref, mk_ref, mv_ref, wo_ref, o_ref):
    ns, tq, _ = x_ref.shape
    rows = ns * tq
    x = x_ref[...].reshape(rows, D_MODEL)
    for gi, m_ref in enumerate((oa_ref, ob_ref, oc_ref, od_ref)):
        x = x + _dot(m_ref[...].reshape(rows, GROUP_W).astype(BF16), wout_ref[gi * GROUP_W:(gi + 1) * GROUP_W, :])
    hb = _rms(x, g_ref[...]).astype(BF16)
    q = (_dot(hb, wq_ref[...]) * MEM_HEAD_DIM ** -0.5).astype(BF16)
    head = lambda h: slice(h * MEM_HEAD_DIM, (h + 1) * MEM_HEAD_DIM)
    pairs = [(sq, h) for sq in range(ns) for h in range(MEM_HEADS)]
    if len(mk_ref.shape) == 4:
        mem = lambda ref, sq, h: ref[sq, :, h, :].astype(BF16)
    else:
        mem = lambda ref, sq, h: ref[sq, :, head(h)].astype(BF16)
    scores = [_dot_nt(q[sq * tq:(sq + 1) * tq, head(h)], mem(mk_ref, sq, h)) for sq, h in pairs]
    probs = []
    for s in scores:
        p = jnp.exp(s - jnp.max(s, axis=-1, keepdims=True))
        probs.append((p / jnp.sum(p, axis=-1, keepdims=True)).astype(BF16))
    outs = [_dot(p, mem(mv_ref, sq, h)).astype(BF16) for p, (sq, h) in zip(probs, pairs)]
    o = jnp.concatenate([jnp.concatenate(outs[sq * MEM_HEADS:(sq + 1) * MEM_HEADS], axis=1) for sq in range(ns)], axis=0)
    o_ref[...] = (x + _dot(o, wo_ref[...])).reshape(ns, tq, D_MODEL)


def _cross(x, mix, w_out, gain, wq, mk, mv, layer, wo, w_layer):
    b, t, _ = x.shape
    nm = mk.shape[2]
    tq = min(ROW_TILE, t)
    ns = CROSS_SEQS if CROSS_SEQS * t <= ROW_TILE else 1
    assert t % tq == 0 and b % ns == 0
    row_spec = lambda w: pl.BlockSpec((ns, tq, w), lambda i, j: (i, j, 0))
    mem_spec = pl.BlockSpec((None, ns) + mk.shape[2:], lambda i, j: (layer, i) + (0,) * (mk.ndim - 2))
    return pl.pallas_call(
        _cross_body,
        grid=(b // ns, t // tq),
        in_specs=[row_spec(D_MODEL)] + [row_spec(GROUP_W)] * 4 + [_weight(w_out, w_layer), _resident((1, D_MODEL)),
                  _weight(wq, w_layer), mem_spec, mem_spec, _weight(wo, w_layer)],
        out_specs=row_spec(D_MODEL),
        out_shape=jax.ShapeDtypeStruct((b, t, D_MODEL), F32),
        compiler_params=_cparams("parallel", "parallel"),
    )(x, *mix, w_out, gain.reshape(1, -1), wq, mk, mv, wo)


def _diff_body(lam_ref, q_ref, kn_ref, vn_ref, kp_ref, vp_ref, bn_ref, o_ref, q8_sc, m_sc, l_sc, acc_sc, *,
               tq, tk, past, lam_init, ns):
    ng = 2 * N_HEADS
    lane = _iota((tq, GROUP_W), 1)
    for sq in range(ns):
        q = q_ref[sq] * (DIFF_HALF ** -0.5 * LOG2E)
        for g in range(ng):
            lo = (g // 2) * HEAD_DIM + (g % 2) * DIFF_HALF
            q8_sc[sq, g * tq:(g + 1) * tq, :] = jnp.where((lane >= lo) & (lane < lo + DIFF_HALF), q, 0.0).astype(BF16)
    m_sc[...] = jnp.full(m_sc.shape, -jnp.inf, F32)
    l_sc[...] = jnp.zeros(l_sc.shape, F32)
    acc_sc[...] = jnp.zeros(acc_sc.shape, F32)
    pq = past + _iota((tq, 1), 0)
    pq_f = pq.astype(F32)
    lanes = _lanes
    pairs = [(sq, g) for sq in range(ns) for g in range(ng)]

    def block(k_ref, v_ref, k0, n, near, pos0=0):
        pk = pos0 + k0 + _iota((1, n), 1)
        pk_f = pk.astype(F32)
        if near:
            allowed = (pk >> 6) <= (pq >> 6)
            shift = pq_f - jnp.abs(pq_f - pk_f)
        s_all = [_dot(q8_sc[sq], k_ref[sq, :, pl.ds(k0, n)].astype(BF16)) for sq in range(ns)]
        scores = {}
        for sq, g in pairs:
            slope = 2.0 ** (-2 * (g // 2 + 1)) * LOG2E
            s = s_all[sq][g * tq:(g + 1) * tq]
            scores[sq, g] = jnp.where(allowed, s + slope * shift, NEG) if near else s + slope * pk_f
        m_prevs = {k: m_sc[k[0], k[1]] for k in pairs}
        m_news = {k: jnp.maximum(m_prevs[k], jnp.max(scores[k], axis=-1, keepdims=True)) for k in pairs}
        probs = {k: jnp.exp2(scores[k] - lanes(m_news[k], n)) for k in pairs}
        alphas = {k: jnp.exp2(m_prevs[k] - m_news[k]) for k in pairs}
        for k in pairs:
            l_sc[k[0], k[1]] = alphas[k] * l_sc[k[0], k[1]] + jnp.sum(probs[k], axis=-1, keepdims=True)
            m_sc[k[0], k[1]] = m_news[k]
        pvs = [_dot_nt(jnp.concatenate([probs[sq, g].astype(BF16) for g in range(ng)], axis=0),
                       v_ref[sq, :, pl.ds(k0, n)].astype(BF16)) for sq in range(ns)]
        for sq, g in pairs:
            acc_sc[sq, g] = acc_sc[sq, g] * lanes(alphas[sq, g], GROUP_W) + pvs[sq][g * tq:(g + 1) * tq]

    def past_step(j, carry):
        block(kp_ref, vp_ref, pl.multiple_of(j * tk, tk), tk, False)
        return carry
    lax.fori_loop(0, past // tk, past_step, 0)
    block(kn_ref, vn_ref, 0, tq, True, pos0=past)

    lam = lam_ref[0]
    lane_head = _iota((1, GROUP_W), 1) >> 6
    ones16 = _head_block_ones(GROUP_W).astype(BF16)
    for sq in range(ns):
        o = jnp.zeros((tq, GROUP_W), F32)
        for h in range(N_HEADS):
            o0 = acc_sc[sq, 2 * h] / lanes(l_sc[sq, 2 * h], GROUP_W)
            o1 = acc_sc[sq, 2 * h + 1] / lanes(l_sc[sq, 2 * h + 1], GROUP_W)
            o = jnp.where(lane_head == h, o0 - lam * o1, o)
        ms = _mm_r(o * o, ones16) * (1.0 / HEAD_DIM)
        o_ref[sq] = o * lax.rsqrt(ms + EPS) * bn_ref[...] * (1.0 - lam_init)


def _diff_attn(lam, q, kn, vn, kp, vp, layer, bnorm, lam_init):
    b, t, _ = q.shape
    past = kp.shape[3]
    ng = 2 * N_HEADS
    ns = DECODE_SEQS
    tq, tk = t, math.gcd(past, 512)
    assert t == CHUNK and past % CHUNK == 0 and tk % LANES == 0 and b % ns == 0
    seq_spec = pl.BlockSpec((ns, GROUP_W, t), lambda i: (i, 0, 0))
    past_spec = pl.BlockSpec((None, ns, GROUP_W, past), lambda i: (layer, i, 0, 0))
    return pl.pallas_call(
        functools.partial(_diff_body, tq=tq, tk=tk, past=past, lam_init=lam_init, ns=ns),
        grid=(b // ns,),
        in_specs=[pl.BlockSpec(memory_space=pltpu.SMEM), pl.BlockSpec((ns, tq, GROUP_W), lambda i: (i, 0, 0)),
                  seq_spec, seq_spec, past_spec, past_spec, _resident((1, GROUP_W))],
        out_specs=pl.BlockSpec((ns, tq, GROUP_W), lambda i: (i, 0, 0)),
        out_shape=jax.ShapeDtypeStruct((b, t, GROUP_W), F32),
        scratch_shapes=[pltpu.VMEM((ns, ng * tq, GROUP_W), BF16), pltpu.VMEM((ns, ng, tq, LANES), F32),
                        pltpu.VMEM((ns, ng, tq, LANES), F32), pltpu.VMEM((ns, ng, tq, GROUP_W), F32)],
        compiler_params=_cparams("parallel"),
    )(lam, q, kn, vn, kp, vp, bnorm)


def _diff_prompt_body(lam_ref, q_ref, k_ref, v_ref, bn_ref, o_ref, qt_sc, m_sc, l_sc, acc_sc, *, tq, tk, lam_init):
    i = pl.program_id(1)
    ng = 2 * N_HEADS
    qt = (q_ref[0] * (DIFF_HALF ** -0.5 * LOG2E)).T
    feat = _iota((GROUP_W, tq), 0)
    for g in range(ng):
        lo = (g // 2) * HEAD_DIM + (g % 2) * DIFF_HALF
        qt_sc[g] = jnp.where((feat >= lo) & (feat < lo + DIFF_HALF), qt, 0.0).astype(BF16)
    m_sc[...] = jnp.full(m_sc.shape, -jnp.inf, F32)
    l_sc[...] = jnp.zeros(l_sc.shape, F32)
    acc_sc[...] = jnp.zeros(acc_sc.shape, F32)
    pq = i * tq + _iota((1, tq), 1)
    pq_f = pq.astype(F32)

    def block(k0, n, near):
        kb = k_ref[0, pl.ds(k0, n), :].astype(BF16)
        vb = v_ref[0, :, pl.ds(k0, n)].astype(BF16)
        pk = k0 + _iota((n, LANES), 0)
        pk_f = pk.astype(F32)
        if near:
            allowed = (_lanes(pk, tq) >> 6) <= (pq >> 6)
            shift = pq_f - jnp.abs(pq_f - _lanes(pk_f, tq))
        groups = range(ng)
        scores = []
        for g in groups:
            slope = 2.0 ** (-2 * (g // 2 + 1)) * LOG2E
            s = _dot(kb, qt_sc[g])
            scores.append(jnp.where(allowed, s + slope * shift, NEG) if near else s + _lanes(slope * pk_f, tq))
        m_prevs = [m_sc[g] for g in groups]
        m_news = [jnp.maximum(m_prevs[g], jnp.max(scores[g], axis=0, keepdims=True)) for g in groups]
        probs = [jnp.exp2(scores[g] - m_news[g]) for g in groups]
        for g in groups:
            alpha = jnp.exp2(m_prevs[g] - m_news[g])
            l_sc[g] = alpha * l_sc[g] + jnp.sum(probs[g], axis=0, keepdims=True)
            m_sc[g] = m_news[g]
            h = g // 2
            acc_sc[g] = acc_sc[g] * alpha + _dot(vb[h * HEAD_DIM:(h + 1) * HEAD_DIM, :], probs[g].astype(BF16))

    def prev_step(j, carry):
        block(pl.multiple_of(j * tk, tk), tk, False)
        return carry
    lax.fori_loop(0, (i * tq) // tk, prev_step, 0)

    @pl.when((i * tq) % tk != 0)
    def _():
        block(pl.multiple_of((i - 1) * tq, tq), tq, False)
    block(pl.multiple_of(i * tq, tq), tq, True)

    lam = lam_ref[0]
    heads = [acc_sc[2 * h] / l_sc[2 * h] - lam * (acc_sc[2 * h + 1] / l_sc[2 * h + 1]) for h in range(N_HEADS)]
    o = jnp.concatenate(heads, axis=0).T
    ms = _mm_r(o * o, _head_block_ones(GROUP_W).astype(BF16)) * (1.0 / HEAD_DIM)
    o_ref[0] = o * lax.rsqrt(ms + EPS) * bn_ref[...] * (1.0 - lam_init)


def _diff_attn_prompt(lam, q, k_rows, v_feat, bnorm, lam_init):
    b, t, _ = q.shape
    ng = 2 * N_HEADS
    tq = min(256, t)
    tk = 2 * tq if t % (2 * tq) == 0 else tq
    assert t % tq == 0 and tq % LANES == 0
    return pl.pallas_call(
        functools.partial(_diff_prompt_body, tq=tq, tk=tk, lam_init=lam_init),
        grid=(b, t // tq),
        in_specs=[pl.BlockSpec(memory_space=pltpu.SMEM), pl.BlockSpec((1, tq, GROUP_W), lambda i, j: (i, j, 0)),
                  pl.BlockSpec((1, t, GROUP_W), lambda i, j: (i, 0, 0)),
                  pl.BlockSpec((1, GROUP_W, t), lambda i, j: (i, 0, 0)), _resident((1, GROUP_W))],
        out_specs=pl.BlockSpec((1, tq, GROUP_W), lambda i, j: (i, j, 0)),
        out_shape=jax.ShapeDtypeStruct((b, t, GROUP_W), F32),
        scratch_shapes=[pltpu.VMEM((ng, GROUP_W, tq), BF16), pltpu.VMEM((ng, 1, tq), F32),
                        pltpu.VMEM((ng, 1, tq), F32), pltpu.VMEM((ng, HEAD_DIM, tq), F32)],
        compiler_params=_cparams("parallel", "parallel"),
    )(lam, q, k_rows, v_feat, bnorm)


def _stick_body(q_ref, kn_ref, vn_ref, kp_ref, vp_ref, o_ref, q4_sc, c_sc, acc_sc, *, tq, tk, past, ns):
    nr = N_HEADS * tq
    lane = _iota((tq, GROUP_W), 1) >> 6
    for sq in range(ns):
        q = q_ref[sq] * (HEAD_DIM ** -0.5 * LOG2E)
        for h in range(N_HEADS):
            q4_sc[sq, h * tq:(h + 1) * tq, :] = jnp.where(lane == h, q, 0.0).astype(BF16)
    c_sc[...] = jnp.zeros(c_sc.shape, F32)
    acc_sc[...] = jnp.zeros(acc_sc.shape, F32)
    pq = past + (_iota((nr, 1), 0) & (tq - 1))
    laters = {w: (_iota((w, w), 0) > _iota((w, w), 1)).astype(BF16)
              for w in {min(tq, STICK_SUB), min(tk, STICK_SUB)}}
    seqs = range(ns)

    def block(k_ref, v_ref, k0, n, masked, pos0=0):
        sub = min(n, STICK_SUB)
        later = laters[sub]
        zs = [_dot(q4_sc[sq], k_ref[sq, :, pl.ds(k0, n)].astype(BF16)) for sq in seqs]
        lss = [jnp.minimum(z, 0.0) - jnp.log2(1.0 + jnp.exp2(-jnp.abs(z))) for z in zs]
        lks = [ls - z for ls, z in zip(lss, zs)]
        if masked:
            mask = (pos0 + k0 + _iota((1, n), 1)) < pq
            lks = [jnp.where(mask, lk, 0.0) for lk in lks]
        his = [lk.astype(BF16) for lk in lks]
        los = [(lk - hi.astype(F32)).astype(BF16) for lk, hi in zip(lks, his)]
        carries = [c_sc[sq] for sq in seqs]
        parts = [[] for _ in seqs]
        for sb in reversed(range(n // sub)):
            sl = slice(sb * sub, (sb + 1) * sub)
            for sq in seqs:
                after = _dot(his[sq][:, sl], later) + _dot(los[sq][:, sl], later)
                parts[sq].append(lss[sq][:, sl] + after + _lanes(carries[sq], sub))
                carries[sq] = carries[sq] + (after[:, 0:1] + lks[sq][:, sb * sub:sb * sub + 1])
        for sq in seqs:
            c_sc[sq] = carries[sq]
            e = parts[sq][0] if len(parts[sq]) == 1 else jnp.concatenate(parts[sq][::-1], axis=1)
            if masked:
                a = jnp.where(mask, jnp.exp2(jnp.where(mask, e, 0.0)), 0.0)
            else:
                a = jnp.exp2(e)
            acc_sc[sq] = acc_sc[sq] + _dot_nt(a.astype(BF16), v_ref[sq, :, pl.ds(k0, n)].astype(BF16))

    block(kn_ref, vn_ref, 0, tq, True, pos0=past)
    nb = past // tk

    def past_step(j, carry):
        block(kp_ref, vp_ref, pl.multiple_of((nb - 1 - j) * tk, tk), tk, False)
        return carry
    lax.fori_loop(0, nb, past_step, 0)

    for sq in seqs:
        o = acc_sc[sq, 0:tq, :]
        for h in range(1, N_HEADS):
            o = jnp.where(lane == h, acc_sc[sq, h * tq:(h + 1) * tq, :], o)
        o_ref[sq] = o


def _stick_attn(q, kn, vn, kp, vp, layer):
    b, t, _ = q.shape
    past = kp.shape[3]
    ns = DECODE_SEQS
    tq, tk = t, math.gcd(past, 1024)
    assert tq & (tq - 1) == 0 and tk % min(tk, STICK_SUB) == 0 and b % ns == 0
    seq_spec = pl.BlockSpec((ns, GROUP_W, t), lambda i: (i, 0, 0))
    past_spec = pl.BlockSpec((None, ns, GROUP_W, past), lambda i: (layer, i, 0, 0))
    return pl.pallas_call(
        functools.partial(_stick_body, tq=tq, tk=tk, past=past, ns=ns),
        grid=(b // ns,),
        in_specs=[pl.BlockSpec((ns, tq, GROUP_W), lambda i: (i, 0, 0)), seq_spec, seq_spec, past_spec, past_spec],
        out_specs=pl.BlockSpec((ns, tq, GROUP_W), lambda i: (i, 0, 0)),
        out_shape=jax.ShapeDtypeStruct((b, t, GROUP_W), F32),
        scratch_shapes=[pltpu.VMEM((ns, N_HEADS * tq, GROUP_W), BF16), pltpu.VMEM((ns, N_HEADS * tq, LANES), F32),
                        pltpu.VMEM((ns, N_HEADS * tq, GROUP_W), F32)],
        compiler_params=_cparams("parallel"),
    )(q, kn, vn, kp, vp)


def _stick_prompt_body(q_ref, k_ref, v_ref, o_ref, qt_sc, c_sc, acc_sc, *, tq, tk):
    i = pl.program_id(1)
    qt = (q_ref[0] * (HEAD_DIM ** -0.5 * LOG2E)).T
    feat_head = _iota((GROUP_W, tq), 0) >> 6
    for h in range(N_HEADS):
        qt_sc[h] = jnp.where(feat_head == h, qt, 0.0).astype(BF16)
    c_sc[...] = jnp.zeros(c_sc.shape, F32)
    acc_sc[...] = jnp.zeros(acc_sc.shape, F32)
    pq = i * tq + _iota((1, tq), 1)
    sub = min(tq, STICK_SUB)
    later = (_iota((sub, sub), 1) > _iota((sub, sub), 0)).astype(BF16)

    def block(k0, n, masked):
        kb = k_ref[0, pl.ds(k0, n), :].astype(BF16)
        vb = v_ref[0, :, pl.ds(k0, n)].astype(BF16)
        if masked:
            mask = _lanes(k0 + _iota((n, LANES), 0), tq) < pq
        heads = range(N_HEADS)
        zs = [_dot(kb, qt_sc[h]) for h in heads]
        lss = [jnp.minimum(z, 0.0) - jnp.log2(1.0 + jnp.exp2(-jnp.abs(z))) for z in zs]
        lks = [ls - z for ls, z in zip(lss, zs)]
        if masked:
            lks = [jnp.where(mask, lk, 0.0) for lk in lks]
        es = []
        for h in heads:
            lk = lks[h]
            hi = lk.astype(BF16)
            lo = (lk - hi.astype(F32)).astype(BF16)
            carry = c_sc[h]
            parts = []
            for sb in reversed(range(n // sub)):
                sl = slice(sb * sub, (sb + 1) * sub)
                after = _dot(later, hi[sl]) + _dot(later, lo[sl])
                parts.append(lss[h][sl] + after + carry)
                carry = carry + (after[0:1] + lk[sb * sub:sb * sub + 1])
            c_sc[h] = carry
            es.append(parts[0] if len(parts) == 1 else jnp.concatenate(parts[::-1], axis=0))
        for h in heads:
            if masked:
                a = jnp.where(mask, jnp.exp2(jnp.where(mask, es[h], 0.0)), 0.0)
            else:
                a = jnp.exp2(es[h])
            acc_sc[h] = acc_sc[h] + _dot(vb[h * HEAD_DIM:(h + 1) * HEAD_DIM, :], a.astype(BF16))

    block(pl.multiple_of(i * tq, tq), tq, True)

    @pl.when((i * tq) % tk != 0)
    def _():
        block(pl.multiple_of((i - 1) * tq, tq), tq, False)
    nfull = (i * tq) // tk

    def prev_step(j, carry):
        block(pl.multiple_of((nfull - 1 - j) * tk, tk), tk, False)
        return carry
    lax.fori_loop(0, nfull, prev_step, 0)

    o_ref[0] = jnp.concatenate([acc_sc[h] for h in range(N_HEADS)], axis=0).T


def _stick_attn_prompt(q, k_rows, v_feat):
    b, t, _ = q.shape
    tq = min(256, t)
    tk = 2 * tq if t % (2 * tq) == 0 else tq
    assert t % tq == 0 and tq % LANES == 0 and tk % min(tk, STICK_SUB) == 0
    return pl.pallas_call(
        functools.partial(_stick_prompt_body, tq=tq, tk=tk),
        grid=(b, t // tq),
        in_specs=[pl.BlockSpec((1, tq, GROUP_W), lambda i, j: (i, j, 0)),
                  pl.BlockSpec((1, t, GROUP_W), lambda i, j: (i, 0, 0)),
                  pl.BlockSpec((1, GROUP_W, t), lambda i, j: (i, 0, 0))],
        out_specs=pl.BlockSpec((1, tq, GROUP_W), lambda i, j: (i, j, 0)),
        out_shape=jax.ShapeDtypeStruct((b, t, GROUP_W), F32),
        scratch_shapes=[pltpu.VMEM((N_HEADS, GROUP_W, tq), BF16), pltpu.VMEM((N_HEADS, 1, tq), F32),
                        pltpu.VMEM((N_HEADS, HEAD_DIM, tq), F32)],
        compiler_params=_cparams("parallel", "parallel"),
    )(q, k_rows, v_feat)


def _pair_mask(n, s, reps=1):
    r, c = _iota((reps * n, n), 0) & (n - 1), _iota((reps * n, n), 1)
    return ((r // (2 * s)) == (c // (2 * s))) & (((r // s) & 1) == 1) & (((c // s) & 1) == 0)


def _state_spec(nb):
    return pl.BlockSpec((nb, N_HEADS, HEAD_DIM, HEAD_DIM), lambda i, j: (i, 0, 0, 0))


def _load_state(s_sc, s0_ref, transpose):
    s_sc[...] = jnp.zeros(s_sc.shape, F32)
    for b in range(s_sc.shape[0]):
        for h in range(N_HEADS):
            blk = s0_ref[b, h]
            s_sc[b, h * HEAD_DIM:(h + 1) * HEAD_DIM, h * HEAD_DIM:(h + 1) * HEAD_DIM] = blk.T if transpose else blk


def _store_state(sout_ref, s_sc, transpose):
    for b in range(s_sc.shape[0]):
        for h in range(N_HEADS):
            blk = s_sc[b, h * HEAD_DIM:(h + 1) * HEAD_DIM, h * HEAD_DIM:(h + 1) * HEAD_DIM]
            sout_ref[b, h] = blk.T if transpose else blk


def _head_rows(x, lane_head):
    return jnp.concatenate([jnp.where(lane_head == h, x, 0.0) for h in range(N_HEADS)], axis=0)


def _head_diag(x, lane_head):
    n = x.shape[0] // N_HEADS
    out = x[:n]
    for h in range(1, N_HEADS):
        out = jnp.where(lane_head == h, x[h * n:(h + 1) * n], out)
    return out


def _gdn_body(x_ref, z_ref, abg_ref, cw_ref, cs_ref, s0_ref, alog_ref, dtb_ref, an_ref,
              o_ref, sout_ref, s_sc, xb_sc, *, nc, nb):
    c = pl.program_id(1)
    L = CHUNK
    pad = 8

    @pl.when(c == 0)
    def _():
        _load_state(s_sc, s0_ref, transpose=False)
        xb_sc[:, pad - 3:pad, :] = cs_ref[...]

    ones_bd = _head_block_ones(GROUP_W)
    ones_bd16 = ones_bd.astype(BF16)
    er, ec = _iota((ABG_W, GROUP_W), 0), _iota((ABG_W, GROUP_W), 1) >> 6
    e_beta = (er == ec).astype(BF16)
    e_g = (er == ec + N_HEADS).astype(BF16)
    ri, ci = _iota((L, L), 0), _iota((L, L), 1)
    tri16 = (ri >= ci).astype(BF16)
    incl = ri >= ci
    strict = ri > ci
    eye = (ri == ci).astype(F32)
    pair_masks = [_pair_mask(L, sz) for sz in (1, 2, 4, 8, 16, 32)]
    lane_head = _iota((1, GROUP_W), 1) >> 6
    cw = cw_ref[...]

    a_all, per_seq = [], []
    for b in range(nb):
        x = x_ref[b]
        xb_sc[b, pad:pad + L, :] = x
        u = (x * cw[3:4] + xb_sc[b, pad - 1:pad - 1 + L, :] * cw[2:3]
             + xb_sc[b, pad - 2:pad - 2 + L, :] * cw[1:2] + xb_sc[b, pad - 3:pad - 3 + L, :] * cw[0:1])
        xb_sc[b, pad - 3:pad, :] = x[L - 3:L, :]
        u = u * _sigmoid(u)
        q = u[:, :GROUP_W]
        k = u[:, GROUP_W:2 * GROUP_W]
        v = u[:, 2 * GROUP_W:]
        q = q * lax.rsqrt(_mm_r(q * q, ones_bd16) + EPS) * HEAD_DIM ** -0.5
        k = k * lax.rsqrt(_mm_r(k * k, ones_bd16) + EPS)

        abg = abg_ref[b]
        beta_n = _sigmoid(abg)
        g_n = -jnp.exp(alog_ref[...]) * _softplus(abg + dtb_ref[...])
        beta = _mm_r(beta_n, e_beta)
        gc_n = _mm_l(tri16, g_n)
        gc_t = gc_n.T
        gc = _mm_r(gc_n, e_g)
        exp_g = jnp.exp(gc)
        g_last = gc[L - 1:L, :]

        kb = k * beta
        k16 = k.astype(BF16)
        a4 = _bdot(_head_rows(kb, lane_head).astype(BF16), k16, _NT)
        qk4 = _bdot(_head_rows(q, lane_head).astype(BF16), k16, _NT)
        dec = []
        for h in range(N_HEADS):
            diff = gc_n[:, N_HEADS + h:N_HEADS + h + 1] - gc_t[N_HEADS + h:N_HEADS + h + 1, :]
            dec.append(jnp.where(incl, jnp.exp(jnp.where(incl, diff, 0.0)), 0.0))
        dec = jnp.stack(dec)
        a_all.append(jnp.where(strict, a4.reshape(N_HEADS, L, L) * dec, 0.0))
        per_seq.append((q, k, v * beta, kb * exp_g, q * exp_g, k * jnp.exp(g_last - gc), jnp.exp(g_last),
                        qk4.reshape(N_HEADS, L, L) * dec))

    a_all = jnp.concatenate(a_all, axis=0)
    t_inv = eye - jnp.where(pair_masks[0], a_all, 0.0)
    a16 = a_all.astype(BF16)
    for pm in pair_masks[1:]:
        t16 = t_inv.astype(BF16)
        t_inv = t_inv - _bdot(t16, _bdot(jnp.where(pm, a16, 0.0), t16, _BNN).astype(BF16), _BNN)

    for b in range(nb):
        q, k, vb, kbg, qg, k_dec, decay_last, qk = per_seq[b]
        s = s_sc[b]
        s16 = s.astype(BF16)
        rhs = vb - _bdot(kbg.astype(BF16), s16, _NN)
        t4 = t_inv[b * N_HEADS:(b + 1) * N_HEADS].reshape(N_HEADS * L, L)
        w = _head_diag(_mm1(t4, rhs), lane_head)
        o = _bdot(qg.astype(BF16), s16, _NN) + _head_diag(_mm1(qk.reshape(N_HEADS * L, L), w), lane_head)
        s_sc[b] = s * decay_last + _mm1(k_dec.T, w) * ones_bd

        ms = _mm_r(o * o, ones_bd16) * (1.0 / HEAD_DIM)
        zg = z_ref[b]
        o_ref[b] = o * lax.rsqrt(ms + EPS) * an_ref[...] * (zg * _sigmoid(zg))

    @pl.when(c == nc - 1)
    def _():
        _store_state(sout_ref, s_sc, transpose=False)


def _gdn(aqkv, az, abg, conv_w, conv_state, s0, alog_row, dtb_row, anorm_row):
    b, t, _ = aqkv.shape
    nc = t // CHUNK
    nb = SEQ_PER_STEP
    assert b % nb == 0
    cmap = lambda i, j: (i, j, 0)
    bmap = lambda i, j: (i, 0, 0)
    return pl.pallas_call(
        functools.partial(_gdn_body, nc=nc, nb=nb),
        grid=(b // nb, nc),
        in_specs=[pl.BlockSpec((nb, CHUNK, 3 * GROUP_W), cmap), pl.BlockSpec((nb, CHUNK, GROUP_W), cmap),
                  pl.BlockSpec((nb, CHUNK, ABG_W), cmap), _resident((CONV_W, 3 * GROUP_W)),
                  pl.BlockSpec((nb, CONV_W - 1, 3 * GROUP_W), bmap), _state_spec(nb),
                  _resident((1, ABG_W)), _resident((1, ABG_W)), _resident((1, GROUP_W))],
        out_specs=[pl.BlockSpec((nb, CHUNK, GROUP_W), cmap), _state_spec(nb)],
        out_shape=[jax.ShapeDtypeStruct((b, t, GROUP_W), F32), jax.ShapeDtypeStruct(s0.shape, F32)],
        scratch_shapes=[pltpu.VMEM((nb, GROUP_W, GROUP_W), F32), pltpu.VMEM((nb, 8 + CHUNK, 3 * GROUP_W), F32)],
        compiler_params=_cparams("parallel", "arbitrary"),
    )(aqkv, az, abg, conv_w, conv_state, s0, alog_row, dtb_row, anorm_row)


def _hgrn_body(q_ref, f_ref, i_ref, g_ref, lb_ref, dn_ref, s0_ref, o_ref, sout_ref, s_sc, *, nc, nb):
    c = pl.program_id(1)
    L = CHUNK

    @pl.when(c == 0)
    def _():
        _load_state(s_sc, s0_ref, transpose=True)

    lb = lb_ref[...]
    ones_bd = _head_block_ones(GROUP_W)
    ones_bd16 = ones_bd.astype(BF16)
    ri, ci = _iota((L, L), 0), _iota((L, L), 1)
    tri16 = (ri >= ci).astype(BF16)
    sizes = (1, 2, 4, 8, 16, 32)
    sel16 = jnp.concatenate([(ci == (ri // (2 * sz)) * (2 * sz) + sz).astype(BF16) for sz in sizes], axis=0)
    eye4 = (_iota((N_HEADS * L, L), 0) & (L - 1)) == _iota((N_HEADS * L, L), 1)
    pair_masks4 = [_pair_mask(L, sz, reps=N_HEADS) for sz in sizes]
    lane_head = _iota((1, GROUP_W), 1) >> 6
    rows = _iota((L, 1), 0)

    pre = []
    for b in range(nb):
        fl = f_ref[b]
        log_f = _log_sigmoid(fl) + jnp.log(1.0 + lb * jnp.exp(-fl))
        k = (1.0 - lb) * _sigmoid(-fl)
        q = q_ref[b] * HEAD_DIM ** -0.5
        bc = _mm_l(tri16, log_f)
        pre.append((q, k, bc))
    b_refs = [_mm_l(sel16, bc) for _, _, bc in pre]
    a_all = [jnp.where(eye4, _mm1(_head_rows(q, lane_head), k, _NT), 0.0) for q, k, _ in pre]
    for lv, sz in enumerate(sizes):
        later = ((rows // sz) & 1) == 1
        for b in range(nb):
            q, k, bc = pre[b]
            b_ref = b_refs[b][lv * L:(lv + 1) * L]
            x = jnp.where(later, q * jnp.exp(jnp.where(later, bc - b_ref, 0.0)), 0.0)
            y = jnp.where(later, 0.0, k * jnp.exp(jnp.where(later, 0.0, b_ref - bc)))
            a_all[b] = a_all[b] + jnp.where(pair_masks4[lv], _mm1(_head_rows(x, lane_head), y, _NT), 0.0)

    for b in range(nb):
        q, k, bc = pre[b]
        v = i_ref[b]
        st = s_sc[b]
        o = _mm1(q * jnp.exp(bc), st, _NT) + _head_diag(_mm1(a_all[b], v), lane_head)
        b_last = bc[L - 1:L, :]
        st_new = st * jnp.exp(b_last) + _mm1(v.T, k * jnp.exp(b_last - bc)) * ones_bd
        s_sc[b] = st_new

        ms = _mm_r(o * o, ones_bd16) * (1.0 / HEAD_DIM)
        zg = g_ref[b]
        o_ref[b] = o * lax.rsqrt(ms + EPS) * dn_ref[...] * (zg * _sigmoid(zg))

    @pl.when(c == nc - 1)
    def _():
        _store_state(sout_ref, s_sc, transpose=True)


def _hgrn(dq, df, di, dg, lb_row, dnorm_row, s0):
    b, t, _ = dq.shape
    nc = t // CHUNK
    nb = SEQ_PER_STEP
    assert b % nb == 0
    cmap = lambda i, j: (i, j, 0)
    bmap = lambda i, j: (i, 0, 0)
    cspec = pl.BlockSpec((nb, CHUNK, GROUP_W), cmap)
    return pl.pallas_call(
        functools.partial(_hgrn_body, nc=nc, nb=nb),
        grid=(b // nb, nc),
        in_specs=[cspec, cspec, cspec, cspec, _resident((1, GROUP_W)), _resident((1, GROUP_W)),
                  _state_spec(nb)],
        out_specs=[cspec, _state_spec(nb)],
        out_shape=[jax.ShapeDtypeStruct((b, t, GROUP_W), F32), jax.ShapeDtypeStruct(s0.shape, F32)],
        scratch_shapes=[pltpu.VMEM((nb, GROUP_W, GROUP_W), F32)],
        compiler_params=_cparams("parallel", "arbitrary"),
    )(dq, df, di, dg, lb_row, dnorm_row, s0)


IN_WIDTHS = (3 * GROUP_W,) + (GROUP_W,) * 11 + (ABG_W,)
KV_SEGMENTS = (3, 4, 6, 7)
K_SEGMENTS = (3, 6)


def _layer(x, l, st, mk, mv, mem_layer, prm, final_gain):
    b, t, _ = x.shape
    m = b * t
    tm = math.gcd(m, ROW_TILE)
    has_past = st["b_k"] is not None
    outs = _fused_linear([x.reshape(m, D_MODEL)], prm["w_in"], l, IN_WIDTHS, gain=prm["norm_mix"][l], tm=tm,
                         feat_major=KV_SEGMENTS, rows_per_seq=t, rows_too=() if has_past else K_SEGMENTS)
    outs = [o if idx in KV_SEGMENTS else o.reshape(b, t, -1) for idx, o in enumerate(outs)]
    (a_qkv, a_z, b_q, b_k, b_v, c_q, c_k, c_v, d_q, d_f, d_i, d_g, a_bg) = outs[:len(IN_WIDTHS)]

    o_a, a_s = _gdn(a_qkv, a_z, a_bg, prm["a_conv_w"][l], st["a_conv"], st["a_S"].astype(F32),
                    prm["alog_row"][l], prm["dtb_row"][l], prm["a_norm"][l])
    lam_init = 0.8 - 0.6 * math.exp(-0.3 * l)
    if has_past:
        o_b = _diff_attn(prm["lam"][l], b_q, b_k, b_v, st["b_k"], st["b_v"], l, prm["b_norm"][l], lam_init)
        o_c = _stick_attn(c_q, c_k, c_v, st["c_k"], st["c_v"], l)
    else:
        b_k_rows, c_k_rows = outs[len(IN_WIDTHS):]
        o_b = _diff_attn_prompt(prm["lam"][l], b_q, b_k_rows, b_v, prm["b_norm"][l], lam_init)
        o_c = _stick_attn_prompt(c_q, c_k_rows, c_v)
    o_d, d_s = _hgrn(d_q, d_f, d_i, d_g, prm["lb"][l], prm["d_norm"][l], st["d_S"].astype(F32))

    x2 = _cross(x, (o_a, o_b, o_c, o_d), prm["w_out"], prm["norm_cross"][l], prm["w_cq"], mk, mv, mem_layer,
                prm["w_co"], l)
    x2 = _ffn(x2.reshape(m, D_MODEL), prm["norm_ffn"][l], prm["w_gate"], prm["w_up"], prm["w_down"], l,
              final_gain=final_gain, tm=tm)

    if t >= CONV_W - 1:
        conv_new = a_qkv[:, t - (CONV_W - 1):, :]
    else:
        conv_new = jnp.concatenate([st["a_conv"], a_qkv], axis=1)[:, -(CONV_W - 1):, :]
    heads = lambda a: jnp.transpose(a.reshape(b, N_HEADS, HEAD_DIM, t), (0, 3, 1, 2))
    new = (conv_new, a_s, heads(b_k), heads(b_v), heads(c_k), heads(c_v), d_s)
    return x2.reshape(b, t, D_MODEL), new


def kernel(x_prompt, x_sample, mem_prompt, state_a_conv, state_a_S, cache_b_k, cache_b_v, cache_c_k, cache_c_v,
           state_d_S, cache_mem_k, cache_mem_v, norm_mix, w_in, a_conv_w, a_A_log, a_dt_bias, a_norm, b_lam_q1,
           b_lam_k1, b_lam_q2, b_lam_k2, b_norm, d_lb, d_norm, w_out, norm_cross, norm_memtok, w_cq, w_ck, w_cv,
           w_co, norm_ffn, w_gate, w_up, w_down, norm_final):
    depth = w_in.shape[0]
    n_main = 4 * GROUP_W
    w_in_p = jnp.concatenate(
        [w_in[:, :, :n_main], w_in[:, :, n_main + 2 * N_HEADS:], w_in[:, :, n_main:n_main + 2 * N_HEADS],
         jnp.zeros((depth, D_MODEL, ABG_W - 2 * N_HEADS), w_in.dtype)], axis=2).astype(BF16)
    pad_row = lambda a: jnp.pad(a.astype(F32), ((0, 0), (N_HEADS, ABG_W - 2 * N_HEADS)))[:, None, :]
    tile_row = lambda a: jnp.tile(a.astype(F32), (1, N_HEADS))[:, None, :]
    p_lb = jax.nn.softmax(d_lb.astype(F32), axis=0)
    lam = (jnp.exp(jnp.sum(b_lam_q1.astype(F32) * b_lam_k1.astype(F32), axis=-1))
           - jnp.exp(jnp.sum(b_lam_q2.astype(F32) * b_lam_k2.astype(F32), axis=-1))
           + jnp.array([0.8 - 0.6 * math.exp(-0.3 * l) for l in range(depth)], F32))
    prm = {
        "w_in": w_in_p, "norm_mix": norm_mix, "a_conv_w": a_conv_w.astype(F32),
        "alog_row": pad_row(a_A_log), "dtb_row": pad_row(a_dt_bias), "a_norm": tile_row(a_norm),
        "lam": lam[:, None], "b_norm": tile_row(b_norm), "lb": (jnp.cumsum(p_lb, axis=0) - p_lb[0])[:, None, :],
        "d_norm": tile_row(d_norm), "w_out": w_out.astype(BF16), "norm_cross": norm_cross,
        "w_cq": w_cq.astype(BF16), "w_co": w_co.astype(BF16), "norm_ffn": norm_ffn,
        "w_gate": w_gate.astype(BF16), "w_up": w_up.astype(BF16), "w_down": w_down.astype(BF16),
    }
    w_ckv = jnp.concatenate([w_ck, w_cv], axis=2).astype(BF16)

    bp, tp, _ = x_prompt.shape
    n_mem = mem_prompt.shape[1]
    h = x_prompt
    p_new = []
    for l in range(depth):
        mk, mv = _fused_linear([mem_prompt.reshape(bp * n_mem, D_MODEL)], w_ckv, l, (D_MODEL, D_MODEL),
                               gain=norm_memtok[l], tm=256)
        mk = mk.reshape(1, bp, n_mem, D_MODEL)
        mv = mv.reshape(1, bp, n_mem, D_MODEL)
        st = {"a_conv": jnp.zeros((bp, CONV_W - 1, 3 * GROUP_W), F32),
              "a_S": jnp.zeros((bp, N_HEADS, HEAD_DIM, HEAD_DIM), F32), "b_k": None, "b_v": None,
              "c_k": None, "c_v": None, "d_S": jnp.zeros((bp, N_HEADS, HEAD_DIM, HEAD_DIM), F32)}
        h, new = _layer(h, l, st, mk, mv, 0, prm, norm_final if l == depth - 1 else None)
        mem4 = lambda a: a.reshape(bp, n_mem, MEM_HEADS, MEM_HEAD_DIM)
        p_new.append(new + (mem4(mk), mem4(mv)))
    y_prompt = h
    p_stacked = [jnp.stack(c) for c in zip(*p_new)]

    bs, ts, _ = x_sample.shape
    past = cache_b_k.shape[2]
    h = x_sample
    s_new = []
    flat = lambda a: jnp.transpose(a, (0, 1, 3, 4, 2)).reshape(depth, bs, GROUP_W, past)
    caches = {"b_k": flat(cache_b_k), "b_v": flat(cache_b_v), "c_k": flat(cache_c_k), "c_v": flat(cache_c_v)}
    for l in range(depth):
        st = {"a_conv": state_a_conv[l], "a_S": state_a_S[l], "d_S": state_d_S[l], **caches}
        h, new = _layer(h, l, st, cache_mem_k, cache_mem_v, l, prm, norm_final if l == depth - 1 else None)
        s_new.append(new)
    y_sample = h
    s_stacked = [jnp.stack(c) for c in zip(*s_new)]

    return (y_prompt, y_sample, *p_stacked, *s_stacked)
```

```python
import functools
import math

import jax
import jax.numpy as jnp
from jax import lax
from jax.experimental import pallas as pl
from jax.experimental.pallas import tpu as pltpu

F32 = jnp.float32
BF16 = jnp.bfloat16

D_MODEL = 1024
GROUP_W = 256
N_HEADS = 4
HEAD_DIM = 64
DIFF_HALF = 32
CHUNK = 64
CONV_W = 4
MEM_HEADS = 4
MEM_HEAD_DIM = 256
D_FF = 2816
EPS = 1e-6
NEG = -1e30
LOG2E = 1.4426950408889634
LANES = 128
ABG_W = 128
FF_CHUNK = 256
ROW_TILE = 512
SUM_ROWS = 16
DECODE_SEQS = 2
CROSS_SEQS = 4
STICK_SUB = 256
SEQ_PER_STEP = 4
VMEM_LIMIT = 56 * 1024 * 1024


def _cparams(*sem):
    return pltpu.CompilerParams(dimension_semantics=sem, vmem_limit_bytes=VMEM_LIMIT)


def _dot(a, b):
    return jnp.dot(a, b, preferred_element_type=F32)


def _dot_nt(a, b):
    return lax.dot_general(a, b, (((1,), (1,)), ((), ())), preferred_element_type=F32)


_NN = (((1,), (0,)), ((), ()))
_NT = (((1,), (1,)), ((), ()))
_BNN = (((2,), (1,)), ((0,), (0,)))


def _split(x):
    hi = x.astype(BF16)
    return hi, (x - hi.astype(F32)).astype(BF16)


def _bdot(a, b, dims):
    return lax.dot_general(a, b, dims, preferred_element_type=F32)


def _mm(a, b, dims=_NN):
    ah, al = a if isinstance(a, tuple) else _split(a)
    bh, bl = b if isinstance(b, tuple) else _split(b)
    return _bdot(ah, bh, dims) + (_bdot(ah, bl, dims) + _bdot(al, bh, dims))


def _mm1(a, b, dims=_NN):
    return _bdot(a.astype(BF16), b.astype(BF16), dims)


def _mm_r(a, b16):
    ah, al = _split(a)
    return _bdot(ah, b16, _NN) + _bdot(al, b16, _NN)


def _mm_l(a16, b):
    bh, bl = _split(b)
    return _bdot(a16, bh, _NN) + _bdot(a16, bl, _NN)


def _rms(x, g):
    return x * lax.rsqrt(jnp.mean(x * x, axis=-1, keepdims=True) + EPS) * g


def _sigmoid(x):
    return 1.0 / (1.0 + jnp.exp(-x))


def _log_sigmoid(x):
    return jnp.minimum(x, 0.0) - jnp.log(1.0 + jnp.exp(-jnp.abs(x)))


def _softplus(x):
    return jnp.maximum(x, 0.0) + jnp.log(1.0 + jnp.exp(-jnp.abs(x)))


def _iota(shape, dim):
    return lax.broadcasted_iota(jnp.int32, shape, dim)


def _lanes(x, n):
    return x[:, :n] if n <= LANES else jnp.concatenate([x] * (n // LANES), axis=1)


def _head_block_ones(n):
    return ((_iota((n, n), 0) >> 6) == (_iota((n, n), 1) >> 6)).astype(F32)


def _resident(shape):
    return pl.BlockSpec(shape, lambda *_: (0,) * len(shape), pipeline_mode=pl.Buffered(1))


def _weight(w, layer):
    return pl.BlockSpec((None,) + w.shape[1:], lambda *_: (layer, 0, 0), pipeline_mode=pl.Buffered(1))


def _linear_body(*refs, ks, segs, has_norm, has_res, feat_major, rows_too):
    n_in = len(ks)
    x_refs = refs[:n_in]
    pos = n_in
    g_ref = None
    if has_norm:
        g_ref = refs[pos]
        pos += 1
    w_ref = refs[pos]
    pos += 1
    r_ref = None
    if has_res:
        r_ref = refs[pos]
        pos += 1
    out_refs = refs[pos:]
    xb = []
    for x_ref in x_refs:
        x = x_ref[...].astype(F32)
        if has_norm:
            x = _rms(x, g_ref[...])
        xb.append(x.astype(BF16))
    extra_refs = dict(zip(rows_too, out_refs[len(segs):]))
    for idx, (o_ref, (s, e)) in enumerate(zip(out_refs, segs)):
        acc = None
        k0 = 0
        for xk, k in zip(xb, ks):
            t = _dot(xk, w_ref[k0:k0 + k, s:e])
            acc = t if acc is None else acc + t
            k0 += k
        if has_res:
            acc = acc + r_ref[:, s:e]
        if idx in feat_major:
            rows = acc.shape[0] // o_ref.shape[0]
            for sq in range(o_ref.shape[0]):
                o_ref[sq] = acc[sq * rows:(sq + 1) * rows].T
        else:
            o_ref[...] = acc
        if idx in extra_refs:
            extra_refs[idx][...] = acc


def _fused_linear(xs, w, layer, widths, gain=None, res=None, tm=256, feat_major=(), rows_per_seq=None, rows_too=()):
    m = xs[0].shape[0]
    ks = tuple(x.shape[1] for x in xs)
    n = w.shape[2]
    segs = []
    s = 0
    for wd in widths:
        segs.append((s, s + wd))
        s += wd
    assert s == n and m % tm == 0
    out_specs, out_shape = [], []
    for idx, wd in enumerate(widths):
        if idx in feat_major:
            if rows_per_seq >= tm:
                tiles = rows_per_seq // tm
                assert rows_per_seq % tm == 0
                out_specs.append(pl.BlockSpec((1, wd, tm), lambda i, tiles=tiles: (i // tiles, 0, i % tiles)))
            else:
                assert tm % rows_per_seq == 0
                out_specs.append(pl.BlockSpec((tm // rows_per_seq, wd, rows_per_seq), lambda i: (i, 0, 0)))
            out_shape.append(jax.ShapeDtypeStruct((m // rows_per_seq, wd, rows_per_seq), F32))
        else:
            out_specs.append(pl.BlockSpec((tm, wd), lambda i: (i, 0)))
            out_shape.append(jax.ShapeDtypeStruct((m, wd), F32))
    for idx in rows_too:
        out_specs.append(pl.BlockSpec((tm, widths[idx]), lambda i: (i, 0)))
        out_shape.append(jax.ShapeDtypeStruct((m, widths[idx]), F32))
    in_specs = [pl.BlockSpec((tm, k), lambda i: (i, 0)) for k in ks]
    args = list(xs)
    if gain is not None:
        in_specs.append(_resident((1, ks[0])))
        args.append(gain.reshape(1, -1))
    assert w.shape[1] == sum(ks)
    in_specs.append(_weight(w, layer))
    args.append(w)
    if res is not None:
        in_specs.append(pl.BlockSpec((tm, n), lambda i: (i, 0)))
        args.append(res)
    return pl.pallas_call(
        functools.partial(_linear_body, ks=ks, segs=tuple(segs), has_norm=gain is not None, has_res=res is not None,
                          feat_major=tuple(feat_major), rows_too=tuple(rows_too)),
        grid=(m // tm,),
        in_specs=in_specs,
        out_specs=out_specs,
        out_shape=out_shape,
        compiler_params=_cparams("parallel"),
    )(*args)


def _ffn_body(*refs, final):
    if final:
        x_ref, g_ref, wg_ref, wu_ref, wd_ref, gf_ref, o_ref = refs
    else:
        x_ref, g_ref, wg_ref, wu_ref, wd_ref, o_ref = refs
    x = x_ref[...]
    hb = _rms(x, g_ref[...]).astype(BF16)
    acc = x
    for c in range(0, D_FF, FF_CHUNK):
        gate = _dot(hb, wg_ref[:, c:c + FF_CHUNK])
        up = _dot(hb, wu_ref[:, c:c + FF_CHUNK])
        act = (gate * _sigmoid(gate) * up).astype(BF16)
        acc = acc + _dot(act, wd_ref[c:c + FF_CHUNK, :])
    if final:
        acc = _rms(acc, gf_ref[...])
    o_ref[...] = acc


def _ffn(x, gain, wg, wu, wd, layer, final_gain=None, tm=256):
    m = x.shape[0]
    final = final_gain is not None
    in_specs = [pl.BlockSpec((tm, D_MODEL), lambda i: (i, 0)), _resident((1, D_MODEL)),
                _weight(wg, layer), _weight(wu, layer), _weight(wd, layer)]
    args = [x, gain.reshape(1, -1), wg, wu, wd]
    if final:
        in_specs.append(_resident((1, D_MODEL)))
        args.append(final_gain.reshape(1, -1))
    return pl.pallas_call(
        functools.partial(_ffn_body, final=final),
        grid=(m // tm,),
        in_specs=in_specs,
        out_specs=pl.BlockSpec((tm, D_MODEL), lambda i: (i, 0)),
        out_shape=jax.ShapeDtypeStruct((m, D_MODEL), F32),
        compiler_params=_cparams("parallel"),
    )(*args)


def _cross_body(x_ref, oa_ref, ob_ref, oc_ref, od_ref, wout_ref, g_ref, wq_ref, mk_ref, mv_ref, wo_ref, o_ref):
    ns, tq, _ = x_ref.shape
    rows = ns * tq
    x = x_ref[...].reshape(rows, D_MODEL)
    for gi, m_ref in enumerate((oa_ref, ob_ref, oc_ref, od_ref)):
        x = x + _dot(m_ref[...].reshape(rows, GROUP_W).astype(BF16), wout_ref[gi * GROUP_W:(gi + 1) * GROUP_W, :])
    hb = _rms(x, g_ref[...]).astype(BF16)
    q = (_dot(hb, wq_ref[...]) * MEM_HEAD_DIM ** -0.5).astype(BF16)
    head = lambda h: slice(h * MEM_HEAD_DIM, (h + 1) * MEM_HEAD_DIM)
    pairs = [(sq, h) for sq in range(ns) for h in range(MEM_HEADS)]
    if len(mk_ref.shape) == 4:
        mem = lambda ref, sq, h: ref[sq, :, h, :].astype(BF16)
    else:
        mem = lambda ref, sq, h: ref[sq, :, head(h)].astype(BF16)
    scores = [_dot_nt(q[sq * tq:(sq + 1) * tq, head(h)], mem(mk_ref, sq, h)) for sq, h in pairs]
    probs = []
    for s in scores:
        p = jnp.exp(s - jnp.max(s, axis=-1, keepdims=True))
        probs.append((p / jnp.sum(p, axis=-1, keepdims=True)).astype(BF16))
    outs = [_dot(p, mem(mv_ref, sq, h)).astype(BF16) for p, (sq, h) in zip(probs, pairs)]
    o = jnp.concatenate([jnp.concatenate(outs[sq * MEM_HEADS:(sq + 1) * MEM_HEADS], axis=1) for sq in range(ns)], axis=0)
    o_ref[...] = (x + _dot(o, wo_ref[...])).reshape(ns, tq, D_MODEL)


def _cross(x, mix, w_out, gain, wq, mk, mv, layer, wo, w_layer):
    b, t, _ = x.shape
    nm = mk.shape[2]
    tq = min(ROW_TILE, t)
    ns = CROSS_SEQS if CROSS_SEQS * t <= ROW_TILE else 1
    assert t % tq == 0 and b % ns == 0
    row_spec = lambda w: pl.BlockSpec((ns, tq, w), lambda i, j: (i, j, 0))
    mem_spec = pl.BlockSpec((None, ns) + mk.shape[2:], lambda i, j: (layer, i) + (0,) * (mk.ndim - 2))
    return pl.pallas_call(
        _cross_body,
        grid=(b // ns, t // tq),
        in_specs=[row_spec(D_MODEL)] + [row_spec(GROUP_W)] * 4 + [_weight(w_out, w_layer), _resident((1, D_MODEL)),
                  _weight(wq, w_layer), mem_spec, mem_spec, _weight(wo, w_layer)],
        out_specs=row_spec(D_MODEL),
        out_shape=jax.ShapeDtypeStruct((b, t, D_MODEL), F32),
        compiler_params=_cparams("parallel", "parallel"),
    )(x, *mix, w_out, gain.reshape(1, -1), wq, mk, mv, wo)


def _diff_body(lam_ref, q_ref, kn_ref, vn_ref, kp_ref, vp_ref, bn_ref, o_ref, q8_sc, m_sc, l_sc, acc_sc, *,
               tq, tk, past, lam_init, ns):
    ng = 2 * N_HEADS
    lane = _iota((tq, GROUP_W), 1)
    for sq in range(ns):
        q = q_ref[sq] * (DIFF_HALF ** -0.5 * LOG2E)
        for g in range(ng):
            lo = (g // 2) * HEAD_DIM + (g % 2) * DIFF_HALF
            q8_sc[sq, g * tq:(g + 1) * tq, :] = jnp.where((lane >= lo) & (lane < lo + DIFF_HALF), q, 0.0).astype(BF16)
    m_sc[...] = jnp.full(m_sc.shape, -jnp.inf, F32)
    l_sc[...] = jnp.zeros(l_sc.shape, F32)
    acc_sc[...] = jnp.zeros(acc_sc.shape, F32)
    pq = past + _iota((tq, 1), 0)
    pq_f = pq.astype(F32)
    lanes = _lanes
    pairs = [(sq, g) for sq in range(ns) for g in range(ng)]

    def block(k_ref, v_ref, k0, n, near, pos0=0):
        pk = pos0 + k0 + _iota((1, n), 1)
        pk_f = pk.astype(F32)
        if near:
            allowed = (pk >> 6) <= (pq >> 6)
            shift = pq_f - jnp.abs(pq_f - pk_f)
        s_all = [_dot(q8_sc[sq], k_ref[sq, :, pl.ds(k0, n)].astype(BF16)) for sq in range(ns)]
        scores = {}
        for sq, g in pairs:
            slope = 2.0 ** (-2 * (g // 2 + 1)) * LOG2E
            s = s_all[sq][g * tq:(g + 1) * tq]
            scores[sq, g] = jnp.where(allowed, s + slope * shift, NEG) if near else s + slope * pk_f
        m_prevs = {k: m_sc[k[0], k[1]] for k in pairs}
        m_news = {k: jnp.maximum(m_prevs[k], jnp.max(scores[k], axis=-1, keepdims=True)) for k in pairs}
        probs = {k: jnp.exp2(scores[k] - lanes(m_news[k], n)) for k in pairs}
        alphas = {k: jnp.exp2(m_prevs[k] - m_news[k]) for k in pairs}
        for k in pairs:
            l_sc[k[0], k[1]] = alphas[k] * l_sc[k[0], k[1]] + jnp.sum(probs[k], axis=-1, keepdims=True)
            m_sc[k[0], k[1]] = m_news[k]
        pvs = [_dot_nt(jnp.concatenate([probs[sq, g].astype(BF16) for g in range(ng)], axis=0),
                       v_ref[sq, :, pl.ds(k0, n)].astype(BF16)) for sq in range(ns)]
        for sq, g in pairs:
            acc_sc[sq, g] = acc_sc[sq, g] * lanes(alphas[sq, g], GROUP_W) + pvs[sq][g * tq:(g + 1) * tq]

    def past_step(j, carry):
        block(kp_ref, vp_ref, pl.multiple_of(j * tk, tk), tk, False)
        return carry
    lax.fori_loop(0, past // tk, past_step, 0)
    block(kn_ref, vn_ref, 0, tq, True, pos0=past)

    lam = lam_ref[0]
    lane_head = _iota((1, GROUP_W), 1) >> 6
    ones16 = _head_block_ones(GROUP_W).astype(BF16)
    for sq in range(ns):
        o = jnp.zeros((tq, GROUP_W), F32)
        for h in range(N_HEADS):
            o0 = acc_sc[sq, 2 * h] / lanes(l_sc[sq, 2 * h], GROUP_W)
            o1 = acc_sc[sq, 2 * h + 1] / lanes(l_sc[sq, 2 * h + 1], GROUP_W)
            o = jnp.where(lane_head == h, o0 - lam * o1, o)
        ms = _mm_r(o * o, ones16) * (1.0 / HEAD_DIM)
        o_ref[sq] = o * lax.rsqrt(ms + EPS) * bn_ref[...] * (1.0 - lam_init)


def _diff_attn(lam, q, kn, vn, kp, vp, layer, bnorm, lam_init):
    b, t, _ = q.shape
    past = kp.shape[3]
    ng = 2 * N_HEADS
    ns = DECODE_SEQS
    tq, tk = t, math.gcd(past, 512)
    assert t == CHUNK and past % CHUNK == 0 and tk % LANES == 0 and b % ns == 0
    seq_spec = pl.BlockSpec((ns, GROUP_W, t), lambda i: (i, 0, 0))
    past_spec = pl.BlockSpec((None, ns, GROUP_W, past), lambda i: (layer, i, 0, 0))
    return pl.pallas_call(
        functools.partial(_diff_body, tq=tq, tk=tk, past=past, lam_init=lam_init, ns=ns),
        grid=(b // ns,),
        in_specs=[pl.BlockSpec(memory_space=pltpu.SMEM), pl.BlockSpec((ns, tq, GROUP_W), lambda i: (i, 0, 0)),
                  seq_spec, seq_spec, past_spec, past_spec, _resident((1, GROUP_W))],
        out_specs=pl.BlockSpec((ns, tq, GROUP_W), lambda i: (i, 0, 0)),
        out_shape=jax.ShapeDtypeStruct((b, t, GROUP_W), F32),
        scratch_shapes=[pltpu.VMEM((ns, ng * tq, GROUP_W), BF16), pltpu.VMEM((ns, ng, tq, LANES), F32),
                        pltpu.VMEM((ns, ng, tq, LANES), F32), pltpu.VMEM((ns, ng, tq, GROUP_W), F32)],
        compiler_params=_cparams("parallel"),
    )(lam, q, kn, vn, kp, vp, bnorm)


def _diff_prompt_body(lam_ref, q_ref, k_ref, v_ref, bn_ref, o_ref, qt_sc, m_sc, acc_sc, *, tq, tk, lam_init):
    i = pl.program_id(1)
    ng = 2 * N_HEADS
    qt = (q_ref[0] * (DIFF_HALF ** -0.5 * LOG2E)).T
    feat = _iota((GROUP_W, tq), 0)
    for g in range(ng):
        lo = (g // 2) * HEAD_DIM + (g % 2) * DIFF_HALF
        qt_sc[g] = jnp.where((feat >= lo) & (feat < lo + DIFF_HALF), qt, 0.0).astype(BF16)
    m_sc[...] = jnp.full(m_sc.shape, -jnp.inf, F32)
    acc_sc[...] = jnp.zeros(acc_sc.shape, F32)
    pq = i * tq + _iota((1, tq), 1)
    pq_f = pq.astype(F32)
    ones_rows = jnp.ones((SUM_ROWS, tk), BF16)

    def block(k0, n, near):
        kb = k_ref[0, pl.ds(k0, n), :].astype(BF16)
        vb = v_ref[0, :, pl.ds(k0, n)].astype(BF16)
        pk = k0 + _iota((n, LANES), 0)
        pk_f = pk.astype(F32)
        if near:
            allowed = (_lanes(pk, tq) >> 6) <= (pq >> 6)
            shift = pq_f - jnp.abs(pq_f - _lanes(pk_f, tq))
        groups = range(ng)
        scores = []
        for g in groups:
            slope = 2.0 ** (-2 * (g // 2 + 1)) * LOG2E
            s = _dot(kb, qt_sc[g])
            scores.append(jnp.where(allowed, s + slope * shift, NEG) if near else s + _lanes(slope * pk_f, tq))
        m_prevs = [m_sc[g] for g in groups]
        m_news = [jnp.maximum(m_prevs[g], jnp.max(scores[g], axis=0, keepdims=True)) for g in groups]
        probs = [jnp.exp2(scores[g] - m_news[g]) for g in groups]
        for g in groups:
            alpha = jnp.exp2(m_prevs[g] - m_news[g])
            m_sc[g] = m_news[g]
            h = g // 2
            pv = _dot(jnp.concatenate([vb[h * HEAD_DIM:(h + 1) * HEAD_DIM, :], ones_rows[:, :n]], axis=0),
                      probs[g].astype(BF16))
            acc_sc[g] = acc_sc[g] * alpha + pv

    def prev_step(j, carry):
        block(pl.multiple_of(j * tk, tk), tk, False)
        return carry
    lax.fori_loop(0, (i * tq) // tk, prev_step, 0)

    @pl.when((i * tq) % tk != 0)
    def _():
        block(pl.multiple_of((i - 1) * tq, tq), tq, False)
    block(pl.multiple_of(i * tq, tq), tq, True)

    lam = lam_ref[0]
    norm = lambda g: acc_sc[g, :HEAD_DIM, :] / acc_sc[g, HEAD_DIM:HEAD_DIM + 1, :]
    heads = [norm(2 * h) - lam * norm(2 * h + 1) for h in range(N_HEADS)]
    o = jnp.concatenate(heads, axis=0).T
    ms = _mm_r(o * o, _head_block_ones(GROUP_W).astype(BF16)) * (1.0 / HEAD_DIM)
    o_ref[0] = o * lax.rsqrt(ms + EPS) * bn_ref[...] * (1.0 - lam_init)


def _diff_attn_prompt(lam, q, k_rows, v_feat, bnorm, lam_init):
    b, t, _ = q.shape
    ng = 2 * N_HEADS
    tq = min(256, t)
    tk = 2 * tq if t % (2 * tq) == 0 else tq
    assert t % tq == 0 and tq % LANES == 0
    return pl.pallas_call(
        functools.partial(_diff_prompt_body, tq=tq, tk=tk, lam_init=lam_init),
        grid=(b, t // tq),
        in_specs=[pl.BlockSpec(memory_space=pltpu.SMEM), pl.BlockSpec((1, tq, GROUP_W), lambda i, j: (i, j, 0)),
                  pl.BlockSpec((1, t, GROUP_W), lambda i, j: (i, 0, 0)),
                  pl.BlockSpec((1, GROUP_W, t), lambda i, j: (i, 0, 0)), _resident((1, GROUP_W))],
        out_specs=pl.BlockSpec((1, tq, GROUP_W), lambda i, j: (i, j, 0)),
        out_shape=jax.ShapeDtypeStruct((b, t, GROUP_W), F32),
        scratch_shapes=[pltpu.VMEM((ng, GROUP_W, tq), BF16), pltpu.VMEM((ng, 1, tq), F32),
                        pltpu.VMEM((ng, HEAD_DIM + SUM_ROWS, tq), F32)],
        compiler_params=_cparams("parallel", "parallel"),
    )(lam, q, k_rows, v_feat, bnorm)


def _stick_body(q_ref, kn_ref, vn_ref, kp_ref, vp_ref, o_ref, q4_sc, c_sc, acc_sc, *, tq, tk, past, ns):
    nr = N_HEADS * tq
    lane = _iota((tq, GROUP_W), 1) >> 6
    for sq in range(ns):
        q = q_ref[sq] * (HEAD_DIM ** -0.5 * LOG2E)
        for h in range(N_HEADS):
            q4_sc[sq, h * tq:(h + 1) * tq, :] = jnp.where(lane == h, q, 0.0).astype(BF16)
    c_sc[...] = jnp.zeros(c_sc.shape, F32)
    acc_sc[...] = jnp.zeros(acc_sc.shape, F32)
    pq = past + (_iota((nr, 1), 0) & (tq - 1))
    laters = {w: (_iota((w, w), 0) > _iota((w, w), 1)).astype(BF16)
              for w in {min(tq, STICK_SUB), min(tk, STICK_SUB)}}
    seqs = range(ns)

    def block(k_ref, v_ref, k0, n, masked, pos0=0):
        sub = min(n, STICK_SUB)
        later = laters[sub]
        zs = [_dot(q4_sc[sq], k_ref[sq, :, pl.ds(k0, n)].astype(BF16)) for sq in seqs]
        lss = [jnp.minimum(z, 0.0) - jnp.log2(1.0 + jnp.exp2(-jnp.abs(z))) for z in zs]
        lks = [ls - z for ls, z in zip(lss, zs)]
        if masked:
            mask = (pos0 + k0 + _iota((1, n), 1)) < pq
            lks = [jnp.where(mask, lk, 0.0) for lk in lks]
        his = [lk.astype(BF16) for lk in lks]
        los = [(lk - hi.astype(F32)).astype(BF16) for lk, hi in zip(lks, his)]
        carries = [c_sc[sq] for sq in seqs]
        parts = [[] for _ in seqs]
        for sb in reversed(range(n // sub)):
            sl = slice(sb * sub, (sb + 1) * sub)
            for sq in seqs:
                after = _dot(his[sq][:, sl], later) + _dot(los[sq][:, sl], later)
                parts[sq].append(lss[sq][:, sl] + after + _lanes(carries[sq], sub))
                carries[sq] = carries[sq] + (after[:, 0:1] + lks[sq][:, sb * sub:sb * sub + 1])
        for sq in seqs:
            c_sc[sq] = carries[sq]
            e = parts[sq][0] if len(parts[sq]) == 1 else jnp.concatenate(parts[sq][::-1], axis=1)
            if masked:
                a = jnp.where(mask, jnp.exp2(jnp.where(mask, e, 0.0)), 0.0)
            else:
                a = jnp.exp2(e)
            acc_sc[sq] = acc_sc[sq] + _dot_nt(a.astype(BF16), v_ref[sq, :, pl.ds(k0, n)].astype(BF16))

    block(kn_ref, vn_ref, 0, tq, True, pos0=past)
    nb = past // tk

    def past_step(j, carry):
        block(kp_ref, vp_ref, pl.multiple_of((nb - 1 - j) * tk, tk), tk, False)
        return carry
    lax.fori_loop(0, nb, past_step, 0)

    for sq in seqs:
        o = acc_sc[sq, 0:tq, :]
        for h in range(1, N_HEADS):
            o = jnp.where(lane == h, acc_sc[sq, h * tq:(h + 1) * tq, :], o)
        o_ref[sq] = o


def _stick_attn(q, kn, vn, kp, vp, layer):
    b, t, _ = q.shape
    past = kp.shape[3]
    ns = DECODE_SEQS
    tq, tk = t, math.gcd(past, 1024)
    assert tq & (tq - 1) == 0 and tk % min(tk, STICK_SUB) == 0 and b % ns == 0
    seq_spec = pl.BlockSpec((ns, GROUP_W, t), lambda i: (i, 0, 0))
    past_spec = pl.BlockSpec((None, ns, GROUP_W, past), lambda i: (layer, i, 0, 0))
    return pl.pallas_call(
        functools.partial(_stick_body, tq=tq, tk=tk, past=past, ns=ns),
        grid=(b // ns,),
        in_specs=[pl.BlockSpec((ns, tq, GROUP_W), lambda i: (i, 0, 0)), seq_spec, seq_spec, past_spec, past_spec],
        out_specs=pl.BlockSpec((ns, tq, GROUP_W), lambda i: (i, 0, 0)),
        out_shape=jax.ShapeDtypeStruct((b, t, GROUP_W), F32),
        scratch_shapes=[pltpu.VMEM((ns, N_HEADS * tq, GROUP_W), BF16), pltpu.VMEM((ns, N_HEADS * tq, LANES), F32),
                        pltpu.VMEM((ns, N_HEADS * tq, GROUP_W), F32)],
        compiler_params=_cparams("parallel"),
    )(q, kn, vn, kp, vp)


def _stick_prompt_body(q_ref, k_ref, v_ref, o_ref, qt_sc, c_sc, acc_sc, *, tq, tk):
    i = pl.program_id(1)
    qt = (q_ref[0] * (HEAD_DIM ** -0.5 * LOG2E)).T
    feat_head = _iota((GROUP_W, tq), 0) >> 6
    for h in range(N_HEADS):
        qt_sc[h] = jnp.where(feat_head == h, qt, 0.0).astype(BF16)
    c_sc[...] = jnp.zeros(c_sc.shape, F32)
    acc_sc[...] = jnp.zeros(acc_sc.shape, F32)
    pq = i * tq + _iota((1, tq), 1)
    sub = min(tq, STICK_SUB)
    later = (_iota((sub, sub), 1) > _iota((sub, sub), 0)).astype(BF16)

    def block(k0, n, masked):
        kb = k_ref[0, pl.ds(k0, n), :].astype(BF16)
        vb = v_ref[0, :, pl.ds(k0, n)].astype(BF16)
        if masked:
            mask = _lanes(k0 + _iota((n, LANES), 0), tq) < pq
        heads = range(N_HEADS)
        zs = [_dot(kb, qt_sc[h]) for h in heads]
        lss = [jnp.minimum(z, 0.0) - jnp.log2(1.0 + jnp.exp2(-jnp.abs(z))) for z in zs]
        lks = [ls - z for ls, z in zip(lss, zs)]
        if masked:
            lks = [jnp.where(mask, lk, 0.0) for lk in lks]
        es = []
        for h in heads:
            lk = lks[h]
            hi = lk.astype(BF16)
            lo = (lk - hi.astype(F32)).astype(BF16)
            carry = c_sc[h]
            parts = []
            for sb in reversed(range(n // sub)):
                sl = slice(sb * sub, (sb + 1) * sub)
                after = _dot(later, hi[sl]) + _dot(later, lo[sl])
                parts.append(lss[h][sl] + after + carry)
                carry = carry + (after[0:1] + lk[sb * sub:sb * sub + 1])
            c_sc[h] = carry
            es.append(parts[0] if len(parts) == 1 else jnp.concatenate(parts[::-1], axis=0))
        for h in heads:
            if masked:
                a = jnp.where(mask, jnp.exp2(jnp.where(mask, es[h], 0.0)), 0.0)
            else:
                a = jnp.exp2(es[h])
            acc_sc[h] = acc_sc[h] + _dot(vb[h * HEAD_DIM:(h + 1) * HEAD_DIM, :], a.astype(BF16))

    block(pl.multiple_of(i * tq, tq), tq, True)

    @pl.when((i * tq) % tk != 0)
    def _():
        block(pl.multiple_of((i - 1) * tq, tq), tq, False)
    nfull = (i * tq) // tk

    def prev_step(j, carry):
        block(pl.multiple_of((nfull - 1 - j) * tk, tk), tk, False)
        return carry
    lax.fori_loop(0, nfull, prev_step, 0)

    o_ref[0] = jnp.concatenate([acc_sc[h] for h in range(N_HEADS)], axis=0).T


def _stick_attn_prompt(q, k_rows, v_feat):
    b, t, _ = q.shape
    tq = min(256, t)
    tk = 2 * tq if t % (2 * tq) == 0 else tq
    assert t % tq == 0 and tq % LANES == 0 and tk % min(tk, STICK_SUB) == 0
    return pl.pallas_call(
        functools.partial(_stick_prompt_body, tq=tq, tk=tk),
        grid=(b, t // tq),
        in_specs=[pl.BlockSpec((1, tq, GROUP_W), lambda i, j: (i, j, 0)),
                  pl.BlockSpec((1, t, GROUP_W), lambda i, j: (i, 0, 0)),
                  pl.BlockSpec((1, GROUP_W, t), lambda i, j: (i, 0, 0))],
        out_specs=pl.BlockSpec((1, tq, GROUP_W), lambda i, j: (i, j, 0)),
        out_shape=jax.ShapeDtypeStruct((b, t, GROUP_W), F32),
        scratch_shapes=[pltpu.VMEM((N_HEADS, GROUP_W, tq), BF16), pltpu.VMEM((N_HEADS, 1, tq), F32),
                        pltpu.VMEM((N_HEADS, HEAD_DIM, tq), F32)],
        compiler_params=_cparams("parallel", "parallel"),
    )(q, k_rows, v_feat)


def _pair_mask(n, s, reps=1):
    r, c = _iota((reps * n, n), 0) & (n - 1), _iota((reps * n, n), 1)
    return ((r // (2 * s)) == (c // (2 * s))) & (((r // s) & 1) == 1) & (((c // s) & 1) == 0)


def _state_spec(nb):
    return pl.BlockSpec((nb, N_HEADS, HEAD_DIM, HEAD_DIM), lambda i, j: (i, 0, 0, 0))


def _load_state(s_sc, s0_ref, transpose):
    s_sc[...] = jnp.zeros(s_sc.shape, F32)
    for b in range(s_sc.shape[0]):
        for h in range(N_HEADS):
            blk = s0_ref[b, h]
            s_sc[b, h * HEAD_DIM:(h + 1) * HEAD_DIM, h * HEAD_DIM:(h + 1) * HEAD_DIM] = blk.T if transpose else blk


def _store_state(sout_ref, s_sc, transpose):
    for b in range(s_sc.shape[0]):
        for h in range(N_HEADS):
            blk = s_sc[b, h * HEAD_DIM:(h + 1) * HEAD_DIM, h * HEAD_DIM:(h + 1) * HEAD_DIM]
            sout_ref[b, h] = blk.T if transpose else blk


def _head_rows(x, lane_head):
    return jnp.concatenate([jnp.where(lane_head == h, x, 0.0) for h in range(N_HEADS)], axis=0)


def _head_diag(x, lane_head):
    n = x.shape[0] // N_HEADS
    out = x[:n]
    for h in range(1, N_HEADS):
        out = jnp.where(lane_head == h, x[h * n:(h + 1) * n], out)
    return out


def _gdn_body(x_ref, z_ref, abg_ref, cw_ref, cs_ref, s0_ref, alog_ref, dtb_ref, an_ref,
              o_ref, sout_ref, s_sc, xb_sc, *, nc, nb):
    c = pl.program_id(1)
    L = CHUNK
    pad = 8

    @pl.when(c == 0)
    def _():
        _load_state(s_sc, s0_ref, transpose=False)
        xb_sc[:, pad - 3:pad, :] = cs_ref[...]

    ones_bd = _head_block_ones(GROUP_W)
    ones_bd16 = ones_bd.astype(BF16)
    er, ec = _iota((ABG_W, GROUP_W), 0), _iota((ABG_W, GROUP_W), 1) >> 6
    e_beta = (er == ec).astype(BF16)
    e_g = (er == ec + N_HEADS).astype(BF16)
    ri, ci = _iota((L, L), 0), _iota((L, L), 1)
    tri16 = (ri >= ci).astype(BF16)
    incl = ri >= ci
    strict = ri > ci
    eye = (ri == ci).astype(F32)
    pair_masks = [_pair_mask(L, sz) for sz in (1, 2, 4, 8, 16, 32)]
    lane_head = _iota((1, GROUP_W), 1) >> 6
    cw = cw_ref[...]

    a_all, per_seq = [], []
    for b in range(nb):
        x = x_ref[b]
        xb_sc[b, pad:pad + L, :] = x
        u = (x * cw[3:4] + xb_sc[b, pad - 1:pad - 1 + L, :] * cw[2:3]
             + xb_sc[b, pad - 2:pad - 2 + L, :] * cw[1:2] + xb_sc[b, pad - 3:pad - 3 + L, :] * cw[0:1])
        xb_sc[b, pad - 3:pad, :] = x[L - 3:L, :]
        u = u * _sigmoid(u)
        q = u[:, :GROUP_W]
        k = u[:, GROUP_W:2 * GROUP_W]
        v = u[:, 2 * GROUP_W:]
        q = q * lax.rsqrt(_mm_r(q * q, ones_bd16) + EPS) * HEAD_DIM ** -0.5
        k = k * lax.rsqrt(_mm_r(k * k, ones_bd16) + EPS)

        abg = abg_ref[b]
        beta_n = _sigmoid(abg)
        g_n = -jnp.exp(alog_ref[...]) * _softplus(abg + dtb_ref[...])
        beta = _mm_r(beta_n, e_beta)
        gc_n = _mm_l(tri16, g_n)
        gc_t = gc_n.T
        gc = _mm_r(gc_n, e_g)
        exp_g = jnp.exp(gc)
        g_last = gc[L - 1:L, :]

        kb = k * beta
        k16 = k.astype(BF16)
        a4 = _bdot(_head_rows(kb, lane_head).astype(BF16), k16, _NT)
        qk4 = _bdot(_head_rows(q, lane_head).astype(BF16), k16, _NT)
        dec = []
        for h in range(N_HEADS):
            diff = gc_n[:, N_HEADS + h:N_HEADS + h + 1] - gc_t[N_HEADS + h:N_HEADS + h + 1, :]
            dec.append(jnp.where(incl, jnp.exp(jnp.where(incl, diff, 0.0)), 0.0))
        dec = jnp.stack(dec)
        a_all.append(jnp.where(strict, a4.reshape(N_HEADS, L, L) * dec, 0.0))
        per_seq.append((q, k, v * beta, kb * exp_g, q * exp_g, k * jnp.exp(g_last - gc), jnp.exp(g_last),
                        qk4.reshape(N_HEADS, L, L) * dec))

    a_all = jnp.concatenate(a_all, axis=0)
    t_inv = eye - jnp.where(pair_masks[0], a_all, 0.0)
    a16 = a_all.astype(BF16)
    for pm in pair_masks[1:]:
        t16 = t_inv.astype(BF16)
        t_inv = t_inv - _bdot(t16, _bdot(jnp.where(pm, a16, 0.0), t16, _BNN).astype(BF16), _BNN)

    for b in range(nb):
        q, k, vb, kbg, qg, k_dec, decay_last, qk = per_seq[b]
        s = s_sc[b]
        s16 = s.astype(BF16)
        rhs = vb - _bdot(kbg.astype(BF16), s16, _NN)
        t4 = t_inv[b * N_HEADS:(b + 1) * N_HEADS].reshape(N_HEADS * L, L)
        w = _head_diag(_mm1(t4, rhs), lane_head)
        o = _bdot(qg.astype(BF16), s16, _NN) + _head_diag(_mm1(qk.reshape(N_HEADS * L, L), w), lane_head)
        s_sc[b] = s * decay_last + _mm1(k_dec.T, w) * ones_bd

        ms = _mm_r(o * o, ones_bd16) * (1.0 / HEAD_DIM)
        zg = z_ref[b]
        o_ref[b] = o * lax.rsqrt(ms + EPS) * an_ref[...] * (zg * _sigmoid(zg))

    @pl.when(c == nc - 1)
    def _():
        _store_state(sout_ref, s_sc, transpose=False)


def _gdn(aqkv, az, abg, conv_w, conv_state, s0, alog_row, dtb_row, anorm_row):
    b, t, _ = aqkv.shape
    nc = t // CHUNK
    nb = SEQ_PER_STEP
    assert b % nb == 0
    cmap = lambda i, j: (i, j, 0)
    bmap = lambda i, j: (i, 0, 0)
    return pl.pallas_call(
        functools.partial(_gdn_body, nc=nc, nb=nb),
        grid=(b // nb, nc),
        in_specs=[pl.BlockSpec((nb, CHUNK, 3 * GROUP_W), cmap), pl.BlockSpec((nb, CHUNK, GROUP_W), cmap),
                  pl.BlockSpec((nb, CHUNK, ABG_W), cmap), _resident((CONV_W, 3 * GROUP_W)),
                  pl.BlockSpec((nb, CONV_W - 1, 3 * GROUP_W), bmap), _state_spec(nb),
                  _resident((1, ABG_W)), _resident((1, ABG_W)), _resident((1, GROUP_W))],
        out_specs=[pl.BlockSpec((nb, CHUNK, GROUP_W), cmap), _state_spec(nb)],
        out_shape=[jax.ShapeDtypeStruct((b, t, GROUP_W), F32), jax.ShapeDtypeStruct(s0.shape, F32)],
        scratch_shapes=[pltpu.VMEM((nb, GROUP_W, GROUP_W), F32), pltpu.VMEM((nb, 8 + CHUNK, 3 * GROUP_W), F32)],
        compiler_params=_cparams("parallel", "arbitrary"),
    )(aqkv, az, abg, conv_w, conv_state, s0, alog_row, dtb_row, anorm_row)


def _hgrn_body(q_ref, f_ref, i_ref, g_ref, lb_ref, dn_ref, s0_ref, o_ref, sout_ref, s_sc, *, nc, nb):
    c = pl.program_id(1)
    L = CHUNK

    @pl.when(c == 0)
    def _():
        _load_state(s_sc, s0_ref, transpose=True)

    lb = lb_ref[...]
    ones_bd = _head_block_ones(GROUP_W)
    ones_bd16 = ones_bd.astype(BF16)
    ri, ci = _iota((L, L), 0), _iota((L, L), 1)
    tri16 = (ri >= ci).astype(BF16)
    sizes = (1, 2, 4, 8, 16, 32)
    sel16 = jnp.concatenate([(ci == (ri // (2 * sz)) * (2 * sz) + sz).astype(BF16) for sz in sizes], axis=0)
    eye4 = (_iota((N_HEADS * L, L), 0) & (L - 1)) == _iota((N_HEADS * L, L), 1)
    pair_masks4 = [_pair_mask(L, sz, reps=N_HEADS) for sz in sizes]
    lane_head = _iota((1, GROUP_W), 1) >> 6
    rows = _iota((L, 1), 0)

    pre = []
    for b in range(nb):
        fl = f_ref[b]
        log_f = _log_sigmoid(fl) + jnp.log(1.0 + lb * jnp.exp(-fl))
        k = (1.0 - lb) * _sigmoid(-fl)
        q = q_ref[b] * HEAD_DIM ** -0.5
        bc = _mm_l(tri16, log_f)
        pre.append((q, k, bc))
    b_refs = [_mm_l(sel16, bc) for _, _, bc in pre]
    a_all = [jnp.where(eye4, _mm1(_head_rows(q, lane_head), k, _NT), 0.0) for q, k, _ in pre]
    for lv, sz in enumerate(sizes):
        later = ((rows // sz) & 1) == 1
        for b in range(nb):
            q, k, bc = pre[b]
            b_ref = b_refs[b][lv * L:(lv + 1) * L]
            x = jnp.where(later, q * jnp.exp(jnp.where(later, bc - b_ref, 0.0)), 0.0)
            y = jnp.where(later, 0.0, k * jnp.exp(jnp.where(later, 0.0, b_ref - bc)))
            a_all[b] = a_all[b] + jnp.where(pair_masks4[lv], _mm1(_head_rows(x, lane_head), y, _NT), 0.0)

    for b in range(nb):
        q, k, bc = pre[b]
        v = i_ref[b]
        st = s_sc[b]
        o = _mm1(q * jnp.exp(bc), st, _NT) + _head_diag(_mm1(a_all[b], v), lane_head)
        b_last = bc[L - 1:L, :]
        st_new = st * jnp.exp(b_last) + _mm1(v.T, k * jnp.exp(b_last - bc)) * ones_bd
        s_sc[b] = st_new

        ms = _mm_r(o * o, ones_bd16) * (1.0 / HEAD_DIM)
        zg = g_ref[b]
        o_ref[b] = o * lax.rsqrt(ms + EPS) * dn_ref[...] * (zg * _sigmoid(zg))

    @pl.when(c == nc - 1)
    def _():
        _store_state(sout_ref, s_sc, transpose=True)


def _hgrn(dq, df, di, dg, lb_row, dnorm_row, s0):
    b, t, _ = dq.shape
    nc = t // CHUNK
    nb = SEQ_PER_STEP
    assert b % nb == 0
    cmap = lambda i, j: (i, j, 0)
    bmap = lambda i, j: (i, 0, 0)
    cspec = pl.BlockSpec((nb, CHUNK, GROUP_W), cmap)
    return pl.pallas_call(
        functools.partial(_hgrn_body, nc=nc, nb=nb),
        grid=(b // nb, nc),
        in_specs=[cspec, cspec, cspec, cspec, _resident((1, GROUP_W)), _resident((1, GROUP_W)),
                  _state_spec(nb)],
        out_specs=[cspec, _state_spec(nb)],
        out_shape=[jax.ShapeDtypeStruct((b, t, GROUP_W), F32), jax.ShapeDtypeStruct(s0.shape, F32)],
        scratch_shapes=[pltpu.VMEM((nb, GROUP_W, GROUP_W), F32)],
        compiler_params=_cparams("parallel", "arbitrary"),
    )(dq, df, di, dg, lb_row, dnorm_row, s0)


IN_WIDTHS = (3 * GROUP_W,) + (GROUP_W,) * 11 + (ABG_W,)
KV_SEGMENTS = (3, 4, 6, 7)
K_SEGMENTS = (3, 6)


def _layer(x, l, st, mk, mv, mem_layer, prm, final_gain):
    b, t, _ = x.shape
    m = b * t
    tm = math.gcd(m, ROW_TILE)
    has_past = st["b_k"] is not None
    outs = _fused_linear([x.reshape(m, D_MODEL)], prm["w_in"], l, IN_WIDTHS, gain=prm["norm_mix"][l], tm=tm,
                         feat_major=KV_SEGMENTS, rows_per_seq=t, rows_too=() if has_past else K_SEGMENTS)
    outs = [o if idx in KV_SEGMENTS else o.reshape(b, t, -1) for idx, o in enumerate(outs)]
    (a_qkv, a_z, b_q, b_k, b_v, c_q, c_k, c_v, d_q, d_f, d_i, d_g, a_bg) = outs[:len(IN_WIDTHS)]

    o_a, a_s = _gdn(a_qkv, a_z, a_bg, prm["a_conv_w"][l], st["a_conv"], st["a_S"].astype(F32),
                    prm["alog_row"][l], prm["dtb_row"][l], prm["a_norm"][l])
    lam_init = 0.8 - 0.6 * math.exp(-0.3 * l)
    if has_past:
        o_b = _diff_attn(prm["lam"][l], b_q, b_k, b_v, st["b_k"], st["b_v"], l, prm["b_norm"][l], lam_init)
        o_c = _stick_attn(c_q, c_k, c_v, st["c_k"], st["c_v"], l)
    else:
        b_k_rows, c_k_rows = outs[len(IN_WIDTHS):]
        o_b = _diff_attn_prompt(prm["lam"][l], b_q, b_k_rows, b_v, prm["b_norm"][l], lam_init)
        o_c = _stick_attn_prompt(c_q, c_k_rows, c_v)
    o_d, d_s = _hgrn(d_q, d_f, d_i, d_g, prm["lb"][l], prm["d_norm"][l], st["d_S"].astype(F32))

    x2 = _cross(x, (o_a, o_b, o_c, o_d), prm["w_out"], prm["norm_cross"][l], prm["w_cq"], mk, mv, mem_layer,
                prm["w_co"], l)
    x2 = _ffn(x2.reshape(m, D_MODEL), prm["norm_ffn"][l], prm["w_gate"], prm["w_up"], prm["w_down"], l,
              final_gain=final_gain, tm=tm)

    if t >= CONV_W - 1:
        conv_new = a_qkv[:, t - (CONV_W - 1):, :]
    else:
        conv_new = jnp.concatenate([st["a_conv"], a_qkv], axis=1)[:, -(CONV_W - 1):, :]
    heads = lambda a: jnp.transpose(a.reshape(b, N_HEADS, HEAD_DIM, t), (0, 3, 1, 2))
    new = (conv_new, a_s, heads(b_k), heads(b_v), heads(c_k), heads(c_v), d_s)
    return x2.reshape(b, t, D_MODEL), new


def kernel(x_prompt, x_sample, mem_prompt, state_a_conv, state_a_S, cache_b_k, cache_b_v, cache_c_k, cache_c_v,
           state_d_S, cache_mem_k, cache_mem_v, norm_mix, w_in, a_conv_w, a_A_log, a_dt_bias, a_norm, b_lam_q1,
           b_lam_k1, b_lam_q2, b_lam_k2, b_norm, d_lb, d_norm, w_out, norm_cross, norm_memtok, w_cq, w_ck, w_cv,
           w_co, norm_ffn, w_gate, w_up, w_down, norm_final):
    depth = w_in.shape[0]
    n_main = 4 * GROUP_W
    w_in_p = jnp.concatenate(
        [w_in[:, :, :n_main], w_in[:, :, n_main + 2 * N_HEADS:], w_in[:, :, n_main:n_main + 2 * N_HEADS],
         jnp.zeros((depth, D_MODEL, ABG_W - 2 * N_HEADS), w_in.dtype)], axis=2).astype(BF16)
    pad_row = lambda a: jnp.pad(a.astype(F32), ((0, 0), (N_HEADS, ABG_W - 2 * N_HEADS)))[:, None, :]
    tile_row = lambda a: jnp.tile(a.astype(F32), (1, N_HEADS))[:, None, :]
    p_lb = jax.nn.softmax(d_lb.astype(F32), axis=0)
    lam = (jnp.exp(jnp.sum(b_lam_q1.astype(F32) * b_lam_k1.astype(F32), axis=-1))
           - jnp.exp(jnp.sum(b_lam_q2.astype(F32) * b_lam_k2.astype(F32), axis=-1))
           + jnp.array([0.8 - 0.6 * math.exp(-0.3 * l) for l in range(depth)], F32))
    prm = {
        "w_in": w_in_p, "norm_mix": norm_mix, "a_conv_w": a_conv_w.astype(F32),
        "alog_row": pad_row(a_A_log), "dtb_row": pad_row(a_dt_bias), "a_norm": tile_row(a_norm),
        "lam": lam[:, None], "b_norm": tile_row(b_norm), "lb": (jnp.cumsum(p_lb, axis=0) - p_lb[0])[:, None, :],
        "d_norm": tile_row(d_norm), "w_out": w_out.astype(BF16), "norm_cross": norm_cross,
        "w_cq": w_cq.astype(BF16), "w_co": w_co.astype(BF16), "norm_ffn": norm_ffn,
        "w_gate": w_gate.astype(BF16), "w_up": w_up.astype(BF16), "w_down": w_down.astype(BF16),
    }
    w_ckv = jnp.concatenate([w_ck, w_cv], axis=2).astype(BF16)

    bp, tp, _ = x_prompt.shape
    n_mem = mem_prompt.shape[1]
    h = x_prompt
    p_new = []
    for l in range(depth):
        mk, mv = _fused_linear([mem_prompt.reshape(bp * n_mem, D_MODEL)], w_ckv, l, (D_MODEL, D_MODEL),
                               gain=norm_memtok[l], tm=256)
        mk = mk.reshape(1, bp, n_mem, D_MODEL)
        mv = mv.reshape(1, bp, n_mem, D_MODEL)
        st = {"a_conv": jnp.zeros((bp, CONV_W - 1, 3 * GROUP_W), F32),
              "a_S": jnp.zeros((bp, N_HEADS, HEAD_DIM, HEAD_DIM), F32), "b_k": None, "b_v": None,
              "c_k": None, "c_v": None, "d_S": jnp.zeros((bp, N_HEADS, HEAD_DIM, HEAD_DIM), F32)}
        h, new = _layer(h, l, st, mk, mv, 0, prm, norm_final if l == depth - 1 else None)
        mem4 = lambda a: a.reshape(bp, n_mem, MEM_HEADS, MEM_HEAD_DIM)
        p_new.append(new + (mem4(mk), mem4(mv)))
    y_prompt = h
    p_stacked = [jnp.stack(c) for c in zip(*p_new)]

    bs, ts, _ = x_sample.shape
    past = cache_b_k.shape[2]
    h = x_sample
    s_new = []
    flat = lambda a: jnp.transpose(a, (0, 1, 3, 4, 2)).reshape(depth, bs, GROUP_W, past)
    caches = {"b_k": flat(cache_b_k), "b_v": flat(cache_b_v), "c_k": flat(cache_c_k), "c_v": flat(cache_c_v)}
    for l in range(depth):
        st = {"a_conv": state_a_conv[l], "a_S": state_a_S[l], "d_S": state_d_S[l], **caches}
        h, new = _layer(h, l, st, cache_mem_k, cache_mem_v, l, prm, norm_final if l == depth - 1 else None)
        s_new.append(new)
    y_sample = h
    s_stacked = [jnp.stack(c) for c in zip(*s_new)]

    return (y_prompt, y_sample, *p_stacked, *s_stacked)
```

```python
import functools
import math

import jax
import jax.numpy as jnp
from jax import lax
from jax.experimental import pallas as pl
from jax.experimental.pallas import tpu as pltpu

F32 = jnp.float32
BF16 = jnp.bfloat16

D_MODEL = 1024
GROUP_W = 256
N_HEADS = 4
HEAD_DIM = 64
DIFF_HALF = 32
CHUNK = 64
CONV_W = 4
MEM_HEADS = 4
MEM_HEAD_DIM = 256
D_FF = 2816
HEAD_SHIFT = HEAD_DIM.bit_length() - 1
CHUNK_SHIFT = CHUNK.bit_length() - 1
EPS = 1e-6
NEG = -1e30
LOG2E = 1.4426950408889634
LANES = 128
ABG_W = 128
FF_CHUNK = 256
ROW_TILE = 512
SUM_ROWS = 16
DECODE_SEQS = 2
CROSS_SEQS = 4
STICK_SUB = 256
SEQ_PER_STEP = 4
VMEM_LIMIT = 56 * 1024 * 1024


def _cparams(*sem):
    return pltpu.CompilerParams(dimension_semantics=sem, vmem_limit_bytes=VMEM_LIMIT)


def _dot(a, b):
    return jnp.dot(a, b, preferred_element_type=F32)


def _dot_nt(a, b):
    return lax.dot_general(a, b, (((1,), (1,)), ((), ())), preferred_element_type=F32)


_NN = (((1,), (0,)), ((), ()))
_NT = (((1,), (1,)), ((), ()))
_BNN = (((2,), (1,)), ((0,), (0,)))


def _split(x):
    hi = x.astype(BF16)
    return hi, (x - hi.astype(F32)).astype(BF16)


def _bdot(a, b, dims):
    return lax.dot_general(a, b, dims, preferred_element_type=F32)


def _mm1(a, b, dims=_NN):
    return _bdot(a.astype(BF16), b.astype(BF16), dims)


def _mm_r(a, b16):
    ah, al = _split(a)
    return _bdot(ah, b16, _NN) + _bdot(al, b16, _NN)


def _mm_l(a16, b):
    bh, bl = _split(b)
    return _bdot(a16, bh, _NN) + _bdot(a16, bl, _NN)


def _rms(x, g):
    return x * lax.rsqrt(jnp.mean(x * x, axis=-1, keepdims=True) + EPS) * g


def _sigmoid(x):
    return 1.0 / (1.0 + jnp.exp(-x))


def _log_sigmoid(x):
    return jnp.minimum(x, 0.0) - jnp.log(1.0 + jnp.exp(-jnp.abs(x)))


def _softplus(x):
    return jnp.maximum(x, 0.0) + jnp.log(1.0 + jnp.exp(-jnp.abs(x)))


def _iota(shape, dim):
    return lax.broadcasted_iota(jnp.int32, shape, dim)


def _lanes(x, n):
    return x[:, :n] if n <= LANES else jnp.concatenate([x] * (n // LANES), axis=1)


def _head_block_ones(n):
    return ((_iota((n, n), 0) >> HEAD_SHIFT) == (_iota((n, n), 1) >> HEAD_SHIFT)).astype(F32)


def _resident(shape):
    return pl.BlockSpec(shape, lambda *_: (0,) * len(shape), pipeline_mode=pl.Buffered(1))


def _weight(w, layer):
    return pl.BlockSpec((None,) + w.shape[1:], lambda *_: (layer, 0, 0), pipeline_mode=pl.Buffered(1))


def _linear_body(x_ref, g_ref, w_ref, *out_refs, segs, feat_major, rows_too):
    xb = _rms(x_ref[...], g_ref[...]).astype(BF16)
    extra_refs = dict(zip(rows_too, out_refs[len(segs):]))
    for idx, (o_ref, (s, e)) in enumerate(zip(out_refs, segs)):
        acc = _dot(xb, w_ref[:, s:e])
        if idx in feat_major:
            rows = acc.shape[0] // o_ref.shape[0]
            for sq in range(o_ref.shape[0]):
                o_ref[sq] = acc[sq * rows:(sq + 1) * rows].T
        else:
            o_ref[...] = acc
        if idx in extra_refs:
            extra_refs[idx][...] = acc


def _fused_linear(x, w, layer, widths, gain, tm, feat_major=(), rows_per_seq=None, rows_too=()):
    m, k = x.shape
    n = w.shape[2]
    segs = []
    s = 0
    for wd in widths:
        segs.append((s, s + wd))
        s += wd
    assert s == n and m % tm == 0
    out_specs, out_shape = [], []
    for idx, wd in enumerate(widths):
        if idx in feat_major:
            if rows_per_seq >= tm:
                tiles = rows_per_seq // tm
                assert rows_per_seq % tm == 0
                out_specs.append(pl.BlockSpec((1, wd, tm), lambda i, tiles=tiles: (i // tiles, 0, i % tiles)))
            else:
                assert tm % rows_per_seq == 0
                out_specs.append(pl.BlockSpec((tm // rows_per_seq, wd, rows_per_seq), lambda i: (i, 0, 0)))
            out_shape.append(jax.ShapeDtypeStruct((m // rows_per_seq, wd, rows_per_seq), F32))
        else:
            out_specs.append(pl.BlockSpec((tm, wd), lambda i: (i, 0)))
            out_shape.append(jax.ShapeDtypeStruct((m, wd), F32))
    for idx in rows_too:
        out_specs.append(pl.BlockSpec((tm, widths[idx]), lambda i: (i, 0)))
        out_shape.append(jax.ShapeDtypeStruct((m, widths[idx]), F32))
    assert w.shape[1] == k
    return pl.pallas_call(
        functools.partial(_linear_body, segs=tuple(segs), feat_major=tuple(feat_major), rows_too=tuple(rows_too)),
        grid=(m // tm,),
        in_specs=[pl.BlockSpec((tm, k), lambda i: (i, 0)), _resident((1, k)), _weight(w, layer)],
        out_specs=out_specs,
        out_shape=out_shape,
        compiler_params=_cparams("parallel"),
    )(x, gain.reshape(1, -1), w)


def _ffn_body(*refs, final):
    if final:
        x_ref, g_ref, wg_ref, wu_ref, wd_ref, gf_ref, o_ref = refs
    else:
        x_ref, g_ref, wg_ref, wu_ref, wd_ref, o_ref = refs
    x = x_ref[...]
    hb = _rms(x, g_ref[...]).astype(BF16)
    acc = x
    for c in range(0, D_FF, FF_CHUNK):
        gate = _dot(hb, wg_ref[:, c:c + FF_CHUNK])
        up = _dot(hb, wu_ref[:, c:c + FF_CHUNK])
        act = (gate * _sigmoid(gate) * up).astype(BF16)
        acc = acc + _dot(act, wd_ref[c:c + FF_CHUNK, :])
    if final:
        acc = _rms(acc, gf_ref[...])
    o_ref[...] = acc


def _ffn(x, gain, wg, wu, wd, layer, final_gain=None, tm=256):
    m = x.shape[0]
    final = final_gain is not None
    in_specs = [pl.BlockSpec((tm, D_MODEL), lambda i: (i, 0)), _resident((1, D_MODEL)),
                _weight(wg, layer), _weight(wu, layer), _weight(wd, layer)]
    args = [x, gain.reshape(1, -1), wg, wu, wd]
    if final:
        in_specs.append(_resident((1, D_MODEL)))
        args.append(final_gain.reshape(1, -1))
    return pl.pallas_call(
        functools.partial(_ffn_body, final=final),
        grid=(m // tm,),
        in_specs=in_specs,
        out_specs=pl.BlockSpec((tm, D_MODEL), lambda i: (i, 0)),
        out_shape=jax.ShapeDtypeStruct((m, D_MODEL), F32),
        compiler_params=_cparams("parallel"),
    )(*args)


def _cross_body(x_ref, oa_ref, ob_ref, oc_ref, od_ref, wout_ref, g_ref, wq_ref, mk_ref, mv_ref, wo_ref, o_ref):
    ns, tq, _ = x_ref.shape
    rows = ns * tq
    x = x_ref[...].reshape(rows, D_MODEL)
    for gi, m_ref in enumerate((oa_ref, ob_ref, oc_ref, od_ref)):
        x = x + _dot(m_ref[...].reshape(rows, GROUP_W).astype(BF16), wout_ref[gi * GROUP_W:(gi + 1) * GROUP_W, :])
    hb = _rms(x, g_ref[...]).astype(BF16)
    q = (_dot(hb, wq_ref[...]) * MEM_HEAD_DIM ** -0.5).astype(BF16)
    head = lambda h: slice(h * MEM_HEAD_DIM, (h + 1) * MEM_HEAD_DIM)
    pairs = [(sq, h) for sq in range(ns) for h in range(MEM_HEADS)]
    if len(mk_ref.shape) == 4:
        mem = lambda ref, sq, h: ref[sq, :, h, :].astype(BF16)
    else:
        mem = lambda ref, sq, h: ref[sq, :, head(h)].astype(BF16)
    scores = [_dot_nt(q[sq * tq:(sq + 1) * tq, head(h)], mem(mk_ref, sq, h)) for sq, h in pairs]
    probs = []
    for s in scores:
        p = jnp.exp(s - jnp.max(s, axis=-1, keepdims=True))
        probs.append((p / jnp.sum(p, axis=-1, keepdims=True)).astype(BF16))
    outs = [_dot(p, mem(mv_ref, sq, h)).astype(BF16) for p, (sq, h) in zip(probs, pairs)]
    o = jnp.concatenate([jnp.concatenate(outs[sq * MEM_HEADS:(sq + 1) * MEM_HEADS], axis=1) for sq in range(ns)], axis=0)
    o_ref[...] = (x + _dot(o, wo_ref[...])).reshape(ns, tq, D_MODEL)


def _cross(x, mix, w_out, gain, wq, mk, mv, layer, wo, w_layer):
    b, t, _ = x.shape
    nm = mk.shape[2]
    tq = min(ROW_TILE, t)
    ns = CROSS_SEQS if CROSS_SEQS * t <= ROW_TILE else 1
    assert t % tq == 0 and b % ns == 0
    row_spec = lambda w: pl.BlockSpec((ns, tq, w), lambda i, j: (i, j, 0))
    mem_spec = pl.BlockSpec((None, ns) + mk.shape[2:], lambda i, j: (layer, i) + (0,) * (mk.ndim - 2))
    return pl.pallas_call(
        _cross_body,
        grid=(b // ns, t // tq),
        in_specs=[row_spec(D_MODEL)] + [row_spec(GROUP_W)] * 4 + [_weight(w_out, w_layer), _resident((1, D_MODEL)),
                  _weight(wq, w_layer), mem_spec, mem_spec, _weight(wo, w_layer)],
        out_specs=row_spec(D_MODEL),
        out_shape=jax.ShapeDtypeStruct((b, t, D_MODEL), F32),
        compiler_params=_cparams("parallel", "parallel"),
    )(x, *mix, w_out, gain.reshape(1, -1), wq, mk, mv, wo)


def _diff_body(lam_ref, q_ref, kn_ref, vn_ref, kp_ref, vp_ref, bn_ref, o_ref, q8_sc, m_sc, l_sc, acc_sc, *,
               tq, tk, past, lam_init, ns):
    ng = 2 * N_HEADS
    lane = _iota((tq, GROUP_W), 1)
    for sq in range(ns):
        q = q_ref[sq] * (DIFF_HALF ** -0.5 * LOG2E)
        for g in range(ng):
            lo = (g // 2) * HEAD_DIM + (g % 2) * DIFF_HALF
            q8_sc[sq, g * tq:(g + 1) * tq, :] = jnp.where((lane >= lo) & (lane < lo + DIFF_HALF), q, 0.0).astype(BF16)
    m_sc[...] = jnp.full(m_sc.shape, -jnp.inf, F32)
    l_sc[...] = jnp.zeros(l_sc.shape, F32)
    acc_sc[...] = jnp.zeros(acc_sc.shape, F32)
    pq = past + _iota((tq, 1), 0)
    pq_f = pq.astype(F32)
    lanes = _lanes
    pairs = [(sq, g) for sq in range(ns) for g in range(ng)]

    def block(k_ref, v_ref, k0, n, near, pos0=0):
        pk = pos0 + k0 + _iota((1, n), 1)
        pk_f = pk.astype(F32)
        if near:
            allowed = (pk >> CHUNK_SHIFT) <= (pq >> CHUNK_SHIFT)
            shift = pq_f - jnp.abs(pq_f - pk_f)
        s_all = [_dot(q8_sc[sq], k_ref[sq, :, pl.ds(k0, n)].astype(BF16)) for sq in range(ns)]
        scores = {}
        for sq, g in pairs:
            slope = 2.0 ** (-2 * (g // 2 + 1)) * LOG2E
            s = s_all[sq][g * tq:(g + 1) * tq]
            scores[sq, g] = jnp.where(allowed, s + slope * shift, NEG) if near else s + slope * pk_f
        m_prevs = {k: m_sc[k[0], k[1]] for k in pairs}
        m_news = {k: jnp.maximum(m_prevs[k], jnp.max(scores[k], axis=-1, keepdims=True)) for k in pairs}
        probs = {k: jnp.exp2(scores[k] - lanes(m_news[k], n)) for k in pairs}
        alphas = {k: jnp.exp2(m_prevs[k] - m_news[k]) for k in pairs}
        for k in pairs:
            l_sc[k[0], k[1]] = alphas[k] * l_sc[k[0], k[1]] + jnp.sum(probs[k], axis=-1, keepdims=True)
            m_sc[k[0], k[1]] = m_news[k]
        pvs = [_dot_nt(jnp.concatenate([probs[sq, g].astype(BF16) for g in range(ng)], axis=0),
                       v_ref[sq, :, pl.ds(k0, n)].astype(BF16)) for sq in range(ns)]
        for sq, g in pairs:
            acc_sc[sq, g] = acc_sc[sq, g] * lanes(alphas[sq, g], GROUP_W) + pvs[sq][g * tq:(g + 1) * tq]

    def past_step(j, carry):
        block(kp_ref, vp_ref, pl.multiple_of(j * tk, tk), tk, False)
        return carry
    lax.fori_loop(0, past // tk, past_step, 0)
    block(kn_ref, vn_ref, 0, tq, True, pos0=past)

    lam = lam_ref[0]
    lane_head = _iota((1, GROUP_W), 1) >> HEAD_SHIFT
    ones16 = _head_block_ones(GROUP_W).astype(BF16)
    for sq in range(ns):
        o = jnp.zeros((tq, GROUP_W), F32)
        for h in range(N_HEADS):
            o0 = acc_sc[sq, 2 * h] / lanes(l_sc[sq, 2 * h], GROUP_W)
            o1 = acc_sc[sq, 2 * h + 1] / lanes(l_sc[sq, 2 * h + 1], GROUP_W)
            o = jnp.where(lane_head == h, o0 - lam * o1, o)
        ms = _mm_r(o * o, ones16) * (1.0 / HEAD_DIM)
        o_ref[sq] = o * lax.rsqrt(ms + EPS) * bn_ref[...] * (1.0 - lam_init)


def _diff_attn(lam, q, kn, vn, kp, vp, layer, bnorm, lam_init):
    b, t, _ = q.shape
    past = kp.shape[3]
    ng = 2 * N_HEADS
    ns = DECODE_SEQS
    tq, tk = t, math.gcd(past, 512)
    assert t == CHUNK and past % CHUNK == 0 and tk % LANES == 0 and b % ns == 0
    seq_spec = pl.BlockSpec((ns, GROUP_W, t), lambda i: (i, 0, 0))
    past_spec = pl.BlockSpec((None, ns, GROUP_W, past), lambda i: (layer, i, 0, 0))
    return pl.pallas_call(
        functools.partial(_diff_body, tq=tq, tk=tk, past=past, lam_init=lam_init, ns=ns),
        grid=(b // ns,),
        in_specs=[pl.BlockSpec(memory_space=pltpu.SMEM), pl.BlockSpec((ns, tq, GROUP_W), lambda i: (i, 0, 0)),
                  seq_spec, seq_spec, past_spec, past_spec, _resident((1, GROUP_W))],
        out_specs=pl.BlockSpec((ns, tq, GROUP_W), lambda i: (i, 0, 0)),
        out_shape=jax.ShapeDtypeStruct((b, t, GROUP_W), F32),
        scratch_shapes=[pltpu.VMEM((ns, ng * tq, GROUP_W), BF16), pltpu.VMEM((ns, ng, tq, LANES), F32),
                        pltpu.VMEM((ns, ng, tq, LANES), F32), pltpu.VMEM((ns, ng, tq, GROUP_W), F32)],
        compiler_params=_cparams("parallel"),
    )(lam, q, kn, vn, kp, vp, bnorm)


def _diff_prompt_body(lam_ref, q_ref, k_ref, v_ref, bn_ref, o_ref, qt_sc, m_sc, acc_sc, *, tq, tk, lam_init):
    i = pl.program_id(1)
    ng = 2 * N_HEADS
    qt = (q_ref[0] * (DIFF_HALF ** -0.5 * LOG2E)).T
    feat = _iota((GROUP_W, tq), 0)
    for g in range(ng):
        lo = (g // 2) * HEAD_DIM + (g % 2) * DIFF_HALF
        qt_sc[g] = jnp.where((feat >= lo) & (feat < lo + DIFF_HALF), qt, 0.0).astype(BF16)
    m_sc[...] = jnp.full(m_sc.shape, -jnp.inf, F32)
    acc_sc[...] = jnp.zeros(acc_sc.shape, F32)
    pq = i * tq + _iota((1, tq), 1)
    pq_f = pq.astype(F32)
    ones_rows = jnp.ones((SUM_ROWS, tk), BF16)

    def block(k0, n, near):
        kb = k_ref[0, pl.ds(k0, n), :].astype(BF16)
        vb = v_ref[0, :, pl.ds(k0, n)].astype(BF16)
        pk = k0 + _iota((n, LANES), 0)
        pk_f = pk.astype(F32)
        if near:
            allowed = (_lanes(pk, tq) >> CHUNK_SHIFT) <= (pq >> CHUNK_SHIFT)
            shift = pq_f - jnp.abs(pq_f - _lanes(pk_f, tq))
        groups = range(ng)
        scores = []
        for g in groups:
            slope = 2.0 ** (-2 * (g // 2 + 1)) * LOG2E
            s = _dot(kb, qt_sc[g])
            scores.append(jnp.where(allowed, s + slope * shift, NEG) if near else s + _lanes(slope * pk_f, tq))
        m_prevs = [m_sc[g] for g in groups]
        m_news = [jnp.maximum(m_prevs[g], jnp.max(scores[g], axis=0, keepdims=True)) for g in groups]
        probs = [jnp.exp2(scores[g] - m_news[g]) for g in groups]
        for g in groups:
            alpha = jnp.exp2(m_prevs[g] - m_news[g])
            m_sc[g] = m_news[g]
            h = g // 2
            pv = _dot(jnp.concatenate([vb[h * HEAD_DIM:(h + 1) * HEAD_DIM, :], ones_rows[:, :n]], axis=0),
                      probs[g].astype(BF16))
            acc_sc[g] = acc_sc[g] * alpha + pv

    def prev_step(j, carry):
        block(pl.multiple_of(j * tk, tk), tk, False)
        return carry
    lax.fori_loop(0, (i * tq) // tk, prev_step, 0)

    @pl.when((i * tq) % tk != 0)
    def _():
        block(pl.multiple_of((i - 1) * tq, tq), tq, False)
    block(pl.multiple_of(i * tq, tq), tq, True)

    lam = lam_ref[0]
    norm = lambda g: acc_sc[g, :HEAD_DIM, :] / acc_sc[g, HEAD_DIM:HEAD_DIM + 1, :]
    heads = [norm(2 * h) - lam * norm(2 * h + 1) for h in range(N_HEADS)]
    o = jnp.concatenate(heads, axis=0).T
    ms = _mm_r(o * o, _head_block_ones(GROUP_W).astype(BF16)) * (1.0 / HEAD_DIM)
    o_ref[0] = o * lax.rsqrt(ms + EPS) * bn_ref[...] * (1.0 - lam_init)


def _diff_attn_prompt(lam, q, k_rows, v_feat, bnorm, lam_init):
    b, t, _ = q.shape
    ng = 2 * N_HEADS
    tq = min(256, t)
    tk = 2 * tq if t % (2 * tq) == 0 else tq
    assert t % tq == 0 and tq % LANES == 0
    return pl.pallas_call(
        functools.partial(_diff_prompt_body, tq=tq, tk=tk, lam_init=lam_init),
        grid=(b, t // tq),
        in_specs=[pl.BlockSpec(memory_space=pltpu.SMEM), pl.BlockSpec((1, tq, GROUP_W), lambda i, j: (i, j, 0)),
                  pl.BlockSpec((1, t, GROUP_W), lambda i, j: (i, 0, 0)),
                  pl.BlockSpec((1, GROUP_W, t), lambda i, j: (i, 0, 0)), _resident((1, GROUP_W))],
        out_specs=pl.BlockSpec((1, tq, GROUP_W), lambda i, j: (i, j, 0)),
        out_shape=jax.ShapeDtypeStruct((b, t, GROUP_W), F32),
        scratch_shapes=[pltpu.VMEM((ng, GROUP_W, tq), BF16), pltpu.VMEM((ng, 1, tq), F32),
                        pltpu.VMEM((ng, HEAD_DIM + SUM_ROWS, tq), F32)],
        compiler_params=_cparams("parallel", "parallel"),
    )(lam, q, k_rows, v_feat, bnorm)


def _stick_body(q_ref, kn_ref, vn_ref, kp_ref, vp_ref, o_ref, q4_sc, c_sc, acc_sc, *, tq, tk, past, ns):
    nr = N_HEADS * tq
    lane = _iota((tq, GROUP_W), 1) >> HEAD_SHIFT
    for sq in range(ns):
        q = q_ref[sq] * (HEAD_DIM ** -0.5 * LOG2E)
        for h in range(N_HEADS):
            q4_sc[sq, h * tq:(h + 1) * tq, :] = jnp.where(lane == h, q, 0.0).astype(BF16)
    c_sc[...] = jnp.zeros(c_sc.shape, F32)
    acc_sc[...] = jnp.zeros(acc_sc.shape, F32)
    pq = past + (_iota((nr, 1), 0) & (tq - 1))
    laters = {w: (_iota((w, w), 0) > _iota((w, w), 1)).astype(BF16)
              for w in {min(tq, STICK_SUB), min(tk, STICK_SUB)}}
    seqs = range(ns)

    def block(k_ref, v_ref, k0, n, masked, pos0=0):
        sub = min(n, STICK_SUB)
        later = laters[sub]
        zs = [_dot(q4_sc[sq], k_ref[sq, :, pl.ds(k0, n)].astype(BF16)) for sq in seqs]
        lss = [jnp.minimum(z, 0.0) - jnp.log2(1.0 + jnp.exp2(-jnp.abs(z))) for z in zs]
        lks = [ls - z for ls, z in zip(lss, zs)]
        if masked:
            mask = (pos0 + k0 + _iota((1, n), 1)) < pq
            lks = [jnp.where(mask, lk, 0.0) for lk in lks]
        his = [lk.astype(BF16) for lk in lks]
        los = [(lk - hi.astype(F32)).astype(BF16) for lk, hi in zip(lks, his)]
        carries = [c_sc[sq] for sq in seqs]
        parts = [[] for _ in seqs]
        for sb in reversed(range(n // sub)):
            sl = slice(sb * sub, (sb + 1) * sub)
            for sq in seqs:
                after = _dot(his[sq][:, sl], later) + _dot(los[sq][:, sl], later)
                parts[sq].append(lss[sq][:, sl] + after + _lanes(carries[sq], sub))
                carries[sq] = carries[sq] + (after[:, 0:1] + lks[sq][:, sb * sub:sb * sub + 1])
        for sq in seqs:
            c_sc[sq] = carries[sq]
            e = parts[sq][0] if len(parts[sq]) == 1 else jnp.concatenate(parts[sq][::-1], axis=1)
            if masked:
                a = jnp.where(mask, jnp.exp2(jnp.where(mask, e, 0.0)), 0.0)
            else:
                a = jnp.exp2(e)
            acc_sc[sq] = acc_sc[sq] + _dot_nt(a.astype(BF16), v_ref[sq, :, pl.ds(k0, n)].astype(BF16))

    block(kn_ref, vn_ref, 0, tq, True, pos0=past)
    nb = past // tk

    def past_step(j, carry):
        block(kp_ref, vp_ref, pl.multiple_of((nb - 1 - j) * tk, tk), tk, False)
        return carry
    lax.fori_loop(0, nb, past_step, 0)

    for sq in seqs:
        o = acc_sc[sq, 0:tq, :]
        for h in range(1, N_HEADS):
            o = jnp.where(lane == h, acc_sc[sq, h * tq:(h + 1) * tq, :], o)
        o_ref[sq] = o


def _stick_attn(q, kn, vn, kp, vp, layer):
    b, t, _ = q.shape
    past = kp.shape[3]
    ns = DECODE_SEQS
    tq, tk = t, math.gcd(past, 1024)
    assert tq & (tq - 1) == 0 and tk % min(tk, STICK_SUB) == 0 and b % ns == 0
    seq_spec = pl.BlockSpec((ns, GROUP_W, t), lambda i: (i, 0, 0))
    past_spec = pl.BlockSpec((None, ns, GROUP_W, past), lambda i: (layer, i, 0, 0))
    return pl.pallas_call(
        functools.partial(_stick_body, tq=tq, tk=tk, past=past, ns=ns),
        grid=(b // ns,),
        in_specs=[pl.BlockSpec((ns, tq, GROUP_W), lambda i: (i, 0, 0)), seq_spec, seq_spec, past_spec, past_spec],
        out_specs=pl.BlockSpec((ns, tq, GROUP_W), lambda i: (i, 0, 0)),
        out_shape=jax.ShapeDtypeStruct((b, t, GROUP_W), F32),
        scratch_shapes=[pltpu.VMEM((ns, N_HEADS * tq, GROUP_W), BF16), pltpu.VMEM((ns, N_HEADS * tq, LANES), F32),
                        pltpu.VMEM((ns, N_HEADS * tq, GROUP_W), F32)],
        compiler_params=_cparams("parallel"),
    )(q, kn, vn, kp, vp)


def _stick_prompt_body(q_ref, k_ref, v_ref, o_ref, qt_sc, c_sc, acc_sc, *, tq, tk):
    i = pl.program_id(1)
    qt = (q_ref[0] * (HEAD_DIM ** -0.5 * LOG2E)).T
    feat_head = _iota((GROUP_W, tq), 0) >> HEAD_SHIFT
    for h in range(N_HEADS):
        qt_sc[h] = jnp.where(feat_head == h, qt, 0.0).astype(BF16)
    c_sc[...] = jnp.zeros(c_sc.shape, F32)
    acc_sc[...] = jnp.zeros(acc_sc.shape, F32)
    pq = i * tq + _iota((1, tq), 1)
    sub = min(tq, STICK_SUB)
    later = (_iota((sub, sub), 1) > _iota((sub, sub), 0)).astype(BF16)

    def block(k0, n, masked):
        kb = k_ref[0, pl.ds(k0, n), :].astype(BF16)
        vb = v_ref[0, :, pl.ds(k0, n)].astype(BF16)
        if masked:
            mask = _lanes(k0 + _iota((n, LANES), 0), tq) < pq
        heads = range(N_HEADS)
        zs = [_dot(kb, qt_sc[h]) for h in heads]
        lss = [jnp.minimum(z, 0.0) - jnp.log2(1.0 + jnp.exp2(-jnp.abs(z))) for z in zs]
        lks = [ls - z for ls, z in zip(lss, zs)]
        if masked:
            lks = [jnp.where(mask, lk, 0.0) for lk in lks]
        es = []
        for h in heads:
            lk = lks[h]
            hi = lk.astype(BF16)
            lo = (lk - hi.astype(F32)).astype(BF16)
            carry = c_sc[h]
            parts = []
            for sb in reversed(range(n // sub)):
                sl = slice(sb * sub, (sb + 1) * sub)
                after = _dot(later, hi[sl]) + _dot(later, lo[sl])
                parts.append(lss[h][sl] + after + carry)
                carry = carry + (after[0:1] + lk[sb * sub:sb * sub + 1])
            c_sc[h] = carry
            es.append(parts[0] if len(parts) == 1 else jnp.concatenate(parts[::-1], axis=0))
        for h in heads:
            if masked:
                a = jnp.where(mask, jnp.exp2(jnp.where(mask, es[h], 0.0)), 0.0)
            else:
                a = jnp.exp2(es[h])
            acc_sc[h] = acc_sc[h] + _dot(vb[h * HEAD_DIM:(h + 1) * HEAD_DIM, :], a.astype(BF16))

    block(pl.multiple_of(i * tq, tq), tq, True)

    @pl.when((i * tq) % tk != 0)
    def _():
        block(pl.multiple_of((i - 1) * tq, tq), tq, False)
    nfull = (i * tq) // tk

    def prev_step(j, carry):
        block(pl.multiple_of((nfull - 1 - j) * tk, tk), tk, False)
        return carry
    lax.fori_loop(0, nfull, prev_step, 0)

    o_ref[0] = jnp.concatenate([acc_sc[h] for h in range(N_HEADS)], axis=0).T


def _stick_attn_prompt(q, k_rows, v_feat):
    b, t, _ = q.shape
    tq = min(256, t)
    tk = 2 * tq if t % (2 * tq) == 0 else tq
    assert t % tq == 0 and tq % LANES == 0 and tk % min(tk, STICK_SUB) == 0
    return pl.pallas_call(
        functools.partial(_stick_prompt_body, tq=tq, tk=tk),
        grid=(b, t // tq),
        in_specs=[pl.BlockSpec((1, tq, GROUP_W), lambda i, j: (i, j, 0)),
                  pl.BlockSpec((1, t, GROUP_W), lambda i, j: (i, 0, 0)),
                  pl.BlockSpec((1, GROUP_W, t), lambda i, j: (i, 0, 0))],
        out_specs=pl.BlockSpec((1, tq, GROUP_W), lambda i, j: (i, j, 0)),
        out_shape=jax.ShapeDtypeStruct((b, t, GROUP_W), F32),
        scratch_shapes=[pltpu.VMEM((N_HEADS, GROUP_W, tq), BF16), pltpu.VMEM((N_HEADS, 1, tq), F32),
                        pltpu.VMEM((N_HEADS, HEAD_DIM, tq), F32)],
        compiler_params=_cparams("parallel", "parallel"),
    )(q, k_rows, v_feat)


def _pair_mask(n, s, reps=1):
    r, c = _iota((reps * n, n), 0) & (n - 1), _iota((reps * n, n), 1)
    return ((r // (2 * s)) == (c // (2 * s))) & (((r // s) & 1) == 1) & (((c // s) & 1) == 0)


def _state_spec(nb):
    return pl.BlockSpec((nb, N_HEADS, HEAD_DIM, HEAD_DIM), lambda i, j: (i, 0, 0, 0))


def _load_state(s_sc, s0_ref, transpose):
    s_sc[...] = jnp.zeros(s_sc.shape, F32)
    for b in range(s_sc.shape[0]):
        for h in range(N_HEADS):
            blk = s0_ref[b, h]
            s_sc[b, h * HEAD_DIM:(h + 1) * HEAD_DIM, h * HEAD_DIM:(h + 1) * HEAD_DIM] = blk.T if transpose else blk


def _store_state(sout_ref, s_sc, transpose):
    for b in range(s_sc.shape[0]):
        for h in range(N_HEADS):
            blk = s_sc[b, h * HEAD_DIM:(h + 1) * HEAD_DIM, h * HEAD_DIM:(h + 1) * HEAD_DIM]
            sout_ref[b, h] = blk.T if transpose else blk


def _head_rows(x, lane_head):
    return jnp.concatenate([jnp.where(lane_head == h, x, 0.0) for h in range(N_HEADS)], axis=0)


def _head_diag(x, lane_head):
    n = x.shape[0] // N_HEADS
    out = x[:n]
    for h in range(1, N_HEADS):
        out = jnp.where(lane_head == h, x[h * n:(h + 1) * n], out)
    return out


def _gdn_body(x_ref, z_ref, abg_ref, cw_ref, cs_ref, s0_ref, alog_ref, dtb_ref, an_ref,
              o_ref, sout_ref, s_sc, xb_sc, *, nc, nb):
    c = pl.program_id(1)
    L = CHUNK
    pad = 8

    @pl.when(c == 0)
    def _():
        _load_state(s_sc, s0_ref, transpose=False)
        xb_sc[:, pad - 3:pad, :] = cs_ref[...]

    ones_bd = _head_block_ones(GROUP_W)
    ones_bd16 = ones_bd.astype(BF16)
    er, ec = _iota((ABG_W, GROUP_W), 0), _iota((ABG_W, GROUP_W), 1) >> HEAD_SHIFT
    e_beta = (er == ec).astype(BF16)
    e_g = (er == ec + N_HEADS).astype(BF16)
    ri, ci = _iota((L, L), 0), _iota((L, L), 1)
    tri16 = (ri >= ci).astype(BF16)
    incl = ri >= ci
    strict = ri > ci
    eye = (ri == ci).astype(F32)
    pair_masks = [_pair_mask(L, sz) for sz in (1, 2, 4, 8, 16, 32)]
    lane_head = _iota((1, GROUP_W), 1) >> HEAD_SHIFT
    cw = cw_ref[...]

    seqs = range(nb)
    us = []
    for b in seqs:
        x = x_ref[b]
        xb_sc[b, pad:pad + L, :] = x
        u = (x * cw[3:4] + xb_sc[b, pad - 1:pad - 1 + L, :] * cw[2:3]
             + xb_sc[b, pad - 2:pad - 2 + L, :] * cw[1:2] + xb_sc[b, pad - 3:pad - 3 + L, :] * cw[0:1])
        xb_sc[b, pad - 3:pad, :] = x[L - 3:L, :]
        us.append(u * _sigmoid(u))
    qs = [u[:, :GROUP_W] for u in us]
    ks = [u[:, GROUP_W:2 * GROUP_W] for u in us]
    vs = [u[:, 2 * GROUP_W:] for u in us]
    qs = [q * lax.rsqrt(_mm_r(q * q, ones_bd16) + EPS) * HEAD_DIM ** -0.5 for q in qs]
    ks = [k * lax.rsqrt(_mm_r(k * k, ones_bd16) + EPS) for k in ks]

    abgs = [abg_ref[b] for b in seqs]
    betas = [_mm_r(_sigmoid(abg), e_beta) for abg in abgs]
    gc_ns = [_mm_l(tri16, -jnp.exp(alog_ref[...]) * _softplus(abg + dtb_ref[...])) for abg in abgs]
    gc_ts = [gc_n.T for gc_n in gc_ns]
    gcs = [_mm_r(gc_n, e_g) for gc_n in gc_ns]

    kbs = [k * beta for k, beta in zip(ks, betas)]
    k16s = [k.astype(BF16) for k in ks]
    a4s = [_bdot(_head_rows(kb, lane_head).astype(BF16), k16, _NT) for kb, k16 in zip(kbs, k16s)]
    qk4s = [_bdot(_head_rows(q, lane_head).astype(BF16), k16, _NT) for q, k16 in zip(qs, k16s)]
    decs = []
    for b in seqs:
        dec = []
        for h in range(N_HEADS):
            diff = gc_ns[b][:, N_HEADS + h:N_HEADS + h + 1] - gc_ts[b][N_HEADS + h:N_HEADS + h + 1, :]
            dec.append(jnp.where(incl, jnp.exp(jnp.where(incl, diff, 0.0)), 0.0))
        decs.append(jnp.stack(dec))

    a_all = jnp.concatenate([jnp.where(strict, a4s[b].reshape(N_HEADS, L, L) * decs[b], 0.0) for b in seqs], axis=0)
    t_inv = eye - jnp.where(pair_masks[0], a_all, 0.0)
    a16 = a_all.astype(BF16)
    for pm in pair_masks[1:]:
        t16 = t_inv.astype(BF16)
        t_inv = t_inv - _bdot(t16, _bdot(jnp.where(pm, a16, 0.0), t16, _BNN).astype(BF16), _BNN)

    exp_gs = [jnp.exp(gc) for gc in gcs]
    s_olds = [s_sc[b] for b in seqs]
    s16s = [s.astype(BF16) for s in s_olds]
    rhss = [vs[b] * betas[b] - _bdot((kbs[b] * exp_gs[b]).astype(BF16), s16s[b], _NN) for b in seqs]
    ws = [_head_diag(_mm1(t_inv[b * N_HEADS:(b + 1) * N_HEADS].reshape(N_HEADS * L, L), rhss[b]), lane_head)
          for b in seqs]
    outs = [_bdot((qs[b] * exp_gs[b]).astype(BF16), s16s[b], _NN)
            + _head_diag(_mm1((qk4s[b].reshape(N_HEADS, L, L) * decs[b]).reshape(N_HEADS * L, L), ws[b]), lane_head)
            for b in seqs]
    for b in seqs:
        g_last = gcs[b][L - 1:L, :]
        k_dec = ks[b] * jnp.exp(g_last - gcs[b])
        s_sc[b] = s_olds[b] * jnp.exp(g_last) + _mm1(k_dec.T, ws[b]) * ones_bd
    for b in seqs:
        o = outs[b]
        ms = _mm_r(o * o, ones_bd16) * (1.0 / HEAD_DIM)
        zg = z_ref[b]
        o_ref[b] = o * lax.rsqrt(ms + EPS) * an_ref[...] * (zg * _sigmoid(zg))

    @pl.when(c == nc - 1)
    def _():
        _store_state(sout_ref, s_sc, transpose=False)


def _gdn(aqkv, az, abg, conv_w, conv_state, s0, alog_row, dtb_row, anorm_row):
    b, t, _ = aqkv.shape
    nc = t // CHUNK
    nb = SEQ_PER_STEP
    assert b % nb == 0
    cmap = lambda i, j: (i, j, 0)
    bmap = lambda i, j: (i, 0, 0)
    return pl.pallas_call(
        functools.partial(_gdn_body, nc=nc, nb=nb),
        grid=(b // nb, nc),
        in_specs=[pl.BlockSpec((nb, CHUNK, 3 * GROUP_W), cmap), pl.BlockSpec((nb, CHUNK, GROUP_W), cmap),
                  pl.BlockSpec((nb, CHUNK, ABG_W), cmap), _resident((CONV_W, 3 * GROUP_W)),
                  pl.BlockSpec((nb, CONV_W - 1, 3 * GROUP_W), bmap), _state_spec(nb),
                  _resident((1, ABG_W)), _resident((1, ABG_W)), _resident((1, GROUP_W))],
        out_specs=[pl.BlockSpec((nb, CHUNK, GROUP_W), cmap), _state_spec(nb)],
        out_shape=[jax.ShapeDtypeStruct((b, t, GROUP_W), F32), jax.ShapeDtypeStruct(s0.shape, F32)],
        scratch_shapes=[pltpu.VMEM((nb, GROUP_W, GROUP_W), F32), pltpu.VMEM((nb, 8 + CHUNK, 3 * GROUP_W), F32)],
        compiler_params=_cparams("parallel", "arbitrary"),
    )(aqkv, az, abg, conv_w, conv_state, s0, alog_row, dtb_row, anorm_row)


def _hgrn_body(q_ref, f_ref, i_ref, g_ref, lb_ref, dn_ref, s0_ref, o_ref, sout_ref, s_sc, *, nc, nb):
    c = pl.program_id(1)
    L = CHUNK

    @pl.when(c == 0)
    def _():
        _load_state(s_sc, s0_ref, transpose=True)

    lb = lb_ref[...]
    ones_bd = _head_block_ones(GROUP_W)
    ones_bd16 = ones_bd.astype(BF16)
    ri, ci = _iota((L, L), 0), _iota((L, L), 1)
    tri16 = (ri >= ci).astype(BF16)
    sizes = (1, 2, 4, 8, 16, 32)
    sel16 = jnp.concatenate([(ci == (ri // (2 * sz)) * (2 * sz) + sz).astype(BF16) for sz in sizes], axis=0)
    eye4 = (_iota((N_HEADS * L, L), 0) & (L - 1)) == _iota((N_HEADS * L, L), 1)
    pair_masks4 = [_pair_mask(L, sz, reps=N_HEADS) for sz in sizes]
    lane_head = _iota((1, GROUP_W), 1) >> HEAD_SHIFT
    rows = _iota((L, 1), 0)

    pre = []
    for b in range(nb):
        fl = f_ref[b]
        log_f = _log_sigmoid(fl) + jnp.log(1.0 + lb * jnp.exp(-fl))
        k = (1.0 - lb) * _sigmoid(-fl)
        q = q_ref[b] * HEAD_DIM ** -0.5
        bc = _mm_l(tri16, log_f)
        pre.append((q, k, bc))
    b_refs = [_mm_l(sel16, bc) for _, _, bc in pre]
    a_all = [jnp.where(eye4, _mm1(_head_rows(q, lane_head), k, _NT), 0.0) for q, k, _ in pre]
    for lv, sz in enumerate(sizes):
        later = ((rows // sz) & 1) == 1
        for b in range(nb):
            q, k, bc = pre[b]
            b_ref = b_refs[b][lv * L:(lv + 1) * L]
            x = jnp.where(later, q * jnp.exp(jnp.where(later, bc - b_ref, 0.0)), 0.0)
            y = jnp.where(later, 0.0, k * jnp.exp(jnp.where(later, 0.0, b_ref - bc)))
            a_all[b] = a_all[b] + jnp.where(pair_masks4[lv], _mm1(_head_rows(x, lane_head), y, _NT), 0.0)

    for b in range(nb):
        q, k, bc = pre[b]
        v = i_ref[b]
        st = s_sc[b]
        o = _mm1(q * jnp.exp(bc), st, _NT) + _head_diag(_mm1(a_all[b], v), lane_head)
        b_last = bc[L - 1:L, :]
        st_new = st * jnp.exp(b_last) + _mm1(v.T, k * jnp.exp(b_last - bc)) * ones_bd
        s_sc[b] = st_new

        ms = _mm_r(o * o, ones_bd16) * (1.0 / HEAD_DIM)
        zg = g_ref[b]
        o_ref[b] = o * lax.rsqrt(ms + EPS) * dn_ref[...] * (zg * _sigmoid(zg))

    @pl.when(c == nc - 1)
    def _():
        _store_state(sout_ref, s_sc, transpose=True)


def _hgrn(dq, df, di, dg, lb_row, dnorm_row, s0):
    b, t, _ = dq.shape
    nc = t // CHUNK
    nb = SEQ_PER_STEP
    assert b % nb == 0
    cmap = lambda i, j: (i, j, 0)
    bmap = lambda i, j: (i, 0, 0)
    cspec = pl.BlockSpec((nb, CHUNK, GROUP_W), cmap)
    return pl.pallas_call(
        functools.partial(_hgrn_body, nc=nc, nb=nb),
        grid=(b // nb, nc),
        in_specs=[cspec, cspec, cspec, cspec, _resident((1, GROUP_W)), _resident((1, GROUP_W)),
                  _state_spec(nb)],
        out_specs=[cspec, _state_spec(nb)],
        out_shape=[jax.ShapeDtypeStruct((b, t, GROUP_W), F32), jax.ShapeDtypeStruct(s0.shape, F32)],
        scratch_shapes=[pltpu.VMEM((nb, GROUP_W, GROUP_W), F32)],
        compiler_params=_cparams("parallel", "arbitrary"),
    )(dq, df, di, dg, lb_row, dnorm_row, s0)


IN_WIDTHS = (3 * GROUP_W,) + (GROUP_W,) * 11 + (ABG_W,)
KV_SEGMENTS = (3, 4, 6, 7)
K_SEGMENTS = (3, 6)


def _layer(x, l, st, mk, mv, mem_layer, prm, final_gain):
    b, t, _ = x.shape
    m = b * t
    tm = math.gcd(m, ROW_TILE)
    has_past = st["b_k"] is not None
    outs = _fused_linear(x.reshape(m, D_MODEL), prm["w_in"], l, IN_WIDTHS, prm["norm_mix"][l], tm,
                         feat_major=KV_SEGMENTS, rows_per_seq=t, rows_too=() if has_past else K_SEGMENTS)
    outs = [o if idx in KV_SEGMENTS else o.reshape(b, t, -1) for idx, o in enumerate(outs)]
    (a_qkv, a_z, b_q, b_k, b_v, c_q, c_k, c_v, d_q, d_f, d_i, d_g, a_bg) = outs[:len(IN_WIDTHS)]

    o_a, a_s = _gdn(a_qkv, a_z, a_bg, prm["a_conv_w"][l], st["a_conv"], st["a_S"].astype(F32),
                    prm["alog_row"][l], prm["dtb_row"][l], prm["a_norm"][l])
    lam_init = 0.8 - 0.6 * math.exp(-0.3 * l)
    if has_past:
        o_b = _diff_attn(prm["lam"][l], b_q, b_k, b_v, st["b_k"], st["b_v"], l, prm["b_norm"][l], lam_init)
        o_c = _stick_attn(c_q, c_k, c_v, st["c_k"], st["c_v"], l)
    else:
        b_k_rows, c_k_rows = outs[len(IN_WIDTHS):]
        o_b = _diff_attn_prompt(prm["lam"][l], b_q, b_k_rows, b_v, prm["b_norm"][l], lam_init)
        o_c = _stick_attn_prompt(c_q, c_k_rows, c_v)
    o_d, d_s = _hgrn(d_q, d_f, d_i, d_g, prm["lb"][l], prm["d_norm"][l], st["d_S"].astype(F32))

    x2 = _cross(x, (o_a, o_b, o_c, o_d), prm["w_out"], prm["norm_cross"][l], prm["w_cq"], mk, mv, mem_layer,
                prm["w_co"], l)
    x2 = _ffn(x2.reshape(m, D_MODEL), prm["norm_ffn"][l], prm["w_gate"], prm["w_up"], prm["w_down"], l,
              final_gain=final_gain, tm=tm)

    if t >= CONV_W - 1:
        conv_new = a_qkv[:, t - (CONV_W - 1):, :]
    else:
        conv_new = jnp.concatenate([st["a_conv"], a_qkv], axis=1)[:, -(CONV_W - 1):, :]
    heads = lambda a: jnp.transpose(a.reshape(b, N_HEADS, HEAD_DIM, t), (0, 3, 1, 2))
    new = (conv_new, a_s, heads(b_k), heads(b_v), heads(c_k), heads(c_v), d_s)
    return x2.reshape(b, t, D_MODEL), new


def kernel(x_prompt, x_sample, mem_prompt, state_a_conv, state_a_S, cache_b_k, cache_b_v, cache_c_k, cache_c_v,
           state_d_S, cache_mem_k, cache_mem_v, norm_mix, w_in, a_conv_w, a_A_log, a_dt_bias, a_norm, b_lam_q1,
           b_lam_k1, b_lam_q2, b_lam_k2, b_norm, d_lb, d_norm, w_out, norm_cross, norm_memtok, w_cq, w_ck, w_cv,
           w_co, norm_ffn, w_gate, w_up, w_down, norm_final):
    depth = w_in.shape[0]
    n_main = 4 * GROUP_W
    w_in_p = jnp.concatenate(
        [w_in[:, :, :n_main], w_in[:, :, n_main + 2 * N_HEADS:], w_in[:, :, n_main:n_main + 2 * N_HEADS],
         jnp.zeros((depth, D_MODEL, ABG_W - 2 * N_HEADS), w_in.dtype)], axis=2).astype(BF16)
    pad_row = lambda a: jnp.pad(a.astype(F32), ((0, 0), (N_HEADS, ABG_W - 2 * N_HEADS)))[:, None, :]
    tile_row = lambda a: jnp.tile(a.astype(F32), (1, N_HEADS))[:, None, :]
    p_lb = jax.nn.softmax(d_lb.astype(F32), axis=0)
    lam = (jnp.exp(jnp.sum(b_lam_q1.astype(F32) * b_lam_k1.astype(F32), axis=-1))
           - jnp.exp(jnp.sum(b_lam_q2.astype(F32) * b_lam_k2.astype(F32), axis=-1))
           + jnp.array([0.8 - 0.6 * math.exp(-0.3 * l) for l in range(depth)], F32))
    prm = {
        "w_in": w_in_p, "norm_mix": norm_mix, "a_conv_w": a_conv_w.astype(F32),
        "alog_row": pad_row(a_A_log), "dtb_row": pad_row(a_dt_bias), "a_norm": tile_row(a_norm),
        "lam": lam[:, None], "b_norm": tile_row(b_norm), "lb": (jnp.cumsum(p_lb, axis=0) - p_lb[0])[:, None, :],
        "d_norm": tile_row(d_norm), "w_out": w_out.astype(BF16), "norm_cross": norm_cross,
        "w_cq": w_cq.astype(BF16), "w_co": w_co.astype(BF16), "norm_ffn": norm_ffn,
        "w_gate": w_gate.astype(BF16), "w_up": w_up.astype(BF16), "w_down": w_down.astype(BF16),
    }
    w_ckv = jnp.concatenate([w_ck, w_cv], axis=2).astype(BF16)

    bp, tp, _ = x_prompt.shape
    n_mem = mem_prompt.shape[1]
    h = x_prompt
    p_new = []
    for l in range(depth):
        mk, mv = _fused_linear(mem_prompt.reshape(bp * n_mem, D_MODEL), w_ckv, l, (D_MODEL, D_MODEL),
                               norm_memtok[l], math.gcd(bp * n_mem, ROW_TILE))
        mk = mk.reshape(1, bp, n_mem, D_MODEL)
        mv = mv.reshape(1, bp, n_mem, D_MODEL)
        st = {"a_conv": jnp.zeros((bp, CONV_W - 1, 3 * GROUP_W), F32),
              "a_S": jnp.zeros((bp, N_HEADS, HEAD_DIM, HEAD_DIM), F32), "b_k": None, "b_v": None,
              "c_k": None, "c_v": None, "d_S": jnp.zeros((bp, N_HEADS, HEAD_DIM, HEAD_DIM), F32)}
        h, new = _layer(h, l, st, mk, mv, 0, prm, norm_final if l == depth - 1 else None)
        mem4 = lambda a: a.reshape(bp, n_mem, MEM_HEADS, MEM_HEAD_DIM)
        p_new.append(new + (mem4(mk), mem4(mv)))
    y_prompt = h
    p_stacked = [jnp.stack(c) for c in zip(*p_new)]

    bs, ts, _ = x_sample.shape
    past = cache_b_k.shape[2]
    h = x_sample
    s_new = []
    flat = lambda a: jnp.transpose(a, (0, 1, 3, 4, 2)).reshape(depth, bs, GROUP_W, past)
    caches = {"b_k": flat(cache_b_k), "b_v": flat(cache_b_v), "c_k": flat(cache_c_k), "c_v": flat(cache_c_v)}
    for l in range(depth):
        st = {"a_conv": state_a_conv[l], "a_S": state_a_S[l], "d_S": state_d_S[l], **caches}
        h, new = _layer(h, l, st, cache_mem_k, cache_mem_v, l, prm, norm_final if l == depth - 1 else None)
        s_new.append(new)
    y_sample = h
    s_stacked = [jnp.stack(c) for c in zip(*s_new)]

    return (y_prompt, y_sample, *p_stacked, *s_stacked)
```

```python
import functools
import math

import jax
import jax.numpy as jnp
from jax import lax
from jax.experimental import pallas as pl
from jax.experimental.pallas import tpu as pltpu

F32 = jnp.float32
BF16 = jnp.bfloat16

D_MODEL = 1024
GROUP_W = 256
N_HEADS = 4
HEAD_DIM = 64
DIFF_HALF = 32
CHUNK = 64
CONV_W = 4
MEM_HEADS = 4
MEM_HEAD_DIM = 256
D_FF = 2816
HEAD_SHIFT = HEAD_DIM.bit_length() - 1
CHUNK_SHIFT = CHUNK.bit_length() - 1
EPS = 1e-6
NEG = -1e30
LOG2E = 1.4426950408889634
LANES = 128
ABG_W = 128
FF_CHUNK = 256
ROW_TILE = 512
SUM_ROWS = 16
DECODE_SEQS = 2
CROSS_SEQS = 4
STICK_DEAD = -160.0
STICK_SUB = 256
SEQ_PER_STEP = 4
VMEM_LIMIT = 56 * 1024 * 1024


def _cparams(*sem):
    return pltpu.CompilerParams(dimension_semantics=sem, vmem_limit_bytes=VMEM_LIMIT)


def _dot(a, b):
    return jnp.dot(a, b, preferred_element_type=F32)


def _dot_nt(a, b):
    return lax.dot_general(a, b, (((1,), (1,)), ((), ())), preferred_element_type=F32)


_NN = (((1,), (0,)), ((), ()))
_NT = (((1,), (1,)), ((), ()))
_BNN = (((2,), (1,)), ((0,), (0,)))


def _split(x):
    hi = x.astype(BF16)
    return hi, (x - hi.astype(F32)).astype(BF16)


def _bdot(a, b, dims):
    return lax.dot_general(a, b, dims, preferred_element_type=F32)


def _mm1(a, b, dims=_NN):
    return _bdot(a.astype(BF16), b.astype(BF16), dims)


def _mm_r(a, b16):
    ah, al = _split(a)
    return _bdot(ah, b16, _NN) + _bdot(al, b16, _NN)


def _mm_l(a16, b):
    bh, bl = _split(b)
    return _bdot(a16, bh, _NN) + _bdot(a16, bl, _NN)


def _rms(x, g):
    return x * lax.rsqrt(jnp.mean(x * x, axis=-1, keepdims=True) + EPS) * g


def _sigmoid(x):
    return 1.0 / (1.0 + jnp.exp(-x))


def _log_sigmoid(x):
    return jnp.minimum(x, 0.0) - jnp.log(1.0 + jnp.exp(-jnp.abs(x)))


def _softplus(x):
    return jnp.maximum(x, 0.0) + jnp.log(1.0 + jnp.exp(-jnp.abs(x)))


def _iota(shape, dim):
    return lax.broadcasted_iota(jnp.int32, shape, dim)


def _lanes(x, n):
    return x[:, :n] if n <= LANES else jnp.concatenate([x] * (n // LANES), axis=1)


def _head_block_ones(n):
    return ((_iota((n, n), 0) >> HEAD_SHIFT) == (_iota((n, n), 1) >> HEAD_SHIFT)).astype(F32)


def _resident(shape):
    return pl.BlockSpec(shape, lambda *_: (0,) * len(shape), pipeline_mode=pl.Buffered(1))


def _weight(w, layer):
    return pl.BlockSpec((None,) + w.shape[1:], lambda *_: (layer, 0, 0), pipeline_mode=pl.Buffered(1))


def _linear_body(x_ref, g_ref, w_ref, *out_refs, segs, feat_major, rows_too):
    xb = _rms(x_ref[...], g_ref[...]).astype(BF16)
    extra_refs = dict(zip(rows_too, out_refs[len(segs):]))
    for idx, (o_ref, (s, e)) in enumerate(zip(out_refs, segs)):
        acc = _dot(xb, w_ref[:, s:e])
        if idx in feat_major:
            rows = acc.shape[0] // o_ref.shape[0]
            for sq in range(o_ref.shape[0]):
                o_ref[sq] = acc[sq * rows:(sq + 1) * rows].T
        else:
            o_ref[...] = acc
        if idx in extra_refs:
            extra_refs[idx][...] = acc


def _fused_linear(x, w, layer, widths, gain, tm, feat_major=(), rows_per_seq=None, rows_too=()):
    m, k = x.shape
    n = w.shape[2]
    segs = []
    s = 0
    for wd in widths:
        segs.append((s, s + wd))
        s += wd
    assert s == n and m % tm == 0
    out_specs, out_shape = [], []
    for idx, wd in enumerate(widths):
        if idx in feat_major:
            if rows_per_seq >= tm:
                tiles = rows_per_seq // tm
                assert rows_per_seq % tm == 0
                out_specs.append(pl.BlockSpec((1, wd, tm), lambda i, tiles=tiles: (i // tiles, 0, i % tiles)))
            else:
                assert tm % rows_per_seq == 0
                out_specs.append(pl.BlockSpec((tm // rows_per_seq, wd, rows_per_seq), lambda i: (i, 0, 0)))
            out_shape.append(jax.ShapeDtypeStruct((m // rows_per_seq, wd, rows_per_seq), F32))
        else:
            out_specs.append(pl.BlockSpec((tm, wd), lambda i: (i, 0)))
            out_shape.append(jax.ShapeDtypeStruct((m, wd), F32))
    for idx in rows_too:
        out_specs.append(pl.BlockSpec((tm, widths[idx]), lambda i: (i, 0)))
        out_shape.append(jax.ShapeDtypeStruct((m, widths[idx]), F32))
    assert w.shape[1] == k
    return pl.pallas_call(
        functools.partial(_linear_body, segs=tuple(segs), feat_major=tuple(feat_major), rows_too=tuple(rows_too)),
        grid=(m // tm,),
        in_specs=[pl.BlockSpec((tm, k), lambda i: (i, 0)), _resident((1, k)), _weight(w, layer)],
        out_specs=out_specs,
        out_shape=out_shape,
        compiler_params=_cparams("parallel"),
    )(x, gain.reshape(1, -1), w)


def _ffn_body(*refs, final):
    if final:
        x_ref, g_ref, wg_ref, wu_ref, wd_ref, gf_ref, o_ref = refs
    else:
        x_ref, g_ref, wg_ref, wu_ref, wd_ref, o_ref = refs
    x = x_ref[...]
    hb = _rms(x, g_ref[...]).astype(BF16)
    acc = x
    for c in range(0, D_FF, FF_CHUNK):
        gate = _dot(hb, wg_ref[:, c:c + FF_CHUNK])
        up = _dot(hb, wu_ref[:, c:c + FF_CHUNK])
        act = (gate * _sigmoid(gate) * up).astype(BF16)
        acc = acc + _dot(act, wd_ref[c:c + FF_CHUNK, :])
    if final:
        acc = _rms(acc, gf_ref[...])
    o_ref[...] = acc


def _ffn(x, gain, wg, wu, wd, layer, final_gain=None, tm=256):
    m = x.shape[0]
    final = final_gain is not None
    in_specs = [pl.BlockSpec((tm, D_MODEL), lambda i: (i, 0)), _resident((1, D_MODEL)),
                _weight(wg, layer), _weight(wu, layer), _weight(wd, layer)]
    args = [x, gain.reshape(1, -1), wg, wu, wd]
    if final:
        in_specs.append(_resident((1, D_MODEL)))
        args.append(final_gain.reshape(1, -1))
    return pl.pallas_call(
        functools.partial(_ffn_body, final=final),
        grid=(m // tm,),
        in_specs=in_specs,
        out_specs=pl.BlockSpec((tm, D_MODEL), lambda i: (i, 0)),
        out_shape=jax.ShapeDtypeStruct((m, D_MODEL), F32),
        compiler_params=_cparams("parallel"),
    )(*args)


def _cross_body(x_ref, oa_ref, ob_ref, oc_ref, od_ref, wout_ref, g_ref, wq_ref, mk_ref, mv_ref, wo_ref, o_ref):
    ns, tq, _ = x_ref.shape
    rows = ns * tq
    x = x_ref[...].reshape(rows, D_MODEL)
    for gi, m_ref in enumerate((oa_ref, ob_ref, oc_ref, od_ref)):
        x = x + _dot(m_ref[...].reshape(rows, GROUP_W).astype(BF16), wout_ref[gi * GROUP_W:(gi + 1) * GROUP_W, :])
    hb = _rms(x, g_ref[...]).astype(BF16)
    q = (_dot(hb, wq_ref[...]) * MEM_HEAD_DIM ** -0.5).astype(BF16)
    head = lambda h: slice(h * MEM_HEAD_DIM, (h + 1) * MEM_HEAD_DIM)
    pairs = [(sq, h) for sq in range(ns) for h in range(MEM_HEADS)]
    if len(mk_ref.shape) == 4:
        mem = lambda ref, sq, h: ref[sq, :, h, :].astype(BF16)
    else:
        mem = lambda ref, sq, h: ref[sq, :, head(h)].astype(BF16)
    scores = [_dot_nt(q[sq * tq:(sq + 1) * tq, head(h)], mem(mk_ref, sq, h)) for sq, h in pairs]
    probs = []
    for s in scores:
        p = jnp.exp(s - jnp.max(s, axis=-1, keepdims=True))
        probs.append((p / jnp.sum(p, axis=-1, keepdims=True)).astype(BF16))
    outs = [_dot(p, mem(mv_ref, sq, h)).astype(BF16) for p, (sq, h) in zip(probs, pairs)]
    o = jnp.concatenate([jnp.concatenate(outs[sq * MEM_HEADS:(sq + 1) * MEM_HEADS], axis=1) for sq in range(ns)], axis=0)
    o_ref[...] = (x + _dot(o, wo_ref[...])).reshape(ns, tq, D_MODEL)


def _cross(x, mix, w_out, gain, wq, mk, mv, layer, wo, w_layer):
    b, t, _ = x.shape
    nm = mk.shape[2]
    tq = min(ROW_TILE, t)
    ns = CROSS_SEQS if CROSS_SEQS * t <= ROW_TILE else 1
    assert t % tq == 0 and b % ns == 0
    row_spec = lambda w: pl.BlockSpec((ns, tq, w), lambda i, j: (i, j, 0))
    mem_spec = pl.BlockSpec((None, ns) + mk.shape[2:], lambda i, j: (layer, i) + (0,) * (mk.ndim - 2))
    return pl.pallas_call(
        _cross_body,
        grid=(b // ns, t // tq),
        in_specs=[row_spec(D_MODEL)] + [row_spec(GROUP_W)] * 4 + [_weight(w_out, w_layer), _resident((1, D_MODEL)),
                  _weight(wq, w_layer), mem_spec, mem_spec, _weight(wo, w_layer)],
        out_specs=row_spec(D_MODEL),
        out_shape=jax.ShapeDtypeStruct((b, t, D_MODEL), F32),
        compiler_params=_cparams("parallel", "parallel"),
    )(x, *mix, w_out, gain.reshape(1, -1), wq, mk, mv, wo)


def _diff_body(lam_ref, q_ref, kn_ref, vn_ref, kp_ref, vp_ref, bn_ref, o_ref, q8_sc, m_sc, l_sc, acc_sc, *,
               tq, tk, past, lam_init, ns):
    ng = 2 * N_HEADS
    lane = _iota((tq, GROUP_W), 1)
    for sq in range(ns):
        q = q_ref[sq] * (DIFF_HALF ** -0.5 * LOG2E)
        for g in range(ng):
            lo = (g // 2) * HEAD_DIM + (g % 2) * DIFF_HALF
            q8_sc[sq, g * tq:(g + 1) * tq, :] = jnp.where((lane >= lo) & (lane < lo + DIFF_HALF), q, 0.0).astype(BF16)
    m_sc[...] = jnp.full(m_sc.shape, -jnp.inf, F32)
    l_sc[...] = jnp.zeros(l_sc.shape, F32)
    acc_sc[...] = jnp.zeros(acc_sc.shape, F32)
    pq = past + _iota((tq, 1), 0)
    pq_f = pq.astype(F32)
    lanes = _lanes
    pairs = [(sq, g) for sq in range(ns) for g in range(ng)]

    def block(k_ref, v_ref, k0, n, near, pos0=0):
        pk = pos0 + k0 + _iota((1, n), 1)
        pk_f = pk.astype(F32)
        if near:
            allowed = (pk >> CHUNK_SHIFT) <= (pq >> CHUNK_SHIFT)
            shift = pq_f - jnp.abs(pq_f - pk_f)
        s_all = [_dot(q8_sc[sq], k_ref[sq, :, pl.ds(k0, n)].astype(BF16)) for sq in range(ns)]
        scores = {}
        for sq, g in pairs:
            slope = 2.0 ** (-2 * (g // 2 + 1)) * LOG2E
            s = s_all[sq][g * tq:(g + 1) * tq]
            scores[sq, g] = jnp.where(allowed, s + slope * shift, NEG) if near else s + slope * pk_f
        m_prevs = {k: m_sc[k[0], k[1]] for k in pairs}
        m_news = {k: jnp.maximum(m_prevs[k], jnp.max(scores[k], axis=-1, keepdims=True)) for k in pairs}
        probs = {k: jnp.exp2(scores[k] - lanes(m_news[k], n)) for k in pairs}
        alphas = {k: jnp.exp2(m_prevs[k] - m_news[k]) for k in pairs}
        for k in pairs:
            l_sc[k[0], k[1]] = alphas[k] * l_sc[k[0], k[1]] + jnp.sum(probs[k], axis=-1, keepdims=True)
            m_sc[k[0], k[1]] = m_news[k]
        pvs = [_dot_nt(jnp.concatenate([probs[sq, g].astype(BF16) for g in range(ng)], axis=0),
                       v_ref[sq, :, pl.ds(k0, n)].astype(BF16)) for sq in range(ns)]
        for sq, g in pairs:
            acc_sc[sq, g] = acc_sc[sq, g] * lanes(alphas[sq, g], GROUP_W) + pvs[sq][g * tq:(g + 1) * tq]

    def past_step(j, carry):
        block(kp_ref, vp_ref, pl.multiple_of(j * tk, tk), tk, False)
        return carry
    lax.fori_loop(0, past // tk, past_step, 0)
    block(kn_ref, vn_ref, 0, tq, True, pos0=past)

    lam = lam_ref[0]
    lane_head = _iota((1, GROUP_W), 1) >> HEAD_SHIFT
    ones16 = _head_block_ones(GROUP_W).astype(BF16)
    for sq in range(ns):
        o = jnp.zeros((tq, GROUP_W), F32)
        for h in range(N_HEADS):
            o0 = acc_sc[sq, 2 * h] / lanes(l_sc[sq, 2 * h], GROUP_W)
            o1 = acc_sc[sq, 2 * h + 1] / lanes(l_sc[sq, 2 * h + 1], GROUP_W)
            o = jnp.where(lane_head == h, o0 - lam * o1, o)
        ms = _mm_r(o * o, ones16) * (1.0 / HEAD_DIM)
        o_ref[sq] = o * lax.rsqrt(ms + EPS) * bn_ref[...] * (1.0 - lam_init)


def _diff_attn(lam, q, kn, vn, kp, vp, layer, bnorm, lam_init):
    b, t, _ = q.shape
    past = kp.shape[3]
    ng = 2 * N_HEADS
    ns = DECODE_SEQS
    tq, tk = t, math.gcd(past, 512)
    assert t == CHUNK and past % CHUNK == 0 and tk % LANES == 0 and b % ns == 0
    seq_spec = pl.BlockSpec((ns, GROUP_W, t), lambda i: (i, 0, 0))
    past_spec = pl.BlockSpec((None, ns, GROUP_W, past), lambda i: (layer, i, 0, 0))
    return pl.pallas_call(
        functools.partial(_diff_body, tq=tq, tk=tk, past=past, lam_init=lam_init, ns=ns),
        grid=(b // ns,),
        in_specs=[pl.BlockSpec(memory_space=pltpu.SMEM), pl.BlockSpec((ns, tq, GROUP_W), lambda i: (i, 0, 0)),
                  seq_spec, seq_spec, past_spec, past_spec, _resident((1, GROUP_W))],
        out_specs=pl.BlockSpec((ns, tq, GROUP_W), lambda i: (i, 0, 0)),
        out_shape=jax.ShapeDtypeStruct((b, t, GROUP_W), F32),
        scratch_shapes=[pltpu.VMEM((ns, ng * tq, GROUP_W), BF16), pltpu.VMEM((ns, ng, tq, LANES), F32),
                        pltpu.VMEM((ns, ng, tq, LANES), F32), pltpu.VMEM((ns, ng, tq, GROUP_W), F32)],
        compiler_params=_cparams("parallel"),
    )(lam, q, kn, vn, kp, vp, bnorm)


def _diff_prompt_body(lam_ref, q_ref, k_ref, v_ref, bn_ref, o_ref, qt_sc, m_sc, acc_sc, *, tq, tk, lam_init):
    i = pl.program_id(1)
    ng = 2 * N_HEADS
    qt = (q_ref[0] * (DIFF_HALF ** -0.5 * LOG2E)).T
    feat = _iota((GROUP_W, tq), 0)
    for g in range(ng):
        lo = (g // 2) * HEAD_DIM + (g % 2) * DIFF_HALF
        qt_sc[g] = jnp.where((feat >= lo) & (feat < lo + DIFF_HALF), qt, 0.0).astype(BF16)
    m_sc[...] = jnp.full(m_sc.shape, -jnp.inf, F32)
    acc_sc[...] = jnp.zeros(acc_sc.shape, F32)
    pq = i * tq + _iota((1, tq), 1)
    pq_f = pq.astype(F32)
    ones_rows = jnp.ones((SUM_ROWS, tk), BF16)

    def block(k0, n, near):
        kb = k_ref[0, pl.ds(k0, n), :].astype(BF16)
        vb = v_ref[0, :, pl.ds(k0, n)].astype(BF16)
        pk = k0 + _iota((n, LANES), 0)
        pk_f = pk.astype(F32)
        if near:
            allowed = (_lanes(pk, tq) >> CHUNK_SHIFT) <= (pq >> CHUNK_SHIFT)
            shift = pq_f - jnp.abs(pq_f - _lanes(pk_f, tq))
        groups = range(ng)
        scores = []
        for g in groups:
            slope = 2.0 ** (-2 * (g // 2 + 1)) * LOG2E
            s = _dot(kb, qt_sc[g])
            scores.append(jnp.where(allowed, s + slope * shift, NEG) if near else s + _lanes(slope * pk_f, tq))
        m_prevs = [m_sc[g] for g in groups]
        m_news = [jnp.maximum(m_prevs[g], jnp.max(scores[g], axis=0, keepdims=True)) for g in groups]
        probs = [jnp.exp2(scores[g] - m_news[g]) for g in groups]
        for g in groups:
            alpha = jnp.exp2(m_prevs[g] - m_news[g])
            m_sc[g] = m_news[g]
            h = g // 2
            pv = _dot(jnp.concatenate([vb[h * HEAD_DIM:(h + 1) * HEAD_DIM, :], ones_rows[:, :n]], axis=0),
                      probs[g].astype(BF16))
            acc_sc[g] = acc_sc[g] * alpha + pv

    def prev_step(j, carry):
        block(pl.multiple_of(j * tk, tk), tk, False)
        return carry
    lax.fori_loop(0, (i * tq) // tk, prev_step, 0)

    @pl.when((i * tq) % tk != 0)
    def _():
        block(pl.multiple_of((i - 1) * tq, tq), tq, False)
    block(pl.multiple_of(i * tq, tq), tq, True)

    lam = lam_ref[0]
    norm = lambda g: acc_sc[g, :HEAD_DIM, :] / acc_sc[g, HEAD_DIM:HEAD_DIM + 1, :]
    heads = [norm(2 * h) - lam * norm(2 * h + 1) for h in range(N_HEADS)]
    o = jnp.concatenate(heads, axis=0).T
    ms = _mm_r(o * o, _head_block_ones(GROUP_W).astype(BF16)) * (1.0 / HEAD_DIM)
    o_ref[0] = o * lax.rsqrt(ms + EPS) * bn_ref[...] * (1.0 - lam_init)


def _diff_attn_prompt(lam, q, k_rows, v_feat, bnorm, lam_init):
    b, t, _ = q.shape
    ng = 2 * N_HEADS
    tq = min(256, t)
    tk = 2 * tq if t % (2 * tq) == 0 else tq
    assert t % tq == 0 and tq % LANES == 0
    return pl.pallas_call(
        functools.partial(_diff_prompt_body, tq=tq, tk=tk, lam_init=lam_init),
        grid=(b, t // tq),
        in_specs=[pl.BlockSpec(memory_space=pltpu.SMEM), pl.BlockSpec((1, tq, GROUP_W), lambda i, j: (i, j, 0)),
                  pl.BlockSpec((1, t, GROUP_W), lambda i, j: (i, 0, 0)),
                  pl.BlockSpec((1, GROUP_W, t), lambda i, j: (i, 0, 0)), _resident((1, GROUP_W))],
        out_specs=pl.BlockSpec((1, tq, GROUP_W), lambda i, j: (i, j, 0)),
        out_shape=jax.ShapeDtypeStruct((b, t, GROUP_W), F32),
        scratch_shapes=[pltpu.VMEM((ng, GROUP_W, tq), BF16), pltpu.VMEM((ng, 1, tq), F32),
                        pltpu.VMEM((ng, HEAD_DIM + SUM_ROWS, tq), F32)],
        compiler_params=_cparams("parallel", "parallel"),
    )(lam, q, k_rows, v_feat, bnorm)


def _stick_body(q_ref, kn_ref, vn_ref, kp_ref, vp_ref, o_ref, q4_sc, c_sc, acc_sc, *, tq, tk, past, ns):
    nr = N_HEADS * tq
    lane = _iota((tq, GROUP_W), 1) >> HEAD_SHIFT
    for sq in range(ns):
        q = q_ref[sq] * (HEAD_DIM ** -0.5 * LOG2E)
        for h in range(N_HEADS):
            q4_sc[sq, h * tq:(h + 1) * tq, :] = jnp.where(lane == h, q, 0.0).astype(BF16)
    c_sc[...] = jnp.zeros(c_sc.shape, F32)
    acc_sc[...] = jnp.zeros(acc_sc.shape, F32)
    pq = past + (_iota((nr, 1), 0) & (tq - 1))
    laters = {w: (_iota((w, w), 0) > _iota((w, w), 1)).astype(BF16)
              for w in {min(tq, STICK_SUB), min(tk, STICK_SUB)}}
    seqs = range(ns)

    def block(k_ref, v_ref, k0, n, masked, pos0=0):
        sub = min(n, STICK_SUB)
        later = laters[sub]
        zs = [_dot(q4_sc[sq], k_ref[sq, :, pl.ds(k0, n)].astype(BF16)) for sq in seqs]
        lss = [jnp.minimum(z, 0.0) - jnp.log2(1.0 + jnp.exp2(-jnp.abs(z))) for z in zs]
        lks = [ls - z for ls, z in zip(lss, zs)]
        if masked:
            mask = (pos0 + k0 + _iota((1, n), 1)) < pq
            lks = [jnp.where(mask, lk, 0.0) for lk in lks]
        his = [lk.astype(BF16) for lk in lks]
        los = [(lk - hi.astype(F32)).astype(BF16) for lk, hi in zip(lks, his)]
        carries = [c_sc[sq] for sq in seqs]
        parts = [[] for _ in seqs]
        for sb in reversed(range(n // sub)):
            sl = slice(sb * sub, (sb + 1) * sub)
            for sq in seqs:
                after = _dot(his[sq][:, sl], later) + _dot(los[sq][:, sl], later)
                parts[sq].append(lss[sq][:, sl] + after + _lanes(carries[sq], sub))
                carries[sq] = carries[sq] + (after[:, 0:1] + lks[sq][:, sb * sub:sb * sub + 1])
        for sq in seqs:
            c_sc[sq] = carries[sq]
            e = parts[sq][0] if len(parts[sq]) == 1 else jnp.concatenate(parts[sq][::-1], axis=1)
            if masked:
                a = jnp.where(mask, jnp.exp2(jnp.where(mask, e, 0.0)), 0.0)
            else:
                a = jnp.exp2(e)
            acc_sc[sq] = acc_sc[sq] + _dot_nt(a.astype(BF16), v_ref[sq, :, pl.ds(k0, n)].astype(BF16))

    block(kn_ref, vn_ref, 0, tq, True, pos0=past)
    nb = past // tk

    def live():
        return jnp.max(c_sc[...]) > STICK_DEAD

    def past_step(state):
        j, _ = state
        block(kp_ref, vp_ref, pl.multiple_of((nb - 1 - j) * tk, tk), tk, False)
        return j + 1, live()
    lax.while_loop(lambda state: (state[0] < nb) & state[1], past_step, (jnp.int32(0), live()))

    for sq in seqs:
        o = acc_sc[sq, 0:tq, :]
        for h in range(1, N_HEADS):
            o = jnp.where(lane == h, acc_sc[sq, h * tq:(h + 1) * tq, :], o)
        o_ref[sq] = o


def _stick_attn(q, kn, vn, kp, vp, layer):
    b, t, _ = q.shape
    past = kp.shape[3]
    ns = DECODE_SEQS
    tq, tk = t, math.gcd(past, 512)
    assert tq & (tq - 1) == 0 and tk % min(tk, STICK_SUB) == 0 and b % ns == 0
    seq_spec = pl.BlockSpec((ns, GROUP_W, t), lambda i: (i, 0, 0))
    past_spec = pl.BlockSpec((None, ns, GROUP_W, past), lambda i: (layer, i, 0, 0))
    return pl.pallas_call(
        functools.partial(_stick_body, tq=tq, tk=tk, past=past, ns=ns),
        grid=(b // ns,),
        in_specs=[pl.BlockSpec((ns, tq, GROUP_W), lambda i: (i, 0, 0)), seq_spec, seq_spec, past_spec, past_spec],
        out_specs=pl.BlockSpec((ns, tq, GROUP_W), lambda i: (i, 0, 0)),
        out_shape=jax.ShapeDtypeStruct((b, t, GROUP_W), F32),
        scratch_shapes=[pltpu.VMEM((ns, N_HEADS * tq, GROUP_W), BF16), pltpu.VMEM((ns, N_HEADS * tq, LANES), F32),
                        pltpu.VMEM((ns, N_HEADS * tq, GROUP_W), F32)],
        compiler_params=_cparams("parallel"),
    )(q, kn, vn, kp, vp)


def _stick_prompt_body(q_ref, k_ref, v_ref, o_ref, qt_sc, c_sc, acc_sc, *, tq, tk):
    i = pl.program_id(1)
    qt = (q_ref[0] * (HEAD_DIM ** -0.5 * LOG2E)).T
    feat_head = _iota((GROUP_W, tq), 0) >> HEAD_SHIFT
    for h in range(N_HEADS):
        qt_sc[h] = jnp.where(feat_head == h, qt, 0.0).astype(BF16)
    c_sc[...] = jnp.zeros(c_sc.shape, F32)
    acc_sc[...] = jnp.zeros(acc_sc.shape, F32)
    pq = i * tq + _iota((1, tq), 1)
    sub = min(tq, STICK_SUB)
    later = (_iota((sub, sub), 1) > _iota((sub, sub), 0)).astype(BF16)

    def block(k0, n, masked):
        kb = k_ref[0, pl.ds(k0, n), :].astype(BF16)
        vb = v_ref[0, :, pl.ds(k0, n)].astype(BF16)
        if masked:
            mask = _lanes(k0 + _iota((n, LANES), 0), tq) < pq
        heads = range(N_HEADS)
        zs = [_dot(kb, qt_sc[h]) for h in heads]
        lss = [jnp.minimum(z, 0.0) - jnp.log2(1.0 + jnp.exp2(-jnp.abs(z))) for z in zs]
        lks = [ls - z for ls, z in zip(lss, zs)]
        if masked:
            lks = [jnp.where(mask, lk, 0.0) for lk in lks]
        es = []
        for h in heads:
            lk = lks[h]
            hi = lk.astype(BF16)
            lo = (lk - hi.astype(F32)).astype(BF16)
            carry = c_sc[h]
            parts = []
            for sb in reversed(range(n // sub)):
                sl = slice(sb * sub, (sb + 1) * sub)
                after = _dot(later, hi[sl]) + _dot(later, lo[sl])
                parts.append(lss[h][sl] + after + carry)
                carry = carry + (after[0:1] + lk[sb * sub:sb * sub + 1])
            c_sc[h] = carry
            es.append(parts[0] if len(parts) == 1 else jnp.concatenate(parts[::-1], axis=0))
        for h in heads:
            if masked:
                a = jnp.where(mask, jnp.exp2(jnp.where(mask, es[h], 0.0)), 0.0)
            else:
                a = jnp.exp2(es[h])
            acc_sc[h] = acc_sc[h] + _dot(vb[h * HEAD_DIM:(h + 1) * HEAD_DIM, :], a.astype(BF16))

    block(pl.multiple_of(i * tq, tq), tq, True)

    @pl.when((i * tq) % tk != 0)
    def _():
        block(pl.multiple_of((i - 1) * tq, tq), tq, False)
    nfull = (i * tq) // tk

    def live():
        return jnp.max(c_sc[...]) > STICK_DEAD

    def prev_step(state):
        j, _ = state
        block(pl.multiple_of((nfull - 1 - j) * tk, tk), tk, False)
        return j + 1, live()
    lax.while_loop(lambda state: (state[0] < nfull) & state[1], prev_step, (jnp.int32(0), live()))

    o_ref[0] = jnp.concatenate([acc_sc[h] for h in range(N_HEADS)], axis=0).T


def _stick_attn_prompt(q, k_rows, v_feat):
    b, t, _ = q.shape
    tq = min(256, t)
    tk = 2 * tq if t % (2 * tq) == 0 else tq
    assert t % tq == 0 and tq % LANES == 0 and tk % min(tk, STICK_SUB) == 0
    return pl.pallas_call(
        functools.partial(_stick_prompt_body, tq=tq, tk=tk),
        grid=(b, t // tq),
        in_specs=[pl.BlockSpec((1, tq, GROUP_W), lambda i, j: (i, j, 0)),
                  pl.BlockSpec((1, t, GROUP_W), lambda i, j: (i, 0, 0)),
                  pl.BlockSpec((1, GROUP_W, t), lambda i, j: (i, 0, 0))],
        out_specs=pl.BlockSpec((1, tq, GROUP_W), lambda i, j: (i, j, 0)),
        out_shape=jax.ShapeDtypeStruct((b, t, GROUP_W), F32),
        scratch_shapes=[pltpu.VMEM((N_HEADS, GROUP_W, tq), BF16), pltpu.VMEM((N_HEADS, 1, tq), F32),
                        pltpu.VMEM((N_HEADS, HEAD_DIM, tq), F32)],
        compiler_params=_cparams("parallel", "parallel"),
    )(q, k_rows, v_feat)


def _pair_mask(n, s, reps=1):
    r, c = _iota((reps * n, n), 0) & (n - 1), _iota((reps * n, n), 1)
    return ((r // (2 * s)) == (c // (2 * s))) & (((r // s) & 1) == 1) & (((c // s) & 1) == 0)


def _state_spec(nb):
    return pl.BlockSpec((nb, N_HEADS, HEAD_DIM, HEAD_DIM), lambda i, j: (i, 0, 0, 0))


def _load_state(s_sc, s0_ref, transpose):
    s_sc[...] = jnp.zeros(s_sc.shape, F32)
    for b in range(s_sc.shape[0]):
        for h in range(N_HEADS):
            blk = s0_ref[b, h]
            s_sc[b, h * HEAD_DIM:(h + 1) * HEAD_DIM, h * HEAD_DIM:(h + 1) * HEAD_DIM] = blk.T if transpose else blk


def _store_state(sout_ref, s_sc, transpose):
    for b in range(s_sc.shape[0]):
        for h in range(N_HEADS):
            blk = s_sc[b, h * HEAD_DIM:(h + 1) * HEAD_DIM, h * HEAD_DIM:(h + 1) * HEAD_DIM]
            sout_ref[b, h] = blk.T if transpose else blk


def _head_rows(x, lane_head):
    return jnp.concatenate([jnp.where(lane_head == h, x, 0.0) for h in range(N_HEADS)], axis=0)


def _head_diag(x, lane_head):
    n = x.shape[0] // N_HEADS
    out = x[:n]
    for h in range(1, N_HEADS):
        out = jnp.where(lane_head == h, x[h * n:(h + 1) * n], out)
    return out


def _gdn_body(x_ref, z_ref, abg_ref, cw_ref, cs_ref, s0_ref, alog_ref, dtb_ref, an_ref,
              o_ref, sout_ref, s_sc, xb_sc, *, nc, nb):
    c = pl.program_id(1)
    L = CHUNK
    pad = 8

    @pl.when(c == 0)
    def _():
        _load_state(s_sc, s0_ref, transpose=False)
        xb_sc[:, pad - 3:pad, :] = cs_ref[...]

    ones_bd = _head_block_ones(GROUP_W)
    ones_bd16 = ones_bd.astype(BF16)
    er, ec = _iota((ABG_W, GROUP_W), 0), _iota((ABG_W, GROUP_W), 1) >> HEAD_SHIFT
    e_beta = (er == ec).astype(BF16)
    e_g = (er == ec + N_HEADS).astype(BF16)
    ri, ci = _iota((L, L), 0), _iota((L, L), 1)
    tri16 = (ri >= ci).astype(BF16)
    incl = ri >= ci
    strict = ri > ci
    eye = (ri == ci).astype(F32)
    pair_masks = [_pair_mask(L, sz) for sz in (1, 2, 4, 8, 16, 32)]
    lane_head = _iota((1, GROUP_W), 1) >> HEAD_SHIFT
    cw = cw_ref[...]

    seqs = range(nb)
    us = []
    for b in seqs:
        x = x_ref[b]
        xb_sc[b, pad:pad + L, :] = x
        u = (x * cw[3:4] + xb_sc[b, pad - 1:pad - 1 + L, :] * cw[2:3]
             + xb_sc[b, pad - 2:pad - 2 + L, :] * cw[1:2] + xb_sc[b, pad - 3:pad - 3 + L, :] * cw[0:1])
        xb_sc[b, pad - 3:pad, :] = x[L - 3:L, :]
        us.append(u * _sigmoid(u))
    qs = [u[:, :GROUP_W] for u in us]
    ks = [u[:, GROUP_W:2 * GROUP_W] for u in us]
    vs = [u[:, 2 * GROUP_W:] for u in us]
    qs = [q * lax.rsqrt(_mm_r(q * q, ones_bd16) + EPS) * HEAD_DIM ** -0.5 for q in qs]
    ks = [k * lax.rsqrt(_mm_r(k * k, ones_bd16) + EPS) for k in ks]

    abgs = [abg_ref[b] for b in seqs]
    betas = [_mm_r(_sigmoid(abg), e_beta) for abg in abgs]
    gc_ns = [_mm_l(tri16, -jnp.exp(alog_ref[...]) * _softplus(abg + dtb_ref[...])) for abg in abgs]
    gc_ts = [gc_n.T for gc_n in gc_ns]
    gcs = [_mm_r(gc_n, e_g) for gc_n in gc_ns]

    kbs = [k * beta for k, beta in zip(ks, betas)]
    k16s = [k.astype(BF16) for k in ks]
    a4s = [_bdot(_head_rows(kb, lane_head).astype(BF16), k16, _NT) for kb, k16 in zip(kbs, k16s)]
    qk4s = [_bdot(_head_rows(q, lane_head).astype(BF16), k16, _NT) for q, k16 in zip(qs, k16s)]
    decs = []
    for b in seqs:
        dec = []
        for h in range(N_HEADS):
            diff = gc_ns[b][:, N_HEADS + h:N_HEADS + h + 1] - gc_ts[b][N_HEADS + h:N_HEADS + h + 1, :]
            dec.append(jnp.where(incl, jnp.exp(jnp.where(incl, diff, 0.0)), 0.0))
        decs.append(jnp.stack(dec))

    a_all = jnp.concatenate([jnp.where(strict, a4s[b].reshape(N_HEADS, L, L) * decs[b], 0.0) for b in seqs], axis=0)
    t_inv = eye - jnp.where(pair_masks[0], a_all, 0.0)
    a16 = a_all.astype(BF16)
    for pm in pair_masks[1:]:
        t16 = t_inv.astype(BF16)
        t_inv = t_inv - _bdot(t16, _bdot(jnp.where(pm, a16, 0.0), t16, _BNN).astype(BF16), _BNN)

    exp_gs = [jnp.exp(gc) for gc in gcs]
    s_olds = [s_sc[b] for b in seqs]
    s16s = [s.astype(BF16) for s in s_olds]
    rhss = [vs[b] * betas[b] - _bdot((kbs[b] * exp_gs[b]).astype(BF16), s16s[b], _NN) for b in seqs]
    ws = [_head_diag(_mm1(t_inv[b * N_HEADS:(b + 1) * N_HEADS].reshape(N_HEADS * L, L), rhss[b]), lane_head)
          for b in seqs]
    outs = [_bdot((qs[b] * exp_gs[b]).astype(BF16), s16s[b], _NN)
            + _head_diag(_mm1((qk4s[b].reshape(N_HEADS, L, L) * decs[b]).reshape(N_HEADS * L, L), ws[b]), lane_head)
            for b in seqs]
    for b in seqs:
        g_last = gcs[b][L - 1:L, :]
        k_dec = ks[b] * jnp.exp(g_last - gcs[b])
        s_sc[b] = s_olds[b] * jnp.exp(g_last) + _mm1(k_dec.T, ws[b]) * ones_bd
    for b in seqs:
        o = outs[b]
        ms = _mm_r(o * o, ones_bd16) * (1.0 / HEAD_DIM)
        zg = z_ref[b]
        o_ref[b] = o * lax.rsqrt(ms + EPS) * an_ref[...] * (zg * _sigmoid(zg))

    @pl.when(c == nc - 1)
    def _():
        _store_state(sout_ref, s_sc, transpose=False)


def _gdn(aqkv, az, abg, conv_w, conv_state, s0, alog_row, dtb_row, anorm_row):
    b, t, _ = aqkv.shape
    nc = t // CHUNK
    nb = SEQ_PER_STEP
    assert b % nb == 0
    cmap = lambda i, j: (i, j, 0)
    bmap = lambda i, j: (i, 0, 0)
    return pl.pallas_call(
        functools.partial(_gdn_body, nc=nc, nb=nb),
        grid=(b // nb, nc),
        in_specs=[pl.BlockSpec((nb, CHUNK, 3 * GROUP_W), cmap), pl.BlockSpec((nb, CHUNK, GROUP_W), cmap),
                  pl.BlockSpec((nb, CHUNK, ABG_W), cmap), _resident((CONV_W, 3 * GROUP_W)),
                  pl.BlockSpec((nb, CONV_W - 1, 3 * GROUP_W), bmap), _state_spec(nb),
                  _resident((1, ABG_W)), _resident((1, ABG_W)), _resident((1, GROUP_W))],
        out_specs=[pl.BlockSpec((nb, CHUNK, GROUP_W), cmap), _state_spec(nb)],
        out_shape=[jax.ShapeDtypeStruct((b, t, GROUP_W), F32), jax.ShapeDtypeStruct(s0.shape, F32)],
        scratch_shapes=[pltpu.VMEM((nb, GROUP_W, GROUP_W), F32), pltpu.VMEM((nb, 8 + CHUNK, 3 * GROUP_W), F32)],
        compiler_params=_cparams("parallel", "arbitrary"),
    )(aqkv, az, abg, conv_w, conv_state, s0, alog_row, dtb_row, anorm_row)


def _hgrn_body(q_ref, f_ref, i_ref, g_ref, lb_ref, dn_ref, s0_ref, o_ref, sout_ref, s_sc, *, nc, nb):
    c = pl.program_id(1)
    L = CHUNK

    @pl.when(c == 0)
    def _():
        _load_state(s_sc, s0_ref, transpose=True)

    lb = lb_ref[...]
    ones_bd = _head_block_ones(GROUP_W)
    ones_bd16 = ones_bd.astype(BF16)
    ri, ci = _iota((L, L), 0), _iota((L, L), 1)
    tri16 = (ri >= ci).astype(BF16)
    sizes = (1, 2, 4, 8, 16, 32)
    sel16 = jnp.concatenate([(ci == (ri // (2 * sz)) * (2 * sz) + sz).astype(BF16) for sz in sizes], axis=0)
    eye4 = (_iota((N_HEADS * L, L), 0) & (L - 1)) == _iota((N_HEADS * L, L), 1)
    pair_masks4 = [_pair_mask(L, sz, reps=N_HEADS) for sz in sizes]
    lane_head = _iota((1, GROUP_W), 1) >> HEAD_SHIFT
    rows = _iota((L, 1), 0)

    pre = []
    for b in range(nb):
        fl = f_ref[b]
        log_f = _log_sigmoid(fl) + jnp.log(1.0 + lb * jnp.exp(-fl))
        k = (1.0 - lb) * _sigmoid(-fl)
        q = q_ref[b] * HEAD_DIM ** -0.5
        bc = _mm_l(tri16, log_f)
        pre.append((q, k, bc))
    b_refs = [_mm_l(sel16, bc) for _, _, bc in pre]
    a_all = [jnp.where(eye4, _mm1(_head_rows(q, lane_head), k, _NT), 0.0) for q, k, _ in pre]
    for lv, sz in enumerate(sizes):
        later = ((rows // sz) & 1) == 1
        for b in range(nb):
            q, k, bc = pre[b]
            b_ref = b_refs[b][lv * L:(lv + 1) * L]
            x = jnp.where(later, q * jnp.exp(jnp.where(later, bc - b_ref, 0.0)), 0.0)
            y = jnp.where(later, 0.0, k * jnp.exp(jnp.where(later, 0.0, b_ref - bc)))
            a_all[b] = a_all[b] + jnp.where(pair_masks4[lv], _mm1(_head_rows(x, lane_head), y, _NT), 0.0)

    for b in range(nb):
        q, k, bc = pre[b]
        v = i_ref[b]
        st = s_sc[b]
        o = _mm1(q * jnp.exp(bc), st, _NT) + _head_diag(_mm1(a_all[b], v), lane_head)
        b_last = bc[L - 1:L, :]
        st_new = st * jnp.exp(b_last) + _mm1(v.T, k * jnp.exp(b_last - bc)) * ones_bd
        s_sc[b] = st_new

        ms = _mm_r(o * o, ones_bd16) * (1.0 / HEAD_DIM)
        zg = g_ref[b]
        o_ref[b] = o * lax.rsqrt(ms + EPS) * dn_ref[...] * (zg * _sigmoid(zg))

    @pl.when(c == nc - 1)
    def _():
        _store_state(sout_ref, s_sc, transpose=True)


def _hgrn(dq, df, di, dg, lb_row, dnorm_row, s0):
    b, t, _ = dq.shape
    nc = t // CHUNK
    nb = SEQ_PER_STEP
    assert b % nb == 0
    cmap = lambda i, j: (i, j, 0)
    bmap = lambda i, j: (i, 0, 0)
    cspec = pl.BlockSpec((nb, CHUNK, GROUP_W), cmap)
    return pl.pallas_call(
        functools.partial(_hgrn_body, nc=nc, nb=nb),
        grid=(b // nb, nc),
        in_specs=[cspec, cspec, cspec, cspec, _resident((1, GROUP_W)), _resident((1, GROUP_W)),
                  _state_spec(nb)],
        out_specs=[cspec, _state_spec(nb)],
        out_shape=[jax.ShapeDtypeStruct((b, t, GROUP_W), F32), jax.ShapeDtypeStruct(s0.shape, F32)],
        scratch_shapes=[pltpu.VMEM((nb, GROUP_W, GROUP_W), F32)],
        compiler_params=_cparams("parallel", "arbitrary"),
    )(dq, df, di, dg, lb_row, dnorm_row, s0)


IN_WIDTHS = (3 * GROUP_W,) + (GROUP_W,) * 11 + (ABG_W,)
KV_SEGMENTS = (3, 4, 6, 7)
K_SEGMENTS = (3, 6)


def _layer(x, l, st, mk, mv, mem_layer, prm, final_gain):
    b, t, _ = x.shape
    m = b * t
    tm = math.gcd(m, ROW_TILE)
    has_past = st["b_k"] is not None
    outs = _fused_linear(x.reshape(m, D_MODEL), prm["w_in"], l, IN_WIDTHS, prm["norm_mix"][l], tm,
                         feat_major=KV_SEGMENTS, rows_per_seq=t, rows_too=() if has_past else K_SEGMENTS)
    outs = [o if idx in KV_SEGMENTS else o.reshape(b, t, -1) for idx, o in enumerate(outs)]
    (a_qkv, a_z, b_q, b_k, b_v, c_q, c_k, c_v, d_q, d_f, d_i, d_g, a_bg) = outs[:len(IN_WIDTHS)]

    o_a, a_s = _gdn(a_qkv, a_z, a_bg, prm["a_conv_w"][l], st["a_conv"], st["a_S"].astype(F32),
                    prm["alog_row"][l], prm["dtb_row"][l], prm["a_norm"][l])
    lam_init = 0.8 - 0.6 * math.exp(-0.3 * l)
    if has_past:
        o_b = _diff_attn(prm["lam"][l], b_q, b_k, b_v, st["b_k"], st["b_v"], l, prm["b_norm"][l], lam_init)
        o_c = _stick_attn(c_q, c_k, c_v, st["c_k"], st["c_v"], l)
    else:
        b_k_rows, c_k_rows = outs[len(IN_WIDTHS):]
        o_b = _diff_attn_prompt(prm["lam"][l], b_q, b_k_rows, b_v, prm["b_norm"][l], lam_init)
        o_c = _stick_attn_prompt(c_q, c_k_rows, c_v)
    o_d, d_s = _hgrn(d_q, d_f, d_i, d_g, prm["lb"][l], prm["d_norm"][l], st["d_S"].astype(F32))

    x2 = _cross(x, (o_a, o_b, o_c, o_d), prm["w_out"], prm["norm_cross"][l], prm["w_cq"], mk, mv, mem_layer,
                prm["w_co"], l)
    x2 = _ffn(x2.reshape(m, D_MODEL), prm["norm_ffn"][l], prm["w_gate"], prm["w_up"], prm["w_down"], l,
              final_gain=final_gain, tm=tm)

    if t >= CONV_W - 1:
        conv_new = a_qkv[:, t - (CONV_W - 1):, :]
    else:
        conv_new = jnp.concatenate([st["a_conv"], a_qkv], axis=1)[:, -(CONV_W - 1):, :]
    heads = lambda a: jnp.transpose(a.reshape(b, N_HEADS, HEAD_DIM, t), (0, 3, 1, 2))
    new = (conv_new, a_s, heads(b_k), heads(b_v), heads(c_k), heads(c_v), d_s)
    return x2.reshape(b, t, D_MODEL), new


def kernel(x_prompt, x_sample, mem_prompt, state_a_conv, state_a_S, cache_b_k, cache_b_v, cache_c_k, cache_c_v,
           state_d_S, cache_mem_k, cache_mem_v, norm_mix, w_in, a_conv_w, a_A_log, a_dt_bias, a_norm, b_lam_q1,
           b_lam_k1, b_lam_q2, b_lam_k2, b_norm, d_lb, d_norm, w_out, norm_cross, norm_memtok, w_cq, w_ck, w_cv,
           w_co, norm_ffn, w_gate, w_up, w_down, norm_final):
    depth = w_in.shape[0]
    n_main = 4 * GROUP_W
    w_in_p = jnp.concatenate(
        [w_in[:, :, :n_main], w_in[:, :, n_main + 2 * N_HEADS:], w_in[:, :, n_main:n_main + 2 * N_HEADS],
         jnp.zeros((depth, D_MODEL, ABG_W - 2 * N_HEADS), w_in.dtype)], axis=2).astype(BF16)
    pad_row = lambda a: jnp.pad(a.astype(F32), ((0, 0), (N_HEADS, ABG_W - 2 * N_HEADS)))[:, None, :]
    tile_row = lambda a: jnp.tile(a.astype(F32), (1, N_HEADS))[:, None, :]
    p_lb = jax.nn.softmax(d_lb.astype(F32), axis=0)
    lam = (jnp.exp(jnp.sum(b_lam_q1.astype(F32) * b_lam_k1.astype(F32), axis=-1))
           - jnp.exp(jnp.sum(b_lam_q2.astype(F32) * b_lam_k2.astype(F32), axis=-1))
           + jnp.array([0.8 - 0.6 * math.exp(-0.3 * l) for l in range(depth)], F32))
    prm = {
        "w_in": w_in_p, "norm_mix": norm_mix, "a_conv_w": a_conv_w.astype(F32),
        "alog_row": pad_row(a_A_log), "dtb_row": pad_row(a_dt_bias), "a_norm": tile_row(a_norm),
        "lam": lam[:, None], "b_norm": tile_row(b_norm), "lb": (jnp.cumsum(p_lb, axis=0) - p_lb[0])[:, None, :],
        "d_norm": tile_row(d_norm), "w_out": w_out.astype(BF16), "norm_cross": norm_cross,
        "w_cq": w_cq.astype(BF16), "w_co": w_co.astype(BF16), "norm_ffn": norm_ffn,
        "w_gate": w_gate.astype(BF16), "w_up": w_up.astype(BF16), "w_down": w_down.astype(BF16),
    }
    w_ckv = jnp.concatenate([w_ck, w_cv], axis=2).astype(BF16)

    bp, tp, _ = x_prompt.shape
    n_mem = mem_prompt.shape[1]
    h = x_prompt
    p_new = []
    for l in range(depth):
        mk, mv = _fused_linear(mem_prompt.reshape(bp * n_mem, D_MODEL), w_ckv, l, (D_MODEL, D_MODEL),
                               norm_memtok[l], math.gcd(bp * n_mem, ROW_TILE))
        mk = mk.reshape(1, bp, n_mem, D_MODEL)
        mv = mv.reshape(1, bp, n_mem, D_MODEL)
        st = {"a_conv": jnp.zeros((bp, CONV_W - 1, 3 * GROUP_W), F32),
              "a_S": jnp.zeros((bp, N_HEADS, HEAD_DIM, HEAD_DIM), F32), "b_k": None, "b_v": None,
              "c_k": None, "c_v": None, "d_S": jnp.zeros((bp, N_HEADS, HEAD_DIM, HEAD_DIM), F32)}
        h, new = _layer(h, l, st, mk, mv, 0, prm, norm_final if l == depth - 1 else None)
        mem4 = lambda a: a.reshape(bp, n_mem, MEM_HEADS, MEM_HEAD_DIM)
        p_new.append(new + (mem4(mk), mem4(mv)))
    y_prompt = h
    p_stacked = [jnp.stack(c) for c in zip(*p_new)]

    bs, ts, _ = x_sample.shape
    past = cache_b_k.shape[2]
    h = x_sample
    s_new = []
    flat = lambda a: jnp.transpose(a, (0, 1, 3, 4, 2)).reshape(depth, bs, GROUP_W, past)
    caches = {"b_k": flat(cache_b_k), "b_v": flat(cache_b_v), "c_k": flat(cache_c_k), "c_v": flat(cache_c_v)}
    for l in range(depth):
        st = {"a_conv": state_a_conv[l], "a_S": state_a_S[l], "d_S": state_d_S[l], **caches}
        h, new = _layer(h, l, st, cache_mem_k, cache_mem_v, l, prm, norm_final if l == depth - 1 else None)
        s_new.append(new)
    y_sample = h
    s_stacked = [jnp.stack(c) for c in zip(*s_new)]

    return (y_prompt, y_sample, *p_stacked, *s_stacked)
```

```python
import functools
import math

import jax
import jax.numpy as jnp
from jax import lax
from jax.experimental import pallas as pl
from jax.experimental.pallas import tpu as pltpu

F32 = jnp.float32
BF16 = jnp.bfloat16

D_MODEL = 1024
GROUP_W = 256
N_HEADS = 4
HEAD_DIM = 64
DIFF_HALF = 32
CHUNK = 64
CONV_W = 4
MEM_HEADS = 4
MEM_HEAD_DIM = 256
D_FF = 2816
HEAD_SHIFT = HEAD_DIM.bit_length() - 1
CHUNK_SHIFT = CHUNK.bit_length() - 1
EPS = 1e-6
NEG = -1e30
LOG2E = 1.4426950408889634
LANES = 128
ABG_W = 128
FF_CHUNK = 256
ROW_TILE = 512
SUM_ROWS = 16
DECODE_SEQS = 2
CROSS_SEQS = 4
STICK_DEAD = -160.0
STICK_SUB = 256
SEQ_PER_STEP = 4
VMEM_LIMIT = 56 * 1024 * 1024


def _cparams(*sem):
    return pltpu.CompilerParams(dimension_semantics=sem, vmem_limit_bytes=VMEM_LIMIT)


def _dot(a, b):
    return jnp.dot(a, b, preferred_element_type=F32)


def _dot_nt(a, b):
    return lax.dot_general(a, b, (((1,), (1,)), ((), ())), preferred_element_type=F32)


_NN = (((1,), (0,)), ((), ()))
_NT = (((1,), (1,)), ((), ()))
_BNN = (((2,), (1,)), ((0,), (0,)))


def _split(x):
    hi = x.astype(BF16)
    return hi, (x - hi.astype(F32)).astype(BF16)


def _bdot(a, b, dims):
    return lax.dot_general(a, b, dims, preferred_element_type=F32)


def _mm1(a, b, dims=_NN):
    return _bdot(a.astype(BF16), b.astype(BF16), dims)


def _mm_r(a, b16):
    ah, al = _split(a)
    return _bdot(ah, b16, _NN) + _bdot(al, b16, _NN)


def _mm_l(a16, b):
    bh, bl = _split(b)
    return _bdot(a16, bh, _NN) + _bdot(a16, bl, _NN)


def _rms(x, g):
    return x * lax.rsqrt(jnp.mean(x * x, axis=-1, keepdims=True) + EPS) * g


def _sigmoid(x):
    return 1.0 / (1.0 + jnp.exp(-x))


def _log_sigmoid(x):
    return jnp.minimum(x, 0.0) - jnp.log(1.0 + jnp.exp(-jnp.abs(x)))


def _softplus(x):
    return jnp.maximum(x, 0.0) + jnp.log(1.0 + jnp.exp(-jnp.abs(x)))


def _iota(shape, dim):
    return lax.broadcasted_iota(jnp.int32, shape, dim)


def _lanes(x, n):
    return x[:, :n] if n <= LANES else jnp.concatenate([x] * (n // LANES), axis=1)


def _head_block_ones(n):
    return ((_iota((n, n), 0) >> HEAD_SHIFT) == (_iota((n, n), 1) >> HEAD_SHIFT)).astype(F32)


def _resident(shape):
    return pl.BlockSpec(shape, lambda *_: (0,) * len(shape), pipeline_mode=pl.Buffered(1))


def _weight(w, layer):
    return pl.BlockSpec((None,) + w.shape[1:], lambda *_: (layer, 0, 0), pipeline_mode=pl.Buffered(1))


def _linear_body(x_ref, g_ref, w_ref, *out_refs, segs, feat_major, rows_too):
    xb = _rms(x_ref[...], g_ref[...]).astype(BF16)
    extra_refs = dict(zip(rows_too, out_refs[len(segs):]))
    for idx, (o_ref, (s, e)) in enumerate(zip(out_refs, segs)):
        acc = _dot(xb, w_ref[:, s:e])
        if idx in feat_major:
            rows = acc.shape[0] // o_ref.shape[0]
            for sq in range(o_ref.shape[0]):
                o_ref[sq] = acc[sq * rows:(sq + 1) * rows].T
        else:
            o_ref[...] = acc
        if idx in extra_refs:
            extra_refs[idx][...] = acc


def _fused_linear(x, w, layer, widths, gain, tm, feat_major=(), rows_per_seq=None, rows_too=()):
    m, k = x.shape
    n = w.shape[2]
    segs = []
    s = 0
    for wd in widths:
        segs.append((s, s + wd))
        s += wd
    assert s == n and m % tm == 0
    out_specs, out_shape = [], []
    for idx, wd in enumerate(widths):
        if idx in feat_major:
            if rows_per_seq >= tm:
                tiles = rows_per_seq // tm
                assert rows_per_seq % tm == 0
                out_specs.append(pl.BlockSpec((1, wd, tm), lambda i, tiles=tiles: (i // tiles, 0, i % tiles)))
            else:
                assert tm % rows_per_seq == 0
                out_specs.append(pl.BlockSpec((tm // rows_per_seq, wd, rows_per_seq), lambda i: (i, 0, 0)))
            out_shape.append(jax.ShapeDtypeStruct((m // rows_per_seq, wd, rows_per_seq), F32))
        else:
            out_specs.append(pl.BlockSpec((tm, wd), lambda i: (i, 0)))
            out_shape.append(jax.ShapeDtypeStruct((m, wd), F32))
    for idx in rows_too:
        out_specs.append(pl.BlockSpec((tm, widths[idx]), lambda i: (i, 0)))
        out_shape.append(jax.ShapeDtypeStruct((m, widths[idx]), F32))
    assert w.shape[1] == k
    return pl.pallas_call(
        functools.partial(_linear_body, segs=tuple(segs), feat_major=tuple(feat_major), rows_too=tuple(rows_too)),
        grid=(m // tm,),
        in_specs=[pl.BlockSpec((tm, k), lambda i: (i, 0)), _resident((1, k)), _weight(w, layer)],
        out_specs=out_specs,
        out_shape=out_shape,
        compiler_params=_cparams("parallel"),
    )(x, gain.reshape(1, -1), w)


def _ffn_body(*refs, final):
    if final:
        x_ref, g_ref, wg_ref, wu_ref, wd_ref, gf_ref, o_ref = refs
    else:
        x_ref, g_ref, wg_ref, wu_ref, wd_ref, o_ref = refs
    x = x_ref[...]
    hb = _rms(x, g_ref[...]).astype(BF16)
    acc = x
    for c in range(0, D_FF, FF_CHUNK):
        gate = _dot(hb, wg_ref[:, c:c + FF_CHUNK])
        up = _dot(hb, wu_ref[:, c:c + FF_CHUNK])
        act = (gate * _sigmoid(gate) * up).astype(BF16)
        acc = acc + _dot(act, wd_ref[c:c + FF_CHUNK, :])
    if final:
        acc = _rms(acc, gf_ref[...])
    o_ref[...] = acc


def _ffn(x, gain, wg, wu, wd, layer, final_gain=None, tm=256):
    m = x.shape[0]
    final = final_gain is not None
    in_specs = [pl.BlockSpec((tm, D_MODEL), lambda i: (i, 0)), _resident((1, D_MODEL)),
                _weight(wg, layer), _weight(wu, layer), _weight(wd, layer)]
    args = [x, gain.reshape(1, -1), wg, wu, wd]
    if final:
        in_specs.append(_resident((1, D_MODEL)))
        args.append(final_gain.reshape(1, -1))
    return pl.pallas_call(
        functools.partial(_ffn_body, final=final),
        grid=(m // tm,),
        in_specs=in_specs,
        out_specs=pl.BlockSpec((tm, D_MODEL), lambda i: (i, 0)),
        out_shape=jax.ShapeDtypeStruct((m, D_MODEL), F32),
        compiler_params=_cparams("parallel"),
    )(*args)


def _cross_body(x_ref, oa_ref, ob_ref, oc_ref, od_ref, wout_ref, g_ref, wq_ref, mk_ref, mv_ref, wo_ref, o_ref):
    ns, tq, _ = x_ref.shape
    rows = ns * tq
    x = x_ref[...].reshape(rows, D_MODEL)
    for gi, m_ref in enumerate((oa_ref, ob_ref, oc_ref, od_ref)):
        x = x + _dot(m_ref[...].reshape(rows, GROUP_W).astype(BF16), wout_ref[gi * GROUP_W:(gi + 1) * GROUP_W, :])
    hb = _rms(x, g_ref[...]).astype(BF16)
    q = (_dot(hb, wq_ref[...]) * MEM_HEAD_DIM ** -0.5).astype(BF16)
    head = lambda h: slice(h * MEM_HEAD_DIM, (h + 1) * MEM_HEAD_DIM)
    pairs = [(sq, h) for sq in range(ns) for h in range(MEM_HEADS)]
    if len(mk_ref.shape) == 4:
        mem = lambda ref, sq, h: ref[sq, :, h, :].astype(BF16)
    else:
        mem = lambda ref, sq, h: ref[sq, :, head(h)].astype(BF16)
    scores = [_dot_nt(q[sq * tq:(sq + 1) * tq, head(h)], mem(mk_ref, sq, h)) for sq, h in pairs]
    probs = []
    for s in scores:
        p = jnp.exp(s - jnp.max(s, axis=-1, keepdims=True))
        probs.append((p / jnp.sum(p, axis=-1, keepdims=True)).astype(BF16))
    outs = [_dot(p, mem(mv_ref, sq, h)).astype(BF16) for p, (sq, h) in zip(probs, pairs)]
    o = jnp.concatenate([jnp.concatenate(outs[sq * MEM_HEADS:(sq + 1) * MEM_HEADS], axis=1) for sq in range(ns)], axis=0)
    o_ref[...] = (x + _dot(o, wo_ref[...])).reshape(ns, tq, D_MODEL)


def _cross(x, mix, w_out, gain, wq, mk, mv, layer, wo, w_layer):
    b, t, _ = x.shape
    nm = mk.shape[2]
    tq = min(ROW_TILE, t)
    ns = CROSS_SEQS if CROSS_SEQS * t <= ROW_TILE else 1
    assert t % tq == 0 and b % ns == 0
    row_spec = lambda w: pl.BlockSpec((ns, tq, w), lambda i, j: (i, j, 0))
    mem_spec = pl.BlockSpec((None, ns) + mk.shape[2:], lambda i, j: (layer, i) + (0,) * (mk.ndim - 2))
    return pl.pallas_call(
        _cross_body,
        grid=(b // ns, t // tq),
        in_specs=[row_spec(D_MODEL)] + [row_spec(GROUP_W)] * 4 + [_weight(w_out, w_layer), _resident((1, D_MODEL)),
                  _weight(wq, w_layer), mem_spec, mem_spec, _weight(wo, w_layer)],
        out_specs=row_spec(D_MODEL),
        out_shape=jax.ShapeDtypeStruct((b, t, D_MODEL), F32),
        compiler_params=_cparams("parallel", "parallel"),
    )(x, *mix, w_out, gain.reshape(1, -1), wq, mk, mv, wo)


def _diff_body(lam_ref, q_ref, kn_ref, vn_ref, kp_ref, vp_ref, bn_ref, o_ref, q8_sc, m_sc, l_sc, acc_sc, *,
               tq, tk, past, lam_init, ns):
    ng = 2 * N_HEADS
    lane = _iota((tq, GROUP_W), 1)
    for sq in range(ns):
        q = q_ref[sq] * (DIFF_HALF ** -0.5 * LOG2E)
        for g in range(ng):
            lo = (g // 2) * HEAD_DIM + (g % 2) * DIFF_HALF
            q8_sc[sq, g * tq:(g + 1) * tq, :] = jnp.where((lane >= lo) & (lane < lo + DIFF_HALF), q, 0.0).astype(BF16)
    m_sc[...] = jnp.full(m_sc.shape, -jnp.inf, F32)
    l_sc[...] = jnp.zeros(l_sc.shape, F32)
    acc_sc[...] = jnp.zeros(acc_sc.shape, F32)
    pq = past + _iota((tq, 1), 0)
    pq_f = pq.astype(F32)
    lanes = _lanes
    pairs = [(sq, g) for sq in range(ns) for g in range(ng)]

    def block(k_ref, v_ref, k0, n, near, pos0=0):
        pk = pos0 + k0 + _iota((1, n), 1)
        pk_f = pk.astype(F32)
        if near:
            allowed = (pk >> CHUNK_SHIFT) <= (pq >> CHUNK_SHIFT)
            shift = pq_f - jnp.abs(pq_f - pk_f)
        s_all = [_dot(q8_sc[sq], k_ref[sq, :, pl.ds(k0, n)].astype(BF16)) for sq in range(ns)]
        scores = {}
        for sq, g in pairs:
            slope = 2.0 ** (-2 * (g // 2 + 1)) * LOG2E
            s = s_all[sq][g * tq:(g + 1) * tq]
            scores[sq, g] = jnp.where(allowed, s + slope * shift, NEG) if near else s + slope * pk_f
        m_prevs = {k: m_sc[k[0], k[1]] for k in pairs}
        m_news = {k: jnp.maximum(m_prevs[k], jnp.max(scores[k], axis=-1, keepdims=True)) for k in pairs}
        probs = {k: jnp.exp2(scores[k] - lanes(m_news[k], n)) for k in pairs}
        alphas = {k: jnp.exp2(m_prevs[k] - m_news[k]) for k in pairs}
        for k in pairs:
            l_sc[k[0], k[1]] = alphas[k] * l_sc[k[0], k[1]] + jnp.sum(probs[k], axis=-1, keepdims=True)
            m_sc[k[0], k[1]] = m_news[k]
        pvs = [_dot_nt(jnp.concatenate([probs[sq, g].astype(BF16) for g in range(ng)], axis=0),
                       v_ref[sq, :, pl.ds(k0, n)].astype(BF16)) for sq in range(ns)]
        for sq, g in pairs:
            acc_sc[sq, g] = acc_sc[sq, g] * lanes(alphas[sq, g], GROUP_W) + pvs[sq][g * tq:(g + 1) * tq]

    def past_step(j, carry):
        block(kp_ref, vp_ref, pl.multiple_of(j * tk, tk), tk, False)
        return carry
    lax.fori_loop(0, past // tk, past_step, 0)
    block(kn_ref, vn_ref, 0, tq, True, pos0=past)

    lam = lam_ref[0]
    lane_head = _iota((1, GROUP_W), 1) >> HEAD_SHIFT
    ones16 = _head_block_ones(GROUP_W).astype(BF16)
    for sq in range(ns):
        o = jnp.zeros((tq, GROUP_W), F32)
        for h in range(N_HEADS):
            o0 = acc_sc[sq, 2 * h] / lanes(l_sc[sq, 2 * h], GROUP_W)
            o1 = acc_sc[sq, 2 * h + 1] / lanes(l_sc[sq, 2 * h + 1], GROUP_W)
            o = jnp.where(lane_head == h, o0 - lam * o1, o)
        ms = _mm_r(o * o, ones16) * (1.0 / HEAD_DIM)
        o_ref[sq] = o * lax.rsqrt(ms + EPS) * bn_ref[...] * (1.0 - lam_init)


def _diff_attn(lam, q, kn, vn, kp, vp, layer, bnorm, lam_init):
    b, t, _ = q.shape
    past = kp.shape[3]
    ng = 2 * N_HEADS
    ns = DECODE_SEQS
    tq, tk = t, math.gcd(past, 512)
    assert t == CHUNK and past % CHUNK == 0 and tk % LANES == 0 and b % ns == 0
    seq_spec = pl.BlockSpec((ns, GROUP_W, t), lambda i: (i, 0, 0))
    past_spec = pl.BlockSpec((None, ns, GROUP_W, past), lambda i: (layer, i, 0, 0))
    return pl.pallas_call(
        functools.partial(_diff_body, tq=tq, tk=tk, past=past, lam_init=lam_init, ns=ns),
        grid=(b // ns,),
        in_specs=[pl.BlockSpec(memory_space=pltpu.SMEM), pl.BlockSpec((ns, tq, GROUP_W), lambda i: (i, 0, 0)),
                  seq_spec, seq_spec, past_spec, past_spec, _resident((1, GROUP_W))],
        out_specs=pl.BlockSpec((ns, tq, GROUP_W), lambda i: (i, 0, 0)),
        out_shape=jax.ShapeDtypeStruct((b, t, GROUP_W), F32),
        scratch_shapes=[pltpu.VMEM((ns, ng * tq, GROUP_W), BF16), pltpu.VMEM((ns, ng, tq, LANES), F32),
                        pltpu.VMEM((ns, ng, tq, LANES), F32), pltpu.VMEM((ns, ng, tq, GROUP_W), F32)],
        compiler_params=_cparams("parallel"),
    )(lam, q, kn, vn, kp, vp, bnorm)


def _diff_prompt_body(lam_ref, q_ref, k_ref, v_ref, bn_ref, o_ref, qt_sc, m_sc, acc_sc, *, tq, tk, lam_init):
    i = pl.program_id(1)
    ng = 2 * N_HEADS
    qt = (q_ref[0] * (DIFF_HALF ** -0.5 * LOG2E)).T
    feat = _iota((GROUP_W, tq), 0)
    for g in range(ng):
        lo = (g // 2) * HEAD_DIM + (g % 2) * DIFF_HALF
        qt_sc[g] = jnp.where((feat >= lo) & (feat < lo + DIFF_HALF), qt, 0.0).astype(BF16)
    m_sc[...] = jnp.full(m_sc.shape, -jnp.inf, F32)
    acc_sc[...] = jnp.zeros(acc_sc.shape, F32)
    pq = i * tq + _iota((1, tq), 1)
    pq_f = pq.astype(F32)
    ones_rows = jnp.ones((SUM_ROWS, tk), BF16)

    def block(k0, n, near):
        kb = k_ref[0, pl.ds(k0, n), :].astype(BF16)
        vb = v_ref[0, :, pl.ds(k0, n)].astype(BF16)
        pk = k0 + _iota((n, LANES), 0)
        pk_f = pk.astype(F32)
        if near:
            allowed = (_lanes(pk, tq) >> CHUNK_SHIFT) <= (pq >> CHUNK_SHIFT)
            shift = pq_f - jnp.abs(pq_f - _lanes(pk_f, tq))
        groups = range(ng)
        scores = []
        for g in groups:
            slope = 2.0 ** (-2 * (g // 2 + 1)) * LOG2E
            s = _dot(kb, qt_sc[g])
            scores.append(jnp.where(allowed, s + slope * shift, NEG) if near else s + _lanes(slope * pk_f, tq))
        m_prevs = [m_sc[g] for g in groups]
        m_news = [jnp.maximum(m_prevs[g], jnp.max(scores[g], axis=0, keepdims=True)) for g in groups]
        probs = [jnp.exp2(scores[g] - m_news[g]) for g in groups]
        for g in groups:
            alpha = jnp.exp2(m_prevs[g] - m_news[g])
            m_sc[g] = m_news[g]
            h = g // 2
            pv = _dot(jnp.concatenate([vb[h * HEAD_DIM:(h + 1) * HEAD_DIM, :], ones_rows[:, :n]], axis=0),
                      probs[g].astype(BF16))
            acc_sc[g] = acc_sc[g] * alpha + pv

    def prev_step(j, carry):
        block(pl.multiple_of(j * tk, tk), tk, False)
        return carry
    lax.fori_loop(0, (i * tq) // tk, prev_step, 0)

    @pl.when((i * tq) % tk != 0)
    def _():
        block(pl.multiple_of((i - 1) * tq, tq), tq, False)
    block(pl.multiple_of(i * tq, tq), tq, True)

    lam = lam_ref[0]
    norm = lambda g: acc_sc[g, :HEAD_DIM, :] / acc_sc[g, HEAD_DIM:HEAD_DIM + 1, :]
    heads = [norm(2 * h) - lam * norm(2 * h + 1) for h in range(N_HEADS)]
    o = jnp.concatenate(heads, axis=0).T
    ms = _mm_r(o * o, _head_block_ones(GROUP_W).astype(BF16)) * (1.0 / HEAD_DIM)
    o_ref[0] = o * lax.rsqrt(ms + EPS) * bn_ref[...] * (1.0 - lam_init)


def _diff_attn_prompt(lam, q, k_rows, v_feat, bnorm, lam_init):
    b, t, _ = q.shape
    ng = 2 * N_HEADS
    tq = min(256, t)
    tk = 2 * tq if t % (2 * tq) == 0 else tq
    assert t % tq == 0 and tq % LANES == 0
    return pl.pallas_call(
        functools.partial(_diff_prompt_body, tq=tq, tk=tk, lam_init=lam_init),
        grid=(b, t // tq),
        in_specs=[pl.BlockSpec(memory_space=pltpu.SMEM), pl.BlockSpec((1, tq, GROUP_W), lambda i, j: (i, j, 0)),
                  pl.BlockSpec((1, t, GROUP_W), lambda i, j: (i, 0, 0)),
                  pl.BlockSpec((1, GROUP_W, t), lambda i, j: (i, 0, 0)), _resident((1, GROUP_W))],
        out_specs=pl.BlockSpec((1, tq, GROUP_W), lambda i, j: (i, j, 0)),
        out_shape=jax.ShapeDtypeStruct((b, t, GROUP_W), F32),
        scratch_shapes=[pltpu.VMEM((ng, GROUP_W, tq), BF16), pltpu.VMEM((ng, 1, tq), F32),
                        pltpu.VMEM((ng, HEAD_DIM + SUM_ROWS, tq), F32)],
        compiler_params=_cparams("parallel", "parallel"),
    )(lam, q, k_rows, v_feat, bnorm)


def _stick_body(*refs, tq, tk, past, ns, near):
    if near:
        q_ref, kn_ref, vn_ref, kp_ref, vp_ref, o_ref, acc_out, c_out, q4_sc, c_sc, acc_sc = refs
    else:
        q_ref, kp_ref, vp_ref, acc_in, c_in, o_ref, q4_sc, c_sc, acc_sc = refs
    nr = N_HEADS * tq
    lane = _iota((tq, GROUP_W), 1) >> HEAD_SHIFT
    for sq in range(ns):
        q = q_ref[sq] * (HEAD_DIM ** -0.5 * LOG2E)
        for h in range(N_HEADS):
            q4_sc[sq, h * tq:(h + 1) * tq, :] = jnp.where(lane == h, q, 0.0).astype(BF16)
    if near:
        c_sc[...] = jnp.zeros(c_sc.shape, F32)
        acc_sc[...] = jnp.zeros(acc_sc.shape, F32)
    else:
        c_sc[...] = c_in[...]
        acc_sc[...] = acc_in[...]
    pq = past + (_iota((nr, 1), 0) & (tq - 1))
    laters = {w: (_iota((w, w), 0) > _iota((w, w), 1)).astype(BF16)
              for w in {min(tq, STICK_SUB), min(tk, STICK_SUB)}}
    seqs = range(ns)

    def block(k_ref, v_ref, k0, n, masked, pos0=0):
        sub = min(n, STICK_SUB)
        later = laters[sub]
        zs = [_dot(q4_sc[sq], k_ref[sq, :, pl.ds(k0, n)].astype(BF16)) for sq in seqs]
        lss = [jnp.minimum(z, 0.0) - jnp.log2(1.0 + jnp.exp2(-jnp.abs(z))) for z in zs]
        lks = [ls - z for ls, z in zip(lss, zs)]
        if masked:
            mask = (pos0 + k0 + _iota((1, n), 1)) < pq
            lks = [jnp.where(mask, lk, 0.0) for lk in lks]
        his = [lk.astype(BF16) for lk in lks]
        los = [(lk - hi.astype(F32)).astype(BF16) for lk, hi in zip(lks, his)]
        carries = [c_sc[sq] for sq in seqs]
        parts = [[] for _ in seqs]
        for sb in reversed(range(n // sub)):
            sl = slice(sb * sub, (sb + 1) * sub)
            for sq in seqs:
                after = _dot(his[sq][:, sl], later) + _dot(los[sq][:, sl], later)
                parts[sq].append(lss[sq][:, sl] + after + _lanes(carries[sq], sub))
                carries[sq] = carries[sq] + (after[:, 0:1] + lks[sq][:, sb * sub:sb * sub + 1])
        for sq in seqs:
            c_sc[sq] = carries[sq]
            e = parts[sq][0] if len(parts[sq]) == 1 else jnp.concatenate(parts[sq][::-1], axis=1)
            if masked:
                a = jnp.where(mask, jnp.exp2(jnp.where(mask, e, 0.0)), 0.0)
            else:
                a = jnp.exp2(e)
            acc_sc[sq] = acc_sc[sq] + _dot_nt(a.astype(BF16), v_ref[sq, :, pl.ds(k0, n)].astype(BF16))

    if near:
        block(kn_ref, vn_ref, 0, tq, True, pos0=past)
        block(kp_ref, vp_ref, 0, tk, False)
        acc_out[...] = acc_sc[...]
        c_out[...] = c_sc[...]
    else:
        nb = past // tk - 1

        def live():
            return jnp.max(c_sc[...]) > STICK_DEAD

        def past_step(state):
            j, _ = state
            block(kp_ref, vp_ref, pl.multiple_of((nb - 1 - j) * tk, tk), tk, False)
            return j + 1, live()
        lax.while_loop(lambda state: (state[0] < nb) & state[1], past_step, (jnp.int32(0), live()))

    for sq in seqs:
        o = acc_sc[sq, 0:tq, :]
        for h in range(1, N_HEADS):
            o = jnp.where(lane == h, acc_sc[sq, h * tq:(h + 1) * tq, :], o)
        o_ref[sq] = o


def _stick_attn(q, kn, vn, kp, vp, layer):
    b, t, _ = q.shape
    past = kp.shape[3]
    ns = DECODE_SEQS
    tq, tk = t, math.gcd(past, 512)
    assert tq & (tq - 1) == 0 and tk % min(tk, STICK_SUB) == 0 and b % ns == 0
    nr = N_HEADS * tq
    row_spec = lambda w: pl.BlockSpec((ns, tq, w), lambda i: (i, 0, 0))
    seq_spec = pl.BlockSpec((ns, GROUP_W, t), lambda i: (i, 0, 0))
    acc_spec = pl.BlockSpec((ns, nr, GROUP_W), lambda i: (i, 0, 0))
    carry_spec = pl.BlockSpec((ns, nr, LANES), lambda i: (i, 0, 0))
    acc_shape = jax.ShapeDtypeStruct((b, nr, GROUP_W), F32)
    carry_shape = jax.ShapeDtypeStruct((b, nr, LANES), F32)
    scratch = [pltpu.VMEM((ns, nr, GROUP_W), BF16), pltpu.VMEM((ns, nr, LANES), F32), pltpu.VMEM((ns, nr, GROUP_W), F32)]
    last = past // tk - 1
    latest_spec = pl.BlockSpec((None, ns, GROUP_W, tk), lambda i: (layer, i, 0, last))
    out, acc, carry = pl.pallas_call(
        functools.partial(_stick_body, tq=tq, tk=tk, past=past, ns=ns, near=True),
        grid=(b // ns,),
        in_specs=[row_spec(GROUP_W), seq_spec, seq_spec, latest_spec, latest_spec],
        out_specs=[row_spec(GROUP_W), acc_spec, carry_spec],
        out_shape=[jax.ShapeDtypeStruct((b, t, GROUP_W), F32), acc_shape, carry_shape],
        scratch_shapes=scratch,
        compiler_params=_cparams("parallel"),
    )(q, kn, vn, kp, vp)
    if last == 0:
        return out

    def earlier_keys():
        past_spec = pl.BlockSpec((None, ns, GROUP_W, past), lambda i: (layer, i, 0, 0))
        return pl.pallas_call(
            functools.partial(_stick_body, tq=tq, tk=tk, past=past, ns=ns, near=False),
            grid=(b // ns,),
            in_specs=[row_spec(GROUP_W), past_spec, past_spec, acc_spec, carry_spec],
            out_specs=row_spec(GROUP_W),
            out_shape=jax.ShapeDtypeStruct((b, t, GROUP_W), F32),
            scratch_shapes=scratch,
            compiler_params=_cparams("parallel"),
        )(q, kp, vp, acc, carry)

    return lax.cond(jnp.max(carry) > STICK_DEAD, earlier_keys, lambda: out)


def _stick_prompt_body(q_ref, k_ref, v_ref, o_ref, qt_sc, c_sc, acc_sc, *, tq, tk):
    i = pl.program_id(1)
    qt = (q_ref[0] * (HEAD_DIM ** -0.5 * LOG2E)).T
    feat_head = _iota((GROUP_W, tq), 0) >> HEAD_SHIFT
    for h in range(N_HEADS):
        qt_sc[h] = jnp.where(feat_head == h, qt, 0.0).astype(BF16)
    c_sc[...] = jnp.zeros(c_sc.shape, F32)
    acc_sc[...] = jnp.zeros(acc_sc.shape, F32)
    pq = i * tq + _iota((1, tq), 1)
    sub = min(tq, STICK_SUB)
    later = (_iota((sub, sub), 1) > _iota((sub, sub), 0)).astype(BF16)

    def block(k0, n, masked):
        kb = k_ref[0, pl.ds(k0, n), :].astype(BF16)
        vb = v_ref[0, :, pl.ds(k0, n)].astype(BF16)
        if masked:
            mask = _lanes(k0 + _iota((n, LANES), 0), tq) < pq
        heads = range(N_HEADS)
        zs = [_dot(kb, qt_sc[h]) for h in heads]
        lss = [jnp.minimum(z, 0.0) - jnp.log2(1.0 + jnp.exp2(-jnp.abs(z))) for z in zs]
        lks = [ls - z for ls, z in zip(lss, zs)]
        if masked:
            lks = [jnp.where(mask, lk, 0.0) for lk in lks]
        es = []
        for h in heads:
            lk = lks[h]
            hi = lk.astype(BF16)
            lo = (lk - hi.astype(F32)).astype(BF16)
            carry = c_sc[h]
            parts = []
            for sb in reversed(range(n // sub)):
                sl = slice(sb * sub, (sb + 1) * sub)
                after = _dot(later, hi[sl]) + _dot(later, lo[sl])
                parts.append(lss[h][sl] + after + carry)
                carry = carry + (after[0:1] + lk[sb * sub:sb * sub + 1])
            c_sc[h] = carry
            es.append(parts[0] if len(parts) == 1 else jnp.concatenate(parts[::-1], axis=0))
        for h in heads:
            if masked:
                a = jnp.where(mask, jnp.exp2(jnp.where(mask, es[h], 0.0)), 0.0)
            else:
                a = jnp.exp2(es[h])
            acc_sc[h] = acc_sc[h] + _dot(vb[h * HEAD_DIM:(h + 1) * HEAD_DIM, :], a.astype(BF16))

    block(pl.multiple_of(i * tq, tq), tq, True)

    @pl.when((i * tq) % tk != 0)
    def _():
        block(pl.multiple_of((i - 1) * tq, tq), tq, False)
    nfull = (i * tq) // tk

    def live():
        return jnp.max(c_sc[...]) > STICK_DEAD

    def prev_step(state):
        j, _ = state
        block(pl.multiple_of((nfull - 1 - j) * tk, tk), tk, False)
        return j + 1, live()
    lax.while_loop(lambda state: (state[0] < nfull) & state[1], prev_step, (jnp.int32(0), live()))

    o_ref[0] = jnp.concatenate([acc_sc[h] for h in range(N_HEADS)], axis=0).T


def _stick_attn_prompt(q, k_rows, v_feat):
    b, t, _ = q.shape
    tq = min(256, t)
    tk = 2 * tq if t % (2 * tq) == 0 else tq
    assert t % tq == 0 and tq % LANES == 0 and tk % min(tk, STICK_SUB) == 0
    return pl.pallas_call(
        functools.partial(_stick_prompt_body, tq=tq, tk=tk),
        grid=(b, t // tq),
        in_specs=[pl.BlockSpec((1, tq, GROUP_W), lambda i, j: (i, j, 0)),
                  pl.BlockSpec((1, t, GROUP_W), lambda i, j: (i, 0, 0)),
                  pl.BlockSpec((1, GROUP_W, t), lambda i, j: (i, 0, 0))],
        out_specs=pl.BlockSpec((1, tq, GROUP_W), lambda i, j: (i, j, 0)),
        out_shape=jax.ShapeDtypeStruct((b, t, GROUP_W), F32),
        scratch_shapes=[pltpu.VMEM((N_HEADS, GROUP_W, tq), BF16), pltpu.VMEM((N_HEADS, 1, tq), F32),
                        pltpu.VMEM((N_HEADS, HEAD_DIM, tq), F32)],
        compiler_params=_cparams("parallel", "parallel"),
    )(q, k_rows, v_feat)


def _pair_mask(n, s, reps=1):
    r, c = _iota((reps * n, n), 0) & (n - 1), _iota((reps * n, n), 1)
    return ((r // (2 * s)) == (c // (2 * s))) & (((r // s) & 1) == 1) & (((c // s) & 1) == 0)


def _state_spec(nb):
    return pl.BlockSpec((nb, N_HEADS, HEAD_DIM, HEAD_DIM), lambda i, j: (i, 0, 0, 0))


def _load_state(s_sc, s0_ref, transpose):
    s_sc[...] = jnp.zeros(s_sc.shape, F32)
    for b in range(s_sc.shape[0]):
        for h in range(N_HEADS):
            blk = s0_ref[b, h]
            s_sc[b, h * HEAD_DIM:(h + 1) * HEAD_DIM, h * HEAD_DIM:(h + 1) * HEAD_DIM] = blk.T if transpose else blk


def _store_state(sout_ref, s_sc, transpose):
    for b in range(s_sc.shape[0]):
        for h in range(N_HEADS):
            blk = s_sc[b, h * HEAD_DIM:(h + 1) * HEAD_DIM, h * HEAD_DIM:(h + 1) * HEAD_DIM]
            sout_ref[b, h] = blk.T if transpose else blk


def _head_rows(x, lane_head):
    return jnp.concatenate([jnp.where(lane_head == h, x, 0.0) for h in range(N_HEADS)], axis=0)


def _head_diag(x, lane_head):
    n = x.shape[0] // N_HEADS
    out = x[:n]
    for h in range(1, N_HEADS):
        out = jnp.where(lane_head == h, x[h * n:(h + 1) * n], out)
    return out


def _gdn_body(x_ref, z_ref, abg_ref, cw_ref, cs_ref, s0_ref, alog_ref, dtb_ref, an_ref,
              o_ref, sout_ref, s_sc, xb_sc, *, nc, nb):
    c = pl.program_id(1)
    L = CHUNK
    pad = 8

    @pl.when(c == 0)
    def _():
        _load_state(s_sc, s0_ref, transpose=False)
        xb_sc[:, pad - 3:pad, :] = cs_ref[...]

    ones_bd = _head_block_ones(GROUP_W)
    ones_bd16 = ones_bd.astype(BF16)
    er, ec = _iota((ABG_W, GROUP_W), 0), _iota((ABG_W, GROUP_W), 1) >> HEAD_SHIFT
    e_beta = (er == ec).astype(BF16)
    e_g = (er == ec + N_HEADS).astype(BF16)
    ri, ci = _iota((L, L), 0), _iota((L, L), 1)
    tri16 = (ri >= ci).astype(BF16)
    incl = ri >= ci
    strict = ri > ci
    eye = (ri == ci).astype(F32)
    pair_masks = [_pair_mask(L, sz) for sz in (1, 2, 4, 8, 16, 32)]
    lane_head = _iota((1, GROUP_W), 1) >> HEAD_SHIFT
    cw = cw_ref[...]

    seqs = range(nb)
    us = []
    for b in seqs:
        x = x_ref[b]
        xb_sc[b, pad:pad + L, :] = x
        u = (x * cw[3:4] + xb_sc[b, pad - 1:pad - 1 + L, :] * cw[2:3]
             + xb_sc[b, pad - 2:pad - 2 + L, :] * cw[1:2] + xb_sc[b, pad - 3:pad - 3 + L, :] * cw[0:1])
        xb_sc[b, pad - 3:pad, :] = x[L - 3:L, :]
        us.append(u * _sigmoid(u))
    qs = [u[:, :GROUP_W] for u in us]
    ks = [u[:, GROUP_W:2 * GROUP_W] for u in us]
    vs = [u[:, 2 * GROUP_W:] for u in us]
    qs = [q * lax.rsqrt(_mm_r(q * q, ones_bd16) + EPS) * HEAD_DIM ** -0.5 for q in qs]
    ks = [k * lax.rsqrt(_mm_r(k * k, ones_bd16) + EPS) for k in ks]

    abgs = [abg_ref[b] for b in seqs]
    betas = [_mm_r(_sigmoid(abg), e_beta) for abg in abgs]
    gc_ns = [_mm_l(tri16, -jnp.exp(alog_ref[...]) * _softplus(abg + dtb_ref[...])) for abg in abgs]
    gc_ts = [gc_n.T for gc_n in gc_ns]
    gcs = [_mm_r(gc_n, e_g) for gc_n in gc_ns]

    kbs = [k * beta for k, beta in zip(ks, betas)]
    k16s = [k.astype(BF16) for k in ks]
    a4s = [_bdot(_head_rows(kb, lane_head).astype(BF16), k16, _NT) for kb, k16 in zip(kbs, k16s)]
    qk4s = [_bdot(_head_rows(q, lane_head).astype(BF16), k16, _NT) for q, k16 in zip(qs, k16s)]
    decs = []
    for b in seqs:
        dec = []
        for h in range(N_HEADS):
            diff = gc_ns[b][:, N_HEADS + h:N_HEADS + h + 1] - gc_ts[b][N_HEADS + h:N_HEADS + h + 1, :]
            dec.append(jnp.where(incl, jnp.exp(jnp.where(incl, diff, 0.0)), 0.0))
        decs.append(jnp.stack(dec))

    a_all = jnp.concatenate([jnp.where(strict, a4s[b].reshape(N_HEADS, L, L) * decs[b], 0.0) for b in seqs], axis=0)
    t_inv = eye - jnp.where(pair_masks[0], a_all, 0.0)
    a16 = a_all.astype(BF16)
    for pm in pair_masks[1:]:
        t16 = t_inv.astype(BF16)
        t_inv = t_inv - _bdot(t16, _bdot(jnp.where(pm, a16, 0.0), t16, _BNN).astype(BF16), _BNN)

    exp_gs = [jnp.exp(gc) for gc in gcs]
    s_olds = [s_sc[b] for b in seqs]
    s16s = [s.astype(BF16) for s in s_olds]
    rhss = [vs[b] * betas[b] - _bdot((kbs[b] * exp_gs[b]).astype(BF16), s16s[b], _NN) for b in seqs]
    ws = [_head_diag(_mm1(t_inv[b * N_HEADS:(b + 1) * N_HEADS].reshape(N_HEADS * L, L), rhss[b]), lane_head)
          for b in seqs]
    outs = [_bdot((qs[b] * exp_gs[b]).astype(BF16), s16s[b], _NN)
            + _head_diag(_mm1((qk4s[b].reshape(N_HEADS, L, L) * decs[b]).reshape(N_HEADS * L, L), ws[b]), lane_head)
            for b in seqs]
    for b in seqs:
        g_last = gcs[b][L - 1:L, :]
        k_dec = ks[b] * jnp.exp(g_last - gcs[b])
        s_sc[b] = s_olds[b] * jnp.exp(g_last) + _mm1(k_dec.T, ws[b]) * ones_bd
    for b in seqs:
        o = outs[b]
        ms = _mm_r(o * o, ones_bd16) * (1.0 / HEAD_DIM)
        zg = z_ref[b]
        o_ref[b] = o * lax.rsqrt(ms + EPS) * an_ref[...] * (zg * _sigmoid(zg))

    @pl.when(c == nc - 1)
    def _():
        _store_state(sout_ref, s_sc, transpose=False)


def _gdn(aqkv, az, abg, conv_w, conv_state, s0, alog_row, dtb_row, anorm_row):
    b, t, _ = aqkv.shape
    nc = t // CHUNK
    nb = SEQ_PER_STEP
    assert b % nb == 0
    cmap = lambda i, j: (i, j, 0)
    bmap = lambda i, j: (i, 0, 0)
    return pl.pallas_call(
        functools.partial(_gdn_body, nc=nc, nb=nb),
        grid=(b // nb, nc),
        in_specs=[pl.BlockSpec((nb, CHUNK, 3 * GROUP_W), cmap), pl.BlockSpec((nb, CHUNK, GROUP_W), cmap),
                  pl.BlockSpec((nb, CHUNK, ABG_W), cmap), _resident((CONV_W, 3 * GROUP_W)),
                  pl.BlockSpec((nb, CONV_W - 1, 3 * GROUP_W), bmap), _state_spec(nb),
                  _resident((1, ABG_W)), _resident((1, ABG_W)), _resident((1, GROUP_W))],
        out_specs=[pl.BlockSpec((nb, CHUNK, GROUP_W), cmap), _state_spec(nb)],
        out_shape=[jax.ShapeDtypeStruct((b, t, GROUP_W), F32), jax.ShapeDtypeStruct(s0.shape, F32)],
        scratch_shapes=[pltpu.VMEM((nb, GROUP_W, GROUP_W), F32), pltpu.VMEM((nb, 8 + CHUNK, 3 * GROUP_W), F32)],
        compiler_params=_cparams("parallel", "arbitrary"),
    )(aqkv, az, abg, conv_w, conv_state, s0, alog_row, dtb_row, anorm_row)


def _hgrn_body(q_ref, f_ref, i_ref, g_ref, lb_ref, dn_ref, s0_ref, o_ref, sout_ref, s_sc, *, nc, nb):
    c = pl.program_id(1)
    L = CHUNK

    @pl.when(c == 0)
    def _():
        _load_state(s_sc, s0_ref, transpose=True)

    lb = lb_ref[...]
    ones_bd = _head_block_ones(GROUP_W)
    ones_bd16 = ones_bd.astype(BF16)
    ri, ci = _iota((L, L), 0), _iota((L, L), 1)
    tri16 = (ri >= ci).astype(BF16)
    sizes = (1, 2, 4, 8, 16, 32)
    sel16 = jnp.concatenate([(ci == (ri // (2 * sz)) * (2 * sz) + sz).astype(BF16) for sz in sizes], axis=0)
    eye4 = (_iota((N_HEADS * L, L), 0) & (L - 1)) == _iota((N_HEADS * L, L), 1)
    pair_masks4 = [_pair_mask(L, sz, reps=N_HEADS) for sz in sizes]
    lane_head = _iota((1, GROUP_W), 1) >> HEAD_SHIFT
    rows = _iota((L, 1), 0)

    pre = []
    for b in range(nb):
        fl = f_ref[b]
        log_f = _log_sigmoid(fl) + jnp.log(1.0 + lb * jnp.exp(-fl))
        k = (1.0 - lb) * _sigmoid(-fl)
        q = q_ref[b] * HEAD_DIM ** -0.5
        bc = _mm_l(tri16, log_f)
        pre.append((q, k, bc))
    b_refs = [_mm_l(sel16, bc) for _, _, bc in pre]
    a_all = [jnp.where(eye4, _mm1(_head_rows(q, lane_head), k, _NT), 0.0) for q, k, _ in pre]
    for lv, sz in enumerate(sizes):
        later = ((rows // sz) & 1) == 1
        for b in range(nb):
            q, k, bc = pre[b]
            b_ref = b_refs[b][lv * L:(lv + 1) * L]
            x = jnp.where(later, q * jnp.exp(jnp.where(later, bc - b_ref, 0.0)), 0.0)
            y = jnp.where(later, 0.0, k * jnp.exp(jnp.where(later, 0.0, b_ref - bc)))
            a_all[b] = a_all[b] + jnp.where(pair_masks4[lv], _mm1(_head_rows(x, lane_head), y, _NT), 0.0)

    for b in range(nb):
        q, k, bc = pre[b]
        v = i_ref[b]
        st = s_sc[b]
        o = _mm1(q * jnp.exp(bc), st, _NT) + _head_diag(_mm1(a_all[b], v), lane_head)
        b_last = bc[L - 1:L, :]
        st_new = st * jnp.exp(b_last) + _mm1(v.T, k * jnp.exp(b_last - bc)) * ones_bd
        s_sc[b] = st_new

        ms = _mm_r(o * o, ones_bd16) * (1.0 / HEAD_DIM)
        zg = g_ref[b]
        o_ref[b] = o * lax.rsqrt(ms + EPS) * dn_ref[...] * (zg * _sigmoid(zg))

    @pl.when(c == nc - 1)
    def _():
        _store_state(sout_ref, s_sc, transpose=True)


def _hgrn(dq, df, di, dg, lb_row, dnorm_row, s0):
    b, t, _ = dq.shape
    nc = t // CHUNK
    nb = SEQ_PER_STEP
    assert b % nb == 0
    cmap = lambda i, j: (i, j, 0)
    bmap = lambda i, j: (i, 0, 0)
    cspec = pl.BlockSpec((nb, CHUNK, GROUP_W), cmap)
    return pl.pallas_call(
        functools.partial(_hgrn_body, nc=nc, nb=nb),
        grid=(b // nb, nc),
        in_specs=[cspec, cspec, cspec, cspec, _resident((1, GROUP_W)), _resident((1, GROUP_W)),
                  _state_spec(nb)],
        out_specs=[cspec, _state_spec(nb)],
        out_shape=[jax.ShapeDtypeStruct((b, t, GROUP_W), F32), jax.ShapeDtypeStruct(s0.shape, F32)],
        scratch_shapes=[pltpu.VMEM((nb, GROUP_W, GROUP_W), F32)],
        compiler_params=_cparams("parallel", "arbitrary"),
    )(dq, df, di, dg, lb_row, dnorm_row, s0)


IN_WIDTHS = (3 * GROUP_W,) + (GROUP_W,) * 11 + (ABG_W,)
KV_SEGMENTS = (3, 4, 6, 7)
K_SEGMENTS = (3, 6)


def _layer(x, l, st, mk, mv, mem_layer, prm, final_gain):
    b, t, _ = x.shape
    m = b * t
    tm = math.gcd(m, ROW_TILE)
    has_past = st["b_k"] is not None
    outs = _fused_linear(x.reshape(m, D_MODEL), prm["w_in"], l, IN_WIDTHS, prm["norm_mix"][l], tm,
                         feat_major=KV_SEGMENTS, rows_per_seq=t, rows_too=() if has_past else K_SEGMENTS)
    outs = [o if idx in KV_SEGMENTS else o.reshape(b, t, -1) for idx, o in enumerate(outs)]
    (a_qkv, a_z, b_q, b_k, b_v, c_q, c_k, c_v, d_q, d_f, d_i, d_g, a_bg) = outs[:len(IN_WIDTHS)]

    o_a, a_s = _gdn(a_qkv, a_z, a_bg, prm["a_conv_w"][l], st["a_conv"], st["a_S"].astype(F32),
                    prm["alog_row"][l], prm["dtb_row"][l], prm["a_norm"][l])
    lam_init = 0.8 - 0.6 * math.exp(-0.3 * l)
    if has_past:
        o_b = _diff_attn(prm["lam"][l], b_q, b_k, b_v, st["b_k"], st["b_v"], l, prm["b_norm"][l], lam_init)
        o_c = _stick_attn(c_q, c_k, c_v, st["c_k"], st["c_v"], l)
    else:
        b_k_rows, c_k_rows = outs[len(IN_WIDTHS):]
        o_b = _diff_attn_prompt(prm["lam"][l], b_q, b_k_rows, b_v, prm["b_norm"][l], lam_init)
        o_c = _stick_attn_prompt(c_q, c_k_rows, c_v)
    o_d, d_s = _hgrn(d_q, d_f, d_i, d_g, prm["lb"][l], prm["d_norm"][l], st["d_S"].astype(F32))

    x2 = _cross(x, (o_a, o_b, o_c, o_d), prm["w_out"], prm["norm_cross"][l], prm["w_cq"], mk, mv, mem_layer,
                prm["w_co"], l)
    x2 = _ffn(x2.reshape(m, D_MODEL), prm["norm_ffn"][l], prm["w_gate"], prm["w_up"], prm["w_down"], l,
              final_gain=final_gain, tm=tm)

    if t >= CONV_W - 1:
        conv_new = a_qkv[:, t - (CONV_W - 1):, :]
    else:
        conv_new = jnp.concatenate([st["a_conv"], a_qkv], axis=1)[:, -(CONV_W - 1):, :]
    heads = lambda a: jnp.transpose(a.reshape(b, N_HEADS, HEAD_DIM, t), (0, 3, 1, 2))
    new = (conv_new, a_s, heads(b_k), heads(b_v), heads(c_k), heads(c_v), d_s)
    return x2.reshape(b, t, D_MODEL), new


def kernel(x_prompt, x_sample, mem_prompt, state_a_conv, state_a_S, cache_b_k, cache_b_v, cache_c_k, cache_c_v,
           state_d_S, cache_mem_k, cache_mem_v, norm_mix, w_in, a_conv_w, a_A_log, a_dt_bias, a_norm, b_lam_q1,
           b_lam_k1, b_lam_q2, b_lam_k2, b_norm, d_lb, d_norm, w_out, norm_cross, norm_memtok, w_cq, w_ck, w_cv,
           w_co, norm_ffn, w_gate, w_up, w_down, norm_final):
    depth = w_in.shape[0]
    n_main = 4 * GROUP_W
    w_in_p = jnp.concatenate(
        [w_in[:, :, :n_main], w_in[:, :, n_main + 2 * N_HEADS:], w_in[:, :, n_main:n_main + 2 * N_HEADS],
         jnp.zeros((depth, D_MODEL, ABG_W - 2 * N_HEADS), w_in.dtype)], axis=2).astype(BF16)
    pad_row = lambda a: jnp.pad(a.astype(F32), ((0, 0), (N_HEADS, ABG_W - 2 * N_HEADS)))[:, None, :]
    tile_row = lambda a: jnp.tile(a.astype(F32), (1, N_HEADS))[:, None, :]
    p_lb = jax.nn.softmax(d_lb.astype(F32), axis=0)
    lam = (jnp.exp(jnp.sum(b_lam_q1.astype(F32) * b_lam_k1.astype(F32), axis=-1))
           - jnp.exp(jnp.sum(b_lam_q2.astype(F32) * b_lam_k2.astype(F32), axis=-1))
           + jnp.array([0.8 - 0.6 * math.exp(-0.3 * l) for l in range(depth)], F32))
    prm = {
        "w_in": w_in_p, "norm_mix": norm_mix, "a_conv_w": a_conv_w.astype(F32),
        "alog_row": pad_row(a_A_log), "dtb_row": pad_row(a_dt_bias), "a_norm": tile_row(a_norm),
        "lam": lam[:, None], "b_norm": tile_row(b_norm), "lb": (jnp.cumsum(p_lb, axis=0) - p_lb[0])[:, None, :],
        "d_norm": tile_row(d_norm), "w_out": w_out.astype(BF16), "norm_cross": norm_cross,
        "w_cq": w_cq.astype(BF16), "w_co": w_co.astype(BF16), "norm_ffn": norm_ffn,
        "w_gate": w_gate.astype(BF16), "w_up": w_up.astype(BF16), "w_down": w_down.astype(BF16),
    }
    w_ckv = jnp.concatenate([w_ck, w_cv], axis=2).astype(BF16)

    bp, tp, _ = x_prompt.shape
    n_mem = mem_prompt.shape[1]
    h = x_prompt
    p_new = []
    for l in range(depth):
        mk, mv = _fused_linear(mem_prompt.reshape(bp * n_mem, D_MODEL), w_ckv, l, (D_MODEL, D_MODEL),
                               norm_memtok[l], math.gcd(bp * n_mem, ROW_TILE))
        mk = mk.reshape(1, bp, n_mem, D_MODEL)
        mv = mv.reshape(1, bp, n_mem, D_MODEL)
        st = {"a_conv": jnp.zeros((bp, CONV_W - 1, 3 * GROUP_W), F32),
              "a_S": jnp.zeros((bp, N_HEADS, HEAD_DIM, HEAD_DIM), F32), "b_k": None, "b_v": None,
              "c_k": None, "c_v": None, "d_S": jnp.zeros((bp, N_HEADS, HEAD_DIM, HEAD_DIM), F32)}
        h, new = _layer(h, l, st, mk, mv, 0, prm, norm_final if l == depth - 1 else None)
        mem4 = lambda a: a.reshape(bp, n_mem, MEM_HEADS, MEM_HEAD_DIM)
        p_new.append(new + (mem4(mk), mem4(mv)))
    y_prompt = h
    p_stacked = [jnp.stack(c) for c in zip(*p_new)]

    bs, ts, _ = x_sample.shape
    past = cache_b_k.shape[2]
    h = x_sample
    s_new = []
    flat = lambda a: jnp.transpose(a, (0, 1, 3, 4, 2)).reshape(depth, bs, GROUP_W, past)
    caches = {"b_k": flat(cache_b_k), "b_v": flat(cache_b_v), "c_k": flat(cache_c_k), "c_v": flat(cache_c_v)}
    for l in range(depth):
        st = {"a_conv": state_a_conv[l], "a_S": state_a_S[l], "d_S": state_d_S[l], **caches}
        h, new = _layer(h, l, st, cache_mem_k, cache_mem_v, l, prm, norm_final if l == depth - 1 else None)
        s_new.append(new)
    y_sample = h
    s_stacked = [jnp.stack(c) for c in zip(*s_new)]

    return (y_prompt, y_sample, *p_stacked, *s_stacked)
```

```python
import functools
import math

import jax
import jax.numpy as jnp
from jax import lax
from jax.experimental import pallas as pl
from jax.experimental.pallas import tpu as pltpu

F32 = jnp.float32
BF16 = jnp.bfloat16

D_MODEL = 1024
GROUP_W = 256
N_HEADS = 4
HEAD_DIM = 64
DIFF_HALF = 32
CHUNK = 64
CONV_W = 4
MEM_HEADS = 4
MEM_HEAD_DIM = 256
D_FF = 2816
HEAD_SHIFT = HEAD_DIM.bit_length() - 1
CHUNK_SHIFT = CHUNK.bit_length() - 1
EPS = 1e-6
NEG = -1e30
LOG2E = 1.4426950408889634
LANES = 128
ABG_W = 128
FF_CHUNK = 256
ROW_TILE = 512
SUM_ROWS = 16
DECODE_SEQS = 2
CROSS_SEQS = 4
DIFF_DEAD = -160.0
STICK_DEAD = -160.0
STICK_SUB = 256
SEQ_PER_STEP = 4
VMEM_LIMIT = 56 * 1024 * 1024


def _cparams(*sem):
    return pltpu.CompilerParams(dimension_semantics=sem, vmem_limit_bytes=VMEM_LIMIT)


def _dot(a, b):
    return jnp.dot(a, b, preferred_element_type=F32)


def _dot_nt(a, b):
    return lax.dot_general(a, b, (((1,), (1,)), ((), ())), preferred_element_type=F32)


_NN = (((1,), (0,)), ((), ()))
_NT = (((1,), (1,)), ((), ()))
_BNN = (((2,), (1,)), ((0,), (0,)))


def _split(x):
    hi = x.astype(BF16)
    return hi, (x - hi.astype(F32)).astype(BF16)


def _bdot(a, b, dims):
    return lax.dot_general(a, b, dims, preferred_element_type=F32)


def _mm1(a, b, dims=_NN):
    return _bdot(a.astype(BF16), b.astype(BF16), dims)


def _mm_r(a, b16):
    ah, al = _split(a)
    return _bdot(ah, b16, _NN) + _bdot(al, b16, _NN)


def _mm_l(a16, b):
    bh, bl = _split(b)
    return _bdot(a16, bh, _NN) + _bdot(a16, bl, _NN)


def _rms(x, g):
    return x * lax.rsqrt(jnp.mean(x * x, axis=-1, keepdims=True) + EPS) * g


def _sigmoid(x):
    return 1.0 / (1.0 + jnp.exp(-x))


def _log_sigmoid(x):
    return jnp.minimum(x, 0.0) - jnp.log(1.0 + jnp.exp(-jnp.abs(x)))


def _softplus(x):
    return jnp.maximum(x, 0.0) + jnp.log(1.0 + jnp.exp(-jnp.abs(x)))


def _iota(shape, dim):
    return lax.broadcasted_iota(jnp.int32, shape, dim)


def _lanes(x, n):
    return x[:, :n] if n <= LANES else jnp.concatenate([x] * (n // LANES), axis=1)


def _head_block_ones(n):
    return ((_iota((n, n), 0) >> HEAD_SHIFT) == (_iota((n, n), 1) >> HEAD_SHIFT)).astype(F32)


def _resident(shape):
    return pl.BlockSpec(shape, lambda *_: (0,) * len(shape), pipeline_mode=pl.Buffered(1))


def _weight(w, layer):
    return pl.BlockSpec((None,) + w.shape[1:], lambda *_: (layer, 0, 0), pipeline_mode=pl.Buffered(1))


def _linear_body(x_ref, g_ref, w_ref, *out_refs, segs, feat_major, rows_too):
    xb = _rms(x_ref[...], g_ref[...]).astype(BF16)
    extra_refs = dict(zip(rows_too, out_refs[len(segs):]))
    for idx, (o_ref, (s, e)) in enumerate(zip(out_refs, segs)):
        acc = _dot(xb, w_ref[:, s:e])
        if idx in feat_major:
            rows = acc.shape[0] // o_ref.shape[0]
            for sq in range(o_ref.shape[0]):
                o_ref[sq] = acc[sq * rows:(sq + 1) * rows].T
        else:
            o_ref[...] = acc
        if idx in extra_refs:
            extra_refs[idx][...] = acc


def _fused_linear(x, w, layer, widths, gain, tm, feat_major=(), rows_per_seq=None, rows_too=()):
    m, k = x.shape
    n = w.shape[2]
    segs = []
    s = 0
    for wd in widths:
        segs.append((s, s + wd))
        s += wd
    assert s == n and m % tm == 0
    out_specs, out_shape = [], []
    for idx, wd in enumerate(widths):
        if idx in feat_major:
            if rows_per_seq >= tm:
                tiles = rows_per_seq // tm
                assert rows_per_seq % tm == 0
                out_specs.append(pl.BlockSpec((1, wd, tm), lambda i, tiles=tiles: (i // tiles, 0, i % tiles)))
            else:
                assert tm % rows_per_seq == 0
                out_specs.append(pl.BlockSpec((tm // rows_per_seq, wd, rows_per_seq), lambda i: (i, 0, 0)))
            out_shape.append(jax.ShapeDtypeStruct((m // rows_per_seq, wd, rows_per_seq), F32))
        else:
            out_specs.append(pl.BlockSpec((tm, wd), lambda i: (i, 0)))
            out_shape.append(jax.ShapeDtypeStruct((m, wd), F32))
    for idx in rows_too:
        out_specs.append(pl.BlockSpec((tm, widths[idx]), lambda i: (i, 0)))
        out_shape.append(jax.ShapeDtypeStruct((m, widths[idx]), F32))
    assert w.shape[1] == k
    return pl.pallas_call(
        functools.partial(_linear_body, segs=tuple(segs), feat_major=tuple(feat_major), rows_too=tuple(rows_too)),
        grid=(m // tm,),
        in_specs=[pl.BlockSpec((tm, k), lambda i: (i, 0)), _resident((1, k)), _weight(w, layer)],
        out_specs=out_specs,
        out_shape=out_shape,
        compiler_params=_cparams("parallel"),
    )(x, gain.reshape(1, -1), w)


def _ffn_body(*refs, final):
    if final:
        x_ref, g_ref, wg_ref, wu_ref, wd_ref, gf_ref, o_ref = refs
    else:
        x_ref, g_ref, wg_ref, wu_ref, wd_ref, o_ref = refs
    x = x_ref[...]
    hb = _rms(x, g_ref[...]).astype(BF16)
    acc = x
    for c in range(0, D_FF, FF_CHUNK):
        gate = _dot(hb, wg_ref[:, c:c + FF_CHUNK])
        up = _dot(hb, wu_ref[:, c:c + FF_CHUNK])
        act = (gate * _sigmoid(gate) * up).astype(BF16)
        acc = acc + _dot(act, wd_ref[c:c + FF_CHUNK, :])
    if final:
        acc = _rms(acc, gf_ref[...])
    o_ref[...] = acc


def _ffn(x, gain, wg, wu, wd, layer, final_gain=None, tm=256):
    m = x.shape[0]
    final = final_gain is not None
    in_specs = [pl.BlockSpec((tm, D_MODEL), lambda i: (i, 0)), _resident((1, D_MODEL)),
                _weight(wg, layer), _weight(wu, layer), _weight(wd, layer)]
    args = [x, gain.reshape(1, -1), wg, wu, wd]
    if final:
        in_specs.append(_resident((1, D_MODEL)))
        args.append(final_gain.reshape(1, -1))
    return pl.pallas_call(
        functools.partial(_ffn_body, final=final),
        grid=(m // tm,),
        in_specs=in_specs,
        out_specs=pl.BlockSpec((tm, D_MODEL), lambda i: (i, 0)),
        out_shape=jax.ShapeDtypeStruct((m, D_MODEL), F32),
        compiler_params=_cparams("parallel"),
    )(*args)


def _cross_body(x_ref, oa_ref, ob_ref, oc_ref, od_ref, wout_ref, g_ref, wq_ref, mk_ref, mv_ref, wo_ref, o_ref):
    ns, tq, _ = x_ref.shape
    rows = ns * tq
    x = x_ref[...].reshape(rows, D_MODEL)
    for gi, m_ref in enumerate((oa_ref, ob_ref, oc_ref, od_ref)):
        x = x + _dot(m_ref[...].reshape(rows, GROUP_W).astype(BF16), wout_ref[gi * GROUP_W:(gi + 1) * GROUP_W, :])
    hb = _rms(x, g_ref[...]).astype(BF16)
    q = (_dot(hb, wq_ref[...]) * MEM_HEAD_DIM ** -0.5).astype(BF16)
    head = lambda h: slice(h * MEM_HEAD_DIM, (h + 1) * MEM_HEAD_DIM)
    pairs = [(sq, h) for sq in range(ns) for h in range(MEM_HEADS)]
    if len(mk_ref.shape) == 4:
        mem = lambda ref, sq, h: ref[sq, :, h, :].astype(BF16)
    else:
        mem = lambda ref, sq, h: ref[sq, :, head(h)].astype(BF16)
    scores = [_dot_nt(q[sq * tq:(sq + 1) * tq, head(h)], mem(mk_ref, sq, h)) for sq, h in pairs]
    probs = []
    for s in scores:
        p = jnp.exp(s - jnp.max(s, axis=-1, keepdims=True))
        probs.append((p / jnp.sum(p, axis=-1, keepdims=True)).astype(BF16))
    outs = [_dot(p, mem(mv_ref, sq, h)).astype(BF16) for p, (sq, h) in zip(probs, pairs)]
    o = jnp.concatenate([jnp.concatenate(outs[sq * MEM_HEADS:(sq + 1) * MEM_HEADS], axis=1) for sq in range(ns)], axis=0)
    o_ref[...] = (x + _dot(o, wo_ref[...])).reshape(ns, tq, D_MODEL)


def _cross(x, mix, w_out, gain, wq, mk, mv, layer, wo, w_layer):
    b, t, _ = x.shape
    nm = mk.shape[2]
    tq = min(ROW_TILE, t)
    ns = CROSS_SEQS if CROSS_SEQS * t <= ROW_TILE else 1
    assert t % tq == 0 and b % ns == 0
    row_spec = lambda w: pl.BlockSpec((ns, tq, w), lambda i, j: (i, j, 0))
    mem_spec = pl.BlockSpec((None, ns) + mk.shape[2:], lambda i, j: (layer, i) + (0,) * (mk.ndim - 2))
    return pl.pallas_call(
        _cross_body,
        grid=(b // ns, t // tq),
        in_specs=[row_spec(D_MODEL)] + [row_spec(GROUP_W)] * 4 + [_weight(w_out, w_layer), _resident((1, D_MODEL)),
                  _weight(wq, w_layer), mem_spec, mem_spec, _weight(wo, w_layer)],
        out_specs=row_spec(D_MODEL),
        out_shape=jax.ShapeDtypeStruct((b, t, D_MODEL), F32),
        compiler_params=_cparams("parallel", "parallel"),
    )(x, *mix, w_out, gain.reshape(1, -1), wq, mk, mv, wo)


def _diff_body(lam_ref, q_ref, kn_ref, vn_ref, kp_ref, vp_ref, bn_ref, o_ref, q8_sc, m_sc, l_sc, acc_sc, *,
               tq, tk, past, lam_init, ns):
    ng = 2 * N_HEADS
    lane = _iota((tq, GROUP_W), 1)
    for sq in range(ns):
        q = q_ref[sq] * (DIFF_HALF ** -0.5 * LOG2E)
        for g in range(ng):
            lo = (g // 2) * HEAD_DIM + (g % 2) * DIFF_HALF
            q8_sc[sq, g * tq:(g + 1) * tq, :] = jnp.where((lane >= lo) & (lane < lo + DIFF_HALF), q, 0.0).astype(BF16)
    m_sc[...] = jnp.full(m_sc.shape, -jnp.inf, F32)
    l_sc[...] = jnp.zeros(l_sc.shape, F32)
    acc_sc[...] = jnp.zeros(acc_sc.shape, F32)
    pq = past + _iota((tq, 1), 0)
    pq_f = pq.astype(F32)
    lanes = _lanes
    pairs = [(sq, g) for sq in range(ns) for g in range(ng)]

    def block(k_ref, v_ref, k0, n, near, pos0=0):
        pk = pos0 + k0 + _iota((1, n), 1)
        pk_f = pk.astype(F32)
        if near:
            allowed = (pk >> CHUNK_SHIFT) <= (pq >> CHUNK_SHIFT)
            shift = pq_f - jnp.abs(pq_f - pk_f)
        s_all = [_dot(q8_sc[sq], k_ref[sq, :, pl.ds(k0, n)].astype(BF16)) for sq in range(ns)]
        scores = {}
        for sq, g in pairs:
            slope = 2.0 ** (-2 * (g // 2 + 1)) * LOG2E
            s = s_all[sq][g * tq:(g + 1) * tq]
            scores[sq, g] = jnp.where(allowed, s + slope * shift, NEG) if near else s + slope * pk_f
        m_prevs = {k: m_sc[k[0], k[1]] for k in pairs}
        m_news = {k: jnp.maximum(m_prevs[k], jnp.max(scores[k], axis=-1, keepdims=True)) for k in pairs}
        probs = {k: jnp.exp2(scores[k] - lanes(m_news[k], n)) for k in pairs}
        alphas = {k: jnp.exp2(m_prevs[k] - m_news[k]) for k in pairs}
        for k in pairs:
            l_sc[k[0], k[1]] = alphas[k] * l_sc[k[0], k[1]] + jnp.sum(probs[k], axis=-1, keepdims=True)
            m_sc[k[0], k[1]] = m_news[k]
        pvs = [_dot_nt(jnp.concatenate([probs[sq, g].astype(BF16) for g in range(ng)], axis=0),
                       v_ref[sq, :, pl.ds(k0, n)].astype(BF16)) for sq in range(ns)]
        for sq, g in pairs:
            acc_sc[sq, g] = acc_sc[sq, g] * lanes(alphas[sq, g], GROUP_W) + pvs[sq][g * tq:(g + 1) * tq]

    def past_step(j, carry):
        block(kp_ref, vp_ref, pl.multiple_of(j * tk, tk), tk, False)
        return carry
    lax.fori_loop(0, past // tk, past_step, 0)
    block(kn_ref, vn_ref, 0, tq, True, pos0=past)

    lam = lam_ref[0]
    lane_head = _iota((1, GROUP_W), 1) >> HEAD_SHIFT
    ones16 = _head_block_ones(GROUP_W).astype(BF16)
    for sq in range(ns):
        o = jnp.zeros((tq, GROUP_W), F32)
        for h in range(N_HEADS):
            o0 = acc_sc[sq, 2 * h] / lanes(l_sc[sq, 2 * h], GROUP_W)
            o1 = acc_sc[sq, 2 * h + 1] / lanes(l_sc[sq, 2 * h + 1], GROUP_W)
            o = jnp.where(lane_head == h, o0 - lam * o1, o)
        ms = _mm_r(o * o, ones16) * (1.0 / HEAD_DIM)
        o_ref[sq] = o * lax.rsqrt(ms + EPS) * bn_ref[...] * (1.0 - lam_init)


def _diff_attn(lam, q, kn, vn, kp, vp, layer, bnorm, lam_init):
    b, t, _ = q.shape
    past = kp.shape[3]
    ng = 2 * N_HEADS
    ns = DECODE_SEQS
    tq, tk = t, math.gcd(past, 512)
    assert t == CHUNK and past % CHUNK == 0 and tk % LANES == 0 and b % ns == 0
    seq_spec = pl.BlockSpec((ns, GROUP_W, t), lambda i: (i, 0, 0))
    past_spec = pl.BlockSpec((None, ns, GROUP_W, past), lambda i: (layer, i, 0, 0))
    return pl.pallas_call(
        functools.partial(_diff_body, tq=tq, tk=tk, past=past, lam_init=lam_init, ns=ns),
        grid=(b // ns,),
        in_specs=[pl.BlockSpec(memory_space=pltpu.SMEM), pl.BlockSpec((ns, tq, GROUP_W), lambda i: (i, 0, 0)),
                  seq_spec, seq_spec, past_spec, past_spec, _resident((1, GROUP_W))],
        out_specs=pl.BlockSpec((ns, tq, GROUP_W), lambda i: (i, 0, 0)),
        out_shape=jax.ShapeDtypeStruct((b, t, GROUP_W), F32),
        scratch_shapes=[pltpu.VMEM((ns, ng * tq, GROUP_W), BF16), pltpu.VMEM((ns, ng, tq, LANES), F32),
                        pltpu.VMEM((ns, ng, tq, LANES), F32), pltpu.VMEM((ns, ng, tq, GROUP_W), F32)],
        compiler_params=_cparams("parallel"),
    )(lam, q, kn, vn, kp, vp, bnorm)


def _diff_prompt_body(lam_ref, q_ref, k_ref, v_ref, bn_ref, o_ref, qt_sc, m_sc, acc_sc, kn_sc, *, tq, tk, lam_init):
    i = pl.program_id(1)
    ng = 2 * N_HEADS
    qt = (q_ref[0] * (DIFF_HALF ** -0.5 * LOG2E)).T
    feat = _iota((GROUP_W, tq), 0)
    for g in range(ng):
        lo = (g // 2) * HEAD_DIM + (g % 2) * DIFF_HALF
        qt_sc[g] = jnp.where((feat >= lo) & (feat < lo + DIFF_HALF), qt, 0.0).astype(BF16)
    m_sc[...] = jnp.full(m_sc.shape, -jnp.inf, F32)
    acc_sc[...] = jnp.zeros(acc_sc.shape, F32)

    group_of = (_iota((GROUP_W, LANES), 0) // DIFF_HALF == _iota((GROUP_W, LANES), 1)).astype(BF16)

    @pl.when(i == 0)
    def _():
        def norm_step(j, best):
            kf = k_ref[0, pl.ds(pl.multiple_of(j * tk, tk), tk), :].astype(BF16).astype(F32)
            return jnp.maximum(best, jnp.max(_mm_r(kf * kf, group_of), axis=0, keepdims=True))
        kn_sc[...] = lax.fori_loop(0, k_ref.shape[1] // tk, norm_step, jnp.zeros((1, LANES), F32))

    def score_bound(g):
        qf = qt_sc[g].astype(F32)
        q2 = jnp.max(jnp.sum(qf * qf, axis=0, keepdims=True))
        return jnp.sqrt(q2 * kn_sc[0, g]) * 1.001 + 1.0
    pq = i * tq + _iota((1, tq), 1)
    pq_f = pq.astype(F32)
    ones_rows = jnp.ones((SUM_ROWS, tk), BF16)

    def block(k0, n, near, groups=range(2 * N_HEADS)):
        kb = k_ref[0, pl.ds(k0, n), :].astype(BF16)
        vb = v_ref[0, :, pl.ds(k0, n)].astype(BF16)
        pk = k0 + _iota((n, LANES), 0)
        pk_f = pk.astype(F32)
        if near:
            allowed = (_lanes(pk, tq) >> CHUNK_SHIFT) <= (pq >> CHUNK_SHIFT)
            shift = pq_f - jnp.abs(pq_f - _lanes(pk_f, tq))
        scores = {}
        for g in groups:
            slope = 2.0 ** (-2 * (g // 2 + 1)) * LOG2E
            s = _dot(kb, qt_sc[g])
            scores[g] = jnp.where(allowed, s + slope * shift, NEG) if near else s + _lanes(slope * pk_f, tq)
        m_prevs = {g: m_sc[g] for g in groups}
        m_news = {g: jnp.maximum(m_prevs[g], jnp.max(scores[g], axis=0, keepdims=True)) for g in groups}
        probs = {g: jnp.exp2(scores[g] - m_news[g]) for g in groups}
        for g in groups:
            alpha = jnp.exp2(m_prevs[g] - m_news[g])
            m_sc[g] = m_news[g]
            h = g // 2
            pv = _dot(jnp.concatenate([vb[h * HEAD_DIM:(h + 1) * HEAD_DIM, :], ones_rows[:, :n]], axis=0),
                      probs[g].astype(BF16))
            acc_sc[g] = acc_sc[g] * alpha + pv

    block(pl.multiple_of(i * tq, tq), tq, True)

    @pl.when((i * tq) % tk != 0)
    def _():
        block(pl.multiple_of((i - 1) * tq, tq), tq, False)
    nfull = (i * tq) // tk

    def dead_blocks(h):
        slope = 2.0 ** (-2 * (h + 1)) * LOG2E
        counts = [jnp.floor((jnp.min(m_sc[g]) + DIFF_DEAD - score_bound(g)) / (slope * tk)) for g in (2 * h, 2 * h + 1)]
        return jnp.clip(jnp.minimum(counts[0], counts[1]), 0, nfull).astype(jnp.int32)
    dead0 = dead_blocks(0)
    dead1 = jnp.minimum(dead_blocks(1), dead0)

    def steps(groups):
        def step(j, carry):
            block(pl.multiple_of(j * tk, tk), tk, False, groups)
            return carry
        return step
    lax.fori_loop(dead0, nfull, steps(range(ng)), 0)
    lax.fori_loop(dead1, dead0, steps(range(2, ng)), 0)
    lax.fori_loop(0, dead1, steps(range(4, ng)), 0)

    lam = lam_ref[0]
    norm = lambda g: acc_sc[g, :HEAD_DIM, :] / acc_sc[g, HEAD_DIM:HEAD_DIM + 1, :]
    heads = [norm(2 * h) - lam * norm(2 * h + 1) for h in range(N_HEADS)]
    o = jnp.concatenate(heads, axis=0).T
    ms = _mm_r(o * o, _head_block_ones(GROUP_W).astype(BF16)) * (1.0 / HEAD_DIM)
    o_ref[0] = o * lax.rsqrt(ms + EPS) * bn_ref[...] * (1.0 - lam_init)


def _diff_attn_prompt(lam, q, k_rows, v_feat, bnorm, lam_init):
    b, t, _ = q.shape
    ng = 2 * N_HEADS
    tq = min(256, t)
    tk = 2 * tq if t % (2 * tq) == 0 else tq
    assert t % tq == 0 and tq % LANES == 0
    return pl.pallas_call(
        functools.partial(_diff_prompt_body, tq=tq, tk=tk, lam_init=lam_init),
        grid=(b, t // tq),
        in_specs=[pl.BlockSpec(memory_space=pltpu.SMEM), pl.BlockSpec((1, tq, GROUP_W), lambda i, j: (i, j, 0)),
                  pl.BlockSpec((1, t, GROUP_W), lambda i, j: (i, 0, 0)),
                  pl.BlockSpec((1, GROUP_W, t), lambda i, j: (i, 0, 0)), _resident((1, GROUP_W))],
        out_specs=pl.BlockSpec((1, tq, GROUP_W), lambda i, j: (i, j, 0)),
        out_shape=jax.ShapeDtypeStruct((b, t, GROUP_W), F32),
        scratch_shapes=[pltpu.VMEM((ng, GROUP_W, tq), BF16), pltpu.VMEM((ng, 1, tq), F32),
                        pltpu.VMEM((ng, HEAD_DIM + SUM_ROWS, tq), F32), pltpu.VMEM((1, LANES), F32)],
        compiler_params=_cparams("parallel", "arbitrary"),
    )(lam, q, k_rows, v_feat, bnorm)


def _stick_body(*refs, tq, tk, past, ns, near):
    if near:
        q_ref, kn_ref, vn_ref, kp_ref, vp_ref, o_ref, acc_out, c_out, q4_sc, c_sc, acc_sc = refs
    else:
        q_ref, kp_ref, vp_ref, acc_in, c_in, o_ref, q4_sc, c_sc, acc_sc = refs
    nr = N_HEADS * tq
    lane = _iota((tq, GROUP_W), 1) >> HEAD_SHIFT
    for sq in range(ns):
        q = q_ref[sq] * (HEAD_DIM ** -0.5 * LOG2E)
        for h in range(N_HEADS):
            q4_sc[sq, h * tq:(h + 1) * tq, :] = jnp.where(lane == h, q, 0.0).astype(BF16)
    if near:
        c_sc[...] = jnp.zeros(c_sc.shape, F32)
        acc_sc[...] = jnp.zeros(acc_sc.shape, F32)
    else:
        c_sc[...] = c_in[...]
        acc_sc[...] = acc_in[...]
    pq = past + (_iota((nr, 1), 0) & (tq - 1))
    laters = {w: (_iota((w, w), 0) > _iota((w, w), 1)).astype(BF16)
              for w in {min(tq, STICK_SUB), min(tk, STICK_SUB)}}
    seqs = range(ns)

    def block(k_ref, v_ref, k0, n, masked, pos0=0):
        sub = min(n, STICK_SUB)
        later = laters[sub]
        zs = [_dot(q4_sc[sq], k_ref[sq, :, pl.ds(k0, n)].astype(BF16)) for sq in seqs]
        lss = [jnp.minimum(z, 0.0) - jnp.log2(1.0 + jnp.exp2(-jnp.abs(z))) for z in zs]
        lks = [ls - z for ls, z in zip(lss, zs)]
        if masked:
            mask = (pos0 + k0 + _iota((1, n), 1)) < pq
            lks = [jnp.where(mask, lk, 0.0) for lk in lks]
        his = [lk.astype(BF16) for lk in lks]
        los = [(lk - hi.astype(F32)).astype(BF16) for lk, hi in zip(lks, his)]
        carries = [c_sc[sq] for sq in seqs]
        parts = [[] for _ in seqs]
        for sb in reversed(range(n // sub)):
            sl = slice(sb * sub, (sb + 1) * sub)
            for sq in seqs:
                after = _dot(his[sq][:, sl], later) + _dot(los[sq][:, sl], later)
                parts[sq].append(lss[sq][:, sl] + after + _lanes(carries[sq], sub))
                carries[sq] = carries[sq] + (after[:, 0:1] + lks[sq][:, sb * sub:sb * sub + 1])
        for sq in seqs:
            c_sc[sq] = carries[sq]
            e = parts[sq][0] if len(parts[sq]) == 1 else jnp.concatenate(parts[sq][::-1], axis=1)
            if masked:
                a = jnp.where(mask, jnp.exp2(jnp.where(mask, e, 0.0)), 0.0)
            else:
                a = jnp.exp2(e)
            acc_sc[sq] = acc_sc[sq] + _dot_nt(a.astype(BF16), v_ref[sq, :, pl.ds(k0, n)].astype(BF16))

    if near:
        block(kn_ref, vn_ref, 0, tq, True, pos0=past)
        block(kp_ref, vp_ref, 0, tk, False)
        acc_out[...] = acc_sc[...]
        c_out[...] = c_sc[...]
    else:
        nb = past // tk - 1

        def live():
            return jnp.max(c_sc[...]) > STICK_DEAD

        def past_step(state):
            j, _ = state
            block(kp_ref, vp_ref, pl.multiple_of((nb - 1 - j) * tk, tk), tk, False)
            return j + 1, live()
        lax.while_loop(lambda state: (state[0] < nb) & state[1], past_step, (jnp.int32(0), live()))

    for sq in seqs:
        o = acc_sc[sq, 0:tq, :]
        for h in range(1, N_HEADS):
            o = jnp.where(lane == h, acc_sc[sq, h * tq:(h + 1) * tq, :], o)
        o_ref[sq] = o


def _stick_attn(q, kn, vn, kp, vp, layer):
    b, t, _ = q.shape
    past = kp.shape[3]
    ns = DECODE_SEQS
    tq, tk = t, math.gcd(past, 512)
    assert tq & (tq - 1) == 0 and tk % min(tk, STICK_SUB) == 0 and b % ns == 0
    nr = N_HEADS * tq
    row_spec = lambda w: pl.BlockSpec((ns, tq, w), lambda i: (i, 0, 0))
    seq_spec = pl.BlockSpec((ns, GROUP_W, t), lambda i: (i, 0, 0))
    acc_spec = pl.BlockSpec((ns, nr, GROUP_W), lambda i: (i, 0, 0))
    carry_spec = pl.BlockSpec((ns, nr, LANES), lambda i: (i, 0, 0))
    acc_shape = jax.ShapeDtypeStruct((b, nr, GROUP_W), F32)
    carry_shape = jax.ShapeDtypeStruct((b, nr, LANES), F32)
    scratch = [pltpu.VMEM((ns, nr, GROUP_W), BF16), pltpu.VMEM((ns, nr, LANES), F32), pltpu.VMEM((ns, nr, GROUP_W), F32)]
    last = past // tk - 1
    latest_spec = pl.BlockSpec((None, ns, GROUP_W, tk), lambda i: (layer, i, 0, last))
    out, acc, carry = pl.pallas_call(
        functools.partial(_stick_body, tq=tq, tk=tk, past=past, ns=ns, near=True),
        grid=(b // ns,),
        in_specs=[row_spec(GROUP_W), seq_spec, seq_spec, latest_spec, latest_spec],
        out_specs=[row_spec(GROUP_W), acc_spec, carry_spec],
        out_shape=[jax.ShapeDtypeStruct((b, t, GROUP_W), F32), acc_shape, carry_shape],
        scratch_shapes=scratch,
        compiler_params=_cparams("parallel"),
    )(q, kn, vn, kp, vp)
    if last == 0:
        return out

    def earlier_keys():
        past_spec = pl.BlockSpec((None, ns, GROUP_W, past), lambda i: (layer, i, 0, 0))
        return pl.pallas_call(
            functools.partial(_stick_body, tq=tq, tk=tk, past=past, ns=ns, near=False),
            grid=(b // ns,),
            in_specs=[row_spec(GROUP_W), past_spec, past_spec, acc_spec, carry_spec],
            out_specs=row_spec(GROUP_W),
            out_shape=jax.ShapeDtypeStruct((b, t, GROUP_W), F32),
            scratch_shapes=scratch,
            compiler_params=_cparams("parallel"),
        )(q, kp, vp, acc, carry)

    return lax.cond(jnp.max(carry) > STICK_DEAD, earlier_keys, lambda: out)


def _stick_prompt_body(q_ref, k_ref, v_ref, o_ref, qt_sc, c_sc, acc_sc, *, tq, tk):
    i = pl.program_id(1)
    qt = (q_ref[0] * (HEAD_DIM ** -0.5 * LOG2E)).T
    feat_head = _iota((GROUP_W, tq), 0) >> HEAD_SHIFT
    for h in range(N_HEADS):
        qt_sc[h] = jnp.where(feat_head == h, qt, 0.0).astype(BF16)
    c_sc[...] = jnp.zeros(c_sc.shape, F32)
    acc_sc[...] = jnp.zeros(acc_sc.shape, F32)
    pq = i * tq + _iota((1, tq), 1)
    sub = min(tq, STICK_SUB)
    later = (_iota((sub, sub), 1) > _iota((sub, sub), 0)).astype(BF16)

    def block(k0, n, masked):
        kb = k_ref[0, pl.ds(k0, n), :].astype(BF16)
        vb = v_ref[0, :, pl.ds(k0, n)].astype(BF16)
        if masked:
            mask = _lanes(k0 + _iota((n, LANES), 0), tq) < pq
        heads = range(N_HEADS)
        zs = [_dot(kb, qt_sc[h]) for h in heads]
        lss = [jnp.minimum(z, 0.0) - jnp.log2(1.0 + jnp.exp2(-jnp.abs(z))) for z in zs]
        lks = [ls - z for ls, z in zip(lss, zs)]
        if masked:
            lks = [jnp.where(mask, lk, 0.0) for lk in lks]
        es = []
        for h in heads:
            lk = lks[h]
            hi = lk.astype(BF16)
            lo = (lk - hi.astype(F32)).astype(BF16)
            carry = c_sc[h]
            parts = []
            for sb in reversed(range(n // sub)):
                sl = slice(sb * sub, (sb + 1) * sub)
                after = _dot(later, hi[sl]) + _dot(later, lo[sl])
                parts.append(lss[h][sl] + after + carry)
                carry = carry + (after[0:1] + lk[sb * sub:sb * sub + 1])
            c_sc[h] = carry
            es.append(parts[0] if len(parts) == 1 else jnp.concatenate(parts[::-1], axis=0))
        for h in heads:
            if masked:
                a = jnp.where(mask, jnp.exp2(jnp.where(mask, es[h], 0.0)), 0.0)
            else:
                a = jnp.exp2(es[h])
            acc_sc[h] = acc_sc[h] + _dot(vb[h * HEAD_DIM:(h + 1) * HEAD_DIM, :], a.astype(BF16))

    block(pl.multiple_of(i * tq, tq), tq, True)

    @pl.when((i * tq) % tk != 0)
    def _():
        block(pl.multiple_of((i - 1) * tq, tq), tq, False)
    nfull = (i * tq) // tk

    def live():
        return jnp.max(c_sc[...]) > STICK_DEAD

    def prev_step(state):
        j, _ = state
        block(pl.multiple_of((nfull - 1 - j) * tk, tk), tk, False)
        return j + 1, live()
    lax.while_loop(lambda state: (state[0] < nfull) & state[1], prev_step, (jnp.int32(0), live()))

    o_ref[0] = jnp.concatenate([acc_sc[h] for h in range(N_HEADS)], axis=0).T


def _stick_attn_prompt(q, k_rows, v_feat):
    b, t, _ = q.shape
    tq = min(256, t)
    tk = 2 * tq if t % (2 * tq) == 0 else tq
    assert t % tq == 0 and tq % LANES == 0 and tk % min(tk, STICK_SUB) == 0
    return pl.pallas_call(
        functools.partial(_stick_prompt_body, tq=tq, tk=tk),
        grid=(b, t // tq),
        in_specs=[pl.BlockSpec((1, tq, GROUP_W), lambda i, j: (i, j, 0)),
                  pl.BlockSpec((1, t, GROUP_W), lambda i, j: (i, 0, 0)),
                  pl.BlockSpec((1, GROUP_W, t), lambda i, j: (i, 0, 0))],
        out_specs=pl.BlockSpec((1, tq, GROUP_W), lambda i, j: (i, j, 0)),
        out_shape=jax.ShapeDtypeStruct((b, t, GROUP_W), F32),
        scratch_shapes=[pltpu.VMEM((N_HEADS, GROUP_W, tq), BF16), pltpu.VMEM((N_HEADS, 1, tq), F32),
                        pltpu.VMEM((N_HEADS, HEAD_DIM, tq), F32)],
        compiler_params=_cparams("parallel", "parallel"),
    )(q, k_rows, v_feat)


def _pair_mask(n, s, reps=1):
    r, c = _iota((reps * n, n), 0) & (n - 1), _iota((reps * n, n), 1)
    return ((r // (2 * s)) == (c // (2 * s))) & (((r // s) & 1) == 1) & (((c // s) & 1) == 0)


def _state_spec(nb):
    return pl.BlockSpec((nb, N_HEADS, HEAD_DIM, HEAD_DIM), lambda i, j: (i, 0, 0, 0))


def _load_state(s_sc, s0_ref, transpose):
    s_sc[...] = jnp.zeros(s_sc.shape, F32)
    for b in range(s_sc.shape[0]):
        for h in range(N_HEADS):
            blk = s0_ref[b, h]
            s_sc[b, h * HEAD_DIM:(h + 1) * HEAD_DIM, h * HEAD_DIM:(h + 1) * HEAD_DIM] = blk.T if transpose else blk


def _store_state(sout_ref, s_sc, transpose):
    for b in range(s_sc.shape[0]):
        for h in range(N_HEADS):
            blk = s_sc[b, h * HEAD_DIM:(h + 1) * HEAD_DIM, h * HEAD_DIM:(h + 1) * HEAD_DIM]
            sout_ref[b, h] = blk.T if transpose else blk


def _head_rows(x, lane_head):
    return jnp.concatenate([jnp.where(lane_head == h, x, 0.0) for h in range(N_HEADS)], axis=0)


def _head_diag(x, lane_head):
    n = x.shape[0] // N_HEADS
    out = x[:n]
    for h in range(1, N_HEADS):
        out = jnp.where(lane_head == h, x[h * n:(h + 1) * n], out)
    return out


def _gdn_body(x_ref, z_ref, abg_ref, cw_ref, cs_ref, s0_ref, alog_ref, dtb_ref, an_ref,
              o_ref, sout_ref, s_sc, xb_sc, *, nc, nb):
    c = pl.program_id(1)
    L = CHUNK
    pad = 8

    @pl.when(c == 0)
    def _():
        _load_state(s_sc, s0_ref, transpose=False)
        xb_sc[:, pad - 3:pad, :] = cs_ref[...]

    ones_bd = _head_block_ones(GROUP_W)
    ones_bd16 = ones_bd.astype(BF16)
    er, ec = _iota((ABG_W, GROUP_W), 0), _iota((ABG_W, GROUP_W), 1) >> HEAD_SHIFT
    e_beta = (er == ec).astype(BF16)
    e_g = (er == ec + N_HEADS).astype(BF16)
    ri, ci = _iota((L, L), 0), _iota((L, L), 1)
    tri16 = (ri >= ci).astype(BF16)
    incl = ri >= ci
    strict = ri > ci
    eye = (ri == ci).astype(F32)
    pair_masks = [_pair_mask(L, sz) for sz in (1, 2, 4, 8, 16, 32)]
    lane_head = _iota((1, GROUP_W), 1) >> HEAD_SHIFT
    cw = cw_ref[...]

    seqs = range(nb)
    us = []
    for b in seqs:
        x = x_ref[b]
        xb_sc[b, pad:pad + L, :] = x
        u = (x * cw[3:4] + xb_sc[b, pad - 1:pad - 1 + L, :] * cw[2:3]
             + xb_sc[b, pad - 2:pad - 2 + L, :] * cw[1:2] + xb_sc[b, pad - 3:pad - 3 + L, :] * cw[0:1])
        xb_sc[b, pad - 3:pad, :] = x[L - 3:L, :]
        us.append(u * _sigmoid(u))
    qs = [u[:, :GROUP_W] for u in us]
    ks = [u[:, GROUP_W:2 * GROUP_W] for u in us]
    vs = [u[:, 2 * GROUP_W:] for u in us]
    qs = [q * lax.rsqrt(_mm_r(q * q, ones_bd16) + EPS) * HEAD_DIM ** -0.5 for q in qs]
    ks = [k * lax.rsqrt(_mm_r(k * k, ones_bd16) + EPS) for k in ks]

    abgs = [abg_ref[b] for b in seqs]
    betas = [_mm_r(_sigmoid(abg), e_beta) for abg in abgs]
    gc_ns = [_mm_l(tri16, -jnp.exp(alog_ref[...]) * _softplus(abg + dtb_ref[...])) for abg in abgs]
    gc_ts = [gc_n.T for gc_n in gc_ns]
    gcs = [_mm_r(gc_n, e_g) for gc_n in gc_ns]

    kbs = [k * beta for k, beta in zip(ks, betas)]
    k16s = [k.astype(BF16) for k in ks]
    a4s = [_bdot(_head_rows(kb, lane_head).astype(BF16), k16, _NT) for kb, k16 in zip(kbs, k16s)]
    qk4s = [_bdot(_head_rows(q, lane_head).astype(BF16), k16, _NT) for q, k16 in zip(qs, k16s)]
    decs = []
    for b in seqs:
        dec = []
        for h in range(N_HEADS):
            diff = gc_ns[b][:, N_HEADS + h:N_HEADS + h + 1] - gc_ts[b][N_HEADS + h:N_HEADS + h + 1, :]
            dec.append(jnp.where(incl, jnp.exp(jnp.where(incl, diff, 0.0)), 0.0))
        decs.append(jnp.stack(dec))

    a_all = jnp.concatenate([jnp.where(strict, a4s[b].reshape(N_HEADS, L, L) * decs[b], 0.0) for b in seqs], axis=0)
    t_inv = eye - jnp.where(pair_masks[0], a_all, 0.0)
    a16 = a_all.astype(BF16)
    for pm in pair_masks[1:]:
        t16 = t_inv.astype(BF16)
        t_inv = t_inv - _bdot(t16, _bdot(jnp.where(pm, a16, 0.0), t16, _BNN).astype(BF16), _BNN)

    exp_gs = [jnp.exp(gc) for gc in gcs]
    s_olds = [s_sc[b] for b in seqs]
    s16s = [s.astype(BF16) for s in s_olds]
    rhss = [vs[b] * betas[b] - _bdot((kbs[b] * exp_gs[b]).astype(BF16), s16s[b], _NN) for b in seqs]
    ws = [_head_diag(_mm1(t_inv[b * N_HEADS:(b + 1) * N_HEADS].reshape(N_HEADS * L, L), rhss[b]), lane_head)
          for b in seqs]
    outs = [_bdot((qs[b] * exp_gs[b]).astype(BF16), s16s[b], _NN)
            + _head_diag(_mm1((qk4s[b].reshape(N_HEADS, L, L) * decs[b]).reshape(N_HEADS * L, L), ws[b]), lane_head)
            for b in seqs]
    for b in seqs:
        g_last = gcs[b][L - 1:L, :]
        k_dec = ks[b] * jnp.exp(g_last - gcs[b])
        s_sc[b] = s_olds[b] * jnp.exp(g_last) + _mm1(k_dec.T, ws[b]) * ones_bd
    for b in seqs:
        o = outs[b]
        ms = _mm_r(o * o, ones_bd16) * (1.0 / HEAD_DIM)
        zg = z_ref[b]
        o_ref[b] = o * lax.rsqrt(ms + EPS) * an_ref[...] * (zg * _sigmoid(zg))

    @pl.when(c == nc - 1)
    def _():
        _store_state(sout_ref, s_sc, transpose=False)


def _gdn(aqkv, az, abg, conv_w, conv_state, s0, alog_row, dtb_row, anorm_row):
    b, t, _ = aqkv.shape
    nc = t // CHUNK
    nb = SEQ_PER_STEP
    assert b % nb == 0
    cmap = lambda i, j: (i, j, 0)
    bmap = lambda i, j: (i, 0, 0)
    return pl.pallas_call(
        functools.partial(_gdn_body, nc=nc, nb=nb),
        grid=(b // nb, nc),
        in_specs=[pl.BlockSpec((nb, CHUNK, 3 * GROUP_W), cmap), pl.BlockSpec((nb, CHUNK, GROUP_W), cmap),
                  pl.BlockSpec((nb, CHUNK, ABG_W), cmap), _resident((CONV_W, 3 * GROUP_W)),
                  pl.BlockSpec((nb, CONV_W - 1, 3 * GROUP_W), bmap), _state_spec(nb),
                  _resident((1, ABG_W)), _resident((1, ABG_W)), _resident((1, GROUP_W))],
        out_specs=[pl.BlockSpec((nb, CHUNK, GROUP_W), cmap), _state_spec(nb)],
        out_shape=[jax.ShapeDtypeStruct((b, t, GROUP_W), F32), jax.ShapeDtypeStruct(s0.shape, F32)],
        scratch_shapes=[pltpu.VMEM((nb, GROUP_W, GROUP_W), F32), pltpu.VMEM((nb, 8 + CHUNK, 3 * GROUP_W), F32)],
        compiler_params=_cparams("parallel", "arbitrary"),
    )(aqkv, az, abg, conv_w, conv_state, s0, alog_row, dtb_row, anorm_row)


def _hgrn_body(q_ref, f_ref, i_ref, g_ref, lb_ref, dn_ref, s0_ref, o_ref, sout_ref, s_sc, *, nc, nb):
    c = pl.program_id(1)
    L = CHUNK

    @pl.when(c == 0)
    def _():
        _load_state(s_sc, s0_ref, transpose=True)

    lb = lb_ref[...]
    ones_bd = _head_block_ones(GROUP_W)
    ones_bd16 = ones_bd.astype(BF16)
    ri, ci = _iota((L, L), 0), _iota((L, L), 1)
    tri16 = (ri >= ci).astype(BF16)
    sizes = (1, 2, 4, 8, 16, 32)
    sel16 = jnp.concatenate([(ci == (ri // (2 * sz)) * (2 * sz) + sz).astype(BF16) for sz in sizes], axis=0)
    eye4 = (_iota((N_HEADS * L, L), 0) & (L - 1)) == _iota((N_HEADS * L, L), 1)
    pair_masks4 = [_pair_mask(L, sz, reps=N_HEADS) for sz in sizes]
    lane_head = _iota((1, GROUP_W), 1) >> HEAD_SHIFT
    rows = _iota((L, 1), 0)

    pre = []
    for b in range(nb):
        fl = f_ref[b]
        log_f = _log_sigmoid(fl) + jnp.log(1.0 + lb * jnp.exp(-fl))
        k = (1.0 - lb) * _sigmoid(-fl)
        q = q_ref[b] * HEAD_DIM ** -0.5
        bc = _mm_l(tri16, log_f)
        pre.append((q, k, bc))
    b_refs = [_mm_l(sel16, bc) for _, _, bc in pre]
    a_all = [jnp.where(eye4, _mm1(_head_rows(q, lane_head), k, _NT), 0.0) for q, k, _ in pre]
    for lv, sz in enumerate(sizes):
        later = ((rows // sz) & 1) == 1
        for b in range(nb):
            q, k, bc = pre[b]
            b_ref = b_refs[b][lv * L:(lv + 1) * L]
            x = jnp.where(later, q * jnp.exp(jnp.where(later, bc - b_ref, 0.0)), 0.0)
            y = jnp.where(later, 0.0, k * jnp.exp(jnp.where(later, 0.0, b_ref - bc)))
            a_all[b] = a_all[b] + jnp.where(pair_masks4[lv], _mm1(_head_rows(x, lane_head), y, _NT), 0.0)

    for b in range(nb):
        q, k, bc = pre[b]
        v = i_ref[b]
        st = s_sc[b]
        o = _mm1(q * jnp.exp(bc), st, _NT) + _head_diag(_mm1(a_all[b], v), lane_head)
        b_last = bc[L - 1:L, :]
        st_new = st * jnp.exp(b_last) + _mm1(v.T, k * jnp.exp(b_last - bc)) * ones_bd
        s_sc[b] = st_new

        ms = _mm_r(o * o, ones_bd16) * (1.0 / HEAD_DIM)
        zg = g_ref[b]
        o_ref[b] = o * lax.rsqrt(ms + EPS) * dn_ref[...] * (zg * _sigmoid(zg))

    @pl.when(c == nc - 1)
    def _():
        _store_state(sout_ref, s_sc, transpose=True)


def _hgrn(dq, df, di, dg, lb_row, dnorm_row, s0):
    b, t, _ = dq.shape
    nc = t // CHUNK
    nb = SEQ_PER_STEP
    assert b % nb == 0
    cmap = lambda i, j: (i, j, 0)
    bmap = lambda i, j: (i, 0, 0)
    cspec = pl.BlockSpec((nb, CHUNK, GROUP_W), cmap)
    return pl.pallas_call(
        functools.partial(_hgrn_body, nc=nc, nb=nb),
        grid=(b // nb, nc),
        in_specs=[cspec, cspec, cspec, cspec, _resident((1, GROUP_W)), _resident((1, GROUP_W)),
                  _state_spec(nb)],
        out_specs=[cspec, _state_spec(nb)],
        out_shape=[jax.ShapeDtypeStruct((b, t, GROUP_W), F32), jax.ShapeDtypeStruct(s0.shape, F32)],
        scratch_shapes=[pltpu.VMEM((nb, GROUP_W, GROUP_W), F32)],
        compiler_params=_cparams("parallel", "arbitrary"),
    )(dq, df, di, dg, lb_row, dnorm_row, s0)


IN_WIDTHS = (3 * GROUP_W,) + (GROUP_W,) * 11 + (ABG_W,)
KV_SEGMENTS = (3, 4, 6, 7)
K_SEGMENTS = (3, 6)


def _layer(x, l, st, mk, mv, mem_layer, prm, final_gain):
    b, t, _ = x.shape
    m = b * t
    tm = math.gcd(m, ROW_TILE)
    has_past = st["b_k"] is not None
    outs = _fused_linear(x.reshape(m, D_MODEL), prm["w_in"], l, IN_WIDTHS, prm["norm_mix"][l], tm,
                         feat_major=KV_SEGMENTS, rows_per_seq=t, rows_too=() if has_past else K_SEGMENTS)
    outs = [o if idx in KV_SEGMENTS else o.reshape(b, t, -1) for idx, o in enumerate(outs)]
    (a_qkv, a_z, b_q, b_k, b_v, c_q, c_k, c_v, d_q, d_f, d_i, d_g, a_bg) = outs[:len(IN_WIDTHS)]

    o_a, a_s = _gdn(a_qkv, a_z, a_bg, prm["a_conv_w"][l], st["a_conv"], st["a_S"].astype(F32),
                    prm["alog_row"][l], prm["dtb_row"][l], prm["a_norm"][l])
    lam_init = 0.8 - 0.6 * math.exp(-0.3 * l)
    if has_past:
        o_b = _diff_attn(prm["lam"][l], b_q, b_k, b_v, st["b_k"], st["b_v"], l, prm["b_norm"][l], lam_init)
        o_c = _stick_attn(c_q, c_k, c_v, st["c_k"], st["c_v"], l)
    else:
        b_k_rows, c_k_rows = outs[len(IN_WIDTHS):]
        o_b = _diff_attn_prompt(prm["lam"][l], b_q, b_k_rows, b_v, prm["b_norm"][l], lam_init)
        o_c = _stick_attn_prompt(c_q, c_k_rows, c_v)
    o_d, d_s = _hgrn(d_q, d_f, d_i, d_g, prm["lb"][l], prm["d_norm"][l], st["d_S"].astype(F32))

    x2 = _cross(x, (o_a, o_b, o_c, o_d), prm["w_out"], prm["norm_cross"][l], prm["w_cq"], mk, mv, mem_layer,
                prm["w_co"], l)
    x2 = _ffn(x2.reshape(m, D_MODEL), prm["norm_ffn"][l], prm["w_gate"], prm["w_up"], prm["w_down"], l,
              final_gain=final_gain, tm=tm)

    if t >= CONV_W - 1:
        conv_new = a_qkv[:, t - (CONV_W - 1):, :]
    else:
        conv_new = jnp.concatenate([st["a_conv"], a_qkv], axis=1)[:, -(CONV_W - 1):, :]
    heads = lambda a: jnp.transpose(a.reshape(b, N_HEADS, HEAD_DIM, t), (0, 3, 1, 2))
    new = (conv_new, a_s, heads(b_k), heads(b_v), heads(c_k), heads(c_v), d_s)
    return x2.reshape(b, t, D_MODEL), new


def kernel(x_prompt, x_sample, mem_prompt, state_a_conv, state_a_S, cache_b_k, cache_b_v, cache_c_k, cache_c_v,
           state_d_S, cache_mem_k, cache_mem_v, norm_mix, w_in, a_conv_w, a_A_log, a_dt_bias, a_norm, b_lam_q1,
           b_lam_k1, b_lam_q2, b_lam_k2, b_norm, d_lb, d_norm, w_out, norm_cross, norm_memtok, w_cq, w_ck, w_cv,
           w_co, norm_ffn, w_gate, w_up, w_down, norm_final):
    depth = w_in.shape[0]
    n_main = 4 * GROUP_W
    w_in_p = jnp.concatenate(
        [w_in[:, :, :n_main], w_in[:, :, n_main + 2 * N_HEADS:], w_in[:, :, n_main:n_main + 2 * N_HEADS],
         jnp.zeros((depth, D_MODEL, ABG_W - 2 * N_HEADS), w_in.dtype)], axis=2).astype(BF16)
    pad_row = lambda a: jnp.pad(a.astype(F32), ((0, 0), (N_HEADS, ABG_W - 2 * N_HEADS)))[:, None, :]
    tile_row = lambda a: jnp.tile(a.astype(F32), (1, N_HEADS))[:, None, :]
    p_lb = jax.nn.softmax(d_lb.astype(F32), axis=0)
    lam = (jnp.exp(jnp.sum(b_lam_q1.astype(F32) * b_lam_k1.astype(F32), axis=-1))
           - jnp.exp(jnp.sum(b_lam_q2.astype(F32) * b_lam_k2.astype(F32), axis=-1))
           + jnp.array([0.8 - 0.6 * math.exp(-0.3 * l) for l in range(depth)], F32))
    prm = {
        "w_in": w_in_p, "norm_mix": norm_mix, "a_conv_w": a_conv_w.astype(F32),
        "alog_row": pad_row(a_A_log), "dtb_row": pad_row(a_dt_bias), "a_norm": tile_row(a_norm),
        "lam": lam[:, None], "b_norm": tile_row(b_norm), "lb": (jnp.cumsum(p_lb, axis=0) - p_lb[0])[:, None, :],
        "d_norm": tile_row(d_norm), "w_out": w_out.astype(BF16), "norm_cross": norm_cross,
        "w_cq": w_cq.astype(BF16), "w_co": w_co.astype(BF16), "norm_ffn": norm_ffn,
        "w_gate": w_gate.astype(BF16), "w_up": w_up.astype(BF16), "w_down": w_down.astype(BF16),
    }
    w_ckv = jnp.concatenate([w_ck, w_cv], axis=2).astype(BF16)

    bp, tp, _ = x_prompt.shape
    n_mem = mem_prompt.shape[1]
    h = x_prompt
    p_new = []
    for l in range(depth):
        mk, mv = _fused_linear(mem_prompt.reshape(bp * n_mem, D_MODEL), w_ckv, l, (D_MODEL, D_MODEL),
                               norm_memtok[l], math.gcd(bp * n_mem, ROW_TILE))
        mk = mk.reshape(1, bp, n_mem, D_MODEL)
        mv = mv.reshape(1, bp, n_mem, D_MODEL)
        st = {"a_conv": jnp.zeros((bp, CONV_W - 1, 3 * GROUP_W), F32),
              "a_S": jnp.zeros((bp, N_HEADS, HEAD_DIM, HEAD_DIM), F32), "b_k": None, "b_v": None,
              "c_k": None, "c_v": None, "d_S": jnp.zeros((bp, N_HEADS, HEAD_DIM, HEAD_DIM), F32)}
        h, new = _layer(h, l, st, mk, mv, 0, prm, norm_final if l == depth - 1 else None)
        mem4 = lambda a: a.reshape(bp, n_mem, MEM_HEADS, MEM_HEAD_DIM)
        p_new.append(new + (mem4(mk), mem4(mv)))
    y_prompt = h
    p_stacked = [jnp.stack(c) for c in zip(*p_new)]

    bs, ts, _ = x_sample.shape
    past = cache_b_k.shape[2]
    h = x_sample
    s_new = []
    flat = lambda a: jnp.transpose(a, (0, 1, 3, 4, 2)).reshape(depth, bs, GROUP_W, past)
    caches = {"b_k": flat(cache_b_k), "b_v": flat(cache_b_v), "c_k": flat(cache_c_k), "c_v": flat(cache_c_v)}
    for l in range(depth):
        st = {"a_conv": state_a_conv[l], "a_S": state_a_S[l], "d_S": state_d_S[l], **caches}
        h, new = _layer(h, l, st, cache_mem_k, cache_mem_v, l, prm, norm_final if l == depth - 1 else None)
        s_new.append(new)
    y_sample = h
    s_stacked = [jnp.stack(c) for c in zip(*s_new)]

    return (y_prompt, y_sample, *p_stacked, *s_stacked)
```

```python
import functools
import math

import jax
import jax.numpy as jnp
from jax import lax
from jax.experimental import pallas as pl
from jax.experimental.pallas import tpu as pltpu

F32 = jnp.float32
BF16 = jnp.bfloat16

D_MODEL = 1024
GROUP_W = 256
N_HEADS = 4
HEAD_DIM = 64
DIFF_HALF = 32
CHUNK = 64
CONV_W = 4
MEM_HEADS = 4
MEM_HEAD_DIM = 256
D_FF = 2816
HEAD_SHIFT = HEAD_DIM.bit_length() - 1
CHUNK_SHIFT = CHUNK.bit_length() - 1
EPS = 1e-6
NEG = -1e30
LOG2E = 1.4426950408889634
LANES = 128
ABG_W = 128
FF_CHUNK = 256
ROW_TILE = 512
SUM_ROWS = 16
DECODE_SEQS = 2
CROSS_SEQS = 4
DIFF_DEAD = -160.0
STICK_DEAD = -160.0
STICK_SUB = 256
SEQ_PER_STEP = 4
VMEM_LIMIT = 56 * 1024 * 1024


def _cparams(*sem):
    return pltpu.CompilerParams(dimension_semantics=sem, vmem_limit_bytes=VMEM_LIMIT)


def _dot(a, b):
    return jnp.dot(a, b, preferred_element_type=F32)


def _dot_nt(a, b):
    return lax.dot_general(a, b, (((1,), (1,)), ((), ())), preferred_element_type=F32)


_NN = (((1,), (0,)), ((), ()))
_NT = (((1,), (1,)), ((), ()))
_BNN = (((2,), (1,)), ((0,), (0,)))


def _split(x):
    hi = x.astype(BF16)
    return hi, (x - hi.astype(F32)).astype(BF16)


def _bdot(a, b, dims):
    return lax.dot_general(a, b, dims, preferred_element_type=F32)


def _mm1(a, b, dims=_NN):
    return _bdot(a.astype(BF16), b.astype(BF16), dims)


def _mm_r(a, b16):
    ah, al = _split(a)
    return _bdot(ah, b16, _NN) + _bdot(al, b16, _NN)


def _mm_l(a16, b):
    bh, bl = _split(b)
    return _bdot(a16, bh, _NN) + _bdot(a16, bl, _NN)


def _rms(x, g):
    return x * lax.rsqrt(jnp.mean(x * x, axis=-1, keepdims=True) + EPS) * g


def _sigmoid(x):
    return 1.0 / (1.0 + jnp.exp(-x))


def _log_sigmoid(x):
    return jnp.minimum(x, 0.0) - jnp.log(1.0 + jnp.exp(-jnp.abs(x)))


def _softplus(x):
    return jnp.maximum(x, 0.0) + jnp.log(1.0 + jnp.exp(-jnp.abs(x)))


def _iota(shape, dim):
    return lax.broadcasted_iota(jnp.int32, shape, dim)


def _lanes(x, n):
    return x[:, :n] if n <= LANES else jnp.concatenate([x] * (n // LANES), axis=1)


def _head_block_ones(n):
    return ((_iota((n, n), 0) >> HEAD_SHIFT) == (_iota((n, n), 1) >> HEAD_SHIFT)).astype(F32)


def _resident(shape):
    return pl.BlockSpec(shape, lambda *_: (0,) * len(shape), pipeline_mode=pl.Buffered(1))


def _weight(w, layer):
    return pl.BlockSpec((None,) + w.shape[1:], lambda *_: (layer, 0, 0), pipeline_mode=pl.Buffered(1))


def _linear_body(x_ref, g_ref, w_ref, *out_refs, segs, feat_major, rows_too):
    xb = _rms(x_ref[...], g_ref[...]).astype(BF16)
    extra_refs = dict(zip(rows_too, out_refs[len(segs):]))
    for idx, (o_ref, (s, e)) in enumerate(zip(out_refs, segs)):
        acc = _dot(xb, w_ref[:, s:e])
        if idx in feat_major:
            rows = acc.shape[0] // o_ref.shape[0]
            for sq in range(o_ref.shape[0]):
                o_ref[sq] = acc[sq * rows:(sq + 1) * rows].T
        else:
            o_ref[...] = acc
        if idx in extra_refs:
            extra_refs[idx][...] = acc


def _fused_linear(x, w, layer, widths, gain, tm, feat_major=(), rows_per_seq=None, rows_too=()):
    m, k = x.shape
    n = w.shape[2]
    segs = []
    s = 0
    for wd in widths:
        segs.append((s, s + wd))
        s += wd
    assert s == n and m % tm == 0
    out_specs, out_shape = [], []
    for idx, wd in enumerate(widths):
        if idx in feat_major:
            if rows_per_seq >= tm:
                tiles = rows_per_seq // tm
                assert rows_per_seq % tm == 0
                out_specs.append(pl.BlockSpec((1, wd, tm), lambda i, tiles=tiles: (i // tiles, 0, i % tiles)))
            else:
                assert tm % rows_per_seq == 0
                out_specs.append(pl.BlockSpec((tm // rows_per_seq, wd, rows_per_seq), lambda i: (i, 0, 0)))
            out_shape.append(jax.ShapeDtypeStruct((m // rows_per_seq, wd, rows_per_seq), F32))
        else:
            out_specs.append(pl.BlockSpec((tm, wd), lambda i: (i, 0)))
            out_shape.append(jax.ShapeDtypeStruct((m, wd), F32))
    for idx in rows_too:
        out_specs.append(pl.BlockSpec((tm, widths[idx]), lambda i: (i, 0)))
        out_shape.append(jax.ShapeDtypeStruct((m, widths[idx]), F32))
    assert w.shape[1] == k
    return pl.pallas_call(
        functools.partial(_linear_body, segs=tuple(segs), feat_major=tuple(feat_major), rows_too=tuple(rows_too)),
        grid=(m // tm,),
        in_specs=[pl.BlockSpec((tm, k), lambda i: (i, 0)), _resident((1, k)), _weight(w, layer)],
        out_specs=out_specs,
        out_shape=out_shape,
        compiler_params=_cparams("parallel"),
    )(x, gain.reshape(1, -1), w)


def _ffn_body(*refs, final):
    if final:
        x_ref, g_ref, wg_ref, wu_ref, wd_ref, gf_ref, o_ref = refs
    else:
        x_ref, g_ref, wg_ref, wu_ref, wd_ref, o_ref = refs
    x = x_ref[...]
    hb = _rms(x, g_ref[...]).astype(BF16)
    acc = x
    for c in range(0, D_FF, FF_CHUNK):
        gate = _dot(hb, wg_ref[:, c:c + FF_CHUNK])
        up = _dot(hb, wu_ref[:, c:c + FF_CHUNK])
        act = (gate * _sigmoid(gate) * up).astype(BF16)
        acc = acc + _dot(act, wd_ref[c:c + FF_CHUNK, :])
    if final:
        acc = _rms(acc, gf_ref[...])
    o_ref[...] = acc


def _ffn(x, gain, wg, wu, wd, layer, final_gain=None, tm=256):
    m = x.shape[0]
    final = final_gain is not None
    in_specs = [pl.BlockSpec((tm, D_MODEL), lambda i: (i, 0)), _resident((1, D_MODEL)),
                _weight(wg, layer), _weight(wu, layer), _weight(wd, layer)]
    args = [x, gain.reshape(1, -1), wg, wu, wd]
    if final:
        in_specs.append(_resident((1, D_MODEL)))
        args.append(final_gain.reshape(1, -1))
    return pl.pallas_call(
        functools.partial(_ffn_body, final=final),
        grid=(m // tm,),
        in_specs=in_specs,
        out_specs=pl.BlockSpec((tm, D_MODEL), lambda i: (i, 0)),
        out_shape=jax.ShapeDtypeStruct((m, D_MODEL), F32),
        compiler_params=_cparams("parallel"),
    )(*args)


def _cross_body(x_ref, oa_ref, ob_ref, oc_ref, od_ref, wout_ref, g_ref, wq_ref, mk_ref, mv_ref, wo_ref, o_ref):
    ns, tq, _ = x_ref.shape
    rows = ns * tq
    x = x_ref[...].reshape(rows, D_MODEL)
    for gi, m_ref in enumerate((oa_ref, ob_ref, oc_ref, od_ref)):
        x = x + _dot(m_ref[...].reshape(rows, GROUP_W).astype(BF16), wout_ref[gi * GROUP_W:(gi + 1) * GROUP_W, :])
    hb = _rms(x, g_ref[...]).astype(BF16)
    q = (_dot(hb, wq_ref[...]) * MEM_HEAD_DIM ** -0.5).astype(BF16)
    head = lambda h: slice(h * MEM_HEAD_DIM, (h + 1) * MEM_HEAD_DIM)
    pairs = [(sq, h) for sq in range(ns) for h in range(MEM_HEADS)]
    if len(mk_ref.shape) == 4:
        mem = lambda ref, sq, h: ref[sq, :, h, :].astype(BF16)
    else:
        mem = lambda ref, sq, h: ref[sq, :, head(h)].astype(BF16)
    scores = [_dot_nt(q[sq * tq:(sq + 1) * tq, head(h)], mem(mk_ref, sq, h)) for sq, h in pairs]
    probs = []
    for s in scores:
        p = jnp.exp(s - jnp.max(s, axis=-1, keepdims=True))
        probs.append((p / jnp.sum(p, axis=-1, keepdims=True)).astype(BF16))
    outs = [_dot(p, mem(mv_ref, sq, h)).astype(BF16) for p, (sq, h) in zip(probs, pairs)]
    o = jnp.concatenate([jnp.concatenate(outs[sq * MEM_HEADS:(sq + 1) * MEM_HEADS], axis=1) for sq in range(ns)], axis=0)
    o_ref[...] = (x + _dot(o, wo_ref[...])).reshape(ns, tq, D_MODEL)


def _cross(x, mix, w_out, gain, wq, mk, mv, layer, wo, w_layer):
    b, t, _ = x.shape
    nm = mk.shape[2]
    tq = min(ROW_TILE, t)
    ns = CROSS_SEQS if CROSS_SEQS * t <= ROW_TILE else 1
    assert t % tq == 0 and b % ns == 0
    row_spec = lambda w: pl.BlockSpec((ns, tq, w), lambda i, j: (i, j, 0))
    mem_spec = pl.BlockSpec((None, ns) + mk.shape[2:], lambda i, j: (layer, i) + (0,) * (mk.ndim - 2))
    return pl.pallas_call(
        _cross_body,
        grid=(b // ns, t // tq),
        in_specs=[row_spec(D_MODEL)] + [row_spec(GROUP_W)] * 4 + [_weight(w_out, w_layer), _resident((1, D_MODEL)),
                  _weight(wq, w_layer), mem_spec, mem_spec, _weight(wo, w_layer)],
        out_specs=row_spec(D_MODEL),
        out_shape=jax.ShapeDtypeStruct((b, t, D_MODEL), F32),
        compiler_params=_cparams("parallel", "parallel"),
    )(x, *mix, w_out, gain.reshape(1, -1), wq, mk, mv, wo)


def _diff_body(lam_ref, q_ref, kn_ref, vn_ref, kp_ref, vp_ref, bn_ref, o_ref, q8_sc, m_sc, l_sc, acc_sc, *,
               tq, tk, past, lam_init, ns):
    ng = 2 * N_HEADS
    lane = _iota((tq, GROUP_W), 1)
    for sq in range(ns):
        q = q_ref[sq] * (DIFF_HALF ** -0.5 * LOG2E)
        for g in range(ng):
            lo = (g // 2) * HEAD_DIM + (g % 2) * DIFF_HALF
            q8_sc[sq, g * tq:(g + 1) * tq, :] = jnp.where((lane >= lo) & (lane < lo + DIFF_HALF), q, 0.0).astype(BF16)
    m_sc[...] = jnp.full(m_sc.shape, -jnp.inf, F32)
    l_sc[...] = jnp.zeros(l_sc.shape, F32)
    acc_sc[...] = jnp.zeros(acc_sc.shape, F32)
    pq = past + _iota((tq, 1), 0)
    pq_f = pq.astype(F32)
    lanes = _lanes
    pairs = [(sq, g) for sq in range(ns) for g in range(ng)]

    def block(k_ref, v_ref, k0, n, near, pos0=0):
        pk = pos0 + k0 + _iota((1, n), 1)
        pk_f = pk.astype(F32)
        if near:
            allowed = (pk >> CHUNK_SHIFT) <= (pq >> CHUNK_SHIFT)
            shift = pq_f - jnp.abs(pq_f - pk_f)
        s_all = [_dot(q8_sc[sq], k_ref[sq, :, pl.ds(k0, n)].astype(BF16)) for sq in range(ns)]
        scores = {}
        for sq, g in pairs:
            slope = 2.0 ** (-2 * (g // 2 + 1)) * LOG2E
            s = s_all[sq][g * tq:(g + 1) * tq]
            scores[sq, g] = jnp.where(allowed, s + slope * shift, NEG) if near else s + slope * pk_f
        m_prevs = {k: m_sc[k[0], k[1]] for k in pairs}
        m_news = {k: jnp.maximum(m_prevs[k], jnp.max(scores[k], axis=-1, keepdims=True)) for k in pairs}
        probs = {k: jnp.exp2(scores[k] - lanes(m_news[k], n)) for k in pairs}
        alphas = {k: jnp.exp2(m_prevs[k] - m_news[k]) for k in pairs}
        for k in pairs:
            l_sc[k[0], k[1]] = alphas[k] * l_sc[k[0], k[1]] + jnp.sum(probs[k], axis=-1, keepdims=True)
            m_sc[k[0], k[1]] = m_news[k]
        pvs = [_dot_nt(jnp.concatenate([probs[sq, g].astype(BF16) for g in range(ng)], axis=0),
                       v_ref[sq, :, pl.ds(k0, n)].astype(BF16)) for sq in range(ns)]
        for sq, g in pairs:
            acc_sc[sq, g] = acc_sc[sq, g] * lanes(alphas[sq, g], GROUP_W) + pvs[sq][g * tq:(g + 1) * tq]

    def past_step(j, carry):
        block(kp_ref, vp_ref, pl.multiple_of(j * tk, tk), tk, False)
        return carry
    lax.fori_loop(0, past // tk, past_step, 0)
    block(kn_ref, vn_ref, 0, tq, True, pos0=past)

    lam = lam_ref[0]
    lane_head = _iota((1, GROUP_W), 1) >> HEAD_SHIFT
    ones16 = _head_block_ones(GROUP_W).astype(BF16)
    for sq in range(ns):
        o = jnp.zeros((tq, GROUP_W), F32)
        for h in range(N_HEADS):
            o0 = acc_sc[sq, 2 * h] / lanes(l_sc[sq, 2 * h], GROUP_W)
            o1 = acc_sc[sq, 2 * h + 1] / lanes(l_sc[sq, 2 * h + 1], GROUP_W)
            o = jnp.where(lane_head == h, o0 - lam * o1, o)
        ms = _mm_r(o * o, ones16) * (1.0 / HEAD_DIM)
        o_ref[sq] = o * lax.rsqrt(ms + EPS) * bn_ref[...] * (1.0 - lam_init)


def _diff_attn(lam, q, kn, vn, kp, vp, layer, bnorm, lam_init):
    b, t, _ = q.shape
    past = kp.shape[3]
    ng = 2 * N_HEADS
    ns = DECODE_SEQS
    tq, tk = t, math.gcd(past, 512)
    assert t == CHUNK and past % CHUNK == 0 and tk % LANES == 0 and b % ns == 0
    seq_spec = pl.BlockSpec((ns, GROUP_W, t), lambda i: (i, 0, 0))
    past_spec = pl.BlockSpec((None, ns, GROUP_W, past), lambda i: (layer, i, 0, 0))
    return pl.pallas_call(
        functools.partial(_diff_body, tq=tq, tk=tk, past=past, lam_init=lam_init, ns=ns),
        grid=(b // ns,),
        in_specs=[pl.BlockSpec(memory_space=pltpu.SMEM), pl.BlockSpec((ns, tq, GROUP_W), lambda i: (i, 0, 0)),
                  seq_spec, seq_spec, past_spec, past_spec, _resident((1, GROUP_W))],
        out_specs=pl.BlockSpec((ns, tq, GROUP_W), lambda i: (i, 0, 0)),
        out_shape=jax.ShapeDtypeStruct((b, t, GROUP_W), F32),
        scratch_shapes=[pltpu.VMEM((ns, ng * tq, GROUP_W), BF16), pltpu.VMEM((ns, ng, tq, LANES), F32),
                        pltpu.VMEM((ns, ng, tq, LANES), F32), pltpu.VMEM((ns, ng, tq, GROUP_W), F32)],
        compiler_params=_cparams("parallel"),
    )(lam, q, kn, vn, kp, vp, bnorm)


def _diff_prompt_body(lam_ref, q_ref, k_ref, v_ref, bn_ref, o_ref, qt_sc, m_sc, acc_sc, kn_sc, *, tq, tk, lam_init):
    i = pl.program_id(1)
    ng = 2 * N_HEADS
    qt = (q_ref[0] * (DIFF_HALF ** -0.5 * LOG2E)).T
    feat = _iota((GROUP_W, tq), 0)
    for g in range(ng):
        lo = (g // 2) * HEAD_DIM + (g % 2) * DIFF_HALF
        qt_sc[g] = jnp.where((feat >= lo) & (feat < lo + DIFF_HALF), qt, 0.0).astype(BF16)
    m_sc[...] = jnp.full(m_sc.shape, -jnp.inf, F32)
    acc_sc[...] = jnp.zeros(acc_sc.shape, F32)

    group_of = (_iota((GROUP_W, LANES), 0) // DIFF_HALF == _iota((GROUP_W, LANES), 1)).astype(BF16)

    @pl.when(i == 0)
    def _():
        def norm_step(j, best):
            kf = k_ref[0, pl.ds(pl.multiple_of(j * tk, tk), tk), :].astype(BF16).astype(F32)
            return jnp.maximum(best, jnp.max(_mm_r(kf * kf, group_of), axis=0, keepdims=True))
        kn_sc[...] = lax.fori_loop(0, k_ref.shape[1] // tk, norm_step, jnp.zeros((1, LANES), F32))

    def score_bound(g):
        qf = qt_sc[g].astype(F32)
        q2 = jnp.max(jnp.sum(qf * qf, axis=0, keepdims=True))
        return jnp.sqrt(q2 * kn_sc[0, g]) * 1.001 + 1.0
    pq = i * tq + _iota((1, tq), 1)
    pq_f = pq.astype(F32)
    ones_rows = jnp.ones((SUM_ROWS, tk), BF16)

    def block(k0, n, near, groups=range(2 * N_HEADS)):
        kb = k_ref[0, pl.ds(k0, n), :].astype(BF16)
        vb = v_ref[0, :, pl.ds(k0, n)].astype(BF16)
        pk = k0 + _iota((n, LANES), 0)
        pk_f = pk.astype(F32)
        if near:
            allowed = (_lanes(pk, tq) >> CHUNK_SHIFT) <= (pq >> CHUNK_SHIFT)
            shift = pq_f - jnp.abs(pq_f - _lanes(pk_f, tq))
        scores = {}
        for g in groups:
            slope = 2.0 ** (-2 * (g // 2 + 1)) * LOG2E
            s = _dot(kb, qt_sc[g])
            scores[g] = jnp.where(allowed, s + slope * shift, NEG) if near else s + _lanes(slope * pk_f, tq)
        m_prevs = {g: m_sc[g] for g in groups}
        m_news = {g: jnp.maximum(m_prevs[g], jnp.max(scores[g], axis=0, keepdims=True)) for g in groups}
        probs = {g: jnp.exp2(scores[g] - m_news[g]) for g in groups}
        for g in groups:
            alpha = jnp.exp2(m_prevs[g] - m_news[g])
            m_sc[g] = m_news[g]
            h = g // 2
            pv = _dot(jnp.concatenate([vb[h * HEAD_DIM:(h + 1) * HEAD_DIM, :], ones_rows[:, :n]], axis=0),
                      probs[g].astype(BF16))
            acc_sc[g] = acc_sc[g] * alpha + pv

    block(pl.multiple_of(i * tq, tq), tq, True)

    @pl.when((i * tq) % tk != 0)
    def _():
        block(pl.multiple_of((i - 1) * tq, tq), tq, False)
    nfull = (i * tq) // tk

    def dead_blocks(h):
        slope = 2.0 ** (-2 * (h + 1)) * LOG2E
        counts = [jnp.floor((jnp.min(m_sc[g]) + DIFF_DEAD - score_bound(g)) / (slope * tk)) for g in (2 * h, 2 * h + 1)]
        return jnp.clip(jnp.minimum(counts[0], counts[1]), 0, nfull).astype(jnp.int32)
    dead0 = dead_blocks(0)
    dead1 = jnp.minimum(dead_blocks(1), dead0)

    def steps(groups):
        def step(j, carry):
            block(pl.multiple_of(j * tk, tk), tk, False, groups)
            return carry
        return step
    lax.fori_loop(dead0, nfull, steps(range(ng)), 0)
    lax.fori_loop(dead1, dead0, steps(range(2, ng)), 0)
    lax.fori_loop(0, dead1, steps(range(4, ng)), 0)

    lam = lam_ref[0]
    norm = lambda g: acc_sc[g, :HEAD_DIM, :] / acc_sc[g, HEAD_DIM:HEAD_DIM + 1, :]
    heads = [norm(2 * h) - lam * norm(2 * h + 1) for h in range(N_HEADS)]
    o = jnp.concatenate(heads, axis=0).T
    ms = _mm_r(o * o, _head_block_ones(GROUP_W).astype(BF16)) * (1.0 / HEAD_DIM)
    o_ref[0] = o * lax.rsqrt(ms + EPS) * bn_ref[...] * (1.0 - lam_init)


def _diff_attn_prompt(lam, q, k_rows, v_feat, bnorm, lam_init):
    b, t, _ = q.shape
    ng = 2 * N_HEADS
    tq = min(256, t)
    tk = 2 * tq if t % (2 * tq) == 0 else tq
    assert t % tq == 0 and tq % LANES == 0
    return pl.pallas_call(
        functools.partial(_diff_prompt_body, tq=tq, tk=tk, lam_init=lam_init),
        grid=(b, t // tq),
        in_specs=[pl.BlockSpec(memory_space=pltpu.SMEM), pl.BlockSpec((1, tq, GROUP_W), lambda i, j: (i, j, 0)),
                  pl.BlockSpec((1, t, GROUP_W), lambda i, j: (i, 0, 0)),
                  pl.BlockSpec((1, GROUP_W, t), lambda i, j: (i, 0, 0)), _resident((1, GROUP_W))],
        out_specs=pl.BlockSpec((1, tq, GROUP_W), lambda i, j: (i, j, 0)),
        out_shape=jax.ShapeDtypeStruct((b, t, GROUP_W), F32),
        scratch_shapes=[pltpu.VMEM((ng, GROUP_W, tq), BF16), pltpu.VMEM((ng, 1, tq), F32),
                        pltpu.VMEM((ng, HEAD_DIM + SUM_ROWS, tq), F32), pltpu.VMEM((1, LANES), F32)],
        compiler_params=_cparams("parallel", "arbitrary"),
    )(lam, q, k_rows, v_feat, bnorm)


def _stick_body(*refs, tq, tk, past, ns, near):
    if near:
        q_ref, kn_ref, vn_ref, kp_ref, vp_ref, o_ref, acc_out, c_out, q4_sc, c_sc, acc_sc = refs
    else:
        q_ref, kp_ref, vp_ref, acc_in, c_in, o_ref, q4_sc, c_sc, acc_sc = refs
    nr = N_HEADS * tq
    lane = _iota((tq, GROUP_W), 1) >> HEAD_SHIFT
    for sq in range(ns):
        q = q_ref[sq] * (HEAD_DIM ** -0.5 * LOG2E)
        for h in range(N_HEADS):
            q4_sc[sq, h * tq:(h + 1) * tq, :] = jnp.where(lane == h, q, 0.0).astype(BF16)
    if near:
        c_sc[...] = jnp.zeros(c_sc.shape, F32)
        acc_sc[...] = jnp.zeros(acc_sc.shape, F32)
    else:
        c_sc[...] = c_in[...]
        acc_sc[...] = acc_in[...]
    pq = past + (_iota((nr, 1), 0) & (tq - 1))
    laters = {w: (_iota((w, w), 0) > _iota((w, w), 1)).astype(BF16)
              for w in {min(tq, STICK_SUB), min(tk, STICK_SUB)}}
    seqs = range(ns)

    def block(k_ref, v_ref, k0, n, masked, pos0=0):
        sub = min(n, STICK_SUB)
        later = laters[sub]
        zs = [_dot(q4_sc[sq], k_ref[sq, :, pl.ds(k0, n)].astype(BF16)) for sq in seqs]
        lss = [jnp.minimum(z, 0.0) - jnp.log2(1.0 + jnp.exp2(-jnp.abs(z))) for z in zs]
        lks = [ls - z for ls, z in zip(lss, zs)]
        if masked:
            mask = (pos0 + k0 + _iota((1, n), 1)) < pq
            lks = [jnp.where(mask, lk, 0.0) for lk in lks]
        his = [lk.astype(BF16) for lk in lks]
        los = [(lk - hi.astype(F32)).astype(BF16) for lk, hi in zip(lks, his)]
        carries = [c_sc[sq] for sq in seqs]
        parts = [[] for _ in seqs]
        for sb in reversed(range(n // sub)):
            sl = slice(sb * sub, (sb + 1) * sub)
            for sq in seqs:
                after = _dot(his[sq][:, sl], later) + _dot(los[sq][:, sl], later)
                parts[sq].append(lss[sq][:, sl] + after + _lanes(carries[sq], sub))
                carries[sq] = carries[sq] + (after[:, 0:1] + lks[sq][:, sb * sub:sb * sub + 1])
        for sq in seqs:
            c_sc[sq] = carries[sq]
            e = parts[sq][0] if len(parts[sq]) == 1 else jnp.concatenate(parts[sq][::-1], axis=1)
            if masked:
                a = jnp.where(mask, jnp.exp2(jnp.where(mask, e, 0.0)), 0.0)
            else:
                a = jnp.exp2(e)
            acc_sc[sq] = acc_sc[sq] + _dot_nt(a.astype(BF16), v_ref[sq, :, pl.ds(k0, n)].astype(BF16))

    if near:
        block(kn_ref, vn_ref, 0, tq, True, pos0=past)
        block(kp_ref, vp_ref, 0, tk, False)
        acc_out[...] = acc_sc[...]
        c_out[...] = c_sc[...]
    else:
        nb = past // tk - 1

        def live():
            return jnp.max(c_sc[...]) > STICK_DEAD

        def past_step(state):
            j, _ = state
            block(kp_ref, vp_ref, pl.multiple_of((nb - 1 - j) * tk, tk), tk, False)
            return j + 1, live()
        lax.while_loop(lambda state: (state[0] < nb) & state[1], past_step, (jnp.int32(0), live()))

    for sq in seqs:
        o = acc_sc[sq, 0:tq, :]
        for h in range(1, N_HEADS):
            o = jnp.where(lane == h, acc_sc[sq, h * tq:(h + 1) * tq, :], o)
        o_ref[sq] = o


def _stick_attn(q, kn, vn, kp, vp, layer):
    b, t, _ = q.shape
    past = kp.shape[3]
    ns = DECODE_SEQS
    tq, tk = t, math.gcd(past, 512)
    assert tq & (tq - 1) == 0 and tk % min(tk, STICK_SUB) == 0 and b % ns == 0
    nr = N_HEADS * tq
    row_spec = lambda w: pl.BlockSpec((ns, tq, w), lambda i: (i, 0, 0))
    seq_spec = pl.BlockSpec((ns, GROUP_W, t), lambda i: (i, 0, 0))
    acc_spec = pl.BlockSpec((ns, nr, GROUP_W), lambda i: (i, 0, 0))
    carry_spec = pl.BlockSpec((ns, nr, LANES), lambda i: (i, 0, 0))
    acc_shape = jax.ShapeDtypeStruct((b, nr, GROUP_W), F32)
    carry_shape = jax.ShapeDtypeStruct((b, nr, LANES), F32)
    scratch = [pltpu.VMEM((ns, nr, GROUP_W), BF16), pltpu.VMEM((ns, nr, LANES), F32), pltpu.VMEM((ns, nr, GROUP_W), F32)]
    last = past // tk - 1
    latest_spec = pl.BlockSpec((None, ns, GROUP_W, tk), lambda i: (layer, i, 0, last))
    out, acc, carry = pl.pallas_call(
        functools.partial(_stick_body, tq=tq, tk=tk, past=past, ns=ns, near=True),
        grid=(b // ns,),
        in_specs=[row_spec(GROUP_W), seq_spec, seq_spec, latest_spec, latest_spec],
        out_specs=[row_spec(GROUP_W), acc_spec, carry_spec],
        out_shape=[jax.ShapeDtypeStruct((b, t, GROUP_W), F32), acc_shape, carry_shape],
        scratch_shapes=scratch,
        compiler_params=_cparams("parallel"),
    )(q, kn, vn, kp, vp)
    if last == 0:
        return out

    def earlier_keys():
        past_spec = pl.BlockSpec((None, ns, GROUP_W, past), lambda i: (layer, i, 0, 0))
        return pl.pallas_call(
            functools.partial(_stick_body, tq=tq, tk=tk, past=past, ns=ns, near=False),
            grid=(b // ns,),
            in_specs=[row_spec(GROUP_W), past_spec, past_spec, acc_spec, carry_spec],
            out_specs=row_spec(GROUP_W),
            out_shape=jax.ShapeDtypeStruct((b, t, GROUP_W), F32),
            scratch_shapes=scratch,
            compiler_params=_cparams("parallel"),
        )(q, kp, vp, acc, carry)

    return lax.cond(jnp.max(carry) > STICK_DEAD, earlier_keys, lambda: out)


def _stick_prompt_body(q_ref, k_ref, v_ref, o_ref, qt_sc, c_sc, acc_sc, *, tq, tk):
    i = pl.program_id(1)
    qt = (q_ref[0] * (HEAD_DIM ** -0.5 * LOG2E)).T
    feat_head = _iota((GROUP_W, tq), 0) >> HEAD_SHIFT
    for h in range(N_HEADS):
        qt_sc[h] = jnp.where(feat_head == h, qt, 0.0).astype(BF16)
    c_sc[...] = jnp.zeros(c_sc.shape, F32)
    acc_sc[...] = jnp.zeros(acc_sc.shape, F32)
    pq = i * tq + _iota((1, tq), 1)
    sub = min(tq, STICK_SUB)
    later = (_iota((sub, sub), 1) > _iota((sub, sub), 0)).astype(BF16)

    def block(k0, n, masked):
        kb = k_ref[0, pl.ds(k0, n), :].astype(BF16)
        vb = v_ref[0, :, pl.ds(k0, n)].astype(BF16)
        if masked:
            mask = _lanes(k0 + _iota((n, LANES), 0), tq) < pq
        heads = range(N_HEADS)
        zs = [_dot(kb, qt_sc[h]) for h in heads]
        lss = [jnp.minimum(z, 0.0) - jnp.log2(1.0 + jnp.exp2(-jnp.abs(z))) for z in zs]
        lks = [ls - z for ls, z in zip(lss, zs)]
        if masked:
            lks = [jnp.where(mask, lk, 0.0) for lk in lks]
        es = []
        for h in heads:
            lk = lks[h]
            hi = lk.astype(BF16)
            lo = (lk - hi.astype(F32)).astype(BF16)
            carry = c_sc[h]
            parts = []
            for sb in reversed(range(n // sub)):
                sl = slice(sb * sub, (sb + 1) * sub)
                after = _dot(later, hi[sl]) + _dot(later, lo[sl])
                parts.append(lss[h][sl] + after + carry)
                carry = carry + (after[0:1] + lk[sb * sub:sb * sub + 1])
            c_sc[h] = carry
            es.append(parts[0] if len(parts) == 1 else jnp.concatenate(parts[::-1], axis=0))
        for h in heads:
            if masked:
                a = jnp.where(mask, jnp.exp2(jnp.where(mask, es[h], 0.0)), 0.0)
            else:
                a = jnp.exp2(es[h])
            acc_sc[h] = acc_sc[h] + _dot(vb[h * HEAD_DIM:(h + 1) * HEAD_DIM, :], a.astype(BF16))

    block(pl.multiple_of(i * tq, tq), tq, True)

    @pl.when((i * tq) % tk != 0)
    def _():
        block(pl.multiple_of((i - 1) * tq, tq), tq, False)
    nfull = (i * tq) // tk

    def live():
        return jnp.max(c_sc[...]) > STICK_DEAD

    def prev_step(state):
        j, _ = state
        block(pl.multiple_of((nfull - 1 - j) * tk, tk), tk, False)
        return j + 1, live()
    lax.while_loop(lambda state: (state[0] < nfull) & state[1], prev_step, (jnp.int32(0), live()))

    o_ref[0] = jnp.concatenate([acc_sc[h] for h in range(N_HEADS)], axis=0).T


def _stick_attn_prompt(q, k_rows, v_feat):
    b, t, _ = q.shape
    tq = min(256, t)
    tk = tq
    assert t % tq == 0 and tq % LANES == 0 and tk % min(tk, STICK_SUB) == 0
    return pl.pallas_call(
        functools.partial(_stick_prompt_body, tq=tq, tk=tk),
        grid=(b, t // tq),
        in_specs=[pl.BlockSpec((1, tq, GROUP_W), lambda i, j: (i, j, 0)),
                  pl.BlockSpec((1, t, GROUP_W), lambda i, j: (i, 0, 0)),
                  pl.BlockSpec((1, GROUP_W, t), lambda i, j: (i, 0, 0))],
        out_specs=pl.BlockSpec((1, tq, GROUP_W), lambda i, j: (i, j, 0)),
        out_shape=jax.ShapeDtypeStruct((b, t, GROUP_W), F32),
        scratch_shapes=[pltpu.VMEM((N_HEADS, GROUP_W, tq), BF16), pltpu.VMEM((N_HEADS, 1, tq), F32),
                        pltpu.VMEM((N_HEADS, HEAD_DIM, tq), F32)],
        compiler_params=_cparams("parallel", "parallel"),
    )(q, k_rows, v_feat)


def _pair_mask(n, s, reps=1):
    r, c = _iota((reps * n, n), 0) & (n - 1), _iota((reps * n, n), 1)
    return ((r // (2 * s)) == (c // (2 * s))) & (((r // s) & 1) == 1) & (((c // s) & 1) == 0)


def _state_spec(nb):
    return pl.BlockSpec((nb, N_HEADS, HEAD_DIM, HEAD_DIM), lambda i, j: (i, 0, 0, 0))


def _load_state(s_sc, s0_ref, transpose):
    s_sc[...] = jnp.zeros(s_sc.shape, F32)
    for b in range(s_sc.shape[0]):
        for h in range(N_HEADS):
            blk = s0_ref[b, h]
            s_sc[b, h * HEAD_DIM:(h + 1) * HEAD_DIM, h * HEAD_DIM:(h + 1) * HEAD_DIM] = blk.T if transpose else blk


def _store_state(sout_ref, s_sc, transpose):
    for b in range(s_sc.shape[0]):
        for h in range(N_HEADS):
            blk = s_sc[b, h * HEAD_DIM:(h + 1) * HEAD_DIM, h * HEAD_DIM:(h + 1) * HEAD_DIM]
            sout_ref[b, h] = blk.T if transpose else blk


def _head_rows(x, lane_head):
    return jnp.concatenate([jnp.where(lane_head == h, x, 0.0) for h in range(N_HEADS)], axis=0)


def _head_diag(x, lane_head):
    n = x.shape[0] // N_HEADS
    out = x[:n]
    for h in range(1, N_HEADS):
        out = jnp.where(lane_head == h, x[h * n:(h + 1) * n], out)
    return out


def _gdn_body(x_ref, z_ref, abg_ref, cw_ref, cs_ref, s0_ref, alog_ref, dtb_ref, an_ref,
              o_ref, sout_ref, s_sc, xb_sc, *, nc, nb):
    c = pl.program_id(1)
    L = CHUNK
    pad = 8

    @pl.when(c == 0)
    def _():
        _load_state(s_sc, s0_ref, transpose=False)
        xb_sc[:, pad - 3:pad, :] = cs_ref[...]

    ones_bd = _head_block_ones(GROUP_W)
    ones_bd16 = ones_bd.astype(BF16)
    er, ec = _iota((ABG_W, GROUP_W), 0), _iota((ABG_W, GROUP_W), 1) >> HEAD_SHIFT
    e_beta = (er == ec).astype(BF16)
    e_g = (er == ec + N_HEADS).astype(BF16)
    ri, ci = _iota((L, L), 0), _iota((L, L), 1)
    tri16 = (ri >= ci).astype(BF16)
    incl = ri >= ci
    strict = ri > ci
    eye = (ri == ci).astype(F32)
    pair_masks = [_pair_mask(L, sz) for sz in (1, 2, 4, 8, 16, 32)]
    lane_head = _iota((1, GROUP_W), 1) >> HEAD_SHIFT
    cw = cw_ref[...]

    seqs = range(nb)
    us = []
    for b in seqs:
        x = x_ref[b]
        xb_sc[b, pad:pad + L, :] = x
        u = (x * cw[3:4] + xb_sc[b, pad - 1:pad - 1 + L, :] * cw[2:3]
             + xb_sc[b, pad - 2:pad - 2 + L, :] * cw[1:2] + xb_sc[b, pad - 3:pad - 3 + L, :] * cw[0:1])
        xb_sc[b, pad - 3:pad, :] = x[L - 3:L, :]
        us.append(u * _sigmoid(u))
    qs = [u[:, :GROUP_W] for u in us]
    ks = [u[:, GROUP_W:2 * GROUP_W] for u in us]
    vs = [u[:, 2 * GROUP_W:] for u in us]
    qs = [q * lax.rsqrt(_mm_r(q * q, ones_bd16) + EPS) * HEAD_DIM ** -0.5 for q in qs]
    ks = [k * lax.rsqrt(_mm_r(k * k, ones_bd16) + EPS) for k in ks]

    abgs = [abg_ref[b] for b in seqs]
    betas = [_mm_r(_sigmoid(abg), e_beta) for abg in abgs]
    gc_ns = [_mm_l(tri16, -jnp.exp(alog_ref[...]) * _softplus(abg + dtb_ref[...])) for abg in abgs]
    gc_ts = [gc_n.T for gc_n in gc_ns]
    gcs = [_mm_r(gc_n, e_g) for gc_n in gc_ns]

    kbs = [k * beta for k, beta in zip(ks, betas)]
    k16s = [k.astype(BF16) for k in ks]
    a4s = [_bdot(_head_rows(kb, lane_head).astype(BF16), k16, _NT) for kb, k16 in zip(kbs, k16s)]
    qk4s = [_bdot(_head_rows(q, lane_head).astype(BF16), k16, _NT) for q, k16 in zip(qs, k16s)]
    decs = []
    for b in seqs:
        dec = []
        for h in range(N_HEADS):
            diff = gc_ns[b][:, N_HEADS + h:N_HEADS + h + 1] - gc_ts[b][N_HEADS + h:N_HEADS + h + 1, :]
            dec.append(jnp.where(incl, jnp.exp(jnp.where(incl, diff, 0.0)), 0.0))
        decs.append(jnp.stack(dec))

    a_all = jnp.concatenate([jnp.where(strict, a4s[b].reshape(N_HEADS, L, L) * decs[b], 0.0) for b in seqs], axis=0)
    t_inv = eye - jnp.where(pair_masks[0], a_all, 0.0)
    a16 = a_all.astype(BF16)
    for pm in pair_masks[1:]:
        t16 = t_inv.astype(BF16)
        t_inv = t_inv - _bdot(t16, _bdot(jnp.where(pm, a16, 0.0), t16, _BNN).astype(BF16), _BNN)

    exp_gs = [jnp.exp(gc) for gc in gcs]
    s_olds = [s_sc[b] for b in seqs]
    s16s = [s.astype(BF16) for s in s_olds]
    rhss = [vs[b] * betas[b] - _bdot((kbs[b] * exp_gs[b]).astype(BF16), s16s[b], _NN) for b in seqs]
    ws = [_head_diag(_mm1(t_inv[b * N_HEADS:(b + 1) * N_HEADS].reshape(N_HEADS * L, L), rhss[b]), lane_head)
          for b in seqs]
    outs = [_bdot((qs[b] * exp_gs[b]).astype(BF16), s16s[b], _NN)
            + _head_diag(_mm1((qk4s[b].reshape(N_HEADS, L, L) * decs[b]).reshape(N_HEADS * L, L), ws[b]), lane_head)
            for b in seqs]
    for b in seqs:
        g_last = gcs[b][L - 1:L, :]
        k_dec = ks[b] * jnp.exp(g_last - gcs[b])
        s_sc[b] = s_olds[b] * jnp.exp(g_last) + _mm1(k_dec.T, ws[b]) * ones_bd
    for b in seqs:
        o = outs[b]
        ms = _mm_r(o * o, ones_bd16) * (1.0 / HEAD_DIM)
        zg = z_ref[b]
        o_ref[b] = o * lax.rsqrt(ms + EPS) * an_ref[...] * (zg * _sigmoid(zg))

    @pl.when(c == nc - 1)
    def _():
        _store_state(sout_ref, s_sc, transpose=False)


def _gdn(aqkv, az, abg, conv_w, conv_state, s0, alog_row, dtb_row, anorm_row):
    b, t, _ = aqkv.shape
    nc = t // CHUNK
    nb = SEQ_PER_STEP
    assert b % nb == 0
    cmap = lambda i, j: (i, j, 0)
    bmap = lambda i, j: (i, 0, 0)
    return pl.pallas_call(
        functools.partial(_gdn_body, nc=nc, nb=nb),
        grid=(b // nb, nc),
        in_specs=[pl.BlockSpec((nb, CHUNK, 3 * GROUP_W), cmap), pl.BlockSpec((nb, CHUNK, GROUP_W), cmap),
                  pl.BlockSpec((nb, CHUNK, ABG_W), cmap), _resident((CONV_W, 3 * GROUP_W)),
                  pl.BlockSpec((nb, CONV_W - 1, 3 * GROUP_W), bmap), _state_spec(nb),
                  _resident((1, ABG_W)), _resident((1, ABG_W)), _resident((1, GROUP_W))],
        out_specs=[pl.BlockSpec((nb, CHUNK, GROUP_W), cmap), _state_spec(nb)],
        out_shape=[jax.ShapeDtypeStruct((b, t, GROUP_W), F32), jax.ShapeDtypeStruct(s0.shape, F32)],
        scratch_shapes=[pltpu.VMEM((nb, GROUP_W, GROUP_W), F32), pltpu.VMEM((nb, 8 + CHUNK, 3 * GROUP_W), F32)],
        compiler_params=_cparams("parallel", "arbitrary"),
    )(aqkv, az, abg, conv_w, conv_state, s0, alog_row, dtb_row, anorm_row)


def _hgrn_body(q_ref, f_ref, i_ref, g_ref, lb_ref, dn_ref, s0_ref, o_ref, sout_ref, s_sc, *, nc, nb):
    c = pl.program_id(1)
    L = CHUNK

    @pl.when(c == 0)
    def _():
        _load_state(s_sc, s0_ref, transpose=True)

    lb = lb_ref[...]
    ones_bd = _head_block_ones(GROUP_W)
    ones_bd16 = ones_bd.astype(BF16)
    ri, ci = _iota((L, L), 0), _iota((L, L), 1)
    tri16 = (ri >= ci).astype(BF16)
    sizes = (1, 2, 4, 8, 16, 32)
    sel16 = jnp.concatenate([(ci == (ri // (2 * sz)) * (2 * sz) + sz).astype(BF16) for sz in sizes], axis=0)
    eye4 = (_iota((N_HEADS * L, L), 0) & (L - 1)) == _iota((N_HEADS * L, L), 1)
    pair_masks4 = [_pair_mask(L, sz, reps=N_HEADS) for sz in sizes]
    lane_head = _iota((1, GROUP_W), 1) >> HEAD_SHIFT
    rows = _iota((L, 1), 0)

    pre = []
    for b in range(nb):
        fl = f_ref[b]
        log_f = _log_sigmoid(fl) + jnp.log(1.0 + lb * jnp.exp(-fl))
        k = (1.0 - lb) * _sigmoid(-fl)
        q = q_ref[b] * HEAD_DIM ** -0.5
        bc = _mm_l(tri16, log_f)
        pre.append((q, k, bc))
    b_refs = [_mm_l(sel16, bc) for _, _, bc in pre]
    a_all = [jnp.where(eye4, _mm1(_head_rows(q, lane_head), k, _NT), 0.0) for q, k, _ in pre]
    for lv, sz in enumerate(sizes):
        later = ((rows // sz) & 1) == 1
        for b in range(nb):
            q, k, bc = pre[b]
            b_ref = b_refs[b][lv * L:(lv + 1) * L]
            x = jnp.where(later, q * jnp.exp(jnp.where(later, bc - b_ref, 0.0)), 0.0)
            y = jnp.where(later, 0.0, k * jnp.exp(jnp.where(later, 0.0, b_ref - bc)))
            a_all[b] = a_all[b] + jnp.where(pair_masks4[lv], _mm1(_head_rows(x, lane_head), y, _NT), 0.0)

    for b in range(nb):
        q, k, bc = pre[b]
        v = i_ref[b]
        st = s_sc[b]
        o = _mm1(q * jnp.exp(bc), st, _NT) + _head_diag(_mm1(a_all[b], v), lane_head)
        b_last = bc[L - 1:L, :]
        st_new = st * jnp.exp(b_last) + _mm1(v.T, k * jnp.exp(b_last - bc)) * ones_bd
        s_sc[b] = st_new

        ms = _mm_r(o * o, ones_bd16) * (1.0 / HEAD_DIM)
        zg = g_ref[b]
        o_ref[b] = o * lax.rsqrt(ms + EPS) * dn_ref[...] * (zg * _sigmoid(zg))

    @pl.when(c == nc - 1)
    def _():
        _store_state(sout_ref, s_sc, transpose=True)


def _hgrn(dq, df, di, dg, lb_row, dnorm_row, s0):
    b, t, _ = dq.shape
    nc = t // CHUNK
    nb = SEQ_PER_STEP
    assert b % nb == 0
    cmap = lambda i, j: (i, j, 0)
    bmap = lambda i, j: (i, 0, 0)
    cspec = pl.BlockSpec((nb, CHUNK, GROUP_W), cmap)
    return pl.pallas_call(
        functools.partial(_hgrn_body, nc=nc, nb=nb),
        grid=(b // nb, nc),
        in_specs=[cspec, cspec, cspec, cspec, _resident((1, GROUP_W)), _resident((1, GROUP_W)),
                  _state_spec(nb)],
        out_specs=[cspec, _state_spec(nb)],
        out_shape=[jax.ShapeDtypeStruct((b, t, GROUP_W), F32), jax.ShapeDtypeStruct(s0.shape, F32)],
        scratch_shapes=[pltpu.VMEM((nb, GROUP_W, GROUP_W), F32)],
        compiler_params=_cparams("parallel", "arbitrary"),
    )(dq, df, di, dg, lb_row, dnorm_row, s0)


IN_WIDTHS = (3 * GROUP_W,) + (GROUP_W,) * 11 + (ABG_W,)
KV_SEGMENTS = (3, 4, 6, 7)
K_SEGMENTS = (3, 6)


def _layer(x, l, st, mk, mv, mem_layer, prm, final_gain):
    b, t, _ = x.shape
    m = b * t
    tm = math.gcd(m, ROW_TILE)
    has_past = st["b_k"] is not None
    outs = _fused_linear(x.reshape(m, D_MODEL), prm["w_in"], l, IN_WIDTHS, prm["norm_mix"][l], tm,
                         feat_major=KV_SEGMENTS, rows_per_seq=t, rows_too=() if has_past else K_SEGMENTS)
    outs = [o if idx in KV_SEGMENTS else o.reshape(b, t, -1) for idx, o in enumerate(outs)]
    (a_qkv, a_z, b_q, b_k, b_v, c_q, c_k, c_v, d_q, d_f, d_i, d_g, a_bg) = outs[:len(IN_WIDTHS)]

    o_a, a_s = _gdn(a_qkv, a_z, a_bg, prm["a_conv_w"][l], st["a_conv"], st["a_S"].astype(F32),
                    prm["alog_row"][l], prm["dtb_row"][l], prm["a_norm"][l])
    lam_init = 0.8 - 0.6 * math.exp(-0.3 * l)
    if has_past:
        o_b = _diff_attn(prm["lam"][l], b_q, b_k, b_v, st["b_k"], st["b_v"], l, prm["b_norm"][l], lam_init)
        o_c = _stick_attn(c_q, c_k, c_v, st["c_k"], st["c_v"], l)
    else:
        b_k_rows, c_k_rows = outs[len(IN_WIDTHS):]
        o_b = _diff_attn_prompt(prm["lam"][l], b_q, b_k_rows, b_v, prm["b_norm"][l], lam_init)
        o_c = _stick_attn_prompt(c_q, c_k_rows, c_v)
    o_d, d_s = _hgrn(d_q, d_f, d_i, d_g, prm["lb"][l], prm["d_norm"][l], st["d_S"].astype(F32))

    x2 = _cross(x, (o_a, o_b, o_c, o_d), prm["w_out"], prm["norm_cross"][l], prm["w_cq"], mk, mv, mem_layer,
                prm["w_co"], l)
    x2 = _ffn(x2.reshape(m, D_MODEL), prm["norm_ffn"][l], prm["w_gate"], prm["w_up"], prm["w_down"], l,
              final_gain=final_gain, tm=tm)

    if t >= CONV_W - 1:
        conv_new = a_qkv[:, t - (CONV_W - 1):, :]
    else:
        conv_new = jnp.concatenate([st["a_conv"], a_qkv], axis=1)[:, -(CONV_W - 1):, :]
    heads = lambda a: jnp.transpose(a.reshape(b, N_HEADS, HEAD_DIM, t), (0, 3, 1, 2))
    new = (conv_new, a_s, heads(b_k), heads(b_v), heads(c_k), heads(c_v), d_s)
    return x2.reshape(b, t, D_MODEL), new


def kernel(x_prompt, x_sample, mem_prompt, state_a_conv, state_a_S, cache_b_k, cache_b_v, cache_c_k, cache_c_v,
           state_d_S, cache_mem_k, cache_mem_v, norm_mix, w_in, a_conv_w, a_A_log, a_dt_bias, a_norm, b_lam_q1,
           b_lam_k1, b_lam_q2, b_lam_k2, b_norm, d_lb, d_norm, w_out, norm_cross, norm_memtok, w_cq, w_ck, w_cv,
           w_co, norm_ffn, w_gate, w_up, w_down, norm_final):
    depth = w_in.shape[0]
    n_main = 4 * GROUP_W
    w_in_p = jnp.concatenate(
        [w_in[:, :, :n_main], w_in[:, :, n_main + 2 * N_HEADS:], w_in[:, :, n_main:n_main + 2 * N_HEADS],
         jnp.zeros((depth, D_MODEL, ABG_W - 2 * N_HEADS), w_in.dtype)], axis=2).astype(BF16)
    pad_row = lambda a: jnp.pad(a.astype(F32), ((0, 0), (N_HEADS, ABG_W - 2 * N_HEADS)))[:, None, :]
    tile_row = lambda a: jnp.tile(a.astype(F32), (1, N_HEADS))[:, None, :]
    p_lb = jax.nn.softmax(d_lb.astype(F32), axis=0)
    lam = (jnp.exp(jnp.sum(b_lam_q1.astype(F32) * b_lam_k1.astype(F32), axis=-1))
           - jnp.exp(jnp.sum(b_lam_q2.astype(F32) * b_lam_k2.astype(F32), axis=-1))
           + jnp.array([0.8 - 0.6 * math.exp(-0.3 * l) for l in range(depth)], F32))
    prm = {
        "w_in": w_in_p, "norm_mix": norm_mix, "a_conv_w": a_conv_w.astype(F32),
        "alog_row": pad_row(a_A_log), "dtb_row": pad_row(a_dt_bias), "a_norm": tile_row(a_norm),
        "lam": lam[:, None], "b_norm": tile_row(b_norm), "lb": (jnp.cumsum(p_lb, axis=0) - p_lb[0])[:, None, :],
        "d_norm": tile_row(d_norm), "w_out": w_out.astype(BF16), "norm_cross": norm_cross,
        "w_cq": w_cq.astype(BF16), "w_co": w_co.astype(BF16), "norm_ffn": norm_ffn,
        "w_gate": w_gate.astype(BF16), "w_up": w_up.astype(BF16), "w_down": w_down.astype(BF16),
    }
    w_ckv = jnp.concatenate([w_ck, w_cv], axis=2).astype(BF16)

    bp, tp, _ = x_prompt.shape
    n_mem = mem_prompt.shape[1]
    h = x_prompt
    p_new = []
    for l in range(depth):
        mk, mv = _fused_linear(mem_prompt.reshape(bp * n_mem, D_MODEL), w_ckv, l, (D_MODEL, D_MODEL),
                               norm_memtok[l], math.gcd(bp * n_mem, ROW_TILE))
        mk = mk.reshape(1, bp, n_mem, D_MODEL)
        mv = mv.reshape(1, bp, n_mem, D_MODEL)
        st = {"a_conv": jnp.zeros((bp, CONV_W - 1, 3 * GROUP_W), F32),
              "a_S": jnp.zeros((bp, N_HEADS, HEAD_DIM, HEAD_DIM), F32), "b_k": None, "b_v": None,
              "c_k": None, "c_v": None, "d_S": jnp.zeros((bp, N_HEADS, HEAD_DIM, HEAD_DIM), F32)}
        h, new = _layer(h, l, st, mk, mv, 0, prm, norm_final if l == depth - 1 else None)
        mem4 = lambda a: a.reshape(bp, n_mem, MEM_HEADS, MEM_HEAD_DIM)
        p_new.append(new + (mem4(mk), mem4(mv)))
    y_prompt = h
    p_stacked = [jnp.stack(c) for c in zip(*p_new)]

    bs, ts, _ = x_sample.shape
    past = cache_b_k.shape[2]
    h = x_sample
    s_new = []
    flat = lambda a: jnp.transpose(a, (0, 1, 3, 4, 2)).reshape(depth, bs, GROUP_W, past)
    caches = {"b_k": flat(cache_b_k), "b_v": flat(cache_b_v), "c_k": flat(cache_c_k), "c_v": flat(cache_c_v)}
    for l in range(depth):
        st = {"a_conv": state_a_conv[l], "a_S": state_a_S[l], "d_S": state_d_S[l], **caches}
        h, new = _layer(h, l, st, cache_mem_k, cache_mem_v, l, prm, norm_final if l == depth - 1 else None)
        s_new.append(new)
    y_sample = h
    s_stacked = [jnp.stack(c) for c in zip(*s_new)]

    return (y_prompt, y_sample, *p_stacked, *s_stacked)
```

```python
import functools
import math

import jax
import jax.numpy as jnp
from jax import lax
from jax.experimental import pallas as pl
from jax.experimental.pallas import tpu as pltpu

F32 = jnp.float32
BF16 = jnp.bfloat16

D_MODEL = 1024
GROUP_W = 256
N_HEADS = 4
HEAD_DIM = 64
DIFF_HALF = 32
CHUNK = 64
CONV_W = 4
MEM_HEADS = 4
MEM_HEAD_DIM = 256
D_FF = 2816
HEAD_SHIFT = HEAD_DIM.bit_length() - 1
CHUNK_SHIFT = CHUNK.bit_length() - 1
EPS = 1e-6
NEG = -1e30
LOG2E = 1.4426950408889634
LANES = 128
ABG_W = 128
FF_CHUNK = 256
ROW_TILE = 512
SUM_ROWS = 16
DECODE_SEQS = 2
CROSS_SEQS = 4
DIFF_DEAD = -160.0
STICK_DEAD = -160.0
STICK_SUB = 256
SEQ_PER_STEP = 4
VMEM_LIMIT = 56 * 1024 * 1024


def _cparams(*sem):
    return pltpu.CompilerParams(dimension_semantics=sem, vmem_limit_bytes=VMEM_LIMIT)


def _dot(a, b):
    return jnp.dot(a, b, preferred_element_type=F32)


def _dot_nt(a, b):
    return lax.dot_general(a, b, (((1,), (1,)), ((), ())), preferred_element_type=F32)


_NN = (((1,), (0,)), ((), ()))
_NT = (((1,), (1,)), ((), ()))
_BNN = (((2,), (1,)), ((0,), (0,)))


def _split(x):
    hi = x.astype(BF16)
    return hi, (x - hi.astype(F32)).astype(BF16)


def _bdot(a, b, dims):
    return lax.dot_general(a, b, dims, preferred_element_type=F32)


def _mm1(a, b, dims=_NN):
    return _bdot(a.astype(BF16), b.astype(BF16), dims)


def _mm_r(a, b16):
    ah, al = _split(a)
    return _bdot(ah, b16, _NN) + _bdot(al, b16, _NN)


def _mm_l(a16, b):
    bh, bl = _split(b)
    return _bdot(a16, bh, _NN) + _bdot(a16, bl, _NN)


def _rms(x, g):
    return x * lax.rsqrt(jnp.mean(x * x, axis=-1, keepdims=True) + EPS) * g


def _sigmoid(x):
    return 1.0 / (1.0 + jnp.exp(-x))


def _log_sigmoid(x):
    return jnp.minimum(x, 0.0) - jnp.log(1.0 + jnp.exp(-jnp.abs(x)))


def _softplus(x):
    return jnp.maximum(x, 0.0) + jnp.log(1.0 + jnp.exp(-jnp.abs(x)))


def _iota(shape, dim):
    return lax.broadcasted_iota(jnp.int32, shape, dim)


def _lanes(x, n):
    return x[:, :n] if n <= LANES else jnp.concatenate([x] * (n // LANES), axis=1)


def _head_block_ones(n):
    return ((_iota((n, n), 0) >> HEAD_SHIFT) == (_iota((n, n), 1) >> HEAD_SHIFT)).astype(F32)


def _resident(shape):
    return pl.BlockSpec(shape, lambda *_: (0,) * len(shape), pipeline_mode=pl.Buffered(1))


def _weight(w, layer):
    return pl.BlockSpec((None,) + w.shape[1:], lambda *_: (layer, 0, 0), pipeline_mode=pl.Buffered(1))


def _linear_body(x_ref, g_ref, w_ref, *out_refs, segs, feat_major, rows_too):
    xb = _rms(x_ref[...], g_ref[...]).astype(BF16)
    extra_refs = dict(zip(rows_too, out_refs[len(segs):]))
    for idx, (o_ref, (s, e)) in enumerate(zip(out_refs, segs)):
        acc = _dot(xb, w_ref[:, s:e])
        if idx in feat_major:
            rows = acc.shape[0] // o_ref.shape[0]
            for sq in range(o_ref.shape[0]):
                o_ref[sq] = acc[sq * rows:(sq + 1) * rows].T
        else:
            o_ref[...] = acc
        if idx in extra_refs:
            extra_refs[idx][...] = acc


def _fused_linear(x, w, layer, widths, gain, tm, feat_major=(), rows_per_seq=None, rows_too=()):
    m, k = x.shape
    n = w.shape[2]
    segs = []
    s = 0
    for wd in widths:
        segs.append((s, s + wd))
        s += wd
    assert s == n and m % tm == 0
    out_specs, out_shape = [], []
    for idx, wd in enumerate(widths):
        if idx in feat_major:
            if rows_per_seq >= tm:
                tiles = rows_per_seq // tm
                assert rows_per_seq % tm == 0
                out_specs.append(pl.BlockSpec((1, wd, tm), lambda i, tiles=tiles: (i // tiles, 0, i % tiles)))
            else:
                assert tm % rows_per_seq == 0
                out_specs.append(pl.BlockSpec((tm // rows_per_seq, wd, rows_per_seq), lambda i: (i, 0, 0)))
            out_shape.append(jax.ShapeDtypeStruct((m // rows_per_seq, wd, rows_per_seq), F32))
        else:
            out_specs.append(pl.BlockSpec((tm, wd), lambda i: (i, 0)))
            out_shape.append(jax.ShapeDtypeStruct((m, wd), F32))
    for idx in rows_too:
        out_specs.append(pl.BlockSpec((tm, widths[idx]), lambda i: (i, 0)))
        out_shape.append(jax.ShapeDtypeStruct((m, widths[idx]), F32))
    assert w.shape[1] == k
    return pl.pallas_call(
        functools.partial(_linear_body, segs=tuple(segs), feat_major=tuple(feat_major), rows_too=tuple(rows_too)),
        grid=(m // tm,),
        in_specs=[pl.BlockSpec((tm, k), lambda i: (i, 0)), _resident((1, k)), _weight(w, layer)],
        out_specs=out_specs,
        out_shape=out_shape,
        compiler_params=_cparams("parallel"),
    )(x, gain.reshape(1, -1), w)


def _ffn_body(*refs, final):
    if final:
        x_ref, g_ref, wg_ref, wu_ref, wd_ref, gf_ref, o_ref = refs
    else:
        x_ref, g_ref, wg_ref, wu_ref, wd_ref, o_ref = refs
    x = x_ref[...]
    hb = _rms(x, g_ref[...]).astype(BF16)
    acc = x
    for c in range(0, D_FF, FF_CHUNK):
        gate = _dot(hb, wg_ref[:, c:c + FF_CHUNK])
        up = _dot(hb, wu_ref[:, c:c + FF_CHUNK])
        act = (gate * _sigmoid(gate) * up).astype(BF16)
        acc = acc + _dot(act, wd_ref[c:c + FF_CHUNK, :])
    if final:
        acc = _rms(acc, gf_ref[...])
    o_ref[...] = acc


def _ffn(x, gain, wg, wu, wd, layer, final_gain=None, tm=256):
    m = x.shape[0]
    final = final_gain is not None
    in_specs = [pl.BlockSpec((tm, D_MODEL), lambda i: (i, 0)), _resident((1, D_MODEL)),
                _weight(wg, layer), _weight(wu, layer), _weight(wd, layer)]
    args = [x, gain.reshape(1, -1), wg, wu, wd]
    if final:
        in_specs.append(_resident((1, D_MODEL)))
        args.append(final_gain.reshape(1, -1))
    return pl.pallas_call(
        functools.partial(_ffn_body, final=final),
        grid=(m // tm,),
        in_specs=in_specs,
        out_specs=pl.BlockSpec((tm, D_MODEL), lambda i: (i, 0)),
        out_shape=jax.ShapeDtypeStruct((m, D_MODEL), F32),
        compiler_params=_cparams("parallel"),
    )(*args)


def _cross_body(x_ref, oa_ref, ob_ref, oc_ref, od_ref, wout_ref, g_ref, wq_ref, mk_ref, mv_ref, wo_ref, o_ref):
    ns, tq, _ = x_ref.shape
    rows = ns * tq
    x = x_ref[...].reshape(rows, D_MODEL)
    for gi, m_ref in enumerate((oa_ref, ob_ref, oc_ref, od_ref)):
        x = x + _dot(m_ref[...].reshape(rows, GROUP_W).astype(BF16), wout_ref[gi * GROUP_W:(gi + 1) * GROUP_W, :])
    hb = _rms(x, g_ref[...]).astype(BF16)
    q = (_dot(hb, wq_ref[...]) * MEM_HEAD_DIM ** -0.5).astype(BF16)
    head = lambda h: slice(h * MEM_HEAD_DIM, (h + 1) * MEM_HEAD_DIM)
    pairs = [(sq, h) for sq in range(ns) for h in range(MEM_HEADS)]
    if len(mk_ref.shape) == 4:
        head_major = {}

        def mem(ref, sq, h):
            if (id(ref), sq) not in head_major:
                head_major[id(ref), sq] = jnp.swapaxes(ref[sq], 0, 1).astype(BF16)
            return head_major[id(ref), sq][h]
    else:
        mem = lambda ref, sq, h: ref[sq, :, head(h)].astype(BF16)
    scores = [_dot_nt(q[sq * tq:(sq + 1) * tq, head(h)], mem(mk_ref, sq, h)) for sq, h in pairs]
    probs = []
    for s in scores:
        p = jnp.exp(s - jnp.max(s, axis=-1, keepdims=True))
        probs.append((p / jnp.sum(p, axis=-1, keepdims=True)).astype(BF16))
    outs = [_dot(p, mem(mv_ref, sq, h)).astype(BF16) for p, (sq, h) in zip(probs, pairs)]
    o = jnp.concatenate([jnp.concatenate(outs[sq * MEM_HEADS:(sq + 1) * MEM_HEADS], axis=1) for sq in range(ns)], axis=0)
    o_ref[...] = (x + _dot(o, wo_ref[...])).reshape(ns, tq, D_MODEL)


def _cross(x, mix, w_out, gain, wq, mk, mv, layer, wo, w_layer):
    b, t, _ = x.shape
    nm = mk.shape[2]
    tq = min(ROW_TILE, t)
    ns = CROSS_SEQS if CROSS_SEQS * t <= ROW_TILE else 1
    assert t % tq == 0 and b % ns == 0
    row_spec = lambda w: pl.BlockSpec((ns, tq, w), lambda i, j: (i, j, 0))
    mem_spec = pl.BlockSpec((None, ns) + mk.shape[2:], lambda i, j: (layer, i) + (0,) * (mk.ndim - 2))
    return pl.pallas_call(
        _cross_body,
        grid=(b // ns, t // tq),
        in_specs=[row_spec(D_MODEL)] + [row_spec(GROUP_W)] * 4 + [_weight(w_out, w_layer), _resident((1, D_MODEL)),
                  _weight(wq, w_layer), mem_spec, mem_spec, _weight(wo, w_layer)],
        out_specs=row_spec(D_MODEL),
        out_shape=jax.ShapeDtypeStruct((b, t, D_MODEL), F32),
        compiler_params=_cparams("parallel", "parallel"),
    )(x, *mix, w_out, gain.reshape(1, -1), wq, mk, mv, wo)


def _diff_body(lam_ref, q_ref, kn_ref, vn_ref, kp_ref, vp_ref, bn_ref, o_ref, q8_sc, m_sc, l_sc, acc_sc, *,
               tq, tk, past, lam_init, ns):
    ng = 2 * N_HEADS
    lane = _iota((tq, GROUP_W), 1)
    for sq in range(ns):
        q = q_ref[sq] * (DIFF_HALF ** -0.5 * LOG2E)
        for g in range(ng):
            lo = (g // 2) * HEAD_DIM + (g % 2) * DIFF_HALF
            q8_sc[sq, g * tq:(g + 1) * tq, :] = jnp.where((lane >= lo) & (lane < lo + DIFF_HALF), q, 0.0).astype(BF16)
    m_sc[...] = jnp.full(m_sc.shape, -jnp.inf, F32)
    l_sc[...] = jnp.zeros(l_sc.shape, F32)
    acc_sc[...] = jnp.zeros(acc_sc.shape, F32)
    pq = past + _iota((tq, 1), 0)
    pq_f = pq.astype(F32)
    lanes = _lanes
    pairs = [(sq, g) for sq in range(ns) for g in range(ng)]

    def block(k_ref, v_ref, k0, n, near, pos0=0):
        pk = pos0 + k0 + _iota((1, n), 1)
        pk_f = pk.astype(F32)
        if near:
            allowed = (pk >> CHUNK_SHIFT) <= (pq >> CHUNK_SHIFT)
            shift = pq_f - jnp.abs(pq_f - pk_f)
        s_all = [_dot(q8_sc[sq], k_ref[sq, :, pl.ds(k0, n)].astype(BF16)) for sq in range(ns)]
        scores = {}
        for sq, g in pairs:
            slope = 2.0 ** (-2 * (g // 2 + 1)) * LOG2E
            s = s_all[sq][g * tq:(g + 1) * tq]
            scores[sq, g] = jnp.where(allowed, s + slope * shift, NEG) if near else s + slope * pk_f
        m_prevs = {k: m_sc[k[0], k[1]] for k in pairs}
        m_news = {k: jnp.maximum(m_prevs[k], jnp.max(scores[k], axis=-1, keepdims=True)) for k in pairs}
        probs = {k: jnp.exp2(scores[k] - lanes(m_news[k], n)) for k in pairs}
        alphas = {k: jnp.exp2(m_prevs[k] - m_news[k]) for k in pairs}
        for k in pairs:
            l_sc[k[0], k[1]] = alphas[k] * l_sc[k[0], k[1]] + jnp.sum(probs[k], axis=-1, keepdims=True)
            m_sc[k[0], k[1]] = m_news[k]
        pvs = [_dot_nt(jnp.concatenate([probs[sq, g].astype(BF16) for g in range(ng)], axis=0),
                       v_ref[sq, :, pl.ds(k0, n)].astype(BF16)) for sq in range(ns)]
        for sq, g in pairs:
            acc_sc[sq, g] = acc_sc[sq, g] * lanes(alphas[sq, g], GROUP_W) + pvs[sq][g * tq:(g + 1) * tq]

    def past_step(j, carry):
        block(kp_ref, vp_ref, pl.multiple_of(j * tk, tk), tk, False)
        return carry
    lax.fori_loop(0, past // tk, past_step, 0)
    block(kn_ref, vn_ref, 0, tq, True, pos0=past)

    lam = lam_ref[0]
    lane_head = _iota((1, GROUP_W), 1) >> HEAD_SHIFT
    ones16 = _head_block_ones(GROUP_W).astype(BF16)
    for sq in range(ns):
        o = jnp.zeros((tq, GROUP_W), F32)
        for h in range(N_HEADS):
            o0 = acc_sc[sq, 2 * h] / lanes(l_sc[sq, 2 * h], GROUP_W)
            o1 = acc_sc[sq, 2 * h + 1] / lanes(l_sc[sq, 2 * h + 1], GROUP_W)
            o = jnp.where(lane_head == h, o0 - lam * o1, o)
        ms = _mm_r(o * o, ones16) * (1.0 / HEAD_DIM)
        o_ref[sq] = o * lax.rsqrt(ms + EPS) * bn_ref[...] * (1.0 - lam_init)


def _diff_attn(lam, q, kn, vn, kp, vp, layer, bnorm, lam_init):
    b, t, _ = q.shape
    past = kp.shape[3]
    ng = 2 * N_HEADS
    ns = DECODE_SEQS
    tq, tk = t, math.gcd(past, 512)
    assert t == CHUNK and past % CHUNK == 0 and tk % LANES == 0 and b % ns == 0
    seq_spec = pl.BlockSpec((ns, GROUP_W, t), lambda i: (i, 0, 0))
    past_spec = pl.BlockSpec((None, ns, GROUP_W, past), lambda i: (layer, i, 0, 0))
    return pl.pallas_call(
        functools.partial(_diff_body, tq=tq, tk=tk, past=past, lam_init=lam_init, ns=ns),
        grid=(b // ns,),
        in_specs=[pl.BlockSpec(memory_space=pltpu.SMEM), pl.BlockSpec((ns, tq, GROUP_W), lambda i: (i, 0, 0)),
                  seq_spec, seq_spec, past_spec, past_spec, _resident((1, GROUP_W))],
        out_specs=pl.BlockSpec((ns, tq, GROUP_W), lambda i: (i, 0, 0)),
        out_shape=jax.ShapeDtypeStruct((b, t, GROUP_W), F32),
        scratch_shapes=[pltpu.VMEM((ns, ng * tq, GROUP_W), BF16), pltpu.VMEM((ns, ng, tq, LANES), F32),
                        pltpu.VMEM((ns, ng, tq, LANES), F32), pltpu.VMEM((ns, ng, tq, GROUP_W), F32)],
        compiler_params=_cparams("parallel"),
    )(lam, q, kn, vn, kp, vp, bnorm)


def _diff_prompt_body(lam_ref, q_ref, k_ref, v_ref, bn_ref, o_ref, qt_sc, m_sc, acc_sc, kn_sc, *, tq, tk, lam_init):
    i = pl.program_id(1)
    ng = 2 * N_HEADS
    qt = (q_ref[0] * (DIFF_HALF ** -0.5 * LOG2E)).T
    feat = _iota((GROUP_W, tq), 0)
    for g in range(ng):
        lo = (g // 2) * HEAD_DIM + (g % 2) * DIFF_HALF
        qt_sc[g] = jnp.where((feat >= lo) & (feat < lo + DIFF_HALF), qt, 0.0).astype(BF16)
    m_sc[...] = jnp.full(m_sc.shape, -jnp.inf, F32)
    acc_sc[...] = jnp.zeros(acc_sc.shape, F32)

    group_of = (_iota((GROUP_W, LANES), 0) // DIFF_HALF == _iota((GROUP_W, LANES), 1)).astype(BF16)

    @pl.when(i == 0)
    def _():
        def norm_step(j, best):
            kf = k_ref[0, pl.ds(pl.multiple_of(j * tk, tk), tk), :].astype(BF16).astype(F32)
            return jnp.maximum(best, jnp.max(_mm_r(kf * kf, group_of), axis=0, keepdims=True))
        kn_sc[...] = lax.fori_loop(0, k_ref.shape[1] // tk, norm_step, jnp.zeros((1, LANES), F32))

    def score_bound(g):
        qf = qt_sc[g].astype(F32)
        q2 = jnp.max(jnp.sum(qf * qf, axis=0, keepdims=True))
        return jnp.sqrt(q2 * kn_sc[0, g]) * 1.001 + 1.0
    pq = i * tq + _iota((1, tq), 1)
    pq_f = pq.astype(F32)
    ones_rows = jnp.ones((SUM_ROWS, tk), BF16)

    def block(k0, n, near, groups=range(2 * N_HEADS)):
        kb = k_ref[0, pl.ds(k0, n), :].astype(BF16)
        vb = v_ref[0, :, pl.ds(k0, n)].astype(BF16)
        pk = k0 + _iota((n, LANES), 0)
        pk_f = pk.astype(F32)
        if near:
            allowed = (_lanes(pk, tq) >> CHUNK_SHIFT) <= (pq >> CHUNK_SHIFT)
            shift = pq_f - jnp.abs(pq_f - _lanes(pk_f, tq))
        scores = {}
        for g in groups:
            slope = 2.0 ** (-2 * (g // 2 + 1)) * LOG2E
            s = _dot(kb, qt_sc[g])
            scores[g] = jnp.where(allowed, s + slope * shift, NEG) if near else s + _lanes(slope * pk_f, tq)
        m_prevs = {g: m_sc[g] for g in groups}
        m_news = {g: jnp.maximum(m_prevs[g], jnp.max(scores[g], axis=0, keepdims=True)) for g in groups}
        probs = {g: jnp.exp2(scores[g] - m_news[g]) for g in groups}
        for g in groups:
            alpha = jnp.exp2(m_prevs[g] - m_news[g])
            m_sc[g] = m_news[g]
            h = g // 2
            pv = _dot(jnp.concatenate([vb[h * HEAD_DIM:(h + 1) * HEAD_DIM, :], ones_rows[:, :n]], axis=0),
                      probs[g].astype(BF16))
            acc_sc[g] = acc_sc[g] * alpha + pv

    block(pl.multiple_of(i * tq, tq), tq, True)

    @pl.when((i * tq) % tk != 0)
    def _():
        block(pl.multiple_of((i - 1) * tq, tq), tq, False)
    nfull = (i * tq) // tk

    def dead_blocks(h):
        slope = 2.0 ** (-2 * (h + 1)) * LOG2E
        counts = [jnp.floor((jnp.min(m_sc[g]) + DIFF_DEAD - score_bound(g)) / (slope * tk)) for g in (2 * h, 2 * h + 1)]
        return jnp.clip(jnp.minimum(counts[0], counts[1]), 0, nfull).astype(jnp.int32)
    dead0 = dead_blocks(0)
    dead1 = jnp.minimum(dead_blocks(1), dead0)

    def steps(groups):
        def step(j, carry):
            block(pl.multiple_of(j * tk, tk), tk, False, groups)
            return carry
        return step
    lax.fori_loop(dead0, nfull, steps(range(ng)), 0)
    lax.fori_loop(dead1, dead0, steps(range(2, ng)), 0)
    lax.fori_loop(0, dead1, steps(range(4, ng)), 0)

    lam = lam_ref[0]
    norm = lambda g: acc_sc[g, :HEAD_DIM, :] / acc_sc[g, HEAD_DIM:HEAD_DIM + 1, :]
    heads = [norm(2 * h) - lam * norm(2 * h + 1) for h in range(N_HEADS)]
    o = jnp.concatenate(heads, axis=0).T
    ms = _mm_r(o * o, _head_block_ones(GROUP_W).astype(BF16)) * (1.0 / HEAD_DIM)
    o_ref[0] = o * lax.rsqrt(ms + EPS) * bn_ref[...] * (1.0 - lam_init)


def _diff_attn_prompt(lam, q, k_rows, v_feat, bnorm, lam_init):
    b, t, _ = q.shape
    ng = 2 * N_HEADS
    tq = min(256, t)
    tk = 2 * tq if t % (2 * tq) == 0 else tq
    assert t % tq == 0 and tq % LANES == 0
    return pl.pallas_call(
        functools.partial(_diff_prompt_body, tq=tq, tk=tk, lam_init=lam_init),
        grid=(b, t // tq),
        in_specs=[pl.BlockSpec(memory_space=pltpu.SMEM), pl.BlockSpec((1, tq, GROUP_W), lambda i, j: (i, j, 0)),
                  pl.BlockSpec((1, t, GROUP_W), lambda i, j: (i, 0, 0)),
                  pl.BlockSpec((1, GROUP_W, t), lambda i, j: (i, 0, 0)), _resident((1, GROUP_W))],
        out_specs=pl.BlockSpec((1, tq, GROUP_W), lambda i, j: (i, j, 0)),
        out_shape=jax.ShapeDtypeStruct((b, t, GROUP_W), F32),
        scratch_shapes=[pltpu.VMEM((ng, GROUP_W, tq), BF16), pltpu.VMEM((ng, 1, tq), F32),
                        pltpu.VMEM((ng, HEAD_DIM + SUM_ROWS, tq), F32), pltpu.VMEM((1, LANES), F32)],
        compiler_params=_cparams("parallel", "arbitrary"),
    )(lam, q, k_rows, v_feat, bnorm)


def _stick_body(*refs, tq, tk, past, ns, near):
    if near:
        q_ref, kn_ref, vn_ref, kp_ref, vp_ref, o_ref, acc_out, c_out, q4_sc, c_sc, acc_sc = refs
    else:
        q_ref, kp_ref, vp_ref, acc_in, c_in, o_ref, q4_sc, c_sc, acc_sc = refs
    nr = N_HEADS * tq
    lane = _iota((tq, GROUP_W), 1) >> HEAD_SHIFT
    for sq in range(ns):
        q = q_ref[sq] * (HEAD_DIM ** -0.5 * LOG2E)
        for h in range(N_HEADS):
            q4_sc[sq, h * tq:(h + 1) * tq, :] = jnp.where(lane == h, q, 0.0).astype(BF16)
    if near:
        c_sc[...] = jnp.zeros(c_sc.shape, F32)
        acc_sc[...] = jnp.zeros(acc_sc.shape, F32)
    else:
        c_sc[...] = c_in[...]
        acc_sc[...] = acc_in[...]
    pq = past + (_iota((nr, 1), 0) & (tq - 1))
    laters = {w: (_iota((w, w), 0) > _iota((w, w), 1)).astype(BF16)
              for w in {min(tq, STICK_SUB), min(tk, STICK_SUB)}}
    seqs = range(ns)

    def block(k_ref, v_ref, k0, n, masked, pos0=0):
        sub = min(n, STICK_SUB)
        later = laters[sub]
        zs = [_dot(q4_sc[sq], k_ref[sq, :, pl.ds(k0, n)].astype(BF16)) for sq in seqs]
        lss = [jnp.minimum(z, 0.0) - jnp.log2(1.0 + jnp.exp2(-jnp.abs(z))) for z in zs]
        lks = [ls - z for ls, z in zip(lss, zs)]
        if masked:
            mask = (pos0 + k0 + _iota((1, n), 1)) < pq
            lks = [jnp.where(mask, lk, 0.0) for lk in lks]
        his = [lk.astype(BF16) for lk in lks]
        los = [(lk - hi.astype(F32)).astype(BF16) for lk, hi in zip(lks, his)]
        carries = [c_sc[sq] for sq in seqs]
        parts = [[] for _ in seqs]
        for sb in reversed(range(n // sub)):
            sl = slice(sb * sub, (sb + 1) * sub)
            for sq in seqs:
                after = _dot(his[sq][:, sl], later) + _dot(los[sq][:, sl], later)
                parts[sq].append(lss[sq][:, sl] + after + _lanes(carries[sq], sub))
                carries[sq] = carries[sq] + (after[:, 0:1] + lks[sq][:, sb * sub:sb * sub + 1])
        for sq in seqs:
            c_sc[sq] = carries[sq]
            e = parts[sq][0] if len(parts[sq]) == 1 else jnp.concatenate(parts[sq][::-1], axis=1)
            if masked:
                a = jnp.where(mask, jnp.exp2(jnp.where(mask, e, 0.0)), 0.0)
            else:
                a = jnp.exp2(e)
            acc_sc[sq] = acc_sc[sq] + _dot_nt(a.astype(BF16), v_ref[sq, :, pl.ds(k0, n)].astype(BF16))

    if near:
        block(kn_ref, vn_ref, 0, tq, True, pos0=past)
        block(kp_ref, vp_ref, 0, tk, False)
        acc_out[...] = acc_sc[...]
        c_out[...] = c_sc[...]
    else:
        nb = past // tk - 1

        def live():
            return jnp.max(c_sc[...]) > STICK_DEAD

        def past_step(state):
            j, _ = state
            block(kp_ref, vp_ref, pl.multiple_of((nb - 1 - j) * tk, tk), tk, False)
            return j + 1, live()
        lax.while_loop(lambda state: (state[0] < nb) & state[1], past_step, (jnp.int32(0), live()))

    for sq in seqs:
        o = acc_sc[sq, 0:tq, :]
        for h in range(1, N_HEADS):
            o = jnp.where(lane == h, acc_sc[sq, h * tq:(h + 1) * tq, :], o)
        o_ref[sq] = o


def _stick_attn(q, kn, vn, kp, vp, layer):
    b, t, _ = q.shape
    past = kp.shape[3]
    ns = DECODE_SEQS
    tq, tk = t, math.gcd(past, 512)
    assert tq & (tq - 1) == 0 and tk % min(tk, STICK_SUB) == 0 and b % ns == 0
    nr = N_HEADS * tq
    row_spec = lambda w: pl.BlockSpec((ns, tq, w), lambda i: (i, 0, 0))
    seq_spec = pl.BlockSpec((ns, GROUP_W, t), lambda i: (i, 0, 0))
    acc_spec = pl.BlockSpec((ns, nr, GROUP_W), lambda i: (i, 0, 0))
    carry_spec = pl.BlockSpec((ns, nr, LANES), lambda i: (i, 0, 0))
    acc_shape = jax.ShapeDtypeStruct((b, nr, GROUP_W), F32)
    carry_shape = jax.ShapeDtypeStruct((b, nr, LANES), F32)
    scratch = [pltpu.VMEM((ns, nr, GROUP_W), BF16), pltpu.VMEM((ns, nr, LANES), F32), pltpu.VMEM((ns, nr, GROUP_W), F32)]
    last = past // tk - 1
    latest_spec = pl.BlockSpec((None, ns, GROUP_W, tk), lambda i: (layer, i, 0, last))
    out, acc, carry = pl.pallas_call(
        functools.partial(_stick_body, tq=tq, tk=tk, past=past, ns=ns, near=True),
        grid=(b // ns,),
        in_specs=[row_spec(GROUP_W), seq_spec, seq_spec, latest_spec, latest_spec],
        out_specs=[row_spec(GROUP_W), acc_spec, carry_spec],
        out_shape=[jax.ShapeDtypeStruct((b, t, GROUP_W), F32), acc_shape, carry_shape],
        scratch_shapes=scratch,
        compiler_params=_cparams("parallel"),
    )(q, kn, vn, kp, vp)
    if last == 0:
        return out

    def earlier_keys():
        past_spec = pl.BlockSpec((None, ns, GROUP_W, past), lambda i: (layer, i, 0, 0))
        return pl.pallas_call(
            functools.partial(_stick_body, tq=tq, tk=tk, past=past, ns=ns, near=False),
            grid=(b // ns,),
            in_specs=[row_spec(GROUP_W), past_spec, past_spec, acc_spec, carry_spec],
            out_specs=row_spec(GROUP_W),
            out_shape=jax.ShapeDtypeStruct((b, t, GROUP_W), F32),
            scratch_shapes=scratch,
            compiler_params=_cparams("parallel"),
        )(q, kp, vp, acc, carry)

    return lax.cond(jnp.max(carry) > STICK_DEAD, earlier_keys, lambda: out)


def _stick_prompt_body(q_ref, k_ref, v_ref, o_ref, qt_sc, c_sc, acc_sc, *, tq, tk):
    i = pl.program_id(1)
    qt = (q_ref[0] * (HEAD_DIM ** -0.5 * LOG2E)).T
    feat_head = _iota((GROUP_W, tq), 0) >> HEAD_SHIFT
    for h in range(N_HEADS):
        qt_sc[h] = jnp.where(feat_head == h, qt, 0.0).astype(BF16)
    c_sc[...] = jnp.zeros(c_sc.shape, F32)
    acc_sc[...] = jnp.zeros(acc_sc.shape, F32)
    pq = i * tq + _iota((1, tq), 1)
    sub = min(tq, STICK_SUB)
    later = (_iota((sub, sub), 1) > _iota((sub, sub), 0)).astype(BF16)

    def block(k0, n, masked):
        kb = k_ref[0, pl.ds(k0, n), :].astype(BF16)
        vb = v_ref[0, :, pl.ds(k0, n)].astype(BF16)
        if masked:
            mask = _lanes(k0 + _iota((n, LANES), 0), tq) < pq
        heads = range(N_HEADS)
        zs = [_dot(kb, qt_sc[h]) for h in heads]
        lss = [jnp.minimum(z, 0.0) - jnp.log2(1.0 + jnp.exp2(-jnp.abs(z))) for z in zs]
        lks = [ls - z for ls, z in zip(lss, zs)]
        if masked:
            lks = [jnp.where(mask, lk, 0.0) for lk in lks]
        es = []
        for h in heads:
            lk = lks[h]
            hi = lk.astype(BF16)
            lo = (lk - hi.astype(F32)).astype(BF16)
            carry = c_sc[h]
            parts = []
            for sb in reversed(range(n // sub)):
                sl = slice(sb * sub, (sb + 1) * sub)
                after = _dot(later, hi[sl]) + _dot(later, lo[sl])
                parts.append(lss[h][sl] + after + carry)
                carry = carry + (after[0:1] + lk[sb * sub:sb * sub + 1])
            c_sc[h] = carry
            es.append(parts[0] if len(parts) == 1 else jnp.concatenate(parts[::-1], axis=0))
        for h in heads:
            if masked:
                a = jnp.where(mask, jnp.exp2(jnp.where(mask, es[h], 0.0)), 0.0)
            else:
                a = jnp.exp2(es[h])
            acc_sc[h] = acc_sc[h] + _dot(vb[h * HEAD_DIM:(h + 1) * HEAD_DIM, :], a.astype(BF16))

    block(pl.multiple_of(i * tq, tq), tq, True)

    @pl.when((i * tq) % tk != 0)
    def _():
        block(pl.multiple_of((i - 1) * tq, tq), tq, False)
    nfull = (i * tq) // tk

    def live():
        return jnp.max(c_sc[...]) > STICK_DEAD

    def prev_step(state):
        j, _ = state
        block(pl.multiple_of((nfull - 1 - j) * tk, tk), tk, False)
        return j + 1, live()
    lax.while_loop(lambda state: (state[0] < nfull) & state[1], prev_step, (jnp.int32(0), live()))

    o_ref[0] = jnp.concatenate([acc_sc[h] for h in range(N_HEADS)], axis=0).T


def _stick_attn_prompt(q, k_rows, v_feat):
    b, t, _ = q.shape
    tq = min(256, t)
    tk = tq
    assert t % tq == 0 and tq % LANES == 0 and tk % min(tk, STICK_SUB) == 0
    return pl.pallas_call(
        functools.partial(_stick_prompt_body, tq=tq, tk=tk),
        grid=(b, t // tq),
        in_specs=[pl.BlockSpec((1, tq, GROUP_W), lambda i, j: (i, j, 0)),
                  pl.BlockSpec((1, t, GROUP_W), lambda i, j: (i, 0, 0)),
                  pl.BlockSpec((1, GROUP_W, t), lambda i, j: (i, 0, 0))],
        out_specs=pl.BlockSpec((1, tq, GROUP_W), lambda i, j: (i, j, 0)),
        out_shape=jax.ShapeDtypeStruct((b, t, GROUP_W), F32),
        scratch_shapes=[pltpu.VMEM((N_HEADS, GROUP_W, tq), BF16), pltpu.VMEM((N_HEADS, 1, tq), F32),
                        pltpu.VMEM((N_HEADS, HEAD_DIM, tq), F32)],
        compiler_params=_cparams("parallel", "parallel"),
    )(q, k_rows, v_feat)


def _pair_mask(n, s, reps=1):
    r, c = _iota((reps * n, n), 0) & (n - 1), _iota((reps * n, n), 1)
    return ((r // (2 * s)) == (c // (2 * s))) & (((r // s) & 1) == 1) & (((c // s) & 1) == 0)


def _state_spec(nb):
    return pl.BlockSpec((nb, N_HEADS, HEAD_DIM, HEAD_DIM), lambda i, j: (i, 0, 0, 0))


def _load_state(s_sc, s0_ref, transpose):
    s_sc[...] = jnp.zeros(s_sc.shape, F32)
    for b in range(s_sc.shape[0]):
        for h in range(N_HEADS):
            blk = s0_ref[b, h]
            s_sc[b, h * HEAD_DIM:(h + 1) * HEAD_DIM, h * HEAD_DIM:(h + 1) * HEAD_DIM] = blk.T if transpose else blk


def _store_state(sout_ref, s_sc, transpose):
    for b in range(s_sc.shape[0]):
        for h in range(N_HEADS):
            blk = s_sc[b, h * HEAD_DIM:(h + 1) * HEAD_DIM, h * HEAD_DIM:(h + 1) * HEAD_DIM]
            sout_ref[b, h] = blk.T if transpose else blk


def _head_rows(x, lane_head):
    return jnp.concatenate([jnp.where(lane_head == h, x, 0.0) for h in range(N_HEADS)], axis=0)


def _head_diag(x, lane_head):
    n = x.shape[0] // N_HEADS
    out = x[:n]
    for h in range(1, N_HEADS):
        out = jnp.where(lane_head == h, x[h * n:(h + 1) * n], out)
    return out


def _gdn_body(x_ref, z_ref, abg_ref, cw_ref, cs_ref, s0_ref, alog_ref, dtb_ref, an_ref,
              o_ref, sout_ref, s_sc, xb_sc, *, nc, nb):
    c = pl.program_id(1)
    L = CHUNK
    pad = 8

    @pl.when(c == 0)
    def _():
        _load_state(s_sc, s0_ref, transpose=False)
        xb_sc[:, pad - 3:pad, :] = cs_ref[...]

    ones_bd = _head_block_ones(GROUP_W)
    ones_bd16 = ones_bd.astype(BF16)
    er, ec = _iota((ABG_W, GROUP_W), 0), _iota((ABG_W, GROUP_W), 1) >> HEAD_SHIFT
    e_beta = (er == ec).astype(BF16)
    e_g = (er == ec + N_HEADS).astype(BF16)
    ri, ci = _iota((L, L), 0), _iota((L, L), 1)
    tri16 = (ri >= ci).astype(BF16)
    incl = ri >= ci
    strict = ri > ci
    eye = (ri == ci).astype(F32)
    pair_masks = [_pair_mask(L, sz) for sz in (1, 2, 4, 8, 16, 32)]
    lane_head = _iota((1, GROUP_W), 1) >> HEAD_SHIFT
    cw = cw_ref[...]

    seqs = range(nb)
    us = []
    for b in seqs:
        x = x_ref[b]
        xb_sc[b, pad:pad + L, :] = x
        u = (x * cw[3:4] + xb_sc[b, pad - 1:pad - 1 + L, :] * cw[2:3]
             + xb_sc[b, pad - 2:pad - 2 + L, :] * cw[1:2] + xb_sc[b, pad - 3:pad - 3 + L, :] * cw[0:1])
        xb_sc[b, pad - 3:pad, :] = x[L - 3:L, :]
        us.append(u * _sigmoid(u))
    qs = [u[:, :GROUP_W] for u in us]
    ks = [u[:, GROUP_W:2 * GROUP_W] for u in us]
    vs = [u[:, 2 * GROUP_W:] for u in us]
    qs = [q * lax.rsqrt(_mm_r(q * q, ones_bd16) + EPS) * HEAD_DIM ** -0.5 for q in qs]
    ks = [k * lax.rsqrt(_mm_r(k * k, ones_bd16) + EPS) for k in ks]

    abgs = [abg_ref[b] for b in seqs]
    betas = [_mm_r(_sigmoid(abg), e_beta) for abg in abgs]
    gc_ns = [_mm_l(tri16, -jnp.exp(alog_ref[...]) * _softplus(abg + dtb_ref[...])) for abg in abgs]
    gc_ts = [gc_n.T for gc_n in gc_ns]
    gcs = [_mm_r(gc_n, e_g) for gc_n in gc_ns]

    kbs = [k * beta for k, beta in zip(ks, betas)]
    k16s = [k.astype(BF16) for k in ks]
    a4s = [_bdot(_head_rows(kb, lane_head).astype(BF16), k16, _NT) for kb, k16 in zip(kbs, k16s)]
    qk4s = [_bdot(_head_rows(q, lane_head).astype(BF16), k16, _NT) for q, k16 in zip(qs, k16s)]
    decs = []
    for b in seqs:
        dec = []
        for h in range(N_HEADS):
            diff = gc_ns[b][:, N_HEADS + h:N_HEADS + h + 1] - gc_ts[b][N_HEADS + h:N_HEADS + h + 1, :]
            dec.append(jnp.where(incl, jnp.exp(jnp.where(incl, diff, 0.0)), 0.0))
        decs.append(jnp.stack(dec))

    a_all = jnp.concatenate([jnp.where(strict, a4s[b].reshape(N_HEADS, L, L) * decs[b], 0.0) for b in seqs], axis=0)
    t_inv = eye - jnp.where(pair_masks[0], a_all, 0.0)
    a16 = a_all.astype(BF16)
    for pm in pair_masks[1:]:
        t16 = t_inv.astype(BF16)
        t_inv = t_inv - _bdot(t16, _bdot(jnp.where(pm, a16, 0.0), t16, _BNN).astype(BF16), _BNN)

    exp_gs = [jnp.exp(gc) for gc in gcs]
    s_olds = [s_sc[b] for b in seqs]
    s16s = [s.astype(BF16) for s in s_olds]
    rhss = [vs[b] * betas[b] - _bdot((kbs[b] * exp_gs[b]).astype(BF16), s16s[b], _NN) for b in seqs]
    ws = [_head_diag(_mm1(t_inv[b * N_HEADS:(b + 1) * N_HEADS].reshape(N_HEADS * L, L), rhss[b]), lane_head)
          for b in seqs]
    outs = [_bdot((qs[b] * exp_gs[b]).astype(BF16), s16s[b], _NN)
            + _head_diag(_mm1((qk4s[b].reshape(N_HEADS, L, L) * decs[b]).reshape(N_HEADS * L, L), ws[b]), lane_head)
            for b in seqs]
    for b in seqs:
        g_last = gcs[b][L - 1:L, :]
        k_dec = ks[b] * jnp.exp(g_last - gcs[b])
        s_sc[b] = s_olds[b] * jnp.exp(g_last) + _mm1(k_dec.T, ws[b]) * ones_bd
    for b in seqs:
        o = outs[b]
        ms = _mm_r(o * o, ones_bd16) * (1.0 / HEAD_DIM)
        zg = z_ref[b]
        o_ref[b] = o * lax.rsqrt(ms + EPS) * an_ref[...] * (zg * _sigmoid(zg))

    @pl.when(c == nc - 1)
    def _():
        _store_state(sout_ref, s_sc, transpose=False)


def _gdn(aqkv, az, abg, conv_w, conv_state, s0, alog_row, dtb_row, anorm_row):
    b, t, _ = aqkv.shape
    nc = t // CHUNK
    nb = SEQ_PER_STEP
    assert b % nb == 0
    cmap = lambda i, j: (i, j, 0)
    bmap = lambda i, j: (i, 0, 0)
    return pl.pallas_call(
        functools.partial(_gdn_body, nc=nc, nb=nb),
        grid=(b // nb, nc),
        in_specs=[pl.BlockSpec((nb, CHUNK, 3 * GROUP_W), cmap), pl.BlockSpec((nb, CHUNK, GROUP_W), cmap),
                  pl.BlockSpec((nb, CHUNK, ABG_W), cmap), _resident((CONV_W, 3 * GROUP_W)),
                  pl.BlockSpec((nb, CONV_W - 1, 3 * GROUP_W), bmap), _state_spec(nb),
                  _resident((1, ABG_W)), _resident((1, ABG_W)), _resident((1, GROUP_W))],
        out_specs=[pl.BlockSpec((nb, CHUNK, GROUP_W), cmap), _state_spec(nb)],
        out_shape=[jax.ShapeDtypeStruct((b, t, GROUP_W), F32), jax.ShapeDtypeStruct(s0.shape, F32)],
        scratch_shapes=[pltpu.VMEM((nb, GROUP_W, GROUP_W), F32), pltpu.VMEM((nb, 8 + CHUNK, 3 * GROUP_W), F32)],
        compiler_params=_cparams("parallel", "arbitrary"),
    )(aqkv, az, abg, conv_w, conv_state, s0, alog_row, dtb_row, anorm_row)


def _hgrn_body(q_ref, f_ref, i_ref, g_ref, lb_ref, dn_ref, s0_ref, o_ref, sout_ref, s_sc, *, nc, nb):
    c = pl.program_id(1)
    L = CHUNK

    @pl.when(c == 0)
    def _():
        _load_state(s_sc, s0_ref, transpose=True)

    lb = lb_ref[...]
    ones_bd = _head_block_ones(GROUP_W)
    ones_bd16 = ones_bd.astype(BF16)
    ri, ci = _iota((L, L), 0), _iota((L, L), 1)
    tri16 = (ri >= ci).astype(BF16)
    sizes = (1, 2, 4, 8, 16, 32)
    sel16 = jnp.concatenate([(ci == (ri // (2 * sz)) * (2 * sz) + sz).astype(BF16) for sz in sizes], axis=0)
    eye4 = (_iota((N_HEADS * L, L), 0) & (L - 1)) == _iota((N_HEADS * L, L), 1)
    pair_masks4 = [_pair_mask(L, sz, reps=N_HEADS) for sz in sizes]
    lane_head = _iota((1, GROUP_W), 1) >> HEAD_SHIFT
    rows = _iota((L, 1), 0)

    pre = []
    for b in range(nb):
        fl = f_ref[b]
        log_f = _log_sigmoid(fl) + jnp.log(1.0 + lb * jnp.exp(-fl))
        k = (1.0 - lb) * _sigmoid(-fl)
        q = q_ref[b] * HEAD_DIM ** -0.5
        bc = _mm_l(tri16, log_f)
        pre.append((q, k, bc))
    b_refs = [_mm_l(sel16, bc) for _, _, bc in pre]
    a_all = [jnp.where(eye4, _mm1(_head_rows(q, lane_head), k, _NT), 0.0) for q, k, _ in pre]
    for lv, sz in enumerate(sizes):
        later = ((rows // sz) & 1) == 1
        for b in range(nb):
            q, k, bc = pre[b]
            b_ref = b_refs[b][lv * L:(lv + 1) * L]
            x = jnp.where(later, q * jnp.exp(jnp.where(later, bc - b_ref, 0.0)), 0.0)
            y = jnp.where(later, 0.0, k * jnp.exp(jnp.where(later, 0.0, b_ref - bc)))
            a_all[b] = a_all[b] + jnp.where(pair_masks4[lv], _mm1(_head_rows(x, lane_head), y, _NT), 0.0)

    for b in range(nb):
        q, k, bc = pre[b]
        v = i_ref[b]
        st = s_sc[b]
        o = _mm1(q * jnp.exp(bc), st, _NT) + _head_diag(_mm1(a_all[b], v), lane_head)
        b_last = bc[L - 1:L, :]
        st_new = st * jnp.exp(b_last) + _mm1(v.T, k * jnp.exp(b_last - bc)) * ones_bd
        s_sc[b] = st_new

        ms = _mm_r(o * o, ones_bd16) * (1.0 / HEAD_DIM)
        zg = g_ref[b]
        o_ref[b] = o * lax.rsqrt(ms + EPS) * dn_ref[...] * (zg * _sigmoid(zg))

    @pl.when(c == nc - 1)
    def _():
        _store_state(sout_ref, s_sc, transpose=True)


def _hgrn(dq, df, di, dg, lb_row, dnorm_row, s0):
    b, t, _ = dq.shape
    nc = t // CHUNK
    nb = SEQ_PER_STEP
    assert b % nb == 0
    cmap = lambda i, j: (i, j, 0)
    bmap = lambda i, j: (i, 0, 0)
    cspec = pl.BlockSpec((nb, CHUNK, GROUP_W), cmap)
    return pl.pallas_call(
        functools.partial(_hgrn_body, nc=nc, nb=nb),
        grid=(b // nb, nc),
        in_specs=[cspec, cspec, cspec, cspec, _resident((1, GROUP_W)), _resident((1, GROUP_W)),
                  _state_spec(nb)],
        out_specs=[cspec, _state_spec(nb)],
        out_shape=[jax.ShapeDtypeStruct((b, t, GROUP_W), F32), jax.ShapeDtypeStruct(s0.shape, F32)],
        scratch_shapes=[pltpu.VMEM((nb, GROUP_W, GROUP_W), F32)],
        compiler_params=_cparams("parallel", "arbitrary"),
    )(dq, df, di, dg, lb_row, dnorm_row, s0)


IN_WIDTHS = (3 * GROUP_W,) + (GROUP_W,) * 11 + (ABG_W,)
KV_SEGMENTS = (3, 4, 6, 7)
K_SEGMENTS = (3, 6)


def _layer(x, l, st, mk, mv, mem_layer, prm, final_gain):
    b, t, _ = x.shape
    m = b * t
    tm = math.gcd(m, ROW_TILE)
    has_past = st["b_k"] is not None
    outs = _fused_linear(x.reshape(m, D_MODEL), prm["w_in"], l, IN_WIDTHS, prm["norm_mix"][l], tm,
                         feat_major=KV_SEGMENTS, rows_per_seq=t, rows_too=() if has_past else K_SEGMENTS)
    outs = [o if idx in KV_SEGMENTS else o.reshape(b, t, -1) for idx, o in enumerate(outs)]
    (a_qkv, a_z, b_q, b_k, b_v, c_q, c_k, c_v, d_q, d_f, d_i, d_g, a_bg) = outs[:len(IN_WIDTHS)]

    o_a, a_s = _gdn(a_qkv, a_z, a_bg, prm["a_conv_w"][l], st["a_conv"], st["a_S"].astype(F32),
                    prm["alog_row"][l], prm["dtb_row"][l], prm["a_norm"][l])
    lam_init = 0.8 - 0.6 * math.exp(-0.3 * l)
    if has_past:
        o_b = _diff_attn(prm["lam"][l], b_q, b_k, b_v, st["b_k"], st["b_v"], l, prm["b_norm"][l], lam_init)
        o_c = _stick_attn(c_q, c_k, c_v, st["c_k"], st["c_v"], l)
    else:
        b_k_rows, c_k_rows = outs[len(IN_WIDTHS):]
        o_b = _diff_attn_prompt(prm["lam"][l], b_q, b_k_rows, b_v, prm["b_norm"][l], lam_init)
        o_c = _stick_attn_prompt(c_q, c_k_rows, c_v)
    o_d, d_s = _hgrn(d_q, d_f, d_i, d_g, prm["lb"][l], prm["d_norm"][l], st["d_S"].astype(F32))

    x2 = _cross(x, (o_a, o_b, o_c, o_d), prm["w_out"], prm["norm_cross"][l], prm["w_cq"], mk, mv, mem_layer,
                prm["w_co"], l)
    x2 = _ffn(x2.reshape(m, D_MODEL), prm["norm_ffn"][l], prm["w_gate"], prm["w_up"], prm["w_down"], l,
              final_gain=final_gain, tm=tm)

    if t >= CONV_W - 1:
        conv_new = a_qkv[:, t - (CONV_W - 1):, :]
    else:
        conv_new = jnp.concatenate([st["a_conv"], a_qkv], axis=1)[:, -(CONV_W - 1):, :]
    heads = lambda a: jnp.transpose(a.reshape(b, N_HEADS, HEAD_DIM, t), (0, 3, 1, 2))
    new = (conv_new, a_s, heads(b_k), heads(b_v), heads(c_k), heads(c_v), d_s)
    return x2.reshape(b, t, D_MODEL), new


def kernel(x_prompt, x_sample, mem_prompt, state_a_conv, state_a_S, cache_b_k, cache_b_v, cache_c_k, cache_c_v,
           state_d_S, cache_mem_k, cache_mem_v, norm_mix, w_in, a_conv_w, a_A_log, a_dt_bias, a_norm, b_lam_q1,
           b_lam_k1, b_lam_q2, b_lam_k2, b_norm, d_lb, d_norm, w_out, norm_cross, norm_memtok, w_cq, w_ck, w_cv,
           w_co, norm_ffn, w_gate, w_up, w_down, norm_final):
    depth = w_in.shape[0]
    n_main = 4 * GROUP_W
    w_in_p = jnp.concatenate(
        [w_in[:, :, :n_main], w_in[:, :, n_main + 2 * N_HEADS:], w_in[:, :, n_main:n_main + 2 * N_HEADS],
         jnp.zeros((depth, D_MODEL, ABG_W - 2 * N_HEADS), w_in.dtype)], axis=2).astype(BF16)
    pad_row = lambda a: jnp.pad(a.astype(F32), ((0, 0), (N_HEADS, ABG_W - 2 * N_HEADS)))[:, None, :]
    tile_row = lambda a: jnp.tile(a.astype(F32), (1, N_HEADS))[:, None, :]
    p_lb = jax.nn.softmax(d_lb.astype(F32), axis=0)
    lam = (jnp.exp(jnp.sum(b_lam_q1.astype(F32) * b_lam_k1.astype(F32), axis=-1))
           - jnp.exp(jnp.sum(b_lam_q2.astype(F32) * b_lam_k2.astype(F32), axis=-1))
           + jnp.array([0.8 - 0.6 * math.exp(-0.3 * l) for l in range(depth)], F32))
    prm = {
        "w_in": w_in_p, "norm_mix": norm_mix, "a_conv_w": a_conv_w.astype(F32),
        "alog_row": pad_row(a_A_log), "dtb_row": pad_row(a_dt_bias), "a_norm": tile_row(a_norm),
        "lam": lam[:, None], "b_norm": tile_row(b_norm), "lb": (jnp.cumsum(p_lb, axis=0) - p_lb[0])[:, None, :],
        "d_norm": tile_row(d_norm), "w_out": w_out.astype(BF16), "norm_cross": norm_cross,
        "w_cq": w_cq.astype(BF16), "w_co": w_co.astype(BF16), "norm_ffn": norm_ffn,
        "w_gate": w_gate.astype(BF16), "w_up": w_up.astype(BF16), "w_down": w_down.astype(BF16),
    }
    w_ckv = jnp.concatenate([w_ck, w_cv], axis=2).astype(BF16)

    bp, tp, _ = x_prompt.shape
    n_mem = mem_prompt.shape[1]
    h = x_prompt
    p_new = []
    for l in range(depth):
        mk, mv = _fused_linear(mem_prompt.reshape(bp * n_mem, D_MODEL), w_ckv, l, (D_MODEL, D_MODEL),
                               norm_memtok[l], math.gcd(bp * n_mem, ROW_TILE))
        mk = mk.reshape(1, bp, n_mem, D_MODEL)
        mv = mv.reshape(1, bp, n_mem, D_MODEL)
        st = {"a_conv": jnp.zeros((bp, CONV_W - 1, 3 * GROUP_W), F32),
              "a_S": jnp.zeros((bp, N_HEADS, HEAD_DIM, HEAD_DIM), F32), "b_k": None, "b_v": None,
              "c_k": None, "c_v": None, "d_S": jnp.zeros((bp, N_HEADS, HEAD_DIM, HEAD_DIM), F32)}
        h, new = _layer(h, l, st, mk, mv, 0, prm, norm_final if l == depth - 1 else None)
        mem4 = lambda a: a.reshape(bp, n_mem, MEM_HEADS, MEM_HEAD_DIM)
        p_new.append(new + (mem4(mk), mem4(mv)))
    y_prompt = h
    p_stacked = [jnp.stack(c) for c in zip(*p_new)]

    bs, ts, _ = x_sample.shape
    past = cache_b_k.shape[2]
    h = x_sample
    s_new = []
    flat = lambda a: jnp.transpose(a, (0, 1, 3, 4, 2)).reshape(depth, bs, GROUP_W, past)
    caches = {"b_k": flat(cache_b_k), "b_v": flat(cache_b_v), "c_k": flat(cache_c_k), "c_v": flat(cache_c_v)}
    for l in range(depth):
        st = {"a_conv": state_a_conv[l], "a_S": state_a_S[l], "d_S": state_d_S[l], **caches}
        h, new = _layer(h, l, st, cache_mem_k, cache_mem_v, l, prm, norm_final if l == depth - 1 else None)
        s_new.append(new)
    y_sample = h
    s_stacked = [jnp.stack(c) for c in zip(*s_new)]

    return (y_prompt, y_sample, *p_stacked, *s_stacked)
```

```python
import functools
import math

import jax
import jax.numpy as jnp
from jax import lax
from jax.experimental import pallas as pl
from jax.experimental.pallas import tpu as pltpu

F32 = jnp.float32
BF16 = jnp.bfloat16

D_MODEL = 1024
GROUP_W = 256
N_HEADS = 4
HEAD_DIM = 64
DIFF_HALF = 32
CHUNK = 64
CONV_W = 4
MEM_HEADS = 4
MEM_HEAD_DIM = 256
D_FF = 2816
HEAD_SHIFT = HEAD_DIM.bit_length() - 1
CHUNK_SHIFT = CHUNK.bit_length() - 1
EPS = 1e-6
NEG = -1e30
LOG2E = 1.4426950408889634
LANES = 128
ABG_W = 128
FF_CHUNK = 256
ROW_TILE = 512
SUM_ROWS = 16
DECODE_SEQS = 2
CROSS_SEQS = 4
DIFF_DEAD = -160.0
STICK_DEAD = -160.0
STICK_SUB = 256
SEQ_PER_STEP = 8
VMEM_LIMIT = 56 * 1024 * 1024


def _cparams(*sem):
    return pltpu.CompilerParams(dimension_semantics=sem, vmem_limit_bytes=VMEM_LIMIT)


def _dot(a, b):
    return jnp.dot(a, b, preferred_element_type=F32)


def _dot_nt(a, b):
    return lax.dot_general(a, b, (((1,), (1,)), ((), ())), preferred_element_type=F32)


_NN = (((1,), (0,)), ((), ()))
_NT = (((1,), (1,)), ((), ()))
_BNN = (((2,), (1,)), ((0,), (0,)))


def _split(x):
    hi = x.astype(BF16)
    return hi, (x - hi.astype(F32)).astype(BF16)


def _bdot(a, b, dims):
    return lax.dot_general(a, b, dims, preferred_element_type=F32)


def _mm1(a, b, dims=_NN):
    return _bdot(a.astype(BF16), b.astype(BF16), dims)


def _mm_r(a, b16):
    ah, al = _split(a)
    return _bdot(ah, b16, _NN) + _bdot(al, b16, _NN)


def _mm_l(a16, b):
    bh, bl = _split(b)
    return _bdot(a16, bh, _NN) + _bdot(a16, bl, _NN)


def _rms(x, g):
    return x * lax.rsqrt(jnp.mean(x * x, axis=-1, keepdims=True) + EPS) * g


def _sigmoid(x):
    return 1.0 / (1.0 + jnp.exp(-x))


def _log_sigmoid(x):
    return jnp.minimum(x, 0.0) - jnp.log(1.0 + jnp.exp(-jnp.abs(x)))


def _softplus(x):
    return jnp.maximum(x, 0.0) + jnp.log(1.0 + jnp.exp(-jnp.abs(x)))


def _iota(shape, dim):
    return lax.broadcasted_iota(jnp.int32, shape, dim)


def _lanes(x, n):
    return x[:, :n] if n <= LANES else jnp.concatenate([x] * (n // LANES), axis=1)


def _head_block_ones(n):
    return ((_iota((n, n), 0) >> HEAD_SHIFT) == (_iota((n, n), 1) >> HEAD_SHIFT)).astype(F32)


def _resident(shape):
    return pl.BlockSpec(shape, lambda *_: (0,) * len(shape), pipeline_mode=pl.Buffered(1))


def _weight(w, layer):
    return pl.BlockSpec((None,) + w.shape[1:], lambda *_: (layer, 0, 0), pipeline_mode=pl.Buffered(1))


def _linear_body(x_ref, g_ref, w_ref, *out_refs, segs, feat_major, rows_too):
    xb = _rms(x_ref[...], g_ref[...]).astype(BF16)
    extra_refs = dict(zip(rows_too, out_refs[len(segs):]))
    for idx, (o_ref, (s, e)) in enumerate(zip(out_refs, segs)):
        acc = _dot(xb, w_ref[:, s:e])
        if idx in feat_major:
            rows = acc.shape[0] // o_ref.shape[0]
            for sq in range(o_ref.shape[0]):
                o_ref[sq] = acc[sq * rows:(sq + 1) * rows].T
        else:
            o_ref[...] = acc
        if idx in extra_refs:
            extra_refs[idx][...] = acc


def _fused_linear(x, w, layer, widths, gain, tm, feat_major=(), rows_per_seq=None, rows_too=()):
    m, k = x.shape
    n = w.shape[2]
    segs = []
    s = 0
    for wd in widths:
        segs.append((s, s + wd))
        s += wd
    assert s == n and m % tm == 0
    out_specs, out_shape = [], []
    for idx, wd in enumerate(widths):
        if idx in feat_major:
            if rows_per_seq >= tm:
                tiles = rows_per_seq // tm
                assert rows_per_seq % tm == 0
                out_specs.append(pl.BlockSpec((1, wd, tm), lambda i, tiles=tiles: (i // tiles, 0, i % tiles)))
            else:
                assert tm % rows_per_seq == 0
                out_specs.append(pl.BlockSpec((tm // rows_per_seq, wd, rows_per_seq), lambda i: (i, 0, 0)))
            out_shape.append(jax.ShapeDtypeStruct((m // rows_per_seq, wd, rows_per_seq), F32))
        else:
            out_specs.append(pl.BlockSpec((tm, wd), lambda i: (i, 0)))
            out_shape.append(jax.ShapeDtypeStruct((m, wd), F32))
    for idx in rows_too:
        out_specs.append(pl.BlockSpec((tm, widths[idx]), lambda i: (i, 0)))
        out_shape.append(jax.ShapeDtypeStruct((m, widths[idx]), F32))
    assert w.shape[1] == k
    return pl.pallas_call(
        functools.partial(_linear_body, segs=tuple(segs), feat_major=tuple(feat_major), rows_too=tuple(rows_too)),
        grid=(m // tm,),
        in_specs=[pl.BlockSpec((tm, k), lambda i: (i, 0)), _resident((1, k)), _weight(w, layer)],
        out_specs=out_specs,
        out_shape=out_shape,
        compiler_params=_cparams("parallel"),
    )(x, gain.reshape(1, -1), w)


def _ffn_body(*refs, final):
    if final:
        x_ref, g_ref, wg_ref, wu_ref, wd_ref, gf_ref, o_ref = refs
    else:
        x_ref, g_ref, wg_ref, wu_ref, wd_ref, o_ref = refs
    x = x_ref[...]
    hb = _rms(x, g_ref[...]).astype(BF16)
    acc = x
    for c in range(0, D_FF, FF_CHUNK):
        gate = _dot(hb, wg_ref[:, c:c + FF_CHUNK])
        up = _dot(hb, wu_ref[:, c:c + FF_CHUNK])
        act = (gate * _sigmoid(gate) * up).astype(BF16)
        acc = acc + _dot(act, wd_ref[c:c + FF_CHUNK, :])
    if final:
        acc = _rms(acc, gf_ref[...])
    o_ref[...] = acc


def _ffn(x, gain, wg, wu, wd, layer, final_gain=None, tm=256):
    m = x.shape[0]
    final = final_gain is not None
    in_specs = [pl.BlockSpec((tm, D_MODEL), lambda i: (i, 0)), _resident((1, D_MODEL)),
                _weight(wg, layer), _weight(wu, layer), _weight(wd, layer)]
    args = [x, gain.reshape(1, -1), wg, wu, wd]
    if final:
        in_specs.append(_resident((1, D_MODEL)))
        args.append(final_gain.reshape(1, -1))
    return pl.pallas_call(
        functools.partial(_ffn_body, final=final),
        grid=(m // tm,),
        in_specs=in_specs,
        out_specs=pl.BlockSpec((tm, D_MODEL), lambda i: (i, 0)),
        out_shape=jax.ShapeDtypeStruct((m, D_MODEL), F32),
        compiler_params=_cparams("parallel"),
    )(*args)


def _cross_body(x_ref, oa_ref, ob_ref, oc_ref, od_ref, wout_ref, g_ref, wq_ref, mk_ref, mv_ref, wo_ref, o_ref):
    ns, tq, _ = x_ref.shape
    rows = ns * tq
    x = x_ref[...].reshape(rows, D_MODEL)
    for gi, m_ref in enumerate((oa_ref, ob_ref, oc_ref, od_ref)):
        x = x + _dot(m_ref[...].reshape(rows, GROUP_W).astype(BF16), wout_ref[gi * GROUP_W:(gi + 1) * GROUP_W, :])
    hb = _rms(x, g_ref[...]).astype(BF16)
    q = (_dot(hb, wq_ref[...]) * MEM_HEAD_DIM ** -0.5).astype(BF16)
    head = lambda h: slice(h * MEM_HEAD_DIM, (h + 1) * MEM_HEAD_DIM)
    pairs = [(sq, h) for sq in range(ns) for h in range(MEM_HEADS)]
    if len(mk_ref.shape) == 4:
        head_major = {}

        def mem(ref, sq, h):
            if (id(ref), sq) not in head_major:
                head_major[id(ref), sq] = jnp.swapaxes(ref[sq], 0, 1).astype(BF16)
            return head_major[id(ref), sq][h]
    else:
        mem = lambda ref, sq, h: ref[sq, :, head(h)].astype(BF16)
    scores = [_dot_nt(q[sq * tq:(sq + 1) * tq, head(h)], mem(mk_ref, sq, h)) for sq, h in pairs]
    probs = []
    for s in scores:
        p = jnp.exp(s - jnp.max(s, axis=-1, keepdims=True))
        probs.append((p / jnp.sum(p, axis=-1, keepdims=True)).astype(BF16))
    outs = [_dot(p, mem(mv_ref, sq, h)).astype(BF16) for p, (sq, h) in zip(probs, pairs)]
    o = jnp.concatenate([jnp.concatenate(outs[sq * MEM_HEADS:(sq + 1) * MEM_HEADS], axis=1) for sq in range(ns)], axis=0)
    o_ref[...] = (x + _dot(o, wo_ref[...])).reshape(ns, tq, D_MODEL)


def _cross(x, mix, w_out, gain, wq, mk, mv, layer, wo, w_layer):
    b, t, _ = x.shape
    nm = mk.shape[2]
    tq = min(ROW_TILE, t)
    ns = CROSS_SEQS if CROSS_SEQS * t <= ROW_TILE else 1
    assert t % tq == 0 and b % ns == 0
    row_spec = lambda w: pl.BlockSpec((ns, tq, w), lambda i, j: (i, j, 0))
    mem_spec = pl.BlockSpec((None, ns) + mk.shape[2:], lambda i, j: (layer, i) + (0,) * (mk.ndim - 2))
    return pl.pallas_call(
        _cross_body,
        grid=(b // ns, t // tq),
        in_specs=[row_spec(D_MODEL)] + [row_spec(GROUP_W)] * 4 + [_weight(w_out, w_layer), _resident((1, D_MODEL)),
                  _weight(wq, w_layer), mem_spec, mem_spec, _weight(wo, w_layer)],
        out_specs=row_spec(D_MODEL),
        out_shape=jax.ShapeDtypeStruct((b, t, D_MODEL), F32),
        compiler_params=_cparams("parallel", "parallel"),
    )(x, *mix, w_out, gain.reshape(1, -1), wq, mk, mv, wo)


def _diff_body(lam_ref, q_ref, kn_ref, vn_ref, kp_ref, vp_ref, bn_ref, o_ref, q8_sc, m_sc, l_sc, acc_sc, *,
               tq, tk, past, lam_init, ns):
    ng = 2 * N_HEADS
    lane = _iota((tq, GROUP_W), 1)
    for sq in range(ns):
        q = q_ref[sq] * (DIFF_HALF ** -0.5 * LOG2E)
        for g in range(ng):
            lo = (g // 2) * HEAD_DIM + (g % 2) * DIFF_HALF
            q8_sc[sq, g * tq:(g + 1) * tq, :] = jnp.where((lane >= lo) & (lane < lo + DIFF_HALF), q, 0.0).astype(BF16)
    m_sc[...] = jnp.full(m_sc.shape, -jnp.inf, F32)
    l_sc[...] = jnp.zeros(l_sc.shape, F32)
    acc_sc[...] = jnp.zeros(acc_sc.shape, F32)
    pq = past + _iota((tq, 1), 0)
    pq_f = pq.astype(F32)
    lanes = _lanes
    pairs = [(sq, g) for sq in range(ns) for g in range(ng)]

    def block(k_ref, v_ref, k0, n, near, pos0=0):
        pk = pos0 + k0 + _iota((1, n), 1)
        pk_f = pk.astype(F32)
        if near:
            allowed = (pk >> CHUNK_SHIFT) <= (pq >> CHUNK_SHIFT)
            shift = pq_f - jnp.abs(pq_f - pk_f)
        s_all = [_dot(q8_sc[sq], k_ref[sq, :, pl.ds(k0, n)].astype(BF16)) for sq in range(ns)]
        scores = {}
        for sq, g in pairs:
            slope = 2.0 ** (-2 * (g // 2 + 1)) * LOG2E
            s = s_all[sq][g * tq:(g + 1) * tq]
            scores[sq, g] = jnp.where(allowed, s + slope * shift, NEG) if near else s + slope * pk_f
        m_prevs = {k: m_sc[k[0], k[1]] for k in pairs}
        m_news = {k: jnp.maximum(m_prevs[k], jnp.max(scores[k], axis=-1, keepdims=True)) for k in pairs}
        probs = {k: jnp.exp2(scores[k] - lanes(m_news[k], n)) for k in pairs}
        alphas = {k: jnp.exp2(m_prevs[k] - m_news[k]) for k in pairs}
        for k in pairs:
            l_sc[k[0], k[1]] = alphas[k] * l_sc[k[0], k[1]] + jnp.sum(probs[k], axis=-1, keepdims=True)
            m_sc[k[0], k[1]] = m_news[k]
        pvs = [_dot_nt(jnp.concatenate([probs[sq, g].astype(BF16) for g in range(ng)], axis=0),
                       v_ref[sq, :, pl.ds(k0, n)].astype(BF16)) for sq in range(ns)]
        for sq, g in pairs:
            acc_sc[sq, g] = acc_sc[sq, g] * lanes(alphas[sq, g], GROUP_W) + pvs[sq][g * tq:(g + 1) * tq]

    def past_step(j, carry):
        block(kp_ref, vp_ref, pl.multiple_of(j * tk, tk), tk, False)
        return carry
    lax.fori_loop(0, past // tk, past_step, 0)
    block(kn_ref, vn_ref, 0, tq, True, pos0=past)

    lam = lam_ref[0]
    lane_head = _iota((1, GROUP_W), 1) >> HEAD_SHIFT
    ones16 = _head_block_ones(GROUP_W).astype(BF16)
    for sq in range(ns):
        o = jnp.zeros((tq, GROUP_W), F32)
        for h in range(N_HEADS):
            o0 = acc_sc[sq, 2 * h] / lanes(l_sc[sq, 2 * h], GROUP_W)
            o1 = acc_sc[sq, 2 * h + 1] / lanes(l_sc[sq, 2 * h + 1], GROUP_W)
            o = jnp.where(lane_head == h, o0 - lam * o1, o)
        ms = _mm_r(o * o, ones16) * (1.0 / HEAD_DIM)
        o_ref[sq] = o * lax.rsqrt(ms + EPS) * bn_ref[...] * (1.0 - lam_init)


def _diff_attn(lam, q, kn, vn, kp, vp, layer, bnorm, lam_init):
    b, t, _ = q.shape
    past = kp.shape[3]
    ng = 2 * N_HEADS
    ns = DECODE_SEQS
    tq, tk = t, math.gcd(past, 1024)
    assert t == CHUNK and past % CHUNK == 0 and tk % LANES == 0 and b % ns == 0
    seq_spec = pl.BlockSpec((ns, GROUP_W, t), lambda i: (i, 0, 0))
    past_spec = pl.BlockSpec((None, ns, GROUP_W, past), lambda i: (layer, i, 0, 0))
    return pl.pallas_call(
        functools.partial(_diff_body, tq=tq, tk=tk, past=past, lam_init=lam_init, ns=ns),
        grid=(b // ns,),
        in_specs=[pl.BlockSpec(memory_space=pltpu.SMEM), pl.BlockSpec((ns, tq, GROUP_W), lambda i: (i, 0, 0)),
                  seq_spec, seq_spec, past_spec, past_spec, _resident((1, GROUP_W))],
        out_specs=pl.BlockSpec((ns, tq, GROUP_W), lambda i: (i, 0, 0)),
        out_shape=jax.ShapeDtypeStruct((b, t, GROUP_W), F32),
        scratch_shapes=[pltpu.VMEM((ns, ng * tq, GROUP_W), BF16), pltpu.VMEM((ns, ng, tq, LANES), F32),
                        pltpu.VMEM((ns, ng, tq, LANES), F32), pltpu.VMEM((ns, ng, tq, GROUP_W), F32)],
        compiler_params=_cparams("parallel"),
    )(lam, q, kn, vn, kp, vp, bnorm)


def _diff_prompt_body(lam_ref, q_ref, k_ref, v_ref, bn_ref, o_ref, qt_sc, m_sc, acc_sc, kn_sc, *, tq, tk, lam_init):
    i = pl.program_id(1)
    ng = 2 * N_HEADS
    qt = (q_ref[0] * (DIFF_HALF ** -0.5 * LOG2E)).T
    feat = _iota((GROUP_W, tq), 0)
    for g in range(ng):
        lo = (g // 2) * HEAD_DIM + (g % 2) * DIFF_HALF
        qt_sc[g] = jnp.where((feat >= lo) & (feat < lo + DIFF_HALF), qt, 0.0).astype(BF16)
    m_sc[...] = jnp.full(m_sc.shape, -jnp.inf, F32)
    acc_sc[...] = jnp.zeros(acc_sc.shape, F32)

    group_of = (_iota((GROUP_W, LANES), 0) // DIFF_HALF == _iota((GROUP_W, LANES), 1)).astype(BF16)

    @pl.when(i == 0)
    def _():
        def norm_step(j, best):
            kf = k_ref[0, pl.ds(pl.multiple_of(j * tk, tk), tk), :].astype(BF16).astype(F32)
            return jnp.maximum(best, jnp.max(_mm_r(kf * kf, group_of), axis=0, keepdims=True))
        kn_sc[...] = lax.fori_loop(0, k_ref.shape[1] // tk, norm_step, jnp.zeros((1, LANES), F32))

    def score_bound(g):
        qf = qt_sc[g].astype(F32)
        q2 = jnp.max(jnp.sum(qf * qf, axis=0, keepdims=True))
        return jnp.sqrt(q2 * kn_sc[0, g]) * 1.001 + 1.0
    pq = i * tq + _iota((1, tq), 1)
    pq_f = pq.astype(F32)
    ones_rows = jnp.ones((SUM_ROWS, tk), BF16)

    def block(k0, n, near, groups=range(2 * N_HEADS)):
        kb = k_ref[0, pl.ds(k0, n), :].astype(BF16)
        vb = v_ref[0, :, pl.ds(k0, n)].astype(BF16)
        pk = k0 + _iota((n, LANES), 0)
        pk_f = pk.astype(F32)
        if near:
            allowed = (_lanes(pk, tq) >> CHUNK_SHIFT) <= (pq >> CHUNK_SHIFT)
            shift = pq_f - jnp.abs(pq_f - _lanes(pk_f, tq))
        scores = {}
        for g in groups:
            slope = 2.0 ** (-2 * (g // 2 + 1)) * LOG2E
            s = _dot(kb, qt_sc[g])
            scores[g] = jnp.where(allowed, s + slope * shift, NEG) if near else s + _lanes(slope * pk_f, tq)
        m_prevs = {g: m_sc[g] for g in groups}
        m_news = {g: jnp.maximum(m_prevs[g], jnp.max(scores[g], axis=0, keepdims=True)) for g in groups}
        probs = {g: jnp.exp2(scores[g] - m_news[g]) for g in groups}
        for g in groups:
            alpha = jnp.exp2(m_prevs[g] - m_news[g])
            m_sc[g] = m_news[g]
            h = g // 2
            pv = _dot(jnp.concatenate([vb[h * HEAD_DIM:(h + 1) * HEAD_DIM, :], ones_rows[:, :n]], axis=0),
                      probs[g].astype(BF16))
            acc_sc[g] = acc_sc[g] * alpha + pv

    block(pl.multiple_of(i * tq, tq), tq, True)

    @pl.when((i * tq) % tk != 0)
    def _():
        block(pl.multiple_of((i - 1) * tq, tq), tq, False)
    nfull = (i * tq) // tk

    def dead_blocks(h):
        slope = 2.0 ** (-2 * (h + 1)) * LOG2E
        counts = [jnp.floor((jnp.min(m_sc[g]) + DIFF_DEAD - score_bound(g)) / (slope * tk)) for g in (2 * h, 2 * h + 1)]
        return jnp.clip(jnp.minimum(counts[0], counts[1]), 0, nfull).astype(jnp.int32)
    dead0 = dead_blocks(0)
    dead1 = jnp.minimum(dead_blocks(1), dead0)

    def steps(groups):
        def step(j, carry):
            block(pl.multiple_of(j * tk, tk), tk, False, groups)
            return carry
        return step
    lax.fori_loop(dead0, nfull, steps(range(ng)), 0)
    lax.fori_loop(dead1, dead0, steps(range(2, ng)), 0)
    lax.fori_loop(0, dead1, steps(range(4, ng)), 0)

    lam = lam_ref[0]
    norm = lambda g: acc_sc[g, :HEAD_DIM, :] / acc_sc[g, HEAD_DIM:HEAD_DIM + 1, :]
    heads = [norm(2 * h) - lam * norm(2 * h + 1) for h in range(N_HEADS)]
    o = jnp.concatenate(heads, axis=0).T
    ms = _mm_r(o * o, _head_block_ones(GROUP_W).astype(BF16)) * (1.0 / HEAD_DIM)
    o_ref[0] = o * lax.rsqrt(ms + EPS) * bn_ref[...] * (1.0 - lam_init)


def _diff_attn_prompt(lam, q, k_rows, v_feat, bnorm, lam_init):
    b, t, _ = q.shape
    ng = 2 * N_HEADS
    tq = min(256, t)
    tk = 2 * tq if t % (2 * tq) == 0 else tq
    assert t % tq == 0 and tq % LANES == 0
    return pl.pallas_call(
        functools.partial(_diff_prompt_body, tq=tq, tk=tk, lam_init=lam_init),
        grid=(b, t // tq),
        in_specs=[pl.BlockSpec(memory_space=pltpu.SMEM), pl.BlockSpec((1, tq, GROUP_W), lambda i, j: (i, j, 0)),
                  pl.BlockSpec((1, t, GROUP_W), lambda i, j: (i, 0, 0)),
                  pl.BlockSpec((1, GROUP_W, t), lambda i, j: (i, 0, 0)), _resident((1, GROUP_W))],
        out_specs=pl.BlockSpec((1, tq, GROUP_W), lambda i, j: (i, j, 0)),
        out_shape=jax.ShapeDtypeStruct((b, t, GROUP_W), F32),
        scratch_shapes=[pltpu.VMEM((ng, GROUP_W, tq), BF16), pltpu.VMEM((ng, 1, tq), F32),
                        pltpu.VMEM((ng, HEAD_DIM + SUM_ROWS, tq), F32), pltpu.VMEM((1, LANES), F32)],
        compiler_params=_cparams("parallel", "arbitrary"),
    )(lam, q, k_rows, v_feat, bnorm)


def _stick_body(*refs, tq, tk, past, ns, near):
    if near:
        q_ref, kn_ref, vn_ref, kp_ref, vp_ref, o_ref, acc_out, c_out, q4_sc, c_sc, acc_sc = refs
    else:
        q_ref, kp_ref, vp_ref, acc_in, c_in, o_ref, q4_sc, c_sc, acc_sc = refs
    nr = N_HEADS * tq
    lane = _iota((tq, GROUP_W), 1) >> HEAD_SHIFT
    for sq in range(ns):
        q = q_ref[sq] * (HEAD_DIM ** -0.5 * LOG2E)
        for h in range(N_HEADS):
            q4_sc[sq, h * tq:(h + 1) * tq, :] = jnp.where(lane == h, q, 0.0).astype(BF16)
    if near:
        c_sc[...] = jnp.zeros(c_sc.shape, F32)
        acc_sc[...] = jnp.zeros(acc_sc.shape, F32)
    else:
        c_sc[...] = c_in[...]
        acc_sc[...] = acc_in[...]
    pq = past + (_iota((nr, 1), 0) & (tq - 1))
    laters = {w: (_iota((w, w), 0) > _iota((w, w), 1)).astype(BF16)
              for w in {min(tq, STICK_SUB), min(tk, STICK_SUB)}}
    seqs = range(ns)

    def block(k_ref, v_ref, k0, n, masked, pos0=0):
        sub = min(n, STICK_SUB)
        later = laters[sub]
        zs = [_dot(q4_sc[sq], k_ref[sq, :, pl.ds(k0, n)].astype(BF16)) for sq in seqs]
        lss = [jnp.minimum(z, 0.0) - jnp.log2(1.0 + jnp.exp2(-jnp.abs(z))) for z in zs]
        lks = [ls - z for ls, z in zip(lss, zs)]
        if masked:
            mask = (pos0 + k0 + _iota((1, n), 1)) < pq
            lks = [jnp.where(mask, lk, 0.0) for lk in lks]
        his = [lk.astype(BF16) for lk in lks]
        los = [(lk - hi.astype(F32)).astype(BF16) for lk, hi in zip(lks, his)]
        carries = [c_sc[sq] for sq in seqs]
        parts = [[] for _ in seqs]
        for sb in reversed(range(n // sub)):
            sl = slice(sb * sub, (sb + 1) * sub)
            for sq in seqs:
                after = _dot(his[sq][:, sl], later) + _dot(los[sq][:, sl], later)
                parts[sq].append(lss[sq][:, sl] + after + _lanes(carries[sq], sub))
                carries[sq] = carries[sq] + (after[:, 0:1] + lks[sq][:, sb * sub:sb * sub + 1])
        for sq in seqs:
            c_sc[sq] = carries[sq]
            e = parts[sq][0] if len(parts[sq]) == 1 else jnp.concatenate(parts[sq][::-1], axis=1)
            if masked:
                a = jnp.where(mask, jnp.exp2(jnp.where(mask, e, 0.0)), 0.0)
            else:
                a = jnp.exp2(e)
            acc_sc[sq] = acc_sc[sq] + _dot_nt(a.astype(BF16), v_ref[sq, :, pl.ds(k0, n)].astype(BF16))

    if near:
        block(kn_ref, vn_ref, 0, tq, True, pos0=past)
        block(kp_ref, vp_ref, 0, tk, False)
        acc_out[...] = acc_sc[...]
        c_out[...] = c_sc[...]
    else:
        nb = past // tk - 1

        def live():
            return jnp.max(c_sc[...]) > STICK_DEAD

        def past_step(state):
            j, _ = state
            block(kp_ref, vp_ref, pl.multiple_of((nb - 1 - j) * tk, tk), tk, False)
            return j + 1, live()
        lax.while_loop(lambda state: (state[0] < nb) & state[1], past_step, (jnp.int32(0), live()))

    for sq in seqs:
        o = acc_sc[sq, 0:tq, :]
        for h in range(1, N_HEADS):
            o = jnp.where(lane == h, acc_sc[sq, h * tq:(h + 1) * tq, :], o)
        o_ref[sq] = o


def _stick_attn(q, kn, vn, kp, vp, layer):
    b, t, _ = q.shape
    past = kp.shape[3]
    ns = DECODE_SEQS
    tq, tk = t, math.gcd(past, 512)
    assert tq & (tq - 1) == 0 and tk % min(tk, STICK_SUB) == 0 and b % ns == 0
    nr = N_HEADS * tq
    row_spec = lambda w: pl.BlockSpec((ns, tq, w), lambda i: (i, 0, 0))
    seq_spec = pl.BlockSpec((ns, GROUP_W, t), lambda i: (i, 0, 0))
    acc_spec = pl.BlockSpec((ns, nr, GROUP_W), lambda i: (i, 0, 0))
    carry_spec = pl.BlockSpec((ns, nr, LANES), lambda i: (i, 0, 0))
    acc_shape = jax.ShapeDtypeStruct((b, nr, GROUP_W), F32)
    carry_shape = jax.ShapeDtypeStruct((b, nr, LANES), F32)
    scratch = [pltpu.VMEM((ns, nr, GROUP_W), BF16), pltpu.VMEM((ns, nr, LANES), F32), pltpu.VMEM((ns, nr, GROUP_W), F32)]
    last = past // tk - 1
    latest_spec = pl.BlockSpec((None, ns, GROUP_W, tk), lambda i: (layer, i, 0, last))
    out, acc, carry = pl.pallas_call(
        functools.partial(_stick_body, tq=tq, tk=tk, past=past, ns=ns, near=True),
        grid=(b // ns,),
        in_specs=[row_spec(GROUP_W), seq_spec, seq_spec, latest_spec, latest_spec],
        out_specs=[row_spec(GROUP_W), acc_spec, carry_spec],
        out_shape=[jax.ShapeDtypeStruct((b, t, GROUP_W), F32), acc_shape, carry_shape],
        scratch_shapes=scratch,
        compiler_params=_cparams("parallel"),
    )(q, kn, vn, kp, vp)
    if last == 0:
        return out

    def earlier_keys():
        past_spec = pl.BlockSpec((None, ns, GROUP_W, past), lambda i: (layer, i, 0, 0))
        return pl.pallas_call(
            functools.partial(_stick_body, tq=tq, tk=tk, past=past, ns=ns, near=False),
            grid=(b // ns,),
            in_specs=[row_spec(GROUP_W), past_spec, past_spec, acc_spec, carry_spec],
            out_specs=row_spec(GROUP_W),
            out_shape=jax.ShapeDtypeStruct((b, t, GROUP_W), F32),
            scratch_shapes=scratch,
            compiler_params=_cparams("parallel"),
        )(q, kp, vp, acc, carry)

    return lax.cond(jnp.max(carry) > STICK_DEAD, earlier_keys, lambda: out)


def _stick_prompt_body(q_ref, k_ref, v_ref, o_ref, qt_sc, c_sc, acc_sc, *, tq, tk):
    i = pl.program_id(1)
    qt = (q_ref[0] * (HEAD_DIM ** -0.5 * LOG2E)).T
    feat_head = _iota((GROUP_W, tq), 0) >> HEAD_SHIFT
    for h in range(N_HEADS):
        qt_sc[h] = jnp.where(feat_head == h, qt, 0.0).astype(BF16)
    c_sc[...] = jnp.zeros(c_sc.shape, F32)
    acc_sc[...] = jnp.zeros(acc_sc.shape, F32)
    pq = i * tq + _iota((1, tq), 1)
    sub = min(tq, STICK_SUB)
    later = (_iota((sub, sub), 1) > _iota((sub, sub), 0)).astype(BF16)

    def block(k0, n, masked):
        kb = k_ref[0, pl.ds(k0, n), :].astype(BF16)
        vb = v_ref[0, :, pl.ds(k0, n)].astype(BF16)
        if masked:
            mask = _lanes(k0 + _iota((n, LANES), 0), tq) < pq
        heads = range(N_HEADS)
        zs = [_dot(kb, qt_sc[h]) for h in heads]
        lss = [jnp.minimum(z, 0.0) - jnp.log2(1.0 + jnp.exp2(-jnp.abs(z))) for z in zs]
        lks = [ls - z for ls, z in zip(lss, zs)]
        if masked:
            lks = [jnp.where(mask, lk, 0.0) for lk in lks]
        es = []
        for h in heads:
            lk = lks[h]
            hi = lk.astype(BF16)
            lo = (lk - hi.astype(F32)).astype(BF16)
            carry = c_sc[h]
            parts = []
            for sb in reversed(range(n // sub)):
                sl = slice(sb * sub, (sb + 1) * sub)
                after = _dot(later, hi[sl]) + _dot(later, lo[sl])
                parts.append(lss[h][sl] + after + carry)
                carry = carry + (after[0:1] + lk[sb * sub:sb * sub + 1])
            c_sc[h] = carry
            es.append(parts[0] if len(parts) == 1 else jnp.concatenate(parts[::-1], axis=0))
        for h in heads:
            if masked:
                a = jnp.where(mask, jnp.exp2(jnp.where(mask, es[h], 0.0)), 0.0)
            else:
                a = jnp.exp2(es[h])
            acc_sc[h] = acc_sc[h] + _dot(vb[h * HEAD_DIM:(h + 1) * HEAD_DIM, :], a.astype(BF16))

    block(pl.multiple_of(i * tq, tq), tq, True)

    @pl.when((i * tq) % tk != 0)
    def _():
        block(pl.multiple_of((i - 1) * tq, tq), tq, False)
    nfull = (i * tq) // tk

    def live():
        return jnp.max(c_sc[...]) > STICK_DEAD

    def prev_step(state):
        j, _ = state
        block(pl.multiple_of((nfull - 1 - j) * tk, tk), tk, False)
        return j + 1, live()
    lax.while_loop(lambda state: (state[0] < nfull) & state[1], prev_step, (jnp.int32(0), live()))

    o_ref[0] = jnp.concatenate([acc_sc[h] for h in range(N_HEADS)], axis=0).T


def _stick_attn_prompt(q, k_rows, v_feat):
    b, t, _ = q.shape
    tq = min(256, t)
    tk = tq
    assert t % tq == 0 and tq % LANES == 0 and tk % min(tk, STICK_SUB) == 0
    return pl.pallas_call(
        functools.partial(_stick_prompt_body, tq=tq, tk=tk),
        grid=(b, t // tq),
        in_specs=[pl.BlockSpec((1, tq, GROUP_W), lambda i, j: (i, j, 0)),
                  pl.BlockSpec((1, t, GROUP_W), lambda i, j: (i, 0, 0)),
                  pl.BlockSpec((1, GROUP_W, t), lambda i, j: (i, 0, 0))],
        out_specs=pl.BlockSpec((1, tq, GROUP_W), lambda i, j: (i, j, 0)),
        out_shape=jax.ShapeDtypeStruct((b, t, GROUP_W), F32),
        scratch_shapes=[pltpu.VMEM((N_HEADS, GROUP_W, tq), BF16), pltpu.VMEM((N_HEADS, 1, tq), F32),
                        pltpu.VMEM((N_HEADS, HEAD_DIM, tq), F32)],
        compiler_params=_cparams("parallel", "parallel"),
    )(q, k_rows, v_feat)


def _pair_mask(n, s, reps=1):
    r, c = _iota((reps * n, n), 0) & (n - 1), _iota((reps * n, n), 1)
    return ((r // (2 * s)) == (c // (2 * s))) & (((r // s) & 1) == 1) & (((c // s) & 1) == 0)


def _state_spec(nb):
    return pl.BlockSpec((nb, N_HEADS, HEAD_DIM, HEAD_DIM), lambda i, j: (i, 0, 0, 0))


def _load_state(s_sc, s0_ref, transpose):
    s_sc[...] = jnp.zeros(s_sc.shape, F32)
    for b in range(s_sc.shape[0]):
        for h in range(N_HEADS):
            blk = s0_ref[b, h]
            s_sc[b, h * HEAD_DIM:(h + 1) * HEAD_DIM, h * HEAD_DIM:(h + 1) * HEAD_DIM] = blk.T if transpose else blk


def _store_state(sout_ref, s_sc, transpose):
    for b in range(s_sc.shape[0]):
        for h in range(N_HEADS):
            blk = s_sc[b, h * HEAD_DIM:(h + 1) * HEAD_DIM, h * HEAD_DIM:(h + 1) * HEAD_DIM]
            sout_ref[b, h] = blk.T if transpose else blk


def _head_rows(x, lane_head):
    return jnp.concatenate([jnp.where(lane_head == h, x, 0.0) for h in range(N_HEADS)], axis=0)


def _head_diag(x, lane_head):
    n = x.shape[0] // N_HEADS
    out = x[:n]
    for h in range(1, N_HEADS):
        out = jnp.where(lane_head == h, x[h * n:(h + 1) * n], out)
    return out


def _gdn_body(x_ref, z_ref, abg_ref, cw_ref, cs_ref, s0_ref, alog_ref, dtb_ref, an_ref,
              o_ref, sout_ref, s_sc, xb_sc, *, nc, nb):
    c = pl.program_id(1)
    L = CHUNK
    pad = 8

    @pl.when(c == 0)
    def _():
        _load_state(s_sc, s0_ref, transpose=False)
        xb_sc[:, pad - 3:pad, :] = cs_ref[...]

    ones_bd = _head_block_ones(GROUP_W)
    ones_bd16 = ones_bd.astype(BF16)
    er, ec = _iota((ABG_W, GROUP_W), 0), _iota((ABG_W, GROUP_W), 1) >> HEAD_SHIFT
    e_beta = (er == ec).astype(BF16)
    e_g = (er == ec + N_HEADS).astype(BF16)
    ri, ci = _iota((L, L), 0), _iota((L, L), 1)
    tri16 = (ri >= ci).astype(BF16)
    incl = ri >= ci
    strict = ri > ci
    eye = (ri == ci).astype(F32)
    pair_masks = [_pair_mask(L, sz) for sz in (1, 2, 4, 8, 16, 32)]
    lane_head = _iota((1, GROUP_W), 1) >> HEAD_SHIFT
    cw = cw_ref[...]

    seqs = range(nb)
    us = []
    for b in seqs:
        x = x_ref[b]
        xb_sc[b, pad:pad + L, :] = x
        u = (x * cw[3:4] + xb_sc[b, pad - 1:pad - 1 + L, :] * cw[2:3]
             + xb_sc[b, pad - 2:pad - 2 + L, :] * cw[1:2] + xb_sc[b, pad - 3:pad - 3 + L, :] * cw[0:1])
        xb_sc[b, pad - 3:pad, :] = x[L - 3:L, :]
        us.append(u * _sigmoid(u))
    qs = [u[:, :GROUP_W] for u in us]
    ks = [u[:, GROUP_W:2 * GROUP_W] for u in us]
    vs = [u[:, 2 * GROUP_W:] for u in us]
    qs = [q * lax.rsqrt(_mm_r(q * q, ones_bd16) + EPS) * HEAD_DIM ** -0.5 for q in qs]
    ks = [k * lax.rsqrt(_mm_r(k * k, ones_bd16) + EPS) for k in ks]

    abgs = [abg_ref[b] for b in seqs]
    betas = [_mm_r(_sigmoid(abg), e_beta) for abg in abgs]
    gc_ns = [_mm_l(tri16, -jnp.exp(alog_ref[...]) * _softplus(abg + dtb_ref[...])) for abg in abgs]
    gc_ts = [gc_n.T for gc_n in gc_ns]
    gcs = [_mm_r(gc_n, e_g) for gc_n in gc_ns]

    kbs = [k * beta for k, beta in zip(ks, betas)]
    k16s = [k.astype(BF16) for k in ks]
    a4s = [_bdot(_head_rows(kb, lane_head).astype(BF16), k16, _NT) for kb, k16 in zip(kbs, k16s)]
    qk4s = [_bdot(_head_rows(q, lane_head).astype(BF16), k16, _NT) for q, k16 in zip(qs, k16s)]
    decs = []
    for b in seqs:
        dec = []
        for h in range(N_HEADS):
            diff = gc_ns[b][:, N_HEADS + h:N_HEADS + h + 1] - gc_ts[b][N_HEADS + h:N_HEADS + h + 1, :]
            dec.append(jnp.where(incl, jnp.exp(jnp.where(incl, diff, 0.0)), 0.0))
        decs.append(jnp.stack(dec))

    a_all = jnp.concatenate([jnp.where(strict, a4s[b].reshape(N_HEADS, L, L) * decs[b], 0.0) for b in seqs], axis=0)
    t_inv = eye - jnp.where(pair_masks[0], a_all, 0.0)
    a16 = a_all.astype(BF16)
    for pm in pair_masks[1:]:
        t16 = t_inv.astype(BF16)
        t_inv = t_inv - _bdot(t16, _bdot(jnp.where(pm, a16, 0.0), t16, _BNN).astype(BF16), _BNN)

    exp_gs = [jnp.exp(gc) for gc in gcs]
    s_olds = [s_sc[b] for b in seqs]
    s16s = [s.astype(BF16) for s in s_olds]
    rhss = [vs[b] * betas[b] - _bdot((kbs[b] * exp_gs[b]).astype(BF16), s16s[b], _NN) for b in seqs]
    ws = [_head_diag(_mm1(t_inv[b * N_HEADS:(b + 1) * N_HEADS].reshape(N_HEADS * L, L), rhss[b]), lane_head)
          for b in seqs]
    outs = [_bdot((qs[b] * exp_gs[b]).astype(BF16), s16s[b], _NN)
            + _head_diag(_mm1((qk4s[b].reshape(N_HEADS, L, L) * decs[b]).reshape(N_HEADS * L, L), ws[b]), lane_head)
            for b in seqs]
    for b in seqs:
        g_last = gcs[b][L - 1:L, :]
        k_dec = ks[b] * jnp.exp(g_last - gcs[b])
        s_sc[b] = s_olds[b] * jnp.exp(g_last) + _mm1(k_dec.T, ws[b]) * ones_bd
    for b in seqs:
        o = outs[b]
        ms = _mm_r(o * o, ones_bd16) * (1.0 / HEAD_DIM)
        zg = z_ref[b]
        o_ref[b] = o * lax.rsqrt(ms + EPS) * an_ref[...] * (zg * _sigmoid(zg))

    @pl.when(c == nc - 1)
    def _():
        _store_state(sout_ref, s_sc, transpose=False)


def _gdn(aqkv, az, abg, conv_w, conv_state, s0, alog_row, dtb_row, anorm_row):
    b, t, _ = aqkv.shape
    nc = t // CHUNK
    nb = math.gcd(b, SEQ_PER_STEP)
    cmap = lambda i, j: (i, j, 0)
    bmap = lambda i, j: (i, 0, 0)
    return pl.pallas_call(
        functools.partial(_gdn_body, nc=nc, nb=nb),
        grid=(b // nb, nc),
        in_specs=[pl.BlockSpec((nb, CHUNK, 3 * GROUP_W), cmap), pl.BlockSpec((nb, CHUNK, GROUP_W), cmap),
                  pl.BlockSpec((nb, CHUNK, ABG_W), cmap), _resident((CONV_W, 3 * GROUP_W)),
                  pl.BlockSpec((nb, CONV_W - 1, 3 * GROUP_W), bmap), _state_spec(nb),
                  _resident((1, ABG_W)), _resident((1, ABG_W)), _resident((1, GROUP_W))],
        out_specs=[pl.BlockSpec((nb, CHUNK, GROUP_W), cmap), _state_spec(nb)],
        out_shape=[jax.ShapeDtypeStruct((b, t, GROUP_W), F32), jax.ShapeDtypeStruct(s0.shape, F32)],
        scratch_shapes=[pltpu.VMEM((nb, GROUP_W, GROUP_W), F32), pltpu.VMEM((nb, 8 + CHUNK, 3 * GROUP_W), F32)],
        compiler_params=_cparams("parallel", "arbitrary"),
    )(aqkv, az, abg, conv_w, conv_state, s0, alog_row, dtb_row, anorm_row)


def _hgrn_body(q_ref, f_ref, i_ref, g_ref, lb_ref, dn_ref, s0_ref, o_ref, sout_ref, s_sc, *, nc, nb):
    c = pl.program_id(1)
    L = CHUNK

    @pl.when(c == 0)
    def _():
        _load_state(s_sc, s0_ref, transpose=True)

    lb = lb_ref[...]
    ones_bd = _head_block_ones(GROUP_W)
    ones_bd16 = ones_bd.astype(BF16)
    ri, ci = _iota((L, L), 0), _iota((L, L), 1)
    tri16 = (ri >= ci).astype(BF16)
    sizes = (1, 2, 4, 8, 16, 32)
    sel16 = jnp.concatenate([(ci == (ri // (2 * sz)) * (2 * sz) + sz).astype(BF16) for sz in sizes], axis=0)
    eye4 = (_iota((N_HEADS * L, L), 0) & (L - 1)) == _iota((N_HEADS * L, L), 1)
    pair_masks4 = [_pair_mask(L, sz, reps=N_HEADS) for sz in sizes]
    lane_head = _iota((1, GROUP_W), 1) >> HEAD_SHIFT
    rows = _iota((L, 1), 0)

    pre = []
    for b in range(nb):
        fl = f_ref[b]
        log_f = _log_sigmoid(fl) + jnp.log(1.0 + lb * jnp.exp(-fl))
        k = (1.0 - lb) * _sigmoid(-fl)
        q = q_ref[b] * HEAD_DIM ** -0.5
        bc = _mm_l(tri16, log_f)
        pre.append((q, k, bc))
    b_refs = [_mm_l(sel16, bc) for _, _, bc in pre]
    a_all = [jnp.where(eye4, _mm1(_head_rows(q, lane_head), k, _NT), 0.0) for q, k, _ in pre]
    for lv, sz in enumerate(sizes):
        later = ((rows // sz) & 1) == 1
        for b in range(nb):
            q, k, bc = pre[b]
            b_ref = b_refs[b][lv * L:(lv + 1) * L]
            x = jnp.where(later, q * jnp.exp(jnp.where(later, bc - b_ref, 0.0)), 0.0)
            y = jnp.where(later, 0.0, k * jnp.exp(jnp.where(later, 0.0, b_ref - bc)))
            a_all[b] = a_all[b] + jnp.where(pair_masks4[lv], _mm1(_head_rows(x, lane_head), y, _NT), 0.0)

    for b in range(nb):
        q, k, bc = pre[b]
        v = i_ref[b]
        st = s_sc[b]
        o = _mm1(q * jnp.exp(bc), st, _NT) + _head_diag(_mm1(a_all[b], v), lane_head)
        b_last = bc[L - 1:L, :]
        st_new = st * jnp.exp(b_last) + _mm1(v.T, k * jnp.exp(b_last - bc)) * ones_bd
        s_sc[b] = st_new

        ms = _mm_r(o * o, ones_bd16) * (1.0 / HEAD_DIM)
        zg = g_ref[b]
        o_ref[b] = o * lax.rsqrt(ms + EPS) * dn_ref[...] * (zg * _sigmoid(zg))

    @pl.when(c == nc - 1)
    def _():
        _store_state(sout_ref, s_sc, transpose=True)


def _hgrn(dq, df, di, dg, lb_row, dnorm_row, s0):
    b, t, _ = dq.shape
    nc = t // CHUNK
    nb = math.gcd(b, SEQ_PER_STEP)
    cmap = lambda i, j: (i, j, 0)
    bmap = lambda i, j: (i, 0, 0)
    cspec = pl.BlockSpec((nb, CHUNK, GROUP_W), cmap)
    return pl.pallas_call(
        functools.partial(_hgrn_body, nc=nc, nb=nb),
        grid=(b // nb, nc),
        in_specs=[cspec, cspec, cspec, cspec, _resident((1, GROUP_W)), _resident((1, GROUP_W)),
                  _state_spec(nb)],
        out_specs=[cspec, _state_spec(nb)],
        out_shape=[jax.ShapeDtypeStruct((b, t, GROUP_W), F32), jax.ShapeDtypeStruct(s0.shape, F32)],
        scratch_shapes=[pltpu.VMEM((nb, GROUP_W, GROUP_W), F32)],
        compiler_params=_cparams("parallel", "arbitrary"),
    )(dq, df, di, dg, lb_row, dnorm_row, s0)


IN_WIDTHS = (3 * GROUP_W,) + (GROUP_W,) * 11 + (ABG_W,)
KV_SEGMENTS = (3, 4, 6, 7)
K_SEGMENTS = (3, 6)


def _layer(x, l, st, mk, mv, mem_layer, prm, final_gain):
    b, t, _ = x.shape
    m = b * t
    tm = math.gcd(m, ROW_TILE)
    has_past = st["b_k"] is not None
    outs = _fused_linear(x.reshape(m, D_MODEL), prm["w_in"], l, IN_WIDTHS, prm["norm_mix"][l], tm,
                         feat_major=KV_SEGMENTS, rows_per_seq=t, rows_too=() if has_past else K_SEGMENTS)
    outs = [o if idx in KV_SEGMENTS else o.reshape(b, t, -1) for idx, o in enumerate(outs)]
    (a_qkv, a_z, b_q, b_k, b_v, c_q, c_k, c_v, d_q, d_f, d_i, d_g, a_bg) = outs[:len(IN_WIDTHS)]

    o_a, a_s = _gdn(a_qkv, a_z, a_bg, prm["a_conv_w"][l], st["a_conv"], st["a_S"].astype(F32),
                    prm["alog_row"][l], prm["dtb_row"][l], prm["a_norm"][l])
    lam_init = 0.8 - 0.6 * math.exp(-0.3 * l)
    if has_past:
        o_b = _diff_attn(prm["lam"][l], b_q, b_k, b_v, st["b_k"], st["b_v"], l, prm["b_norm"][l], lam_init)
        o_c = _stick_attn(c_q, c_k, c_v, st["c_k"], st["c_v"], l)
    else:
        b_k_rows, c_k_rows = outs[len(IN_WIDTHS):]
        o_b = _diff_attn_prompt(prm["lam"][l], b_q, b_k_rows, b_v, prm["b_norm"][l], lam_init)
        o_c = _stick_attn_prompt(c_q, c_k_rows, c_v)
    o_d, d_s = _hgrn(d_q, d_f, d_i, d_g, prm["lb"][l], prm["d_norm"][l], st["d_S"].astype(F32))

    x2 = _cross(x, (o_a, o_b, o_c, o_d), prm["w_out"], prm["norm_cross"][l], prm["w_cq"], mk, mv, mem_layer,
                prm["w_co"], l)
    x2 = _ffn(x2.reshape(m, D_MODEL), prm["norm_ffn"][l], prm["w_gate"], prm["w_up"], prm["w_down"], l,
              final_gain=final_gain, tm=tm)

    if t >= CONV_W - 1:
        conv_new = a_qkv[:, t - (CONV_W - 1):, :]
    else:
        conv_new = jnp.concatenate([st["a_conv"], a_qkv], axis=1)[:, -(CONV_W - 1):, :]
    heads = lambda a: jnp.transpose(a.reshape(b, N_HEADS, HEAD_DIM, t), (0, 3, 1, 2))
    new = (conv_new, a_s, heads(b_k), heads(b_v), heads(c_k), heads(c_v), d_s)
    return x2.reshape(b, t, D_MODEL), new


def kernel(x_prompt, x_sample, mem_prompt, state_a_conv, state_a_S, cache_b_k, cache_b_v, cache_c_k, cache_c_v,
           state_d_S, cache_mem_k, cache_mem_v, norm_mix, w_in, a_conv_w, a_A_log, a_dt_bias, a_norm, b_lam_q1,
           b_lam_k1, b_lam_q2, b_lam_k2, b_norm, d_lb, d_norm, w_out, norm_cross, norm_memtok, w_cq, w_ck, w_cv,
           w_co, norm_ffn, w_gate, w_up, w_down, norm_final):
    depth = w_in.shape[0]
    n_main = 4 * GROUP_W
    w_in_p = jnp.concatenate(
        [w_in[:, :, :n_main], w_in[:, :, n_main + 2 * N_HEADS:], w_in[:, :, n_main:n_main + 2 * N_HEADS],
         jnp.zeros((depth, D_MODEL, ABG_W - 2 * N_HEADS), w_in.dtype)], axis=2).astype(BF16)
    pad_row = lambda a: jnp.pad(a.astype(F32), ((0, 0), (N_HEADS, ABG_W - 2 * N_HEADS)))[:, None, :]
    tile_row = lambda a: jnp.tile(a.astype(F32), (1, N_HEADS))[:, None, :]
    p_lb = jax.nn.softmax(d_lb.astype(F32), axis=0)
    lam = (jnp.exp(jnp.sum(b_lam_q1.astype(F32) * b_lam_k1.astype(F32), axis=-1))
           - jnp.exp(jnp.sum(b_lam_q2.astype(F32) * b_lam_k2.astype(F32), axis=-1))
           + jnp.array([0.8 - 0.6 * math.exp(-0.3 * l) for l in range(depth)], F32))
    prm = {
        "w_in": w_in_p, "norm_mix": norm_mix, "a_conv_w": a_conv_w.astype(F32),
        "alog_row": pad_row(a_A_log), "dtb_row": pad_row(a_dt_bias), "a_norm": tile_row(a_norm),
        "lam": lam[:, None], "b_norm": tile_row(b_norm), "lb": (jnp.cumsum(p_lb, axis=0) - p_lb[0])[:, None, :],
        "d_norm": tile_row(d_norm), "w_out": w_out.astype(BF16), "norm_cross": norm_cross,
        "w_cq": w_cq.astype(BF16), "w_co": w_co.astype(BF16), "norm_ffn": norm_ffn,
        "w_gate": w_gate.astype(BF16), "w_up": w_up.astype(BF16), "w_down": w_down.astype(BF16),
    }
    w_ckv = jnp.concatenate([w_ck, w_cv], axis=2).astype(BF16)

    bp, tp, _ = x_prompt.shape
    n_mem = mem_prompt.shape[1]
    h = x_prompt
    p_new = []
    for l in range(depth):
        mk, mv = _fused_linear(mem_prompt.reshape(bp * n_mem, D_MODEL), w_ckv, l, (D_MODEL, D_MODEL),
                               norm_memtok[l], math.gcd(bp * n_mem, ROW_TILE))
        mk = mk.reshape(1, bp, n_mem, D_MODEL)
        mv = mv.reshape(1, bp, n_mem, D_MODEL)
        st = {"a_conv": jnp.zeros((bp, CONV_W - 1, 3 * GROUP_W), F32),
              "a_S": jnp.zeros((bp, N_HEADS, HEAD_DIM, HEAD_DIM), F32), "b_k": None, "b_v": None,
              "c_k": None, "c_v": None, "d_S": jnp.zeros((bp, N_HEADS, HEAD_DIM, HEAD_DIM), F32)}
        h, new = _layer(h, l, st, mk, mv, 0, prm, norm_final if l == depth - 1 else None)
        mem4 = lambda a: a.reshape(bp, n_mem, MEM_HEADS, MEM_HEAD_DIM)
        p_new.append(new + (mem4(mk), mem4(mv)))
    y_prompt = h
    p_stacked = [jnp.stack(c) for c in zip(*p_new)]

    bs, ts, _ = x_sample.shape
    past = cache_b_k.shape[2]
    h = x_sample
    s_new = []
    flat = lambda a: jnp.transpose(a, (0, 1, 3, 4, 2)).reshape(depth, bs, GROUP_W, past)
    caches = {"b_k": flat(cache_b_k), "b_v": flat(cache_b_v), "c_k": flat(cache_c_k), "c_v": flat(cache_c_v)}
    for l in range(depth):
        st = {"a_conv": state_a_conv[l], "a_S": state_a_S[l], "d_S": state_d_S[l], **caches}
        h, new = _layer(h, l, st, cache_mem_k, cache_mem_v, l, prm, norm_final if l == depth - 1 else None)
        s_new.append(new)
    y_sample = h
    s_stacked = [jnp.stack(c) for c in zip(*s_new)]

    return (y_prompt, y_sample, *p_stacked, *s_stacked)
```
